```python
import jax
import jax.numpy as jnp
from jax import lax
import numpy as np

D_MODEL = 1024
BATCH = 8
SEQ = 2048
DEPTH = 2

CHUNK = 64

RET_HEADS = D_MODEL // 256
RET_DK = 64
RET_DV = 64
RET_W = RET_HEADS * RET_DV

SSD_HEADS = D_MODEL // 128
SSD_HEAD_DIM = 64
SSD_STATE = 64
SSD_GROUPS = 2
SSD_CONV = 4
SSD_W = SSD_HEADS * SSD_HEAD_DIM
SSD_XBC = SSD_W + 2 * SSD_GROUPS * SSD_STATE

GLA_HEADS = D_MODEL // 256
GLA_DK = 32
GLA_DV = 64
GLA_GATE_RANK = 16
GLA_GATE_TEMP = 16.0
GLA_W = GLA_HEADS * GLA_DV

D_MIX = RET_W + SSD_W + GLA_W

IN_SPLITS = (RET_HEADS * RET_DK, RET_HEADS * RET_DK, RET_W, RET_W,
             SSD_W, SSD_XBC, SSD_HEADS,
             GLA_HEADS * GLA_DK, GLA_HEADS * GLA_DK, GLA_W, GLA_GATE_RANK, GLA_W)
D_IN = sum(IN_SPLITS)

N_EXPERTS = 32
TOP_K = 4
D_FF = D_MODEL
SWIGLU_LIMIT = 7.0
SWIGLU_ALPHA = 1.702
MOE_BLOCK = 128

ROPE_BASE = 10000.0
LN_EPS = 1e-5
NORM_EPS = 1e-6
DEEPNORM_ALPHA = (2.0 * DEPTH) ** 0.25
DEEPNORM_BETA = (8.0 * DEPTH) ** -0.25

kernel_name = 'hybrid_ret_ssd_gla_moe_deepnorm'


def _chunks(t):
    return t.reshape(t.shape[0], t.shape[1] // CHUNK, CHUNK, *t.shape[2:])


def _unchunk(t):
    return t.reshape(t.shape[0], t.shape[1] * t.shape[2], *t.shape[3:])


def _layer_norm(x, g, b):
    xf = x.astype(jnp.float32)
    mu = jnp.mean(xf, -1, keepdims=True)
    var = jnp.mean(jnp.square(xf - mu), -1, keepdims=True)
    return ((xf - mu) * lax.rsqrt(var + LN_EPS) * g + b).astype(x.dtype)


def _rms(t):
    return t * lax.rsqrt(jnp.mean(jnp.square(t), -1, keepdims=True) + NORM_EPS)


def _scan_chunk_states(decay, contrib):
    def step(state, inp):
        a, u = inp
        return a * state + u, state
    init = jnp.zeros_like(contrib[:, 0])
    _, prev = lax.scan(step, init, (jnp.moveaxis(decay, 1, 0), jnp.moveaxis(contrib, 1, 0)))
    return jnp.moveaxis(prev, 0, 1)


def _rotary(t, positions):
    half = t.shape[-1] // 2
    inv_freq = ROPE_BASE ** (-jnp.arange(half, dtype=jnp.float32) / half)
    ang = positions.astype(jnp.float32)[:, :, None] * inv_freq
    cos = jnp.cos(ang)[:, :, None, :]
    sin = jnp.sin(ang)[:, :, None, :]
    t1, t2 = t[..., :half], t[..., half:]
    return jnp.concatenate([t1 * cos - t2 * sin, t1 * sin + t2 * cos], -1)


def _retention(q, k, v, g, positions, norm_w):
    bsz, seq, _ = q.shape
    f32 = jnp.float32
    q = _rotary(q.reshape(bsz, seq, RET_HEADS, RET_DK).astype(f32), positions)
    k = _rotary(k.reshape(bsz, seq, RET_HEADS, RET_DK).astype(f32), positions) * RET_DK ** -0.5
    v = v.reshape(bsz, seq, RET_HEADS, RET_DV).astype(f32)
    log_gamma = jnp.log(1.0 - 2.0 ** (-5.0 - jnp.arange(RET_HEADS, dtype=f32)))
    pos = jnp.arange(CHUNK, dtype=f32)
    dist = pos[:, None] - pos[None, :]
    intra = jnp.where(dist >= 0, jnp.exp(log_gamma[:, None, None] * jnp.maximum(dist, 0.0)), 0.0)
    qc, kc, vc = _chunks(q), _chunks(k), _chunks(v)
    scores = jnp.einsum('bnchd,bnshd->bnhcs', qc, kc) * intra
    o = jnp.einsum('bnhcs,bnshe->bnche', scores, vc)
    k_decay = jnp.exp(log_gamma[None, :] * (CHUNK - 1.0 - pos)[:, None])
    contrib = jnp.einsum('bnshd,bnshe->bnhde', kc * k_decay[:, :, None], vc)
    chunk_decay = jnp.broadcast_to(jnp.exp(log_gamma * CHUNK)[:, None, None],
                                   (bsz, seq // CHUNK, RET_HEADS, 1, 1))
    s_prev = _scan_chunk_states(chunk_decay, contrib)
    q_decay = jnp.exp(log_gamma[None, :] * (pos + 1.0)[:, None])
    o = o + jnp.einsum('bnchd,bnhde->bnche', qc * q_decay[:, :, None], s_prev)
    o = _unchunk(o)
    mu = jnp.mean(o, -1, keepdims=True)
    o = (o - mu) * lax.rsqrt(jnp.mean(jnp.square(o - mu), -1, keepdims=True) + LN_EPS)
    o = o.reshape(bsz, seq, RET_W) * norm_w
    return jax.nn.silu(g.astype(f32)) * o


def _ssd(z, xbc, dt_raw, conv_w, conv_b, dt_bias, a_log, d_skip, norm_w):
    bsz, seq, _ = xbc.shape
    f32 = jnp.float32
    hpg = SSD_HEADS // SSD_GROUPS
    xbc = lax.conv_general_dilated(xbc.astype(f32), conv_w.astype(f32)[:, None, :],
                                   window_strides=(1,), padding=[(SSD_CONV - 1, 0)],
                                   dimension_numbers=('NWC', 'WIO', 'NWC'),
                                   feature_group_count=SSD_XBC)
    xbc = jax.nn.silu(xbc + conv_b)
    xs, bm, cm = jnp.split(xbc, [SSD_W, SSD_W + SSD_GROUPS * SSD_STATE], axis=-1)
    xc = _chunks(xs.reshape(bsz, seq, SSD_GROUPS, hpg, SSD_HEAD_DIM))
    bc = _chunks(bm.reshape(bsz, seq, SSD_GROUPS, SSD_STATE))
    cc = _chunks(cm.reshape(bsz, seq, SSD_GROUPS, SSD_STATE))
    dt = jax.nn.softplus(dt_raw.astype(f32) + dt_bias)
    dtc = _chunks(dt.reshape(bsz, seq, SSD_GROUPS, hpg))
    a = -jnp.exp(a_log.astype(f32)).reshape(SSD_GROUPS, hpg)
    a_cum = jnp.cumsum(dtc * a, axis=2)
    causal = jnp.tril(jnp.ones((CHUNK, CHUNK), dtype=bool))[:, :, None, None]
    seg = a_cum[:, :, :, None] - a_cum[:, :, None, :]
    decay_mat = jnp.exp(jnp.where(causal, seg, -jnp.inf))
    cb = jnp.einsum('bncgk,bnsgk->bncsg', cc, bc)
    xdt = xc * dtc[..., None]
    y = jnp.einsum('bncsgj,bnsgjp->bncgjp', cb[..., None] * decay_mat, xdt)
    state_decay = jnp.exp(a_cum[:, :, -1:] - a_cum)
    contrib = jnp.einsum('bnsgk,bnsgjp->bngjpk', bc, xdt * state_decay[..., None])
    s_prev = _scan_chunk_states(jnp.exp(a_cum[:, :, -1])[..., None, None], contrib)
    y = y + jnp.einsum('bncgk,bngjpk->bncgjp', cc, s_prev) * jnp.exp(a_cum)[..., None]
    y = y + d_skip.astype(f32).reshape(SSD_GROUPS, hpg)[:, :, None] * xc
    y = _unchunk(y).reshape(bsz, seq, SSD_GROUPS, hpg * SSD_HEAD_DIM)
    zg = jax.nn.silu(z.astype(f32)).reshape(bsz, seq, SSD_GROUPS, hpg * SSD_HEAD_DIM)
    return _rms(y * zg).reshape(bsz, seq, SSD_W) * norm_w


def _gla(q, k, v, gk_low, g, w_gk2, b_gk2, norm_w):
    bsz, seq, _ = q.shape
    f32 = jnp.float32
    q = q.reshape(bsz, seq, GLA_HEADS, GLA_DK).astype(f32) * GLA_DK ** -0.5
    k = k.reshape(bsz, seq, GLA_HEADS, GLA_DK).astype(f32)
    v = v.reshape(bsz, seq, GLA_HEADS, GLA_DV).astype(f32)
    gk = jnp.einsum('bsr,rk->bsk', gk_low.astype(f32), w_gk2.astype(f32)) + b_gk2
    log_a = (jax.nn.log_sigmoid(gk) / GLA_GATE_TEMP).reshape(bsz, seq, GLA_HEADS, GLA_DK)
    qc, kc, vc = _chunks(q), _chunks(k), _chunks(v)
    b = jnp.cumsum(_chunks(log_a), axis=2)
    q_t = qc * jnp.exp(b)
    att = jnp.einsum('bnchd,bnshd->bnhcs', q_t, kc * jnp.exp(-b))
    causal = jnp.tril(jnp.ones((CHUNK, CHUNK), dtype=bool))
    o = jnp.einsum('bnhcs,bnshe->bnche', jnp.where(causal, att, 0.0), vc)
    b_last = b[:, :, -1:]
    contrib = jnp.einsum('bnshd,bnshe->bnhde', kc * jnp.exp(b_last - b), vc)
    s_prev = _scan_chunk_states(jnp.exp(b_last[:, :, 0])[..., None], contrib)
    o = o + jnp.einsum('bnchd,bnhde->bnche', q_t, s_prev)
    o = _rms(_unchunk(o)).reshape(bsz, seq, GLA_W) * norm_w
    return jax.nn.silu(g.astype(f32)) * o


def _moe(x, w_router, b_router, w_gate, b_gate, w_up, b_up, w_down, b_down):
    bsz, seq, d = x.shape
    n_tok = bsz * seq
    xf = x.reshape(n_tok, d)
    logits = (xf @ w_router + b_router).astype(jnp.float32)
    top_logit, top_idx = lax.top_k(logits, TOP_K)
    gates = jax.nn.softmax(top_logit, axis=-1)
    n_assign = n_tok * TOP_K
    expert_of = top_idx.reshape(-1)
    order = jnp.argsort(expert_of)
    sorted_e = expert_of[order]
    sorted_tok = order // TOP_K
    counts = jnp.zeros((N_EXPERTS,), jnp.int32).at[expert_of].add(1)
    padded = (counts + MOE_BLOCK - 1) // MOE_BLOCK * MOE_BLOCK
    start_sorted = jnp.cumsum(counts) - counts
    end_padded = jnp.cumsum(padded)
    start_padded = end_padded - padded
    dest = start_padded[sorted_e] + jnp.arange(n_assign, dtype=jnp.int32) - start_sorted[sorted_e]
    n_blocks = -(-n_assign // MOE_BLOCK) + N_EXPERTS
    cap = n_blocks * MOE_BLOCK
    slot_tok = jnp.zeros((cap,), jnp.int32).at[dest].set(sorted_tok)
    block_start = jnp.arange(n_blocks, dtype=jnp.int32) * MOE_BLOCK
    block_expert = jnp.minimum(jnp.searchsorted(end_padded, block_start, side='right'), N_EXPERTS - 1)
    xin = xf[slot_tok].reshape(n_blocks, MOE_BLOCK, d)

    def expert_block(args):
        xb, e = args
        h_g = jnp.minimum(xb @ w_gate[e] + b_gate[e], SWIGLU_LIMIT)
        h_u = jnp.clip(xb @ w_up[e] + b_up[e], -SWIGLU_LIMIT, SWIGLU_LIMIT)
        h = (h_u + 1.0) * h_g * jax.nn.sigmoid(SWIGLU_ALPHA * h_g)
        return h @ w_down[e] + b_down[e]

    yb = lax.map(expert_block, (xin, block_expert)).reshape(cap, d)
    y_sorted = yb[dest] * gates.reshape(-1)[order][:, None]
    y = jax.ops.segment_sum(y_sorted, sorted_tok, num_segments=n_tok)
    return y.reshape(bsz, seq, d).astype(x.dtype)


def setup_inputs(seed: int = 0) -> dict:
    key = jax.random.key(seed)
    ks = jax.random.split(key, 32)
    f32 = jnp.float32
    L = DEPTH

    def nrm(k, shape, s):
        return jax.random.normal(k, shape, f32) * s

    col_scale = np.ones((D_IN,), np.float32)
    off = np.cumsum((0,) + IN_SPLITS)
    for seg, width in ((2, RET_W), (5, SSD_W), (9, GLA_W)):
        col_scale[off[seg]:off[seg] + width] = DEEPNORM_BETA
    x = nrm(ks[0], (BATCH, SEQ, D_MODEL), 1.0)
    positions = (jax.random.randint(ks[1], (BATCH, 1), 0, 16, dtype=jnp.int32) * CHUNK
                 + jnp.arange(SEQ, dtype=jnp.int32)[None, :])
    w_in = nrm(ks[2], (L, D_MODEL, D_IN), D_MODEL ** -0.5) * jnp.asarray(col_scale)
    w_out = nrm(ks[3], (L, D_MIX, D_MODEL), D_MIX ** -0.5 * DEEPNORM_BETA)
    ret_norm_w = 1.0 + nrm(ks[4], (L, RET_W), 0.02)
    ssd_conv_w = nrm(ks[5], (L, SSD_CONV, SSD_XBC), SSD_CONV ** -0.5)
    ssd_conv_b = nrm(ks[6], (L, SSD_XBC), 0.02)
    dt0 = jnp.exp(jax.random.uniform(ks[7], (L, SSD_HEADS), f32, np.log(1e-3), np.log(1e-1)))
    ssd_dt_bias = dt0 + jnp.log(-jnp.expm1(-dt0))
    ssd_a_log = jnp.log(jax.random.uniform(ks[8], (L, SSD_HEADS), f32, 1.0, 16.0))
    ssd_d = 1.0 + nrm(ks[9], (L, SSD_HEADS), 0.02)
    ssd_norm_w = 1.0 + nrm(ks[10], (L, SSD_W), 0.02)
    gla_w_gk2 = nrm(ks[11], (L, GLA_GATE_RANK, GLA_HEADS * GLA_DK), GLA_GATE_RANK ** -0.5)
    gla_b_gk2 = nrm(ks[12], (L, GLA_HEADS * GLA_DK), 0.02)
    gla_norm_w = 1.0 + nrm(ks[13], (L, GLA_W), 0.02)
    ln1_g = 1.0 + nrm(ks[14], (L, D_MODEL), 0.02)
    ln1_b = nrm(ks[15], (L, D_MODEL), 0.02)
    w_router = nrm(ks[16], (L, D_MODEL, N_EXPERTS), D_MODEL ** -0.5)
    b_router = nrm(ks[17], (L, N_EXPERTS), 0.01)
    w_gate = nrm(ks[18], (L, N_EXPERTS, D_MODEL, D_FF), D_MODEL ** -0.5 * DEEPNORM_BETA)
    b_gate = nrm(ks[19], (L, N_EXPERTS, D_FF), 0.02)
    w_up = nrm(ks[20], (L, N_EXPERTS, D_MODEL, D_FF), D_MODEL ** -0.5 * DEEPNORM_BETA)
    b_up = nrm(ks[21], (L, N_EXPERTS, D_FF), 0.02)
    w_down = nrm(ks[22], (L, N_EXPERTS, D_FF, D_MODEL), D_FF ** -0.5 * DEEPNORM_BETA)
    b_down = nrm(ks[23], (L, N_EXPERTS, D_MODEL), 0.02)
    ln2_g = 1.0 + nrm(ks[24], (L, D_MODEL), 0.02)
    ln2_b = nrm(ks[25], (L, D_MODEL), 0.02)
    return {'x': x, 'positions': positions, 'w_in': w_in, 'w_out': w_out,
            'ret_norm_w': ret_norm_w, 'ssd_conv_w': ssd_conv_w, 'ssd_conv_b': ssd_conv_b,
            'ssd_dt_bias': ssd_dt_bias, 'ssd_a_log': ssd_a_log, 'ssd_d': ssd_d,
            'ssd_norm_w': ssd_norm_w, 'gla_w_gk2': gla_w_gk2, 'gla_b_gk2': gla_b_gk2,
            'gla_norm_w': gla_norm_w, 'ln1_g': ln1_g, 'ln1_b': ln1_b,
            'w_router': w_router, 'b_router': b_router, 'w_gate': w_gate, 'b_gate': b_gate,
            'w_up': w_up, 'b_up': b_up, 'w_down': w_down, 'b_down': b_down,
            'ln2_g': ln2_g, 'ln2_b': ln2_b}


def reference(x, positions, w_in, w_out, ret_norm_w, ssd_conv_w, ssd_conv_b, ssd_dt_bias,
              ssd_a_log, ssd_d, ssd_norm_w, gla_w_gk2, gla_b_gk2, gla_norm_w, ln1_g, ln1_b,
              w_router, b_router, w_gate, b_gate, w_up, b_up, w_down, b_down, ln2_g, ln2_b):
    bounds = []
    acc = 0
    for width in IN_SPLITS[:-1]:
        acc += width
        bounds.append(acc)
    for l in range(DEPTH):
        proj = jnp.einsum('bsd,de->bse', x, w_in[l])
        (r_q, r_k, r_v, r_g, s_z, s_xbc, s_dt,
         g_q, g_k, g_v, g_gk, g_g) = jnp.split(proj, bounds, axis=-1)
        h_ret = _retention(r_q, r_k, r_v, r_g, positions, ret_norm_w[l])
        h_ssd = _ssd(s_z, s_xbc, s_dt, ssd_conv_w[l], ssd_conv_b[l], ssd_dt_bias[l],
                     ssd_a_log[l], ssd_d[l], ssd_norm_w[l])
        h_gla = _gla(g_q, g_k, g_v, g_gk, g_g, gla_w_gk2[l], gla_b_gk2[l], gla_norm_w[l])
        h = jnp.concatenate([h_ret, h_ssd, h_gla], axis=-1).astype(x.dtype)
        mix = jnp.einsum('bse,ed->bsd', h, w_out[l])
        x = _layer_norm(DEEPNORM_ALPHA * x + mix, ln1_g[l], ln1_b[l])
        ffn = _moe(x, w_router[l], b_router[l], w_gate[l], b_gate[l], w_up[l], b_up[l],
                   w_down[l], b_down[l])
        x = _layer_norm(DEEPNORM_ALPHA * x + ffn, ln2_g[l], ln2_b[l])
    return x
```

```python
import functools
import math

import jax
import jax.numpy as jnp
from jax import lax
from jax.experimental import pallas as pl
from jax.experimental.pallas import tpu as pltpu

F32 = jnp.float32
BF16 = jnp.bfloat16

D_MODEL = 1024
CHUNK = 64
RET_HEADS, RET_DK, RET_DV = 4, 64, 64
RET_W = RET_HEADS * RET_DV
SSD_HEADS, SSD_HEAD_DIM, SSD_STATE, SSD_GROUPS, SSD_CONV = 8, 64, 64, 2, 4
SSD_W = SSD_HEADS * SSD_HEAD_DIM
SSD_BC = SSD_GROUPS * SSD_STATE
SSD_XBC = SSD_W + 2 * SSD_BC
GLA_HEADS, GLA_DK, GLA_DV, GLA_RANK, GLA_TEMP = 4, 32, 64, 16, 16.0
GLA_QK = GLA_HEADS * GLA_DK
GLA_W = GLA_HEADS * GLA_DV
D_MIX = RET_W + SSD_W + GLA_W
N_EXPERTS, TOP_K, D_FF = 32, 4, 1024
SWIGLU_LIMIT, SWIGLU_ALPHA = 7.0, 1.702
ROPE_BASE = 10000.0
LN_EPS, NORM_EPS = 1e-5, 1e-6
DEPTH = 2
DEEPNORM_ALPHA = (2.0 * DEPTH) ** 0.25

LANES = 128
NEG_BIG = -1e30
VMEM_LIMIT = 56 * 1024 * 1024

_SEGS = (("rq", 256), ("rk", 256), ("rv", 256), ("rg", 256), ("sz", SSD_W), ("sxbc", SSD_XBC),
         ("sdt", SSD_W), ("gq", 128), ("gk", 128), ("gv", 256), ("ggk", 128), ("gg", 256))
COL = {}
_off = 0
for _n, _w in _SEGS:
    COL[_n] = _off
    _off += _w
NP = _off

TS = 256
TM_PROJ = 512
TM_POST = 256
TM_COMB = 512
BM = 256


def _dot(a, b, dims=(((1,), (0,)), ((), ())), precision=None):
    return lax.dot_general(a, b, dims, precision=precision, preferred_element_type=F32)


_NT = (((1,), (1,)), ((), ()))
_TN = (((0,), (0,)), ((), ()))


def _iota(shape, dim):
    return lax.broadcasted_iota(jnp.int32, shape, dim)


def _vdiv(x, n):
    assert n & (n - 1) == 0
    return lax.shift_right_logical(x, n.bit_length() - 1)


def _vmod(x, n):
    assert n & (n - 1) == 0
    return jnp.bitwise_and(x, n - 1)


def _silu(x):
    return x * jax.nn.sigmoid(x)


def _softplus(x):
    return jnp.maximum(x, 0.0) + jnp.log1p(jnp.exp(-jnp.abs(x)))


def _seg_sum64(x):
    first = _iota((1, LANES), 1) < 64
    outs = []
    for j in range(x.shape[-1] // LANES):
        blk = x[:, j * LANES:(j + 1) * LANES]
        lo = jnp.sum(jnp.where(first, blk, 0.0), axis=-1, keepdims=True)
        hi = jnp.sum(jnp.where(first, 0.0, blk), axis=-1, keepdims=True)
        outs.append(jnp.where(first, lo, hi))
    return jnp.concatenate(outs, axis=-1)


def _block_diag(x, reps, row_blk, col_blk):
    t = jnp.concatenate([x] * reps, axis=0)
    keep = _vdiv(_iota(t.shape, 0), row_blk) == _vdiv(_iota(t.shape, 1), col_blk)
    return jnp.where(keep, t, 0.0).astype(BF16)


def _rope_kernel(pos_ref, cos_ref, sin_ref):
    lane = _iota((1, RET_HEADS * RET_DK), 1)
    half = RET_DK // 2
    k = _vmod(lane, half).astype(F32)
    inv_freq = jnp.exp(k * (-math.log(ROPE_BASE) / half))
    ang = pos_ref[...].astype(F32) * inv_freq
    first = _vmod(lane, RET_DK) < half
    cos_ref[...] = jnp.cos(ang)
    sin_ref[...] = jnp.where(first, -1.0, 1.0) * jnp.sin(ang)


def _rope_tables(pos_col):
    t = pos_col.shape[0]
    tm = 512
    w = RET_HEADS * RET_DK
    return pl.pallas_call(
        _rope_kernel,
        grid=(t // tm,),
        in_specs=[pl.BlockSpec((tm, 1), lambda i: (i, 0))],
        out_specs=[pl.BlockSpec((tm, w), lambda i: (i, 0))] * 2,
        out_shape=[jax.ShapeDtypeStruct((t, w), F32)] * 2,
        compiler_params=pltpu.CompilerParams(dimension_semantics=("arbitrary",)),
        name="rope_tables",
    )(pos_col)


def _inproj_kernel(x_ref, w_ref, o_ref):
    o_ref[...] = _dot(x_ref[...].astype(BF16), w_ref[...])


def _inproj(x2, w_p):
    t = x2.shape[0]
    return pl.pallas_call(
        _inproj_kernel,
        grid=(t // TM_PROJ,),
        in_specs=[pl.BlockSpec((TM_PROJ, D_MODEL), lambda i: (i, 0)),
                  pl.BlockSpec((D_MODEL, NP), lambda i: (0, 0))],
        out_specs=pl.BlockSpec((TM_PROJ, NP), lambda i: (i, 0)),
        out_shape=jax.ShapeDtypeStruct((t, NP), F32),
        compiler_params=pltpu.CompilerParams(dimension_semantics=("arbitrary",),
                                             vmem_limit_bytes=VMEM_LIMIT),
        name="inproj",
    )(x2, w_p)


def _mixer_kernel(proj_ref, cos_ref, sin_ref, retw_ref, convw_ref, convb_ref, dtb_ref, alog_ref,
                  dskip_ref, ssdw_ref, wgk_ref, bgk_ref, glaw_ref, h_ref,
                  ret_s, ssd_s, gla_s, stage, xact):
    C = CHUNK

    @pl.when(pl.program_id(1) == 0)
    def _():
        ret_s[...] = jnp.zeros_like(ret_s)
        ssd_s[...] = jnp.zeros_like(ssd_s)
        gla_s[...] = jnp.zeros_like(gla_s)
        stage[0:8, :] = jnp.zeros((8, SSD_XBC), F32)

    stage[8:8 + TS, :] = proj_ref[:, COL["sxbc"]:COL["sxbc"] + SSD_XBC]
    acc = convb_ref[...] + convw_ref[0:1, :] * stage[5:5 + TS, :]
    for j in range(1, SSD_CONV):
        acc = acc + convw_ref[j:j + 1, :] * stage[5 + j:5 + j + TS, :]
    xact[...] = _silu(acc)
    stage[0:8, :] = stage[TS:TS + 8, :]

    lane256 = _iota((1, 256), 1)
    head = _vdiv(lane256, 64).astype(F32)
    log_gamma = jnp.log(1.0 - jnp.exp((-5.0 - head) * math.log(2.0)))
    row = _iota((C, 1), 0).astype(F32)
    dist = row - _vmod(lane256, 64).astype(F32)
    ret_intra = jnp.where(dist >= 0, jnp.exp(log_gamma * jnp.maximum(dist, 0.0)), 0.0)
    ret_qdec = jnp.exp(log_gamma * (row + 1.0))
    ret_kdec = jnp.exp(log_gamma * (C - 1.0 - row))
    ret_cdec = jnp.exp(log_gamma * C)
    first_half = _vmod(lane256, RET_DK) < (RET_DK // 2)

    tri = (_iota((C, C), 0) >= _iota((C, C), 1)).astype(F32)
    causal4 = _iota((C, 256), 0) >= _vmod(_iota((C, 256), 1), 64)
    causal8 = _iota((C, 512), 0) >= _vmod(_iota((C, 512), 1), 64)
    eye8 = _iota((C, 512), 0) == _vmod(_iota((C, 512), 1), 64)

    a_neg = -jnp.exp(alog_ref[...])

    def rot(t, cos, sin):
        sw = jnp.where(first_half, pltpu.roll(t, 256 - 32, 1), pltpu.roll(t, 32, 1))
        return t * cos + sw * sin

    def chunk(c, carry):
        r0 = pl.multiple_of(c * C, C)

        def seg(name, width):
            return proj_ref[pl.ds(r0, C), COL[name]:COL[name] + width]

        cos = cos_ref[pl.ds(r0, C), :]
        sin = sin_ref[pl.ds(r0, C), :]
        q = rot(seg("rq", 256), cos, sin)
        k = rot(seg("rk", 256), cos, sin) * (RET_DK ** -0.5)
        v = seg("rv", 256)
        vb = v.astype(BF16)
        kbd = _block_diag(k, RET_HEADS, C, RET_DK)
        scores = _dot(q.astype(BF16), kbd, _NT) * ret_intra
        vbd = _block_diag(v, RET_HEADS, C, RET_DV)
        s_prev = ret_s[...]
        o = _dot(scores.astype(BF16), vbd) + _dot((q * ret_qdec).astype(BF16), s_prev.astype(BF16))
        contrib = _dot((k * ret_kdec).astype(BF16), vb, _TN)
        keep = _vdiv(_iota((256, 256), 0), RET_DK) == _vdiv(_iota((256, 256), 1), RET_DV)
        ret_s[...] = jnp.where(keep, ret_cdec * s_prev + contrib, 0.0)
        mu = _seg_sum64(o) * (1.0 / RET_DV)
        oc = o - mu
        var = _seg_sum64(oc * oc) * (1.0 / RET_DV)
        o = oc * lax.rsqrt(var + LN_EPS) * retw_ref[...]
        h_ref[pl.ds(r0, C), 0:RET_W] = (_silu(seg("rg", 256)) * o).astype(BF16)

        xs = xact[pl.ds(r0, C), 0:SSD_W]
        bm = xact[pl.ds(r0, C), SSD_W:SSD_W + SSD_BC]
        cm = xact[pl.ds(r0, C), SSD_W + SSD_BC:SSD_XBC]
        cmb = cm.astype(BF16)
        dt = _softplus(seg("sdt", SSD_W) + dtb_ref[...])
        acum = _dot(tri, dt * a_neg, precision=lax.Precision.HIGHEST)
        arow = jnp.sum(jnp.where(eye8, acum, 0.0), axis=0, keepdims=True)
        decay = jnp.exp(jnp.where(causal8, acum - arow, NEG_BIG))
        b8 = jnp.concatenate([bm] * SSD_HEADS, axis=0)
        keep_b = _vdiv(_iota(b8.shape, 0), C * SSD_HEADS // SSD_GROUPS) == _vdiv(_iota(b8.shape, 1), SSD_STATE)
        b8 = jnp.where(keep_b, b8, 0.0).astype(BF16)
        cb = _dot(cmb, b8, _NT)
        m = (cb * decay).astype(BF16)
        xdt = xs * dt
        s2 = ssd_s[...]
        half = SSD_W // SSD_GROUPS
        ys = []
        for g in range(SSD_GROUPS):
            xbd = _block_diag(xdt[:, g * half:(g + 1) * half], SSD_HEADS // SSD_GROUPS, C, SSD_HEAD_DIM)
            ys.append(_dot(m[:, g * half:(g + 1) * half], xbd))
        y = jnp.concatenate(ys, axis=-1)
        y = y + _dot(cmb, s2.astype(BF16)) * jnp.exp(acum)
        y = y + dskip_ref[...] * xs
        a_last = acum[C - 1:C, :]
        sd = jnp.exp(a_last - acum)
        contrib_s = _dot(bm.astype(BF16), (xdt * sd).astype(BF16), _TN)
        keep_s = _vdiv(_iota(s2.shape, 0), SSD_STATE) == _vdiv(_iota(s2.shape, 1), half)
        ssd_s[...] = jnp.where(keep_s, s2 * jnp.exp(a_last) + contrib_s, 0.0)
        yz = y * _silu(seg("sz", SSD_W))
        outs = []
        for g in range(SSD_GROUPS):
            blk = yz[:, g * half:(g + 1) * half]
            ms = jnp.mean(blk * blk, axis=-1, keepdims=True)
            outs.append(blk * lax.rsqrt(ms + NORM_EPS))
        h_ref[pl.ds(r0, C), RET_W:RET_W + SSD_W] = (jnp.concatenate(outs, axis=-1) * ssdw_ref[...]).astype(BF16)

        gq = seg("gq", GLA_QK) * (GLA_DK ** -0.5)
        gkk = seg("gk", GLA_QK)
        gv = seg("gv", GLA_W)
        gkl = _dot(seg("ggk", 128).astype(BF16), wgk_ref[...].astype(BF16)) + bgk_ref[...]
        log_a = (jnp.minimum(gkl, 0.0) - jnp.log1p(jnp.exp(-jnp.abs(gkl)))) * (1.0 / GLA_TEMP)
        b = _dot(tri, log_a, precision=lax.Precision.HIGHEST)
        q_t = (gq * jnp.exp(b)).astype(BF16)
        k_t = gkk * jnp.exp(-b)
        kbd_g = _block_diag(k_t, GLA_HEADS, C, GLA_DK)
        att = jnp.where(causal4, _dot(q_t, kbd_g, _NT), 0.0)
        vbd_g = _block_diag(gv, GLA_HEADS, C, GLA_DV)
        st = gla_s[...]
        og = _dot(att.astype(BF16), vbd_g) + _dot(q_t, st.astype(BF16), _NT)
        b_last = b[C - 1:C, :]
        kd = (gkk * jnp.exp(b_last - b)).astype(BF16)
        contrib_g = _dot(gv.astype(BF16), kd, _TN)
        keep_g = _vdiv(_iota(st.shape, 0), GLA_DV) == _vdiv(_iota(st.shape, 1), GLA_DK)
        gla_s[...] = jnp.where(keep_g, st * jnp.exp(b_last) + contrib_g, 0.0)
        ms = _seg_sum64(og * og) * (1.0 / GLA_DV)
        og = og * lax.rsqrt(ms + NORM_EPS) * glaw_ref[...]
        h_ref[pl.ds(r0, C), RET_W + SSD_W:D_MIX] = (_silu(seg("gg", GLA_W)) * og).astype(BF16)
        return carry

    lax.fori_loop(0, TS // C, chunk, 0)


def _mixer(proj, cos_t, sin_t, params, batch, seq):
    n_s = seq // TS
    row_map = lambda b, s: (b * n_s + s, 0)
    const = lambda b, s: (0, 0)
    specs = [pl.BlockSpec((TS, NP), row_map),
             pl.BlockSpec((TS, 256), row_map),
             pl.BlockSpec((TS, 256), row_map)]
    specs += [pl.BlockSpec(p.shape, const) for p in params]
    return pl.pallas_call(
        _mixer_kernel,
        grid=(batch, n_s),
        in_specs=specs,
        out_specs=pl.BlockSpec((TS, D_MIX), row_map),
        out_shape=jax.ShapeDtypeStruct((batch * seq, D_MIX), BF16),
        scratch_shapes=[pltpu.VMEM((256, 256), F32),
                        pltpu.VMEM((SSD_BC, SSD_W), F32),
                        pltpu.VMEM((GLA_W, GLA_QK), F32),
                        pltpu.VMEM((TS + 8, SSD_XBC), F32),
                        pltpu.VMEM((TS, SSD_XBC), F32)],
        compiler_params=pltpu.CompilerParams(dimension_semantics=("arbitrary", "arbitrary"),
                                             vmem_limit_bytes=VMEM_LIMIT),
        name="mixer",
    )(proj, cos_t, sin_t, *params)


def _layer_norm(y, g, b):
    mu = jnp.mean(y, axis=-1, keepdims=True)
    yc = y - mu
    var = jnp.mean(yc * yc, axis=-1, keepdims=True)
    return yc * lax.rsqrt(var + LN_EPS) * g + b


def _post_kernel(h_ref, x_ref, wout_ref, g_ref, b_ref, wr_ref, br_ref,
                 x1_ref, x1b_ref, mi_ref, mf_ref, cnt_ref, carry):
    tm = TM_POST

    @pl.when(pl.program_id(0) == 0)
    def _():
        carry[...] = jnp.zeros_like(carry)

    mix = _dot(h_ref[...], wout_ref[...])
    x1 = _layer_norm(DEEPNORM_ALPHA * x_ref[...] + mix, g_ref[...], b_ref[...])
    x1_ref[...] = x1
    x1b_ref[...] = x1.astype(BF16)

    logits = _dot(x1, wr_ref[...], precision=lax.Precision.HIGHEST) + br_ref[...]
    lane_i = _iota((tm, LANES), 1)
    lane = lane_i.astype(F32)
    work = logits
    vals, idxs = [], []
    multi = jnp.zeros((tm, LANES), F32)
    for _ in range(TOP_K):
        m = jnp.max(work, axis=-1, keepdims=True)
        idx = jnp.min(jnp.where(work == m, lane, float(LANES)), axis=-1, keepdims=True)
        hit = lane == idx
        multi = multi + hit.astype(F32)
        work = jnp.where(hit, -jnp.inf, work)
        vals.append(m)
        idxs.append(idx)
    exps = [jnp.exp(v - vals[0]) for v in vals]
    denom = exps[0] + exps[1] + exps[2] + exps[3]
    gates = [e / denom for e in exps]

    before = (_iota((tm, tm), 0) > _iota((tm, tm), 1)).astype(BF16)
    prior = _dot(before, multi.astype(BF16)) + carry[...]
    mi = jnp.zeros((tm, LANES), jnp.int32)
    mf = jnp.zeros((tm, LANES), F32)
    for kk in range(TOP_K):
        rank = jnp.sum(jnp.where(lane == idxs[kk], prior, 0.0), axis=-1, keepdims=True)
        mi = jnp.where(lane_i == kk, idxs[kk].astype(jnp.int32), mi)
        mi = jnp.where(lane_i == TOP_K + kk, rank.astype(jnp.int32), mi)
        mf = jnp.where(lane_i == kk, gates[kk], mf)
    mi_ref[...] = mi
    mf_ref[...] = mf
    carry[...] = carry[...] + jnp.sum(multi, axis=0, keepdims=True)
    cnt_ref[...] = jnp.broadcast_to(carry[...], cnt_ref.shape)


def _post(h, x2, w_out_b, ln_g, ln_b, wr_p, br_p):
    t = x2.shape[0]
    tm = TM_POST
    row = lambda i: (i, 0)
    const = lambda i: (0, 0)
    return pl.pallas_call(
        _post_kernel,
        grid=(t // tm,),
        in_specs=[pl.BlockSpec((tm, D_MIX), row), pl.BlockSpec((tm, D_MODEL), row),
                  pl.BlockSpec((D_MIX, D_MODEL), const), pl.BlockSpec((1, D_MODEL), const),
                  pl.BlockSpec((1, D_MODEL), const), pl.BlockSpec((D_MODEL, LANES), const),
                  pl.BlockSpec((1, LANES), const)],
        out_specs=[pl.BlockSpec((tm, D_MODEL), row), pl.BlockSpec((tm, D_MODEL), row),
                   pl.BlockSpec((tm, LANES), row), pl.BlockSpec((tm, LANES), row),
                   pl.BlockSpec((8, LANES), const)],
        out_shape=[jax.ShapeDtypeStruct((t, D_MODEL), F32), jax.ShapeDtypeStruct((t, D_MODEL), BF16),
                   jax.ShapeDtypeStruct((t, LANES), jnp.int32), jax.ShapeDtypeStruct((t, LANES), F32),
                   jax.ShapeDtypeStruct((8, LANES), F32)],
        scratch_shapes=[pltpu.VMEM((1, LANES), F32)],
        compiler_params=pltpu.CompilerParams(dimension_semantics=("arbitrary",),
                                             vmem_limit_bytes=VMEM_LIMIT),
        name="outproj_ln_router",
    )(h, x2, w_out_b, ln_g, ln_b, wr_p, br_p)


def _ffn_kernel(be_ref, nv_ref, x_ref, wg_ref, bg_ref, wu_ref, bu_ref, wd_ref, bd_ref, o_ref,
                wg_b, wu_b, wd_b):
    i = pl.program_id(0)
    valid = i < nv_ref[0]
    e = be_ref[i]
    prev = be_ref[jnp.maximum(i - 1, 0)]
    fresh = jnp.logical_or(i == 0, e != prev)

    @pl.when(jnp.logical_and(valid, fresh))
    def _():
        wg_b[...] = wg_ref[0].astype(BF16)
        wu_b[...] = wu_ref[0].astype(BF16)
        wd_b[...] = wd_ref[0].astype(BF16)

    @pl.when(valid)
    def _():
        x = x_ref[...]
        hg = jnp.minimum(_dot(x, wg_b[...]) + bg_ref[0], SWIGLU_LIMIT)
        hu = jnp.clip(_dot(x, wu_b[...]) + bu_ref[0], -SWIGLU_LIMIT, SWIGLU_LIMIT)
        hh = (hu + 1.0) * hg * jax.nn.sigmoid(SWIGLU_ALPHA * hg)
        o_ref[...] = _dot(hh.astype(BF16), wd_b[...]) + bd_ref[0]


def _ffn(block_expert, n_valid, xin, wg, bg, wu, bu, wd, bd):
    cap = xin.shape[0]
    nb = cap // BM

    def blk(i, be, nv):
        return jnp.minimum(i, nv[0] - 1)

    row = lambda i, be, nv: (blk(i, be, nv), 0)
    wmap = lambda i, be, nv: (be[blk(i, be, nv)], 0, 0)
    grid_spec = pltpu.PrefetchScalarGridSpec(
        num_scalar_prefetch=2,
        grid=(nb,),
        in_specs=[pl.BlockSpec((BM, D_MODEL), row),
                  pl.BlockSpec((1, D_MODEL, D_FF), wmap), pl.BlockSpec((1, 1, D_FF), wmap),
                  pl.BlockSpec((1, D_MODEL, D_FF), wmap), pl.BlockSpec((1, 1, D_FF), wmap),
                  pl.BlockSpec((1, D_FF, D_MODEL), wmap), pl.BlockSpec((1, 1, D_MODEL), wmap)],
        out_specs=pl.BlockSpec((BM, D_MODEL), row),
        scratch_shapes=[pltpu.VMEM((D_MODEL, D_FF), BF16), pltpu.VMEM((D_MODEL, D_FF), BF16),
                        pltpu.VMEM((D_FF, D_MODEL), BF16)],
    )
    return pl.pallas_call(
        _ffn_kernel,
        grid_spec=grid_spec,
        out_shape=jax.ShapeDtypeStruct((cap, D_MODEL), F32),
        compiler_params=pltpu.CompilerParams(dimension_semantics=("arbitrary",),
                                             vmem_limit_bytes=VMEM_LIMIT),
        name="expert_ffn",
    )(block_expert, n_valid, xin, wg, bg, wu, bu, wd, bd)


def _combine_kernel(x_ref, rows_ref, mf_ref, g_ref, b_ref, o_ref):
    mf = mf_ref[...]
    y = DEEPNORM_ALPHA * x_ref[...]
    for kk in range(TOP_K):
        y = y + mf[:, kk:kk + 1] * rows_ref[kk]
    o_ref[...] = _layer_norm(y, g_ref[...], b_ref[...])


def _combine(x1, rows, mf, ln_g, ln_b):
    t = x1.shape[0]
    tm = TM_COMB
    row = lambda i: (i, 0)
    const = lambda i: (0, 0)
    return pl.pallas_call(
        _combine_kernel,
        grid=(t // tm,),
        in_specs=[pl.BlockSpec((tm, D_MODEL), row),
                  pl.BlockSpec((TOP_K, tm, D_MODEL), lambda i: (0, i, 0)),
                  pl.BlockSpec((tm, LANES), row),
                  pl.BlockSpec((1, D_MODEL), const), pl.BlockSpec((1, D_MODEL), const)],
        out_specs=pl.BlockSpec((tm, D_MODEL), row),
        out_shape=jax.ShapeDtypeStruct((t, D_MODEL), F32),
        compiler_params=pltpu.CompilerParams(dimension_semantics=("arbitrary",),
                                             vmem_limit_bytes=VMEM_LIMIT),
        name="combine_ln",
    )(x1, rows, mf, ln_g, ln_b)


def _relayout_w_in(w):
    widths = (256, 256, 256, 256, SSD_W, SSD_XBC, SSD_HEADS, GLA_QK, GLA_QK, GLA_W, GLA_RANK, GLA_W)
    offs = [0]
    for wd in widths:
        offs.append(offs[-1] + wd)
    parts = [w[:, offs[i]:offs[i + 1]] for i in range(len(widths))]
    parts[6] = jnp.repeat(parts[6], SSD_HEAD_DIM, axis=1)
    parts[10] = jnp.pad(parts[10], ((0, 0), (0, LANES - GLA_RANK)))
    return jnp.concatenate(parts, axis=1).astype(BF16)


def _rep_heads(p):
    return jnp.repeat(p, SSD_HEAD_DIM)[None, :]


def kernel(x, positions, w_in, w_out, ret_norm_w, ssd_conv_w, ssd_conv_b, ssd_dt_bias, ssd_a_log, ssd_d,
           ssd_norm_w, gla_w_gk2, gla_b_gk2, gla_norm_w, ln1_g, ln1_b, w_router, b_router, w_gate, b_gate,
           w_up, b_up, w_down, b_down, ln2_g, ln2_b):
    batch, seq, d = x.shape
    t = batch * seq
    depth = w_in.shape[0]
    assert d == D_MODEL and seq % TS == 0 and t % TM_PROJ == 0 and t % TM_COMB == 0
    n_assign = t * TOP_K
    nb = n_assign // BM + N_EXPERTS
    cap = nb * BM

    cos_t, sin_t = _rope_tables(positions.reshape(t, 1))
    x2 = x.reshape(t, d)
    tok = jnp.arange(t, dtype=jnp.int32)

    for l in range(depth):
        proj = _inproj(x2, _relayout_w_in(w_in[l]))
        params = (ret_norm_w[l][None, :], ssd_conv_w[l], ssd_conv_b[l][None, :], _rep_heads(ssd_dt_bias[l]),
                  _rep_heads(ssd_a_log[l]), _rep_heads(ssd_d[l]), ssd_norm_w[l][None, :],
                  jnp.pad(gla_w_gk2[l], ((0, LANES - GLA_RANK), (0, 0))), gla_b_gk2[l][None, :],
                  gla_norm_w[l][None, :])
        h = _mixer(proj, cos_t, sin_t, params, batch, seq)

        wr_p = jnp.pad(w_router[l], ((0, 0), (0, LANES - N_EXPERTS)))
        br_p = jnp.pad(b_router[l], (0, LANES - N_EXPERTS), constant_values=NEG_BIG)[None, :]
        x1, x1b, mi, mf, cnt = _post(h, x2, w_out[l].astype(BF16), ln1_g[l][None, :], ln1_b[l][None, :],
                                     wr_p, br_p)

        counts = cnt[0, :N_EXPERTS].astype(jnp.int32)
        padded = (counts + BM - 1) // BM * BM
        end_padded = jnp.cumsum(padded)
        start_padded = end_padded - padded
        top_idx = mi[:, :TOP_K]
        dest = start_padded[top_idx] + mi[:, TOP_K:2 * TOP_K]
        block_start = jnp.arange(nb, dtype=jnp.int32) * BM
        block_expert = jnp.minimum(jnp.searchsorted(end_padded, block_start, side="right"),
                                   N_EXPERTS - 1).astype(jnp.int32)
        n_valid = (end_padded[-1:] // BM).astype(jnp.int32)

        slot_tok = jnp.zeros((cap,), jnp.int32).at[dest.reshape(-1)].set(jnp.repeat(tok, TOP_K))
        xin = x1b[slot_tok]
        yb = _ffn(block_expert, n_valid, xin, w_gate[l], b_gate[l][:, None, :], w_up[l], b_up[l][:, None, :],
                  w_down[l], b_down[l][:, None, :])
        rows = yb[dest.T]
        x2 = _combine(x1, rows, mf, ln2_g[l][None, :], ln2_b[l][None, :])
    return x2.reshape(batch, seq, d)
```

```python
import functools
import math

import jax
import jax.numpy as jnp
from jax import lax
from jax.experimental import pallas as pl
from jax.experimental.pallas import tpu as pltpu

F32 = jnp.float32
BF16 = jnp.bfloat16

D_MODEL = 1024
CHUNK = 64
RET_HEADS, RET_DK, RET_DV = 4, 64, 64
RET_W = RET_HEADS * RET_DV
SSD_HEADS, SSD_HEAD_DIM, SSD_STATE, SSD_GROUPS, SSD_CONV = 8, 64, 64, 2, 4
SSD_W = SSD_HEADS * SSD_HEAD_DIM
SSD_BC = SSD_GROUPS * SSD_STATE
SSD_XBC = SSD_W + 2 * SSD_BC
GLA_HEADS, GLA_DK, GLA_DV, GLA_RANK, GLA_TEMP = 4, 32, 64, 16, 16.0
GLA_QK = GLA_HEADS * GLA_DK
GLA_W = GLA_HEADS * GLA_DV
D_MIX = RET_W + SSD_W + GLA_W
N_EXPERTS, TOP_K, D_FF = 32, 4, 1024
SWIGLU_LIMIT, SWIGLU_ALPHA = 7.0, 1.702
ROPE_BASE = 10000.0
LN_EPS, NORM_EPS = 1e-5, 1e-6
DEPTH = 2
DEEPNORM_ALPHA = (2.0 * DEPTH) ** 0.25

LANES = 128
NEG_BIG = -1e30
VMEM_LIMIT = 56 * 1024 * 1024

_SEGS = (("rq", 256), ("rk", 256), ("rv", 256), ("rg", 256), ("sz", SSD_W), ("sxbc", SSD_XBC),
         ("sdt", SSD_W), ("gq", 128), ("gk", 128), ("gv", 256), ("ggk", 128), ("gg", 256))
COL = {}
_off = 0
for _n, _w in _SEGS:
    COL[_n] = _off
    _off += _w
NP = _off

TS = 256
TM_PROJ = 512
TM_POST = 256
TM_COMB = 512
BM = 256


def _dot(a, b, dims=(((1,), (0,)), ((), ())), precision=None):
    return lax.dot_general(a, b, dims, precision=precision, preferred_element_type=F32)


_NT = (((1,), (1,)), ((), ()))
_TN = (((0,), (0,)), ((), ()))


def _iota(shape, dim):
    return lax.broadcasted_iota(jnp.int32, shape, dim)


def _vdiv(x, n):
    assert n & (n - 1) == 0
    return lax.shift_right_logical(x, n.bit_length() - 1)


def _vmod(x, n):
    assert n & (n - 1) == 0
    return jnp.bitwise_and(x, n - 1)


def _silu(x):
    return x * jax.nn.sigmoid(x)


def _softplus(x):
    return jnp.maximum(x, 0.0) + jnp.log1p(jnp.exp(-jnp.abs(x)))


def _seg_sum64(x):
    first = _iota((1, LANES), 1) < 64
    outs = []
    for j in range(x.shape[-1] // LANES):
        blk = x[:, j * LANES:(j + 1) * LANES]
        lo = jnp.sum(jnp.where(first, blk, 0.0), axis=-1, keepdims=True)
        hi = jnp.sum(jnp.where(first, 0.0, blk), axis=-1, keepdims=True)
        outs.append(jnp.where(first, lo, hi))
    return jnp.concatenate(outs, axis=-1)


def _block_diag(x, reps, row_blk, col_blk):
    t = jnp.concatenate([x] * reps, axis=0)
    keep = _vdiv(_iota(t.shape, 0), row_blk) == _vdiv(_iota(t.shape, 1), col_blk)
    return jnp.where(keep, t, 0.0).astype(BF16)


def _rope_kernel(pos_ref, cos_ref, sin_ref):
    lane = _iota((1, RET_HEADS * RET_DK), 1)
    half = RET_DK // 2
    k = _vmod(lane, half).astype(F32)
    inv_freq = jnp.exp(k * (-math.log(ROPE_BASE) / half))
    ang = pos_ref[...].astype(F32) * inv_freq
    first = _vmod(lane, RET_DK) < half
    cos_ref[...] = jnp.cos(ang)
    sin_ref[...] = jnp.where(first, -1.0, 1.0) * jnp.sin(ang)


def _rope_tables(pos_col):
    t = pos_col.shape[0]
    tm = 512
    w = RET_HEADS * RET_DK
    return pl.pallas_call(
        _rope_kernel,
        grid=(t // tm,),
        in_specs=[pl.BlockSpec((tm, 1), lambda i: (i, 0))],
        out_specs=[pl.BlockSpec((tm, w), lambda i: (i, 0))] * 2,
        out_shape=[jax.ShapeDtypeStruct((t, w), F32)] * 2,
        compiler_params=pltpu.CompilerParams(dimension_semantics=("arbitrary",)),
        name="rope_tables",
    )(pos_col)


def _inproj_kernel(x_ref, w_ref, o_ref):
    o_ref[...] = _dot(x_ref[...].astype(BF16), w_ref[...])


def _inproj(x2, w_p):
    t = x2.shape[0]
    return pl.pallas_call(
        _inproj_kernel,
        grid=(t // TM_PROJ,),
        in_specs=[pl.BlockSpec((TM_PROJ, D_MODEL), lambda i: (i, 0)),
                  pl.BlockSpec((D_MODEL, NP), lambda i: (0, 0))],
        out_specs=pl.BlockSpec((TM_PROJ, NP), lambda i: (i, 0)),
        out_shape=jax.ShapeDtypeStruct((t, NP), F32),
        compiler_params=pltpu.CompilerParams(dimension_semantics=("arbitrary",),
                                             vmem_limit_bytes=VMEM_LIMIT),
        name="inproj",
    )(x2, w_p)


def _mixer_kernel(proj_ref, cos_ref, sin_ref, retw_ref, convw_ref, convb_ref, dtb_ref, alog_ref,
                  dskip_ref, ssdw_ref, wgk_ref, bgk_ref, glaw_ref, h_ref,
                  ret_s, ssd_s, gla_s, stage, xact):
    C = CHUNK

    @pl.when(pl.program_id(1) == 0)
    def _():
        ret_s[...] = jnp.zeros_like(ret_s)
        ssd_s[...] = jnp.zeros_like(ssd_s)
        gla_s[...] = jnp.zeros_like(gla_s)
        stage[0:8, :] = jnp.zeros((8, SSD_XBC), F32)

    stage[8:8 + TS, :] = proj_ref[:, COL["sxbc"]:COL["sxbc"] + SSD_XBC]
    acc = convb_ref[...] + convw_ref[0:1, :] * stage[5:5 + TS, :]
    for j in range(1, SSD_CONV):
        acc = acc + convw_ref[j:j + 1, :] * stage[5 + j:5 + j + TS, :]
    xact[...] = _silu(acc)
    stage[0:8, :] = stage[TS:TS + 8, :]

    lane256 = _iota((1, 256), 1)
    head = _vdiv(lane256, 64).astype(F32)
    log_gamma = jnp.log(1.0 - jnp.exp((-5.0 - head) * math.log(2.0)))
    row = _iota((C, 1), 0).astype(F32)
    dist = row - _vmod(lane256, 64).astype(F32)
    ret_intra = jnp.where(dist >= 0, jnp.exp(log_gamma * jnp.maximum(dist, 0.0)), 0.0)
    ret_qdec = jnp.exp(log_gamma * (row + 1.0))
    ret_kdec = jnp.exp(log_gamma * (C - 1.0 - row))
    ret_cdec = jnp.exp(log_gamma * C)
    first_half = _vmod(lane256, RET_DK) < (RET_DK // 2)

    tri = (_iota((C, C), 0) >= _iota((C, C), 1)).astype(F32)
    causal4 = _iota((C, 256), 0) >= _vmod(_iota((C, 256), 1), 64)
    causal8 = _iota((C, 512), 0) >= _vmod(_iota((C, 512), 1), 64)
    eye8 = _iota((C, 512), 0) == _vmod(_iota((C, 512), 1), 64)

    a_neg = -jnp.exp(alog_ref[...])

    def rot(t, cos, sin):
        sw = jnp.where(first_half, pltpu.roll(t, 256 - 32, 1), pltpu.roll(t, 32, 1))
        return t * cos + sw * sin

    def chunk(c, carry):
        r0 = pl.multiple_of(c * C, C)

        def seg(name, width):
            return proj_ref[pl.ds(r0, C), COL[name]:COL[name] + width]

        cos = cos_ref[pl.ds(r0, C), :]
        sin = sin_ref[pl.ds(r0, C), :]
        q = rot(seg("rq", 256), cos, sin)
        k = rot(seg("rk", 256), cos, sin) * (RET_DK ** -0.5)
        v = seg("rv", 256)
        vb = v.astype(BF16)
        kbd = _block_diag(k, RET_HEADS, C, RET_DK)
        scores = _dot(q.astype(BF16), kbd, _NT) * ret_intra
        vbd = _block_diag(v, RET_HEADS, C, RET_DV)
        s_prev = ret_s[...]
        o = _dot(scores.astype(BF16), vbd) + _dot((q * ret_qdec).astype(BF16), s_prev.astype(BF16))
        contrib = _dot((k * ret_kdec).astype(BF16), vb, _TN)
        keep = _vdiv(_iota((256, 256), 0), RET_DK) == _vdiv(_iota((256, 256), 1), RET_DV)
        ret_s[...] = jnp.where(keep, ret_cdec * s_prev + contrib, 0.0)
        mu = _seg_sum64(o) * (1.0 / RET_DV)
        oc = o - mu
        var = _seg_sum64(oc * oc) * (1.0 / RET_DV)
        o = oc * lax.rsqrt(var + LN_EPS) * retw_ref[...]
        h_ref[pl.ds(r0, C), 0:RET_W] = (_silu(seg("rg", 256)) * o).astype(BF16)

        xs = xact[pl.ds(r0, C), 0:SSD_W]
        bm = xact[pl.ds(r0, C), SSD_W:SSD_W + SSD_BC]
        cm = xact[pl.ds(r0, C), SSD_W + SSD_BC:SSD_XBC]
        cmb = cm.astype(BF16)
        dt = _softplus(seg("sdt", SSD_W) + dtb_ref[...])
        acum = _dot(tri, dt * a_neg, precision=lax.Precision.HIGHEST)
        arow = jnp.sum(jnp.where(eye8, acum, 0.0), axis=0, keepdims=True)
        decay = jnp.exp(jnp.where(causal8, acum - arow, NEG_BIG))
        b8 = jnp.concatenate([bm] * SSD_HEADS, axis=0)
        keep_b = _vdiv(_iota(b8.shape, 0), C * SSD_HEADS // SSD_GROUPS) == _vdiv(_iota(b8.shape, 1), SSD_STATE)
        b8 = jnp.where(keep_b, b8, 0.0).astype(BF16)
        cb = _dot(cmb, b8, _NT)
        m = (cb * decay).astype(BF16)
        xdt = xs * dt
        s2 = ssd_s[...]
        half = SSD_W // SSD_GROUPS
        ys = []
        for g in range(SSD_GROUPS):
            xbd = _block_diag(xdt[:, g * half:(g + 1) * half], SSD_HEADS // SSD_GROUPS, C, SSD_HEAD_DIM)
            ys.append(_dot(m[:, g * half:(g + 1) * half], xbd))
        y = jnp.concatenate(ys, axis=-1)
        y = y + _dot(cmb, s2.astype(BF16)) * jnp.exp(acum)
        y = y + dskip_ref[...] * xs
        a_last = acum[C - 1:C, :]
        sd = jnp.exp(a_last - acum)
        contrib_s = _dot(bm.astype(BF16), (xdt * sd).astype(BF16), _TN)
        keep_s = _vdiv(_iota(s2.shape, 0), SSD_STATE) == _vdiv(_iota(s2.shape, 1), half)
        ssd_s[...] = jnp.where(keep_s, s2 * jnp.exp(a_last) + contrib_s, 0.0)
        yz = y * _silu(seg("sz", SSD_W))
        outs = []
        for g in range(SSD_GROUPS):
            blk = yz[:, g * half:(g + 1) * half]
            ms = jnp.mean(blk * blk, axis=-1, keepdims=True)
            outs.append(blk * lax.rsqrt(ms + NORM_EPS))
        h_ref[pl.ds(r0, C), RET_W:RET_W + SSD_W] = (jnp.concatenate(outs, axis=-1) * ssdw_ref[...]).astype(BF16)

        gq = seg("gq", GLA_QK) * (GLA_DK ** -0.5)
        gkk = seg("gk", GLA_QK)
        gv = seg("gv", GLA_W)
        gkl = _dot(seg("ggk", 128).astype(BF16), wgk_ref[...].astype(BF16)) + bgk_ref[...]
        log_a = (jnp.minimum(gkl, 0.0) - jnp.log1p(jnp.exp(-jnp.abs(gkl)))) * (1.0 / GLA_TEMP)
        b = _dot(tri, log_a, precision=lax.Precision.HIGHEST)
        q_t = (gq * jnp.exp(b)).astype(BF16)
        k_t = gkk * jnp.exp(-b)
        kbd_g = _block_diag(k_t, GLA_HEADS, C, GLA_DK)
        att = jnp.where(causal4, _dot(q_t, kbd_g, _NT), 0.0)
        vbd_g = _block_diag(gv, GLA_HEADS, C, GLA_DV)
        st = gla_s[...]
        og = _dot(att.astype(BF16), vbd_g) + _dot(q_t, st.astype(BF16), _NT)
        b_last = b[C - 1:C, :]
        kd = (gkk * jnp.exp(b_last - b)).astype(BF16)
        contrib_g = _dot(gv.astype(BF16), kd, _TN)
        keep_g = _vdiv(_iota(st.shape, 0), GLA_DV) == _vdiv(_iota(st.shape, 1), GLA_DK)
        gla_s[...] = jnp.where(keep_g, st * jnp.exp(b_last) + contrib_g, 0.0)
        ms = _seg_sum64(og * og) * (1.0 / GLA_DV)
        og = og * lax.rsqrt(ms + NORM_EPS) * glaw_ref[...]
        h_ref[pl.ds(r0, C), RET_W + SSD_W:D_MIX] = (_silu(seg("gg", GLA_W)) * og).astype(BF16)
        return carry

    lax.fori_loop(0, TS // C, chunk, 0)


def _mixer(proj, cos_t, sin_t, params, batch, seq):
    n_s = seq // TS
    row_map = lambda b, s: (b * n_s + s, 0)
    const = lambda b, s: (0, 0)
    specs = [pl.BlockSpec((TS, NP), row_map),
             pl.BlockSpec((TS, 256), row_map),
             pl.BlockSpec((TS, 256), row_map)]
    specs += [pl.BlockSpec(p.shape, const) for p in params]
    return pl.pallas_call(
        _mixer_kernel,
        grid=(batch, n_s),
        in_specs=specs,
        out_specs=pl.BlockSpec((TS, D_MIX), row_map),
        out_shape=jax.ShapeDtypeStruct((batch * seq, D_MIX), BF16),
        scratch_shapes=[pltpu.VMEM((256, 256), F32),
                        pltpu.VMEM((SSD_BC, SSD_W), F32),
                        pltpu.VMEM((GLA_W, GLA_QK), F32),
                        pltpu.VMEM((TS + 8, SSD_XBC), F32),
                        pltpu.VMEM((TS, SSD_XBC), F32)],
        compiler_params=pltpu.CompilerParams(dimension_semantics=("arbitrary", "arbitrary"),
                                             vmem_limit_bytes=VMEM_LIMIT),
        name="mixer",
    )(proj, cos_t, sin_t, *params)


def _layer_norm(y, g, b):
    mu = jnp.mean(y, axis=-1, keepdims=True)
    yc = y - mu
    var = jnp.mean(yc * yc, axis=-1, keepdims=True)
    return yc * lax.rsqrt(var + LN_EPS) * g + b


def _post_kernel(h_ref, x_ref, wout_ref, g_ref, b_ref, wr_ref, br_ref,
                 x1_ref, x1b_ref, mi_ref, mf_ref, cnt_ref, carry):
    tm = TM_POST

    @pl.when(pl.program_id(0) == 0)
    def _():
        carry[...] = jnp.zeros_like(carry)

    mix = _dot(h_ref[...], wout_ref[...])
    x1 = _layer_norm(DEEPNORM_ALPHA * x_ref[...] + mix, g_ref[...], b_ref[...])
    x1_ref[...] = x1
    x1b_ref[...] = x1.astype(BF16)

    logits = _dot(x1, wr_ref[...], precision=lax.Precision.HIGHEST) + br_ref[...]
    lane_i = _iota((tm, LANES), 1)
    lane = lane_i.astype(F32)
    work = logits
    vals, idxs = [], []
    multi = jnp.zeros((tm, LANES), F32)
    for _ in range(TOP_K):
        m = jnp.max(work, axis=-1, keepdims=True)
        idx = jnp.min(jnp.where(work == m, lane, float(LANES)), axis=-1, keepdims=True)
        hit = lane == idx
        multi = multi + hit.astype(F32)
        work = jnp.where(hit, -jnp.inf, work)
        vals.append(m)
        idxs.append(idx)
    exps = [jnp.exp(v - vals[0]) for v in vals]
    denom = exps[0] + exps[1] + exps[2] + exps[3]
    gates = [e / denom for e in exps]

    before = (_iota((tm, tm), 0) > _iota((tm, tm), 1)).astype(BF16)
    prior = _dot(before, multi.astype(BF16)) + carry[...]
    mi = jnp.zeros((tm, LANES), jnp.int32)
    mf = jnp.zeros((tm, LANES), F32)
    for kk in range(TOP_K):
        rank = jnp.sum(jnp.where(lane == idxs[kk], prior, 0.0), axis=-1, keepdims=True)
        mi = jnp.where(lane_i == kk, idxs[kk].astype(jnp.int32), mi)
        mi = jnp.where(lane_i == TOP_K + kk, rank.astype(jnp.int32), mi)
        mf = jnp.where(lane_i == kk, gates[kk], mf)
    mi_ref[...] = mi
    mf_ref[...] = mf
    carry[...] = carry[...] + jnp.sum(multi, axis=0, keepdims=True)
    cnt_ref[...] = jnp.broadcast_to(carry[...], cnt_ref.shape)


def _post(h, x2, w_out_b, ln_g, ln_b, wr_p, br_p):
    t = x2.shape[0]
    tm = TM_POST
    row = lambda i: (i, 0)
    const = lambda i: (0, 0)
    return pl.pallas_call(
        _post_kernel,
        grid=(t // tm,),
        in_specs=[pl.BlockSpec((tm, D_MIX), row), pl.BlockSpec((tm, D_MODEL), row),
                  pl.BlockSpec((D_MIX, D_MODEL), const), pl.BlockSpec((1, D_MODEL), const),
                  pl.BlockSpec((1, D_MODEL), const), pl.BlockSpec((D_MODEL, LANES), const),
                  pl.BlockSpec((1, LANES), const)],
        out_specs=[pl.BlockSpec((tm, D_MODEL), row), pl.BlockSpec((tm, D_MODEL), row),
                   pl.BlockSpec((tm, LANES), row), pl.BlockSpec((tm, LANES), row),
                   pl.BlockSpec((8, LANES), const)],
        out_shape=[jax.ShapeDtypeStruct((t, D_MODEL), F32), jax.ShapeDtypeStruct((t, D_MODEL), BF16),
                   jax.ShapeDtypeStruct((t, LANES), jnp.int32), jax.ShapeDtypeStruct((t, LANES), F32),
                   jax.ShapeDtypeStruct((8, LANES), F32)],
        scratch_shapes=[pltpu.VMEM((1, LANES), F32)],
        compiler_params=pltpu.CompilerParams(dimension_semantics=("arbitrary",),
                                             vmem_limit_bytes=VMEM_LIMIT),
        name="outproj_ln_router",
    )(h, x2, w_out_b, ln_g, ln_b, wr_p, br_p)


def _ffn_kernel(be_ref, nv_ref, x_ref, wg_ref, bg_ref, wu_ref, bu_ref, wd_ref, bd_ref, o_ref,
                wg_b, wu_b, wd_b):
    i = pl.program_id(0)
    valid = i < nv_ref[0]
    e = be_ref[i]
    prev = be_ref[jnp.maximum(i - 1, 0)]
    fresh = jnp.logical_or(i == 0, e != prev)

    @pl.when(jnp.logical_and(valid, fresh))
    def _():
        wg_b[...] = wg_ref[0, 0].astype(BF16)
        wu_b[...] = wu_ref[0, 0].astype(BF16)
        wd_b[...] = wd_ref[0, 0].astype(BF16)

    @pl.when(valid)
    def _():
        x = x_ref[...]
        hg = jnp.minimum(_dot(x, wg_b[...]) + bg_ref[0, 0], SWIGLU_LIMIT)
        hu = jnp.clip(_dot(x, wu_b[...]) + bu_ref[0, 0], -SWIGLU_LIMIT, SWIGLU_LIMIT)
        hh = (hu + 1.0) * hg * jax.nn.sigmoid(SWIGLU_ALPHA * hg)
        o_ref[...] = (_dot(hh.astype(BF16), wd_b[...]) + bd_ref[0, 0]).astype(o_ref.dtype)


def _ffn(layer, block_expert, n_valid, xin, wg, bg, wu, bu, wd, bd):
    cap = xin.shape[0]
    nb = cap // BM

    def blk(i, be, nv):
        return jnp.minimum(i, nv[0] - 1)

    row = lambda i, be, nv: (blk(i, be, nv), 0)
    wmap = lambda i, be, nv: (layer, be[blk(i, be, nv)], 0, 0)
    grid_spec = pltpu.PrefetchScalarGridSpec(
        num_scalar_prefetch=2,
        grid=(nb,),
        in_specs=[pl.BlockSpec((BM, D_MODEL), row),
                  pl.BlockSpec((1, 1, D_MODEL, D_FF), wmap), pl.BlockSpec((1, 1, 1, D_FF), wmap),
                  pl.BlockSpec((1, 1, D_MODEL, D_FF), wmap), pl.BlockSpec((1, 1, 1, D_FF), wmap),
                  pl.BlockSpec((1, 1, D_FF, D_MODEL), wmap), pl.BlockSpec((1, 1, 1, D_MODEL), wmap)],
        out_specs=pl.BlockSpec((BM, D_MODEL), row),
        scratch_shapes=[pltpu.VMEM((D_MODEL, D_FF), BF16), pltpu.VMEM((D_MODEL, D_FF), BF16),
                        pltpu.VMEM((D_FF, D_MODEL), BF16)],
    )
    return pl.pallas_call(
        _ffn_kernel,
        grid_spec=grid_spec,
        out_shape=jax.ShapeDtypeStruct((cap, D_MODEL), BF16),
        compiler_params=pltpu.CompilerParams(dimension_semantics=("arbitrary",),
                                             vmem_limit_bytes=VMEM_LIMIT),
        name="expert_ffn",
    )(block_expert, n_valid, xin, wg, bg, wu, bu, wd, bd)


def _combine_kernel(x_ref, rows_ref, mf_ref, g_ref, b_ref, o_ref):
    mf = mf_ref[...]
    y = DEEPNORM_ALPHA * x_ref[...]
    for kk in range(TOP_K):
        y = y + mf[:, kk:kk + 1] * rows_ref[kk].astype(F32)
    o_ref[...] = _layer_norm(y, g_ref[...], b_ref[...])


def _combine(x1, rows, mf, ln_g, ln_b):
    t = x1.shape[0]
    tm = TM_COMB
    row = lambda i: (i, 0)
    const = lambda i: (0, 0)
    return pl.pallas_call(
        _combine_kernel,
        grid=(t // tm,),
        in_specs=[pl.BlockSpec((tm, D_MODEL), row),
                  pl.BlockSpec((TOP_K, tm, D_MODEL), lambda i: (0, i, 0)),
                  pl.BlockSpec((tm, LANES), row),
                  pl.BlockSpec((1, D_MODEL), const), pl.BlockSpec((1, D_MODEL), const)],
        out_specs=pl.BlockSpec((tm, D_MODEL), row),
        out_shape=jax.ShapeDtypeStruct((t, D_MODEL), F32),
        compiler_params=pltpu.CompilerParams(dimension_semantics=("arbitrary",),
                                             vmem_limit_bytes=VMEM_LIMIT),
        name="combine_ln",
    )(x1, rows, mf, ln_g, ln_b)


def _relayout_w_in(w):
    widths = (256, 256, 256, 256, SSD_W, SSD_XBC, SSD_HEADS, GLA_QK, GLA_QK, GLA_W, GLA_RANK, GLA_W)
    offs = [0]
    for wd in widths:
        offs.append(offs[-1] + wd)
    parts = [w[:, offs[i]:offs[i + 1]] for i in range(len(widths))]
    parts[6] = jnp.repeat(parts[6], SSD_HEAD_DIM, axis=1)
    parts[10] = jnp.pad(parts[10], ((0, 0), (0, LANES - GLA_RANK)))
    return jnp.concatenate(parts, axis=1).astype(BF16)


def _rep_heads(p):
    return jnp.repeat(p, SSD_HEAD_DIM)[None, :]


def kernel(x, positions, w_in, w_out, ret_norm_w, ssd_conv_w, ssd_conv_b, ssd_dt_bias, ssd_a_log, ssd_d,
           ssd_norm_w, gla_w_gk2, gla_b_gk2, gla_norm_w, ln1_g, ln1_b, w_router, b_router, w_gate, b_gate,
           w_up, b_up, w_down, b_down, ln2_g, ln2_b):
    batch, seq, d = x.shape
    t = batch * seq
    depth = w_in.shape[0]
    assert d == D_MODEL and seq % TS == 0 and t % TM_PROJ == 0 and t % TM_COMB == 0
    n_assign = t * TOP_K
    nb = n_assign // BM + N_EXPERTS
    cap = nb * BM

    cos_t, sin_t = _rope_tables(positions.reshape(t, 1))
    x2 = x.reshape(t, d)
    tok = jnp.arange(t, dtype=jnp.int32)

    for l in range(depth):
        proj = _inproj(x2, _relayout_w_in(w_in[l]))
        params = (ret_norm_w[l][None, :], ssd_conv_w[l], ssd_conv_b[l][None, :], _rep_heads(ssd_dt_bias[l]),
                  _rep_heads(ssd_a_log[l]), _rep_heads(ssd_d[l]), ssd_norm_w[l][None, :],
                  jnp.pad(gla_w_gk2[l], ((0, LANES - GLA_RANK), (0, 0))), gla_b_gk2[l][None, :],
                  gla_norm_w[l][None, :])
        h = _mixer(proj, cos_t, sin_t, params, batch, seq)

        wr_p = jnp.pad(w_router[l], ((0, 0), (0, LANES - N_EXPERTS)))
        br_p = jnp.pad(b_router[l], (0, LANES - N_EXPERTS), constant_values=NEG_BIG)[None, :]
        x1, x1b, mi, mf, cnt = _post(h, x2, w_out[l].astype(BF16), ln1_g[l][None, :], ln1_b[l][None, :],
                                     wr_p, br_p)

        counts = cnt[0, :N_EXPERTS].astype(jnp.int32)
        padded = (counts + BM - 1) // BM * BM
        end_padded = jnp.cumsum(padded)
        start_padded = end_padded - padded
        top_idx = mi[:, :TOP_K]
        dest = start_padded[top_idx] + mi[:, TOP_K:2 * TOP_K]
        block_start = jnp.arange(nb, dtype=jnp.int32) * BM
        block_expert = jnp.minimum(jnp.sum((end_padded[None, :] <= block_start[:, None]).astype(jnp.int32), axis=1),
                                   N_EXPERTS - 1)
        n_valid = (end_padded[-1:] // BM).astype(jnp.int32)

        slot_tok = jnp.zeros((cap,), jnp.int32).at[dest.reshape(-1)].set(jnp.repeat(tok, TOP_K))
        xin = x1b[slot_tok]
        yb = _ffn(l, block_expert, n_valid, xin, w_gate, b_gate[:, :, None, :], w_up, b_up[:, :, None, :],
                  w_down, b_down[:, :, None, :])
        rows = yb[dest.T]
        x2 = _combine(x1, rows, mf, ln2_g[l][None, :], ln2_b[l][None, :])
    return x2.reshape(batch, seq, d)
```

```python
import functools
import math

import jax
import jax.numpy as jnp
from jax import lax
from jax.experimental import pallas as pl
from jax.experimental.pallas import tpu as pltpu
from jax.experimental.pallas import tpu_sc as plsc

F32 = jnp.float32
BF16 = jnp.bfloat16

D_MODEL = 1024
CHUNK = 64
RET_HEADS, RET_DK, RET_DV = 4, 64, 64
RET_W = RET_HEADS * RET_DV
SSD_HEADS, SSD_HEAD_DIM, SSD_STATE, SSD_GROUPS, SSD_CONV = 8, 64, 64, 2, 4
SSD_W = SSD_HEADS * SSD_HEAD_DIM
SSD_BC = SSD_GROUPS * SSD_STATE
SSD_XBC = SSD_W + 2 * SSD_BC
GLA_HEADS, GLA_DK, GLA_DV, GLA_RANK, GLA_TEMP = 4, 32, 64, 16, 16.0
GLA_QK = GLA_HEADS * GLA_DK
GLA_W = GLA_HEADS * GLA_DV
D_MIX = RET_W + SSD_W + GLA_W
N_EXPERTS, TOP_K, D_FF = 32, 4, 1024
SWIGLU_LIMIT, SWIGLU_ALPHA = 7.0, 1.702
ROPE_BASE = 10000.0
LN_EPS, NORM_EPS = 1e-5, 1e-6
DEPTH = 2
DEEPNORM_ALPHA = (2.0 * DEPTH) ** 0.25

LANES = 128
NEG_BIG = -1e30
VMEM_LIMIT = 56 * 1024 * 1024

_SEGS = (("rq", 256), ("rk", 256), ("rv", 256), ("rg", 256), ("sz", SSD_W), ("sxbc", SSD_XBC),
         ("sdt", SSD_W), ("gq", 128), ("gk", 128), ("gv", 256), ("ggk", 128), ("gg", 256))
COL = {}
_off = 0
for _n, _w in _SEGS:
    COL[_n] = _off
    _off += _w
NP = _off

TS = 256
TM_PROJ = 512
TM_POST = 256
TM_COMB = 512
BM = 256
ROW_WORDS = D_MODEL // 2
SC_WORKERS = 32
SC_CHUNK = 64


def _dot(a, b, dims=(((1,), (0,)), ((), ())), precision=None):
    return lax.dot_general(a, b, dims, precision=precision, preferred_element_type=F32)


_NT = (((1,), (1,)), ((), ()))
_TN = (((0,), (0,)), ((), ()))


def _iota(shape, dim):
    return lax.broadcasted_iota(jnp.int32, shape, dim)


def _vdiv(x, n):
    assert n & (n - 1) == 0
    return lax.shift_right_logical(x, n.bit_length() - 1)


def _vmod(x, n):
    assert n & (n - 1) == 0
    return jnp.bitwise_and(x, n - 1)


def _silu(x):
    return x * jax.nn.sigmoid(x)


def _softplus(x):
    return jnp.maximum(x, 0.0) + jnp.log1p(jnp.exp(-jnp.abs(x)))


def _pack_rows(x):
    w = x.shape[1] // 2
    lo = lax.bitcast_convert_type(x[:, :w].astype(BF16).astype(F32), jnp.uint32)
    hi = lax.bitcast_convert_type(x[:, w:].astype(BF16).astype(F32), jnp.uint32)
    return lax.bitcast_convert_type(lax.shift_right_logical(lo, jnp.uint32(16)) | hi, jnp.int32)


def _unpack_rows(words):
    u = lax.bitcast_convert_type(words, jnp.uint32)
    a = lax.bitcast_convert_type(lax.shift_left(u, jnp.uint32(16)), F32)
    b = lax.bitcast_convert_type(u & jnp.uint32(0xFFFF0000), F32)
    return jnp.concatenate([a, b], axis=-1)


def _seg_sum64(x):
    first = _iota((1, LANES), 1) < 64
    outs = []
    for j in range(x.shape[-1] // LANES):
        blk = x[:, j * LANES:(j + 1) * LANES]
        lo = jnp.sum(jnp.where(first, blk, 0.0), axis=-1, keepdims=True)
        hi = jnp.sum(jnp.where(first, 0.0, blk), axis=-1, keepdims=True)
        outs.append(jnp.where(first, lo, hi))
    return jnp.concatenate(outs, axis=-1)


def _block_diag(x, reps, row_blk, col_blk):
    t = jnp.concatenate([x] * reps, axis=0)
    keep = _vdiv(_iota(t.shape, 0), row_blk) == _vdiv(_iota(t.shape, 1), col_blk)
    return jnp.where(keep, t, 0.0).astype(BF16)


def _rope_kernel(pos_ref, cos_ref, sin_ref):
    lane = _iota((1, RET_HEADS * RET_DK), 1)
    half = RET_DK // 2
    k = _vmod(lane, half).astype(F32)
    inv_freq = jnp.exp(k * (-math.log(ROPE_BASE) / half))
    ang = pos_ref[...].astype(F32) * inv_freq
    first = _vmod(lane, RET_DK) < half
    cos_ref[...] = jnp.cos(ang)
    sin_ref[...] = jnp.where(first, -1.0, 1.0) * jnp.sin(ang)


def _rope_tables(pos_col):
    t = pos_col.shape[0]
    tm = 512
    w = RET_HEADS * RET_DK
    return pl.pallas_call(
        _rope_kernel,
        grid=(t // tm,),
        in_specs=[pl.BlockSpec((tm, 1), lambda i: (i, 0))],
        out_specs=[pl.BlockSpec((tm, w), lambda i: (i, 0))] * 2,
        out_shape=[jax.ShapeDtypeStruct((t, w), F32)] * 2,
        compiler_params=pltpu.CompilerParams(dimension_semantics=("arbitrary",)),
        name="rope_tables",
    )(pos_col)


def _inproj_kernel(x_ref, w_ref, o_ref):
    o_ref[...] = _dot(x_ref[...].astype(BF16), w_ref[...])


def _inproj(x2, w_p):
    t = x2.shape[0]
    return pl.pallas_call(
        _inproj_kernel,
        grid=(t // TM_PROJ,),
        in_specs=[pl.BlockSpec((TM_PROJ, D_MODEL), lambda i: (i, 0)),
                  pl.BlockSpec((D_MODEL, NP), lambda i: (0, 0))],
        out_specs=pl.BlockSpec((TM_PROJ, NP), lambda i: (i, 0)),
        out_shape=jax.ShapeDtypeStruct((t, NP), F32),
        compiler_params=pltpu.CompilerParams(dimension_semantics=("arbitrary",),
                                             vmem_limit_bytes=VMEM_LIMIT),
        name="inproj",
    )(x2, w_p)


def _mixer_kernel(proj_ref, cos_ref, sin_ref, retw_ref, convw_ref, convb_ref, dtb_ref, alog_ref,
                  dskip_ref, ssdw_ref, wgk_ref, bgk_ref, glaw_ref, h_ref,
                  ret_s, ssd_s, gla_s, stage, xact):
    C = CHUNK

    @pl.when(pl.program_id(1) == 0)
    def _():
        ret_s[...] = jnp.zeros_like(ret_s)
        ssd_s[...] = jnp.zeros_like(ssd_s)
        gla_s[...] = jnp.zeros_like(gla_s)
        stage[0:8, :] = jnp.zeros((8, SSD_XBC), F32)

    stage[8:8 + TS, :] = proj_ref[:, COL["sxbc"]:COL["sxbc"] + SSD_XBC]
    acc = convb_ref[...] + convw_ref[0:1, :] * stage[5:5 + TS, :]
    for j in range(1, SSD_CONV):
        acc = acc + convw_ref[j:j + 1, :] * stage[5 + j:5 + j + TS, :]
    xact[...] = _silu(acc)
    stage[0:8, :] = stage[TS:TS + 8, :]

    lane256 = _iota((1, 256), 1)
    head = _vdiv(lane256, 64).astype(F32)
    log_gamma = jnp.log(1.0 - jnp.exp((-5.0 - head) * math.log(2.0)))
    row = _iota((C, 1), 0).astype(F32)
    dist = row - _vmod(lane256, 64).astype(F32)
    ret_intra = jnp.where(dist >= 0, jnp.exp(log_gamma * jnp.maximum(dist, 0.0)), 0.0)
    ret_qdec = jnp.exp(log_gamma * (row + 1.0))
    ret_kdec = jnp.exp(log_gamma * (C - 1.0 - row))
    ret_cdec = jnp.exp(log_gamma * C)
    first_half = _vmod(lane256, RET_DK) < (RET_DK // 2)

    tri = (_iota((C, C), 0) >= _iota((C, C), 1)).astype(F32)
    causal4 = _iota((C, 256), 0) >= _vmod(_iota((C, 256), 1), 64)
    causal8 = _iota((C, 512), 0) >= _vmod(_iota((C, 512), 1), 64)
    eye8 = _iota((C, 512), 0) == _vmod(_iota((C, 512), 1), 64)

    a_neg = -jnp.exp(alog_ref[...])

    def rot(t, cos, sin):
        sw = jnp.where(first_half, pltpu.roll(t, 256 - 32, 1), pltpu.roll(t, 32, 1))
        return t * cos + sw * sin

    def chunk(c, carry):
        r0 = pl.multiple_of(c * C, C)

        def seg(name, width):
            return proj_ref[pl.ds(r0, C), COL[name]:COL[name] + width]

        cos = cos_ref[pl.ds(r0, C), :]
        sin = sin_ref[pl.ds(r0, C), :]
        q = rot(seg("rq", 256), cos, sin)
        k = rot(seg("rk", 256), cos, sin) * (RET_DK ** -0.5)
        v = seg("rv", 256)
        vb = v.astype(BF16)
        kbd = _block_diag(k, RET_HEADS, C, RET_DK)
        scores = _dot(q.astype(BF16), kbd, _NT) * ret_intra
        vbd = _block_diag(v, RET_HEADS, C, RET_DV)
        s_prev = ret_s[...]
        o = _dot(scores.astype(BF16), vbd) + _dot((q * ret_qdec).astype(BF16), s_prev.astype(BF16))
        contrib = _dot((k * ret_kdec).astype(BF16), vb, _TN)
        keep = _vdiv(_iota((256, 256), 0), RET_DK) == _vdiv(_iota((256, 256), 1), RET_DV)
        ret_s[...] = jnp.where(keep, ret_cdec * s_prev + contrib, 0.0)
        mu = _seg_sum64(o) * (1.0 / RET_DV)
        oc = o - mu
        var = _seg_sum64(oc * oc) * (1.0 / RET_DV)
        o = oc * lax.rsqrt(var + LN_EPS) * retw_ref[...]
        h_ref[pl.ds(r0, C), 0:RET_W] = (_silu(seg("rg", 256)) * o).astype(BF16)

        xs = xact[pl.ds(r0, C), 0:SSD_W]
        bm = xact[pl.ds(r0, C), SSD_W:SSD_W + SSD_BC]
        cm = xact[pl.ds(r0, C), SSD_W + SSD_BC:SSD_XBC]
        cmb = cm.astype(BF16)
        dt = _softplus(seg("sdt", SSD_W) + dtb_ref[...])
        acum = _dot(tri, dt * a_neg, precision=lax.Precision.HIGHEST)
        arow = jnp.sum(jnp.where(eye8, acum, 0.0), axis=0, keepdims=True)
        decay = jnp.exp(jnp.where(causal8, acum - arow, NEG_BIG))
        b8 = jnp.concatenate([bm] * SSD_HEADS, axis=0)
        keep_b = _vdiv(_iota(b8.shape, 0), C * SSD_HEADS // SSD_GROUPS) == _vdiv(_iota(b8.shape, 1), SSD_STATE)
        b8 = jnp.where(keep_b, b8, 0.0).astype(BF16)
        cb = _dot(cmb, b8, _NT)
        m = (cb * decay).astype(BF16)
        xdt = xs * dt
        s2 = ssd_s[...]
        half = SSD_W // SSD_GROUPS
        ys = []
        for g in range(SSD_GROUPS):
            xbd = _block_diag(xdt[:, g * half:(g + 1) * half], SSD_HEADS // SSD_GROUPS, C, SSD_HEAD_DIM)
            ys.append(_dot(m[:, g * half:(g + 1) * half], xbd))
        y = jnp.concatenate(ys, axis=-1)
        y = y + _dot(cmb, s2.astype(BF16)) * jnp.exp(acum)
        y = y + dskip_ref[...] * xs
        a_last = acum[C - 1:C, :]
        sd = jnp.exp(a_last - acum)
        contrib_s = _dot(bm.astype(BF16), (xdt * sd).astype(BF16), _TN)
        keep_s = _vdiv(_iota(s2.shape, 0), SSD_STATE) == _vdiv(_iota(s2.shape, 1), half)
        ssd_s[...] = jnp.where(keep_s, s2 * jnp.exp(a_last) + contrib_s, 0.0)
        yz = y * _silu(seg("sz", SSD_W))
        outs = []
        for g in range(SSD_GROUPS):
            blk = yz[:, g * half:(g + 1) * half]
            ms = jnp.mean(blk * blk, axis=-1, keepdims=True)
            outs.append(blk * lax.rsqrt(ms + NORM_EPS))
        h_ref[pl.ds(r0, C), RET_W:RET_W + SSD_W] = (jnp.concatenate(outs, axis=-1) * ssdw_ref[...]).astype(BF16)

        gq = seg("gq", GLA_QK) * (GLA_DK ** -0.5)
        gkk = seg("gk", GLA_QK)
        gv = seg("gv", GLA_W)
        gkl = _dot(seg("ggk", 128).astype(BF16), wgk_ref[...].astype(BF16)) + bgk_ref[...]
        log_a = (jnp.minimum(gkl, 0.0) - jnp.log1p(jnp.exp(-jnp.abs(gkl)))) * (1.0 / GLA_TEMP)
        b = _dot(tri, log_a, precision=lax.Precision.HIGHEST)
        q_t = (gq * jnp.exp(b)).astype(BF16)
        k_t = gkk * jnp.exp(-b)
        kbd_g = _block_diag(k_t, GLA_HEADS, C, GLA_DK)
        att = jnp.where(causal4, _dot(q_t, kbd_g, _NT), 0.0)
        vbd_g = _block_diag(gv, GLA_HEADS, C, GLA_DV)
        st = gla_s[...]
        og = _dot(att.astype(BF16), vbd_g) + _dot(q_t, st.astype(BF16), _NT)
        b_last = b[C - 1:C, :]
        kd = (gkk * jnp.exp(b_last - b)).astype(BF16)
        contrib_g = _dot(gv.astype(BF16), kd, _TN)
        keep_g = _vdiv(_iota(st.shape, 0), GLA_DV) == _vdiv(_iota(st.shape, 1), GLA_DK)
        gla_s[...] = jnp.where(keep_g, st * jnp.exp(b_last) + contrib_g, 0.0)
        ms = _seg_sum64(og * og) * (1.0 / GLA_DV)
        og = og * lax.rsqrt(ms + NORM_EPS) * glaw_ref[...]
        h_ref[pl.ds(r0, C), RET_W + SSD_W:D_MIX] = (_silu(seg("gg", GLA_W)) * og).astype(BF16)
        return carry

    lax.fori_loop(0, TS // C, chunk, 0)


def _mixer(proj, cos_t, sin_t, params, batch, seq):
    n_s = seq // TS
    row_map = lambda b, s: (b * n_s + s, 0)
    const = lambda b, s: (0, 0)
    specs = [pl.BlockSpec((TS, NP), row_map),
             pl.BlockSpec((TS, 256), row_map),
             pl.BlockSpec((TS, 256), row_map)]
    specs += [pl.BlockSpec(p.shape, const) for p in params]
    return pl.pallas_call(
        _mixer_kernel,
        grid=(batch, n_s),
        in_specs=specs,
        out_specs=pl.BlockSpec((TS, D_MIX), row_map),
        out_shape=jax.ShapeDtypeStruct((batch * seq, D_MIX), BF16),
        scratch_shapes=[pltpu.VMEM((256, 256), F32),
                        pltpu.VMEM((SSD_BC, SSD_W), F32),
                        pltpu.VMEM((GLA_W, GLA_QK), F32),
                        pltpu.VMEM((TS + 8, SSD_XBC), F32),
                        pltpu.VMEM((TS, SSD_XBC), F32)],
        compiler_params=pltpu.CompilerParams(dimension_semantics=("arbitrary", "arbitrary"),
                                             vmem_limit_bytes=VMEM_LIMIT),
        name="mixer",
    )(proj, cos_t, sin_t, *params)


def _layer_norm(y, g, b):
    mu = jnp.mean(y, axis=-1, keepdims=True)
    yc = y - mu
    var = jnp.mean(yc * yc, axis=-1, keepdims=True)
    return yc * lax.rsqrt(var + LN_EPS) * g + b


def _post_kernel(h_ref, x_ref, wout_ref, g_ref, b_ref, wr_ref, br_ref,
                 x1_ref, x1p_ref, mi_ref, mf_ref, cnt_ref, carry):
    tm = TM_POST

    @pl.when(pl.program_id(0) == 0)
    def _():
        carry[...] = jnp.zeros_like(carry)

    mix = _dot(h_ref[...], wout_ref[...])
    x1 = _layer_norm(DEEPNORM_ALPHA * x_ref[...] + mix, g_ref[...], b_ref[...])
    x1_ref[...] = x1
    x1p_ref[...] = _pack_rows(x1)

    logits = _dot(x1, wr_ref[...], precision=lax.Precision.HIGHEST) + br_ref[...]
    lane_i = _iota((tm, LANES), 1)
    lane = lane_i.astype(F32)
    work = logits
    vals, idxs = [], []
    multi = jnp.zeros((tm, LANES), F32)
    for _ in range(TOP_K):
        m = jnp.max(work, axis=-1, keepdims=True)
        idx = jnp.min(jnp.where(work == m, lane, float(LANES)), axis=-1, keepdims=True)
        hit = lane == idx
        multi = multi + hit.astype(F32)
        work = jnp.where(hit, -jnp.inf, work)
        vals.append(m)
        idxs.append(idx)
    exps = [jnp.exp(v - vals[0]) for v in vals]
    denom = exps[0] + exps[1] + exps[2] + exps[3]
    gates = [e / denom for e in exps]

    before = (_iota((tm, tm), 0) > _iota((tm, tm), 1)).astype(BF16)
    prior = _dot(before, multi.astype(BF16)) + carry[...]
    mi = jnp.zeros((tm, LANES), jnp.int32)
    mf = jnp.zeros((tm, LANES), F32)
    for kk in range(TOP_K):
        rank = jnp.sum(jnp.where(lane == idxs[kk], prior, 0.0), axis=-1, keepdims=True)
        mi = jnp.where(lane_i == kk, idxs[kk].astype(jnp.int32), mi)
        mi = jnp.where(lane_i == TOP_K + kk, rank.astype(jnp.int32), mi)
        mf = jnp.where(lane_i == kk, gates[kk], mf)
    mi_ref[...] = mi
    mf_ref[...] = mf
    carry[...] = carry[...] + jnp.sum(multi, axis=0, keepdims=True)
    cnt_ref[...] = jnp.broadcast_to(carry[...], cnt_ref.shape)


def _post(h, x2, w_out_b, ln_g, ln_b, wr_p, br_p):
    t = x2.shape[0]
    tm = TM_POST
    row = lambda i: (i, 0)
    const = lambda i: (0, 0)
    return pl.pallas_call(
        _post_kernel,
        grid=(t // tm,),
        in_specs=[pl.BlockSpec((tm, D_MIX), row), pl.BlockSpec((tm, D_MODEL), row),
                  pl.BlockSpec((D_MIX, D_MODEL), const), pl.BlockSpec((1, D_MODEL), const),
                  pl.BlockSpec((1, D_MODEL), const), pl.BlockSpec((D_MODEL, LANES), const),
                  pl.BlockSpec((1, LANES), const)],
        out_specs=[pl.BlockSpec((tm, D_MODEL), row), pl.BlockSpec((tm, ROW_WORDS), row),
                   pl.BlockSpec((tm, LANES), row), pl.BlockSpec((tm, LANES), row),
                   pl.BlockSpec((8, LANES), const)],
        out_shape=[jax.ShapeDtypeStruct((t, D_MODEL), F32), jax.ShapeDtypeStruct((t, ROW_WORDS), jnp.int32),
                   jax.ShapeDtypeStruct((t, LANES), jnp.int32), jax.ShapeDtypeStruct((t, LANES), F32),
                   jax.ShapeDtypeStruct((8, LANES), F32)],
        scratch_shapes=[pltpu.VMEM((1, LANES), F32)],
        compiler_params=pltpu.CompilerParams(dimension_semantics=("arbitrary",),
                                             vmem_limit_bytes=VMEM_LIMIT),
        name="outproj_ln_router",
    )(h, x2, w_out_b, ln_g, ln_b, wr_p, br_p)


def _ffn_kernel(be_ref, nv_ref, x_ref, wg_ref, bg_ref, wu_ref, bu_ref, wd_ref, bd_ref, o_ref,
                wg_b, wu_b, wd_b):
    i = pl.program_id(0)
    valid = i < nv_ref[0]
    e = be_ref[i]
    prev = be_ref[jnp.maximum(i - 1, 0)]
    fresh = jnp.logical_or(i == 0, e != prev)

    @pl.when(jnp.logical_and(valid, fresh))
    def _():
        wg_b[...] = wg_ref[0, 0].astype(BF16)
        wu_b[...] = wu_ref[0, 0].astype(BF16)
        wd_b[...] = wd_ref[0, 0].astype(BF16)

    @pl.when(valid)
    def _():
        x = _unpack_rows(x_ref[...]).astype(BF16)
        hg = jnp.minimum(_dot(x, wg_b[...]) + bg_ref[0, 0], SWIGLU_LIMIT)
        hu = jnp.clip(_dot(x, wu_b[...]) + bu_ref[0, 0], -SWIGLU_LIMIT, SWIGLU_LIMIT)
        hh = (hu + 1.0) * hg * jax.nn.sigmoid(SWIGLU_ALPHA * hg)
        o_ref[...] = _pack_rows(_dot(hh.astype(BF16), wd_b[...]) + bd_ref[0, 0])


def _ffn(layer, block_expert, n_valid, xin, wg, bg, wu, bu, wd, bd):
    cap = xin.shape[0]
    nb = cap // BM

    def blk(i, be, nv):
        return jnp.maximum(jnp.minimum(i, nv[0] - 1), 0)

    row = lambda i, be, nv: (blk(i, be, nv), 0)
    wmap = lambda i, be, nv: (layer, be[blk(i, be, nv)], 0, 0)
    grid_spec = pltpu.PrefetchScalarGridSpec(
        num_scalar_prefetch=2,
        grid=(nb,),
        in_specs=[pl.BlockSpec((BM, ROW_WORDS), row),
                  pl.BlockSpec((1, 1, D_MODEL, D_FF), wmap), pl.BlockSpec((1, 1, 1, D_FF), wmap),
                  pl.BlockSpec((1, 1, D_MODEL, D_FF), wmap), pl.BlockSpec((1, 1, 1, D_FF), wmap),
                  pl.BlockSpec((1, 1, D_FF, D_MODEL), wmap), pl.BlockSpec((1, 1, 1, D_MODEL), wmap)],
        out_specs=pl.BlockSpec((BM, ROW_WORDS), row),
        scratch_shapes=[pltpu.VMEM((D_MODEL, D_FF), BF16), pltpu.VMEM((D_MODEL, D_FF), BF16),
                        pltpu.VMEM((D_FF, D_MODEL), BF16)],
    )
    return pl.pallas_call(
        _ffn_kernel,
        grid_spec=grid_spec,
        out_shape=jax.ShapeDtypeStruct((cap, ROW_WORDS), jnp.int32),
        compiler_params=pltpu.CompilerParams(dimension_semantics=("arbitrary",),
                                             vmem_limit_bytes=VMEM_LIMIT),
        name="expert_ffn",
    )(block_expert, n_valid, xin, wg, bg, wu, bu, wd, bd)


def _sc_gather(table, idx3):
    nw, n_chunks, ch = idx3.shape
    width = table.shape[1]
    per_worker = n_chunks * ch
    mesh = plsc.VectorSubcoreMesh(core_axis_name="c", subcore_axis_name="s")
    n_cores = mesh.num_cores
    assert nw == n_cores * mesh.num_subcores and n_chunks % 2 == 0 and ch == SC_CHUNK

    def body(table_hbm, idx_hbm, out_hbm, idx_v, rows0, rows1, sem_g0, sem_g1, sem_w0, sem_w1):
        wid = lax.axis_index("s") * n_cores + lax.axis_index("c")
        base = wid * per_worker
        pltpu.sync_copy(idx_hbm.at[wid], idx_v)

        @pl.loop(0, n_chunks, step=2)
        def _(c):
            g0 = pltpu.async_copy(table_hbm.at[idx_v.at[c]], rows0, sem_g0)
            g1 = pltpu.async_copy(table_hbm.at[idx_v.at[c + 1]], rows1, sem_g1)
            g0.wait()
            w0 = pltpu.async_copy(rows0, out_hbm.at[pl.ds(base + c * ch, ch)], sem_w0)
            g1.wait()
            w1 = pltpu.async_copy(rows1, out_hbm.at[pl.ds(base + (c + 1) * ch, ch)], sem_w1)
            w0.wait()
            w1.wait()

    return pl.kernel(
        body,
        out_type=jax.ShapeDtypeStruct((nw * per_worker, width), table.dtype),
        mesh=mesh,
        scratch_types=[pltpu.VMEM((n_chunks, ch), jnp.int32),
                       pltpu.VMEM((ch, width), table.dtype), pltpu.VMEM((ch, width), table.dtype),
                       pltpu.SemaphoreType.DMA, pltpu.SemaphoreType.DMA,
                       pltpu.SemaphoreType.DMA, pltpu.SemaphoreType.DMA],
        name="sc_row_gather",
    )(table, idx3)


def _combine_kernel(x_ref, rows_ref, mf_ref, g_ref, b_ref, o_ref):
    mf = mf_ref[...]
    y = DEEPNORM_ALPHA * x_ref[...]
    for kk in range(TOP_K):
        y = y + mf[:, kk:kk + 1] * _unpack_rows(rows_ref[kk])
    o_ref[...] = _layer_norm(y, g_ref[...], b_ref[...])


def _combine(x1, rows, mf, ln_g, ln_b):
    t = x1.shape[0]
    tm = TM_COMB
    row = lambda i: (i, 0)
    const = lambda i: (0, 0)
    return pl.pallas_call(
        _combine_kernel,
        grid=(t // tm,),
        in_specs=[pl.BlockSpec((tm, D_MODEL), row),
                  pl.BlockSpec((TOP_K, tm, ROW_WORDS), lambda i: (0, i, 0)),
                  pl.BlockSpec((tm, LANES), row),
                  pl.BlockSpec((1, D_MODEL), const), pl.BlockSpec((1, D_MODEL), const)],
        out_specs=pl.BlockSpec((tm, D_MODEL), row),
        out_shape=jax.ShapeDtypeStruct((t, D_MODEL), F32),
        compiler_params=pltpu.CompilerParams(dimension_semantics=("arbitrary",),
                                             vmem_limit_bytes=VMEM_LIMIT),
        name="combine_ln",
    )(x1, rows, mf, ln_g, ln_b)


def _relayout_w_in(w):
    widths = (256, 256, 256, 256, SSD_W, SSD_XBC, SSD_HEADS, GLA_QK, GLA_QK, GLA_W, GLA_RANK, GLA_W)
    offs = [0]
    for wd in widths:
        offs.append(offs[-1] + wd)
    parts = [w[:, offs[i]:offs[i + 1]] for i in range(len(widths))]
    parts[6] = jnp.repeat(parts[6], SSD_HEAD_DIM, axis=1)
    parts[10] = jnp.pad(parts[10], ((0, 0), (0, LANES - GLA_RANK)))
    return jnp.concatenate(parts, axis=1).astype(BF16)


def _rep_heads(p):
    return jnp.repeat(p, SSD_HEAD_DIM)[None, :]


def kernel(x, positions, w_in, w_out, ret_norm_w, ssd_conv_w, ssd_conv_b, ssd_dt_bias, ssd_a_log, ssd_d,
           ssd_norm_w, gla_w_gk2, gla_b_gk2, gla_norm_w, ln1_g, ln1_b, w_router, b_router, w_gate, b_gate,
           w_up, b_up, w_down, b_down, ln2_g, ln2_b):
    batch, seq, d = x.shape
    t = batch * seq
    depth = w_in.shape[0]
    assert d == D_MODEL and seq % TS == 0 and t % TM_PROJ == 0 and t % TM_COMB == 0
    n_assign = t * TOP_K
    nb = n_assign // BM + N_EXPERTS
    cap = nb * BM

    cos_t, sin_t = _rope_tables(positions.reshape(t, 1))
    x2 = x.reshape(t, d)
    tok = jnp.arange(t, dtype=jnp.int32)

    for l in range(depth):
        proj = _inproj(x2, _relayout_w_in(w_in[l]))
        params = (ret_norm_w[l][None, :], ssd_conv_w[l], ssd_conv_b[l][None, :], _rep_heads(ssd_dt_bias[l]),
                  _rep_heads(ssd_a_log[l]), _rep_heads(ssd_d[l]), ssd_norm_w[l][None, :],
                  jnp.pad(gla_w_gk2[l], ((0, LANES - GLA_RANK), (0, 0))), gla_b_gk2[l][None, :],
                  gla_norm_w[l][None, :])
        h = _mixer(proj, cos_t, sin_t, params, batch, seq)

        wr_p = jnp.pad(w_router[l], ((0, 0), (0, LANES - N_EXPERTS)))
        br_p = jnp.pad(b_router[l], (0, LANES - N_EXPERTS), constant_values=NEG_BIG)[None, :]
        x1, x1p, mi, mf, cnt = _post(h, x2, w_out[l].astype(BF16), ln1_g[l][None, :], ln1_b[l][None, :],
                                     wr_p, br_p)

        counts = cnt[0, :N_EXPERTS].astype(jnp.int32)
        padded = (counts + BM - 1) // BM * BM
        end_padded = jnp.cumsum(padded)
        start_padded = end_padded - padded
        top_idx = mi[:, :TOP_K]
        dest = start_padded[top_idx] + mi[:, TOP_K:2 * TOP_K]
        block_start = jnp.arange(nb, dtype=jnp.int32) * BM
        block_expert = jnp.minimum(jnp.sum((end_padded[None, :] <= block_start[:, None]).astype(jnp.int32), axis=1),
                                   N_EXPERTS - 1)
        n_valid = (end_padded[-1:] // BM).astype(jnp.int32)

        slot_tok = jnp.zeros((cap,), jnp.int32).at[dest.reshape(-1)].set(jnp.repeat(tok, TOP_K))
        xin = _sc_gather(x1p, slot_tok.reshape(SC_WORKERS, -1, SC_CHUNK))
        yb = _ffn(l, block_expert, n_valid, xin, w_gate, b_gate[:, :, None, :], w_up, b_up[:, :, None, :],
                  w_down, b_down[:, :, None, :])
        rows = _sc_gather(yb, dest.T.reshape(SC_WORKERS, -1, SC_CHUNK))
        x2 = _combine(x1, rows.reshape(TOP_K, t, ROW_WORDS), mf, ln2_g[l][None, :], ln2_b[l][None, :])
    return x2.reshape(batch, seq, d)
```

```python
import functools
import math

import jax
import jax.numpy as jnp
from jax import lax
from jax.experimental import pallas as pl
from jax.experimental.pallas import tpu as pltpu
from jax.experimental.pallas import tpu_sc as plsc

F32 = jnp.float32
BF16 = jnp.bfloat16

D_MODEL = 1024
CHUNK = 64
RET_HEADS, RET_DK, RET_DV = 4, 64, 64
RET_W = RET_HEADS * RET_DV
SSD_HEADS, SSD_HEAD_DIM, SSD_STATE, SSD_GROUPS, SSD_CONV = 8, 64, 64, 2, 4
SSD_W = SSD_HEADS * SSD_HEAD_DIM
SSD_BC = SSD_GROUPS * SSD_STATE
SSD_XBC = SSD_W + 2 * SSD_BC
GLA_HEADS, GLA_DK, GLA_DV, GLA_RANK, GLA_TEMP = 4, 32, 64, 16, 16.0
GLA_QK = GLA_HEADS * GLA_DK
GLA_W = GLA_HEADS * GLA_DV
D_MIX = RET_W + SSD_W + GLA_W
N_EXPERTS, TOP_K, D_FF = 32, 4, 1024
SWIGLU_LIMIT, SWIGLU_ALPHA = 7.0, 1.702
ROPE_BASE = 10000.0
LN_EPS, NORM_EPS = 1e-5, 1e-6
DEPTH = 2
DEEPNORM_ALPHA = (2.0 * DEPTH) ** 0.25

LANES = 128
NEG_BIG = -1e30
VMEM_LIMIT = 56 * 1024 * 1024

_SEGS = (("rq", 256), ("rk", 256), ("rv", 256), ("rg", 256), ("sz", SSD_W), ("sxbc", SSD_XBC),
         ("sdt", SSD_W), ("gq", 128), ("gk", 128), ("gv", 256), ("ggk", 128), ("gg", 256))
COL = {}
_off = 0
for _n, _w in _SEGS:
    COL[_n] = _off
    _off += _w
NP = _off

TS = 256
TM_PROJ = 512
TM_POST = 256
TM_COMB = 512
BM = 256
ROW_WORDS = D_MODEL // 2
SC_WORKERS = 32
SC_CHUNK = 64


def _dot(a, b, dims=(((1,), (0,)), ((), ())), precision=None):
    return lax.dot_general(a, b, dims, precision=precision, preferred_element_type=F32)


_NT = (((1,), (1,)), ((), ()))
_TN = (((0,), (0,)), ((), ()))


def _iota(shape, dim):
    return lax.broadcasted_iota(jnp.int32, shape, dim)


def _vdiv(x, n):
    assert n & (n - 1) == 0
    return lax.shift_right_logical(x, n.bit_length() - 1)


def _vmod(x, n):
    assert n & (n - 1) == 0
    return jnp.bitwise_and(x, n - 1)


def _silu(x):
    return x * jax.nn.sigmoid(x)


def _softplus(x):
    return jnp.maximum(x, 0.0) + jnp.log1p(jnp.exp(-jnp.abs(x)))


def _pack_rows(x):
    w = x.shape[1] // 2
    lo = lax.bitcast_convert_type(x[:, :w].astype(BF16).astype(F32), jnp.uint32)
    hi = lax.bitcast_convert_type(x[:, w:].astype(BF16).astype(F32), jnp.uint32)
    return lax.bitcast_convert_type(lax.shift_right_logical(lo, jnp.uint32(16)) | hi, jnp.int32)


def _unpack_rows(words):
    u = lax.bitcast_convert_type(words, jnp.uint32)
    a = lax.bitcast_convert_type(lax.shift_left(u, jnp.uint32(16)), F32)
    b = lax.bitcast_convert_type(u & jnp.uint32(0xFFFF0000), F32)
    return jnp.concatenate([a, b], axis=-1)


def _seg_sum64(x):
    first = _iota((1, LANES), 1) < 64
    outs = []
    for j in range(x.shape[-1] // LANES):
        blk = x[:, j * LANES:(j + 1) * LANES]
        lo = jnp.sum(jnp.where(first, blk, 0.0), axis=-1, keepdims=True)
        hi = jnp.sum(jnp.where(first, 0.0, blk), axis=-1, keepdims=True)
        outs.append(jnp.where(first, lo, hi))
    return jnp.concatenate(outs, axis=-1)


def _block_diag(x, reps, row_blk, col_blk):
    t = jnp.concatenate([x] * reps, axis=0)
    keep = _vdiv(_iota(t.shape, 0), row_blk) == _vdiv(_iota(t.shape, 1), col_blk)
    return jnp.where(keep, t, 0.0).astype(BF16)


def _rope_kernel(pos_ref, cos_ref, sin_ref):
    lane = _iota((1, RET_HEADS * RET_DK), 1)
    half = RET_DK // 2
    k = _vmod(lane, half).astype(F32)
    inv_freq = jnp.exp(k * (-math.log(ROPE_BASE) / half))
    ang = pos_ref[...].astype(F32) * inv_freq
    first = _vmod(lane, RET_DK) < half
    cos_ref[...] = jnp.cos(ang)
    sin_ref[...] = jnp.where(first, -1.0, 1.0) * jnp.sin(ang)


def _rope_tables(pos_col):
    t = pos_col.shape[0]
    tm = 512
    w = RET_HEADS * RET_DK
    return pl.pallas_call(
        _rope_kernel,
        grid=(t // tm,),
        in_specs=[pl.BlockSpec((tm, 1), lambda i: (i, 0))],
        out_specs=[pl.BlockSpec((tm, w), lambda i: (i, 0))] * 2,
        out_shape=[jax.ShapeDtypeStruct((t, w), F32)] * 2,
        compiler_params=pltpu.CompilerParams(dimension_semantics=("arbitrary",)),
        name="rope_tables",
    )(pos_col)


def _inproj_kernel(x_ref, w_ref, o_ref):
    o_ref[...] = _dot(x_ref[...].astype(BF16), w_ref[...])


def _inproj(x2, w_p):
    t = x2.shape[0]
    return pl.pallas_call(
        _inproj_kernel,
        grid=(t // TM_PROJ,),
        in_specs=[pl.BlockSpec((TM_PROJ, D_MODEL), lambda i: (i, 0)),
                  pl.BlockSpec((D_MODEL, NP), lambda i: (0, 0))],
        out_specs=pl.BlockSpec((TM_PROJ, NP), lambda i: (i, 0)),
        out_shape=jax.ShapeDtypeStruct((t, NP), F32),
        compiler_params=pltpu.CompilerParams(dimension_semantics=("arbitrary",),
                                             vmem_limit_bytes=VMEM_LIMIT),
        name="inproj",
    )(x2, w_p)


def _mixer_kernel(proj_ref, cos_ref, sin_ref, retw_ref, convw_ref, convb_ref, dtb_ref, alog_ref,
                  dskip_ref, ssdw_ref, wgk_ref, bgk_ref, glaw_ref, h_ref,
                  ret_s, ssd_s, gla_s, stage, xact):
    C = CHUNK

    @pl.when(pl.program_id(1) == 0)
    def _():
        ret_s[...] = jnp.zeros_like(ret_s)
        ssd_s[...] = jnp.zeros_like(ssd_s)
        gla_s[...] = jnp.zeros_like(gla_s)
        stage[0:8, :] = jnp.zeros((8, SSD_XBC), F32)

    stage[8:8 + TS, :] = proj_ref[:, COL["sxbc"]:COL["sxbc"] + SSD_XBC]
    acc = convb_ref[...] + convw_ref[0:1, :] * stage[5:5 + TS, :]
    for j in range(1, SSD_CONV):
        acc = acc + convw_ref[j:j + 1, :] * stage[5 + j:5 + j + TS, :]
    xact[...] = _silu(acc)
    stage[0:8, :] = stage[TS:TS + 8, :]

    lane256 = _iota((1, 256), 1)
    head = _vdiv(lane256, 64).astype(F32)
    log_gamma = jnp.log(1.0 - jnp.exp((-5.0 - head) * math.log(2.0)))
    row = _iota((C, 1), 0).astype(F32)
    dist = row - _vmod(lane256, 64).astype(F32)
    ret_intra = jnp.where(dist >= 0, jnp.exp(log_gamma * jnp.maximum(dist, 0.0)), 0.0)
    ret_qdec = jnp.exp(log_gamma * (row + 1.0))
    ret_kdec = jnp.exp(log_gamma * (C - 1.0 - row))
    ret_cdec = jnp.exp(log_gamma * C)
    first_half = _vmod(lane256, RET_DK) < (RET_DK // 2)

    tri = (_iota((C, C), 0) >= _iota((C, C), 1)).astype(F32)
    causal4 = _iota((C, 256), 0) >= _vmod(_iota((C, 256), 1), 64)
    causal8 = _iota((C, 512), 0) >= _vmod(_iota((C, 512), 1), 64)
    eye8 = _iota((C, 512), 0) == _vmod(_iota((C, 512), 1), 64)

    a_neg = -jnp.exp(alog_ref[...])

    def rot(t, cos, sin):
        sw = jnp.where(first_half, pltpu.roll(t, 256 - 32, 1), pltpu.roll(t, 32, 1))
        return t * cos + sw * sin

    def chunk(c, carry):
        r0 = pl.multiple_of(c * C, C)

        def seg(name, width):
            return proj_ref[pl.ds(r0, C), COL[name]:COL[name] + width]

        cos = cos_ref[pl.ds(r0, C), :]
        sin = sin_ref[pl.ds(r0, C), :]
        q = rot(seg("rq", 256), cos, sin)
        k = rot(seg("rk", 256), cos, sin) * (RET_DK ** -0.5)
        v = seg("rv", 256)
        vb = v.astype(BF16)
        kbd = _block_diag(k, RET_HEADS, C, RET_DK)
        scores = _dot(q.astype(BF16), kbd, _NT) * ret_intra
        vbd = _block_diag(v, RET_HEADS, C, RET_DV)
        s_prev = ret_s[...]
        o = _dot(scores.astype(BF16), vbd) + _dot((q * ret_qdec).astype(BF16), s_prev.astype(BF16))
        contrib = _dot((k * ret_kdec).astype(BF16), vb, _TN)
        keep = _vdiv(_iota((256, 256), 0), RET_DK) == _vdiv(_iota((256, 256), 1), RET_DV)
        ret_s[...] = jnp.where(keep, ret_cdec * s_prev + contrib, 0.0)
        mu = _seg_sum64(o) * (1.0 / RET_DV)
        oc = o - mu
        var = _seg_sum64(oc * oc) * (1.0 / RET_DV)
        o = oc * lax.rsqrt(var + LN_EPS) * retw_ref[...]
        h_ref[pl.ds(r0, C), 0:RET_W] = (_silu(seg("rg", 256)) * o).astype(BF16)

        xs = xact[pl.ds(r0, C), 0:SSD_W]
        bm = xact[pl.ds(r0, C), SSD_W:SSD_W + SSD_BC]
        cm = xact[pl.ds(r0, C), SSD_W + SSD_BC:SSD_XBC]
        cmb = cm.astype(BF16)
        dt = _softplus(seg("sdt", SSD_W) + dtb_ref[...])
        acum = _dot(tri, dt * a_neg, precision=lax.Precision.HIGHEST)
        arow = jnp.sum(jnp.where(eye8, acum, 0.0), axis=0, keepdims=True)
        decay = jnp.exp(jnp.where(causal8, acum - arow, NEG_BIG))
        b8 = jnp.concatenate([bm] * SSD_HEADS, axis=0)
        keep_b = _vdiv(_iota(b8.shape, 0), C * SSD_HEADS // SSD_GROUPS) == _vdiv(_iota(b8.shape, 1), SSD_STATE)
        b8 = jnp.where(keep_b, b8, 0.0).astype(BF16)
        cb = _dot(cmb, b8, _NT)
        m = (cb * decay).astype(BF16)
        xdt = xs * dt
        s2 = ssd_s[...]
        half = SSD_W // SSD_GROUPS
        ys = []
        for g in range(SSD_GROUPS):
            xbd = _block_diag(xdt[:, g * half:(g + 1) * half], SSD_HEADS // SSD_GROUPS, C, SSD_HEAD_DIM)
            ys.append(_dot(m[:, g * half:(g + 1) * half], xbd))
        y = jnp.concatenate(ys, axis=-1)
        y = y + _dot(cmb, s2.astype(BF16)) * jnp.exp(acum)
        y = y + dskip_ref[...] * xs
        a_last = acum[C - 1:C, :]
        sd = jnp.exp(a_last - acum)
        contrib_s = _dot(bm.astype(BF16), (xdt * sd).astype(BF16), _TN)
        keep_s = _vdiv(_iota(s2.shape, 0), SSD_STATE) == _vdiv(_iota(s2.shape, 1), half)
        ssd_s[...] = jnp.where(keep_s, s2 * jnp.exp(a_last) + contrib_s, 0.0)
        yz = y * _silu(seg("sz", SSD_W))
        outs = []
        for g in range(SSD_GROUPS):
            blk = yz[:, g * half:(g + 1) * half]
            ms = jnp.mean(blk * blk, axis=-1, keepdims=True)
            outs.append(blk * lax.rsqrt(ms + NORM_EPS))
        h_ref[pl.ds(r0, C), RET_W:RET_W + SSD_W] = (jnp.concatenate(outs, axis=-1) * ssdw_ref[...]).astype(BF16)

        gq = seg("gq", GLA_QK) * (GLA_DK ** -0.5)
        gkk = seg("gk", GLA_QK)
        gv = seg("gv", GLA_W)
        gkl = _dot(seg("ggk", 128).astype(BF16), wgk_ref[...].astype(BF16)) + bgk_ref[...]
        log_a = (jnp.minimum(gkl, 0.0) - jnp.log1p(jnp.exp(-jnp.abs(gkl)))) * (1.0 / GLA_TEMP)
        b = _dot(tri, log_a, precision=lax.Precision.HIGHEST)
        q_t = (gq * jnp.exp(b)).astype(BF16)
        k_t = gkk * jnp.exp(-b)
        kbd_g = _block_diag(k_t, GLA_HEADS, C, GLA_DK)
        att = jnp.where(causal4, _dot(q_t, kbd_g, _NT), 0.0)
        vbd_g = _block_diag(gv, GLA_HEADS, C, GLA_DV)
        st = gla_s[...]
        og = _dot(att.astype(BF16), vbd_g) + _dot(q_t, st.astype(BF16), _NT)
        b_last = b[C - 1:C, :]
        kd = (gkk * jnp.exp(b_last - b)).astype(BF16)
        contrib_g = _dot(gv.astype(BF16), kd, _TN)
        keep_g = _vdiv(_iota(st.shape, 0), GLA_DV) == _vdiv(_iota(st.shape, 1), GLA_DK)
        gla_s[...] = jnp.where(keep_g, st * jnp.exp(b_last) + contrib_g, 0.0)
        ms = _seg_sum64(og * og) * (1.0 / GLA_DV)
        og = og * lax.rsqrt(ms + NORM_EPS) * glaw_ref[...]
        h_ref[pl.ds(r0, C), RET_W + SSD_W:D_MIX] = (_silu(seg("gg", GLA_W)) * og).astype(BF16)
        return carry

    lax.fori_loop(0, TS // C, chunk, 0)


def _mixer(proj, cos_t, sin_t, params, batch, seq):
    n_s = seq // TS
    row_map = lambda b, s: (b * n_s + s, 0)
    const = lambda b, s: (0, 0)
    specs = [pl.BlockSpec((TS, NP), row_map),
             pl.BlockSpec((TS, 256), row_map),
             pl.BlockSpec((TS, 256), row_map)]
    specs += [pl.BlockSpec(p.shape, const) for p in params]
    return pl.pallas_call(
        _mixer_kernel,
        grid=(batch, n_s),
        in_specs=specs,
        out_specs=pl.BlockSpec((TS, D_MIX), row_map),
        out_shape=jax.ShapeDtypeStruct((batch * seq, D_MIX), BF16),
        scratch_shapes=[pltpu.VMEM((256, 256), F32),
                        pltpu.VMEM((SSD_BC, SSD_W), F32),
                        pltpu.VMEM((GLA_W, GLA_QK), F32),
                        pltpu.VMEM((TS + 8, SSD_XBC), F32),
                        pltpu.VMEM((TS, SSD_XBC), F32)],
        compiler_params=pltpu.CompilerParams(dimension_semantics=("arbitrary", "arbitrary"),
                                             vmem_limit_bytes=VMEM_LIMIT),
        name="mixer",
    )(proj, cos_t, sin_t, *params)


def _layer_norm(y, g, b):
    mu = jnp.mean(y, axis=-1, keepdims=True)
    yc = y - mu
    var = jnp.mean(yc * yc, axis=-1, keepdims=True)
    return yc * lax.rsqrt(var + LN_EPS) * g + b


def _post_kernel(h_ref, x_ref, wout_ref, g_ref, b_ref, wr_ref, br_ref,
                 x1_ref, x1p_ref, mi_ref, mf_ref, cnt_ref, carry):
    tm = TM_POST

    @pl.when(pl.program_id(0) == 0)
    def _():
        carry[...] = jnp.zeros_like(carry)

    mix = _dot(h_ref[...], wout_ref[...])
    x1 = _layer_norm(DEEPNORM_ALPHA * x_ref[...] + mix, g_ref[...], b_ref[...])
    x1_ref[...] = x1
    x1p_ref[...] = _pack_rows(x1)

    logits = _dot(x1, wr_ref[...], precision=lax.Precision.HIGHEST) + br_ref[...]
    lane_i = _iota((tm, LANES), 1)
    lane = lane_i.astype(F32)
    work = logits
    vals, idxs = [], []
    multi = jnp.zeros((tm, LANES), F32)
    for _ in range(TOP_K):
        m = jnp.max(work, axis=-1, keepdims=True)
        idx = jnp.min(jnp.where(work == m, lane, float(LANES)), axis=-1, keepdims=True)
        hit = lane == idx
        multi = multi + hit.astype(F32)
        work = jnp.where(hit, -jnp.inf, work)
        vals.append(m)
        idxs.append(idx)
    exps = [jnp.exp(v - vals[0]) for v in vals]
    denom = exps[0] + exps[1] + exps[2] + exps[3]
    gates = [e / denom for e in exps]

    before = (_iota((tm, tm), 0) > _iota((tm, tm), 1)).astype(BF16)
    prior = _dot(before, multi.astype(BF16)) + carry[...]
    mi = jnp.zeros((tm, LANES), jnp.int32)
    mf = jnp.zeros((tm, LANES), F32)
    for kk in range(TOP_K):
        rank = jnp.sum(jnp.where(lane == idxs[kk], prior, 0.0), axis=-1, keepdims=True)
        mi = jnp.where(lane_i == kk, idxs[kk].astype(jnp.int32), mi)
        mi = jnp.where(lane_i == TOP_K + kk, rank.astype(jnp.int32), mi)
        mf = jnp.where(lane_i == kk, gates[kk], mf)
    mi_ref[...] = mi
    mf_ref[...] = mf
    carry[...] = carry[...] + jnp.sum(multi, axis=0, keepdims=True)
    cnt_ref[...] = jnp.broadcast_to(carry[...], cnt_ref.shape)


def _post(h, x2, w_out_b, ln_g, ln_b, wr_p, br_p):
    t = x2.shape[0]
    tm = TM_POST
    row = lambda i: (i, 0)
    const = lambda i: (0, 0)
    return pl.pallas_call(
        _post_kernel,
        grid=(t // tm,),
        in_specs=[pl.BlockSpec((tm, D_MIX), row), pl.BlockSpec((tm, D_MODEL), row),
                  pl.BlockSpec((D_MIX, D_MODEL), const), pl.BlockSpec((1, D_MODEL), const),
                  pl.BlockSpec((1, D_MODEL), const), pl.BlockSpec((D_MODEL, LANES), const),
                  pl.BlockSpec((1, LANES), const)],
        out_specs=[pl.BlockSpec((tm, D_MODEL), row), pl.BlockSpec((tm, ROW_WORDS), row),
                   pl.BlockSpec((tm, LANES), row), pl.BlockSpec((tm, LANES), row),
                   pl.BlockSpec((8, LANES), const)],
        out_shape=[jax.ShapeDtypeStruct((t, D_MODEL), F32), jax.ShapeDtypeStruct((t, ROW_WORDS), jnp.int32),
                   jax.ShapeDtypeStruct((t, LANES), jnp.int32), jax.ShapeDtypeStruct((t, LANES), F32),
                   jax.ShapeDtypeStruct((8, LANES), F32)],
        scratch_shapes=[pltpu.VMEM((1, LANES), F32)],
        compiler_params=pltpu.CompilerParams(dimension_semantics=("arbitrary",),
                                             vmem_limit_bytes=VMEM_LIMIT),
        name="outproj_ln_router",
    )(h, x2, w_out_b, ln_g, ln_b, wr_p, br_p)


def _ffn_kernel(be_ref, nv_ref, x_ref, wg_ref, bg_ref, wu_ref, bu_ref, wd_ref, bd_ref, o_ref,
                wg_b, wu_b, wd_b):
    i = pl.program_id(0)
    valid = i < nv_ref[0]
    e = be_ref[i]
    prev = be_ref[jnp.maximum(i - 1, 0)]
    fresh = jnp.logical_or(i == 0, e != prev)

    @pl.when(jnp.logical_and(valid, fresh))
    def _():
        wg_b[...] = wg_ref[0, 0].astype(BF16)
        wu_b[...] = wu_ref[0, 0].astype(BF16)
        wd_b[...] = wd_ref[0, 0].astype(BF16)

    @pl.when(valid)
    def _():
        x = _unpack_rows(x_ref[...]).astype(BF16)
        hg = jnp.minimum(_dot(x, wg_b[...]) + bg_ref[0, 0], SWIGLU_LIMIT)
        hu = jnp.clip(_dot(x, wu_b[...]) + bu_ref[0, 0], -SWIGLU_LIMIT, SWIGLU_LIMIT)
        hh = (hu + 1.0) * hg * jax.nn.sigmoid(SWIGLU_ALPHA * hg)
        o_ref[...] = _pack_rows(_dot(hh.astype(BF16), wd_b[...]) + bd_ref[0, 0])


def _ffn(layer, block_expert, n_valid, xin, wg, bg, wu, bu, wd, bd):
    cap = xin.shape[0]
    nb = cap // BM

    def blk(i, be, nv):
        return jnp.maximum(jnp.minimum(i, nv[0] - 1), 0)

    row = lambda i, be, nv: (blk(i, be, nv), 0)
    wmap = lambda i, be, nv: (layer, be[blk(i, be, nv)], 0, 0)
    grid_spec = pltpu.PrefetchScalarGridSpec(
        num_scalar_prefetch=2,
        grid=(nb,),
        in_specs=[pl.BlockSpec((BM, ROW_WORDS), row),
                  pl.BlockSpec((1, 1, D_MODEL, D_FF), wmap), pl.BlockSpec((1, 1, 1, D_FF), wmap),
                  pl.BlockSpec((1, 1, D_MODEL, D_FF), wmap), pl.BlockSpec((1, 1, 1, D_FF), wmap),
                  pl.BlockSpec((1, 1, D_FF, D_MODEL), wmap), pl.BlockSpec((1, 1, 1, D_MODEL), wmap)],
        out_specs=pl.BlockSpec((BM, ROW_WORDS), row),
        scratch_shapes=[pltpu.VMEM((D_MODEL, D_FF), BF16), pltpu.VMEM((D_MODEL, D_FF), BF16),
                        pltpu.VMEM((D_FF, D_MODEL), BF16)],
    )
    return pl.pallas_call(
        _ffn_kernel,
        grid_spec=grid_spec,
        out_shape=jax.ShapeDtypeStruct((cap, ROW_WORDS), jnp.int32),
        compiler_params=pltpu.CompilerParams(dimension_semantics=("arbitrary",),
                                             vmem_limit_bytes=VMEM_LIMIT),
        name="expert_ffn",
    )(block_expert, n_valid, xin, wg, bg, wu, bu, wd, bd)


def _sc_gather(table, idx3):
    nw, n_chunks, ch = idx3.shape
    width = table.shape[1]
    per_worker = n_chunks * ch
    mesh = plsc.VectorSubcoreMesh(core_axis_name="c", subcore_axis_name="s")
    n_cores = mesh.num_cores
    assert nw == n_cores * mesh.num_subcores and n_chunks % 2 == 0 and ch == SC_CHUNK

    def body(table_hbm, idx_hbm, out_hbm, idx_v, rows0, rows1, sem_g0, sem_g1, sem_w0, sem_w1):
        wid = lax.axis_index("s") * n_cores + lax.axis_index("c")
        base = wid * per_worker
        pltpu.sync_copy(idx_hbm.at[wid], idx_v)

        @pl.loop(0, n_chunks, step=2)
        def _(c):
            g0 = pltpu.async_copy(table_hbm.at[idx_v.at[c]], rows0, sem_g0)
            g1 = pltpu.async_copy(table_hbm.at[idx_v.at[c + 1]], rows1, sem_g1)
            g0.wait()
            w0 = pltpu.async_copy(rows0, out_hbm.at[pl.ds(base + c * ch, ch)], sem_w0)
            g1.wait()
            w1 = pltpu.async_copy(rows1, out_hbm.at[pl.ds(base + (c + 1) * ch, ch)], sem_w1)
            w0.wait()
            w1.wait()

    return pl.kernel(
        body,
        out_type=jax.ShapeDtypeStruct((nw * per_worker, width), table.dtype),
        mesh=mesh,
        scratch_types=[pltpu.VMEM((n_chunks, ch), jnp.int32),
                       pltpu.VMEM((ch, width), table.dtype), pltpu.VMEM((ch, width), table.dtype),
                       pltpu.SemaphoreType.DMA, pltpu.SemaphoreType.DMA,
                       pltpu.SemaphoreType.DMA, pltpu.SemaphoreType.DMA],
        name="sc_row_gather",
    )(table, idx3)


def _sc_scatter(rows, idx3, n_out):
    nw, n_lists, ch = idx3.shape
    n_chunks = n_lists // TOP_K
    width = rows.shape[1]
    per_worker = n_chunks * ch
    mesh = plsc.VectorSubcoreMesh(core_axis_name="c", subcore_axis_name="s")
    n_cores = mesh.num_cores
    assert nw == n_cores * mesh.num_subcores and n_chunks % 2 == 0 and ch == SC_CHUNK
    assert nw * per_worker == rows.shape[0]

    def body(rows_hbm, idx_hbm, out_hbm, idx_v, buf0, buf1, sem_r0, sem_r1, sem_w0, sem_w1):
        wid = lax.axis_index("s") * n_cores + lax.axis_index("c")
        base = wid * per_worker
        pltpu.sync_copy(idx_hbm.at[wid], idx_v)

        @pl.loop(0, n_chunks, step=2)
        def _(c):
            r0 = pltpu.async_copy(rows_hbm.at[pl.ds(base + c * ch, ch)], buf0, sem_r0)
            r1 = pltpu.async_copy(rows_hbm.at[pl.ds(base + (c + 1) * ch, ch)], buf1, sem_r1)
            r0.wait()
            w0 = [pltpu.async_copy(buf0, out_hbm.at[idx_v.at[c * TOP_K + k]], sem_w0) for k in range(TOP_K)]
            r1.wait()
            w1 = [pltpu.async_copy(buf1, out_hbm.at[idx_v.at[(c + 1) * TOP_K + k]], sem_w1) for k in range(TOP_K)]
            for w in w0 + w1:
                w.wait()

    return pl.kernel(
        body,
        out_type=jax.ShapeDtypeStruct((n_out, width), rows.dtype),
        mesh=mesh,
        scratch_types=[pltpu.VMEM((n_lists, ch), jnp.int32),
                       pltpu.VMEM((ch, width), rows.dtype), pltpu.VMEM((ch, width), rows.dtype),
                       pltpu.SemaphoreType.DMA, pltpu.SemaphoreType.DMA,
                       pltpu.SemaphoreType.DMA, pltpu.SemaphoreType.DMA],
        name="sc_row_scatter",
    )(rows, idx3)


def _combine_kernel(x_ref, rows_ref, mf_ref, g_ref, b_ref, o_ref):
    mf = mf_ref[...]
    y = DEEPNORM_ALPHA * x_ref[...]
    for kk in range(TOP_K):
        y = y + mf[:, kk:kk + 1] * _unpack_rows(rows_ref[kk])
    o_ref[...] = _layer_norm(y, g_ref[...], b_ref[...])


def _combine(x1, rows, mf, ln_g, ln_b):
    t = x1.shape[0]
    tm = TM_COMB
    row = lambda i: (i, 0)
    const = lambda i: (0, 0)
    return pl.pallas_call(
        _combine_kernel,
        grid=(t // tm,),
        in_specs=[pl.BlockSpec((tm, D_MODEL), row),
                  pl.BlockSpec((TOP_K, tm, ROW_WORDS), lambda i: (0, i, 0)),
                  pl.BlockSpec((tm, LANES), row),
                  pl.BlockSpec((1, D_MODEL), const), pl.BlockSpec((1, D_MODEL), const)],
        out_specs=pl.BlockSpec((tm, D_MODEL), row),
        out_shape=jax.ShapeDtypeStruct((t, D_MODEL), F32),
        compiler_params=pltpu.CompilerParams(dimension_semantics=("arbitrary",),
                                             vmem_limit_bytes=VMEM_LIMIT),
        name="combine_ln",
    )(x1, rows, mf, ln_g, ln_b)


def _relayout_w_in(w):
    widths = (256, 256, 256, 256, SSD_W, SSD_XBC, SSD_HEADS, GLA_QK, GLA_QK, GLA_W, GLA_RANK, GLA_W)
    offs = [0]
    for wd in widths:
        offs.append(offs[-1] + wd)
    parts = [w[:, offs[i]:offs[i + 1]] for i in range(len(widths))]
    parts[6] = jnp.repeat(parts[6], SSD_HEAD_DIM, axis=1)
    parts[10] = jnp.pad(parts[10], ((0, 0), (0, LANES - GLA_RANK)))
    return jnp.concatenate(parts, axis=1).astype(BF16)


def _rep_heads(p):
    return jnp.repeat(p, SSD_HEAD_DIM)[None, :]


def kernel(x, positions, w_in, w_out, ret_norm_w, ssd_conv_w, ssd_conv_b, ssd_dt_bias, ssd_a_log, ssd_d,
           ssd_norm_w, gla_w_gk2, gla_b_gk2, gla_norm_w, ln1_g, ln1_b, w_router, b_router, w_gate, b_gate,
           w_up, b_up, w_down, b_down, ln2_g, ln2_b):
    batch, seq, d = x.shape
    t = batch * seq
    depth = w_in.shape[0]
    assert d == D_MODEL and seq % TS == 0 and t % TM_PROJ == 0 and t % TM_COMB == 0
    n_assign = t * TOP_K
    nb = n_assign // BM + N_EXPERTS
    cap = nb * BM

    cos_t, sin_t = _rope_tables(positions.reshape(t, 1))
    x2 = x.reshape(t, d)

    for l in range(depth):
        proj = _inproj(x2, _relayout_w_in(w_in[l]))
        params = (ret_norm_w[l][None, :], ssd_conv_w[l], ssd_conv_b[l][None, :], _rep_heads(ssd_dt_bias[l]),
                  _rep_heads(ssd_a_log[l]), _rep_heads(ssd_d[l]), ssd_norm_w[l][None, :],
                  jnp.pad(gla_w_gk2[l], ((0, LANES - GLA_RANK), (0, 0))), gla_b_gk2[l][None, :],
                  gla_norm_w[l][None, :])
        h = _mixer(proj, cos_t, sin_t, params, batch, seq)

        wr_p = jnp.pad(w_router[l], ((0, 0), (0, LANES - N_EXPERTS)))
        br_p = jnp.pad(b_router[l], (0, LANES - N_EXPERTS), constant_values=NEG_BIG)[None, :]
        x1, x1p, mi, mf, cnt = _post(h, x2, w_out[l].astype(BF16), ln1_g[l][None, :], ln1_b[l][None, :],
                                     wr_p, br_p)

        counts = cnt[0, :N_EXPERTS].astype(jnp.int32)
        padded = (counts + BM - 1) // BM * BM
        end_padded = jnp.cumsum(padded)
        start_padded = end_padded - padded
        top_idx = mi[:, :TOP_K]
        dest = start_padded[top_idx] + mi[:, TOP_K:2 * TOP_K]
        block_start = jnp.arange(nb, dtype=jnp.int32) * BM
        block_expert = jnp.minimum(jnp.sum((end_padded[None, :] <= block_start[:, None]).astype(jnp.int32), axis=1),
                                   N_EXPERTS - 1)
        n_valid = (end_padded[-1:] // BM).astype(jnp.int32)

        scatter_idx = dest.reshape(SC_WORKERS, -1, SC_CHUNK, TOP_K).transpose(0, 1, 3, 2)
        xin = _sc_scatter(x1p, scatter_idx.reshape(SC_WORKERS, -1, SC_CHUNK), cap)
        yb = _ffn(l, block_expert, n_valid, xin, w_gate, b_gate[:, :, None, :], w_up, b_up[:, :, None, :],
                  w_down, b_down[:, :, None, :])
        rows = _sc_gather(yb, dest.T.reshape(SC_WORKERS, -1, SC_CHUNK))
        x2 = _combine(x1, rows.reshape(TOP_K, t, ROW_WORDS), mf, ln2_g[l][None, :], ln2_b[l][None, :])
    return x2.reshape(batch, seq, d)
```

```python
import functools
import math

import jax
import jax.numpy as jnp
from jax import lax
from jax.experimental import pallas as pl
from jax.experimental.pallas import tpu as pltpu
from jax.experimental.pallas import tpu_sc as plsc

F32 = jnp.float32
BF16 = jnp.bfloat16

D_MODEL = 1024
CHUNK = 64
RET_HEADS, RET_DK, RET_DV = 4, 64, 64
RET_W = RET_HEADS * RET_DV
SSD_HEADS, SSD_HEAD_DIM, SSD_STATE, SSD_GROUPS, SSD_CONV = 8, 64, 64, 2, 4
SSD_W = SSD_HEADS * SSD_HEAD_DIM
SSD_BC = SSD_GROUPS * SSD_STATE
SSD_XBC = SSD_W + 2 * SSD_BC
GLA_HEADS, GLA_DK, GLA_DV, GLA_RANK, GLA_TEMP = 4, 32, 64, 16, 16.0
GLA_QK = GLA_HEADS * GLA_DK
GLA_W = GLA_HEADS * GLA_DV
D_MIX = RET_W + SSD_W + GLA_W
N_EXPERTS, TOP_K, D_FF = 32, 4, 1024
SWIGLU_LIMIT, SWIGLU_ALPHA = 7.0, 1.702
ROPE_BASE = 10000.0
LN_EPS, NORM_EPS = 1e-5, 1e-6
DEPTH = 2
DEEPNORM_ALPHA = (2.0 * DEPTH) ** 0.25

LANES = 128
NEG_BIG = -1e30
VMEM_LIMIT = 56 * 1024 * 1024

_SEGS = (("rq", 256), ("rk", 256), ("rv", 256), ("rg", 256), ("sz", SSD_W), ("sxbc", SSD_XBC),
         ("sdt", SSD_W), ("gq", 128), ("gk", 128), ("gv", 256), ("ggk", 128), ("gg", 256))
COL = {}
_off = 0
for _n, _w in _SEGS:
    COL[_n] = _off
    _off += _w
NP = _off

TS = 256
MIX_G = 4
TM_PROJ = 512
TM_POST = 512
TM_COMB = 512
BM = 256
ROW_WORDS = D_MODEL // 2
SC_WORKERS = 32
SC_CHUNK = 64


def _dot(a, b, dims=(((1,), (0,)), ((), ())), precision=None):
    return lax.dot_general(a, b, dims, precision=precision, preferred_element_type=F32)


_NT = (((1,), (1,)), ((), ()))
_TN = (((0,), (0,)), ((), ()))


def _iota(shape, dim):
    return lax.broadcasted_iota(jnp.int32, shape, dim)


def _vdiv(x, n):
    assert n & (n - 1) == 0
    return lax.shift_right_logical(x, n.bit_length() - 1)


def _vmod(x, n):
    assert n & (n - 1) == 0
    return jnp.bitwise_and(x, n - 1)


def _silu(x):
    return x * jax.nn.sigmoid(x)


def _softplus(x):
    return jnp.maximum(x, 0.0) + jnp.log(1.0 + jnp.exp(-jnp.abs(x)))


def _pack_rows(x):
    w = x.shape[1] // 2
    lo = lax.bitcast_convert_type(x[:, :w].astype(BF16).astype(F32), jnp.uint32)
    hi = lax.bitcast_convert_type(x[:, w:].astype(BF16).astype(F32), jnp.uint32)
    return lax.bitcast_convert_type(lax.shift_right_logical(lo, jnp.uint32(16)) | hi, jnp.int32)


def _unpack_rows(words):
    u = lax.bitcast_convert_type(words, jnp.uint32)
    a = lax.bitcast_convert_type(lax.shift_left(u, jnp.uint32(16)), F32)
    b = lax.bitcast_convert_type(u & jnp.uint32(0xFFFF0000), F32)
    return jnp.concatenate([a, b], axis=-1)


def _seg_sum64(x):
    first = _iota((1, LANES), 1) < 64
    outs = []
    for j in range(x.shape[-1] // LANES):
        blk = x[:, j * LANES:(j + 1) * LANES]
        lo = jnp.sum(jnp.where(first, blk, 0.0), axis=-1, keepdims=True)
        hi = jnp.sum(jnp.where(first, 0.0, blk), axis=-1, keepdims=True)
        outs.append(jnp.where(first, lo, hi))
    return jnp.concatenate(outs, axis=-1)


def _block_diag(x, reps, row_blk, col_blk):
    t = jnp.concatenate([x] * reps, axis=0)
    keep = _vdiv(_iota(t.shape, 0), row_blk) == _vdiv(_iota(t.shape, 1), col_blk)
    return jnp.where(keep, t, 0.0).astype(BF16)


def _rope_kernel(pos_ref, cos_ref, sin_ref):
    lane = _iota((1, RET_HEADS * RET_DK), 1)
    half = RET_DK // 2
    k = _vmod(lane, half).astype(F32)
    inv_freq = jnp.exp(k * (-math.log(ROPE_BASE) / half))
    ang = pos_ref[...].astype(F32) * inv_freq
    first = _vmod(lane, RET_DK) < half
    cos_ref[...] = jnp.cos(ang)
    sin_ref[...] = jnp.where(first, -1.0, 1.0) * jnp.sin(ang)


def _rope_tables(pos_col):
    t = pos_col.shape[0]
    tm = 512
    w = RET_HEADS * RET_DK
    return pl.pallas_call(
        _rope_kernel,
        grid=(t // tm,),
        in_specs=[pl.BlockSpec((tm, 1), lambda i: (i, 0))],
        out_specs=[pl.BlockSpec((tm, w), lambda i: (i, 0))] * 2,
        out_shape=[jax.ShapeDtypeStruct((t, w), F32)] * 2,
        compiler_params=pltpu.CompilerParams(dimension_semantics=("arbitrary",)),
        name="rope_tables",
    )(pos_col)


def _inproj_kernel(x_ref, w_ref, o_ref):
    o_ref[...] = _dot(x_ref[...].astype(BF16), w_ref[...])


def _inproj(x2, w_p):
    t = x2.shape[0]
    return pl.pallas_call(
        _inproj_kernel,
        grid=(t // TM_PROJ,),
        in_specs=[pl.BlockSpec((TM_PROJ, D_MODEL), lambda i: (i, 0)),
                  pl.BlockSpec((D_MODEL, NP), lambda i: (0, 0))],
        out_specs=pl.BlockSpec((TM_PROJ, NP), lambda i: (i, 0)),
        out_shape=jax.ShapeDtypeStruct((t, NP), F32),
        compiler_params=pltpu.CompilerParams(dimension_semantics=("arbitrary",),
                                             vmem_limit_bytes=VMEM_LIMIT),
        name="inproj",
    )(x2, w_p)


def _mixer_kernel(proj_ref, cos_ref, sin_ref, retw_ref, convw_ref, convb_ref, dtb_ref, alog_ref,
                  dskip_ref, ssdw_ref, wgk_ref, bgk_ref, glaw_ref, h_ref,
                  ret_s, ssd_s, gla_s, stage, xact):
    C = CHUNK

    @pl.when(pl.program_id(1) == 0)
    def _():
        ret_s[...] = jnp.zeros_like(ret_s)
        ssd_s[...] = jnp.zeros_like(ssd_s)
        gla_s[...] = jnp.zeros_like(gla_s)
        for g in range(MIX_G):
            stage[g, 0:8, :] = jnp.zeros((8, SSD_XBC), F32)

    for g in range(MIX_G):
        stage[g, 8:8 + TS, :] = proj_ref[g, :, COL["sxbc"]:COL["sxbc"] + SSD_XBC]
        acc = convb_ref[...] + convw_ref[0:1, :] * stage[g, 5:5 + TS, :]
        for j in range(1, SSD_CONV):
            acc = acc + convw_ref[j:j + 1, :] * stage[g, 5 + j:5 + j + TS, :]
        xact[g] = _silu(acc)
        stage[g, 0:8, :] = stage[g, TS:TS + 8, :]

    lane256 = _iota((1, 256), 1)
    head = _vdiv(lane256, 64).astype(F32)
    log_gamma = jnp.log(1.0 - jnp.exp((-5.0 - head) * math.log(2.0)))
    row = _iota((C, 1), 0).astype(F32)
    dist = row - _vmod(lane256, 64).astype(F32)
    ret_intra = jnp.where(dist >= 0, jnp.exp(log_gamma * jnp.maximum(dist, 0.0)), 0.0)
    ret_qdec = jnp.exp(log_gamma * (row + 1.0))
    ret_kdec = jnp.exp(log_gamma * (C - 1.0 - row))
    ret_cdec = jnp.exp(log_gamma * C)
    first_half = _vmod(lane256, RET_DK) < (RET_DK // 2)

    tri = (_iota((C, C), 0) >= _iota((C, C), 1)).astype(F32)
    causal4 = _iota((C, 256), 0) >= _vmod(_iota((C, 256), 1), 64)
    causal8 = _iota((C, 512), 0) >= _vmod(_iota((C, 512), 1), 64)
    eye8 = _iota((C, 512), 0) == _vmod(_iota((C, 512), 1), 64)

    a_neg = -jnp.exp(alog_ref[...])

    def rot(t, cos, sin):
        sw = jnp.where(first_half, pltpu.roll(t, 256 - 32, 1), pltpu.roll(t, 32, 1))
        return t * cos + sw * sin

    def chunk_of(sq, r0):
        def seg(name, width):
            return proj_ref[sq, pl.ds(r0, C), COL[name]:COL[name] + width]

        cos = cos_ref[sq, pl.ds(r0, C), :]
        sin = sin_ref[sq, pl.ds(r0, C), :]
        q = rot(seg("rq", 256), cos, sin)
        k = rot(seg("rk", 256), cos, sin) * (RET_DK ** -0.5)
        v = seg("rv", 256)
        vb = v.astype(BF16)
        kbd = _block_diag(k, RET_HEADS, C, RET_DK)
        scores = _dot(q.astype(BF16), kbd, _NT) * ret_intra
        vbd = _block_diag(v, RET_HEADS, C, RET_DV)
        s_prev = ret_s[sq]
        o = _dot(scores.astype(BF16), vbd) + _dot((q * ret_qdec).astype(BF16), s_prev.astype(BF16))
        contrib = _dot((k * ret_kdec).astype(BF16), vb, _TN)
        keep = _vdiv(_iota((256, 256), 0), RET_DK) == _vdiv(_iota((256, 256), 1), RET_DV)
        ret_s[sq] = jnp.where(keep, ret_cdec * s_prev + contrib, 0.0)
        mu = _seg_sum64(o) * (1.0 / RET_DV)
        oc = o - mu
        var = _seg_sum64(oc * oc) * (1.0 / RET_DV)
        o = oc * lax.rsqrt(var + LN_EPS) * retw_ref[...]
        h_ref[sq, pl.ds(r0, C), 0:RET_W] = (_silu(seg("rg", 256)) * o).astype(BF16)

        xs = xact[sq, pl.ds(r0, C), 0:SSD_W]
        bm = xact[sq, pl.ds(r0, C), SSD_W:SSD_W + SSD_BC]
        cm = xact[sq, pl.ds(r0, C), SSD_W + SSD_BC:SSD_XBC]
        cmb = cm.astype(BF16)
        dt = _softplus(seg("sdt", SSD_W) + dtb_ref[...])
        acum = _dot(tri, dt * a_neg, precision=lax.Precision.HIGHEST)
        arow = jnp.sum(jnp.where(eye8, acum, 0.0), axis=0, keepdims=True)
        decay = jnp.exp(jnp.where(causal8, acum - arow, NEG_BIG))
        b8 = jnp.concatenate([bm] * SSD_HEADS, axis=0)
        keep_b = _vdiv(_iota(b8.shape, 0), C * SSD_HEADS // SSD_GROUPS) == _vdiv(_iota(b8.shape, 1), SSD_STATE)
        b8 = jnp.where(keep_b, b8, 0.0).astype(BF16)
        cb = _dot(cmb, b8, _NT)
        m = (cb * decay).astype(BF16)
        xdt = xs * dt
        s2 = ssd_s[sq]
        half = SSD_W // SSD_GROUPS
        ys = []
        for g in range(SSD_GROUPS):
            xbd = _block_diag(xdt[:, g * half:(g + 1) * half], SSD_HEADS // SSD_GROUPS, C, SSD_HEAD_DIM)
            ys.append(_dot(m[:, g * half:(g + 1) * half], xbd))
        y = jnp.concatenate(ys, axis=-1)
        y = y + _dot(cmb, s2.astype(BF16)) * jnp.exp(acum)
        y = y + dskip_ref[...] * xs
        a_last = acum[C - 1:C, :]
        sd = jnp.exp(a_last - acum)
        contrib_s = _dot(bm.astype(BF16), (xdt * sd).astype(BF16), _TN)
        keep_s = _vdiv(_iota(s2.shape, 0), SSD_STATE) == _vdiv(_iota(s2.shape, 1), half)
        ssd_s[sq] = jnp.where(keep_s, s2 * jnp.exp(a_last) + contrib_s, 0.0)
        yz = y * _silu(seg("sz", SSD_W))
        outs = []
        for g in range(SSD_GROUPS):
            blk = yz[:, g * half:(g + 1) * half]
            ms = jnp.mean(blk * blk, axis=-1, keepdims=True)
            outs.append(blk * lax.rsqrt(ms + NORM_EPS))
        h_ref[sq, pl.ds(r0, C), RET_W:RET_W + SSD_W] = (jnp.concatenate(outs, axis=-1) * ssdw_ref[...]).astype(BF16)

        gq = seg("gq", GLA_QK) * (GLA_DK ** -0.5)
        gkk = seg("gk", GLA_QK)
        gv = seg("gv", GLA_W)
        gkl = _dot(seg("ggk", 128).astype(BF16), wgk_ref[...].astype(BF16)) + bgk_ref[...]
        log_a = -_softplus(-gkl) * (1.0 / GLA_TEMP)
        b = _dot(tri, log_a, precision=lax.Precision.HIGHEST)
        q_t = (gq * jnp.exp(b)).astype(BF16)
        k_t = gkk * jnp.exp(-b)
        kbd_g = _block_diag(k_t, GLA_HEADS, C, GLA_DK)
        att = jnp.where(causal4, _dot(q_t, kbd_g, _NT), 0.0)
        vbd_g = _block_diag(gv, GLA_HEADS, C, GLA_DV)
        st = gla_s[sq]
        og = _dot(att.astype(BF16), vbd_g) + _dot(q_t, st.astype(BF16), _NT)
        b_last = b[C - 1:C, :]
        kd = (gkk * jnp.exp(b_last - b)).astype(BF16)
        contrib_g = _dot(gv.astype(BF16), kd, _TN)
        keep_g = _vdiv(_iota(st.shape, 0), GLA_DV) == _vdiv(_iota(st.shape, 1), GLA_DK)
        gla_s[sq] = jnp.where(keep_g, st * jnp.exp(b_last) + contrib_g, 0.0)
        ms = _seg_sum64(og * og) * (1.0 / GLA_DV)
        og = og * lax.rsqrt(ms + NORM_EPS) * glaw_ref[...]
        h_ref[sq, pl.ds(r0, C), RET_W + SSD_W:D_MIX] = (_silu(seg("gg", GLA_W)) * og).astype(BF16)

    def chunk(c, carry):
        r0 = pl.multiple_of(c * C, C)
        for sq in range(MIX_G):
            chunk_of(sq, r0)
        return carry

    lax.fori_loop(0, TS // C, chunk, 0)


def _mixer(proj, cos_t, sin_t, params, batch, seq):
    assert batch % MIX_G == 0
    row_map = lambda b, s: (b, s, 0)
    const = lambda b, s: (0, 0)
    specs = [pl.BlockSpec((MIX_G, TS, NP), row_map),
             pl.BlockSpec((MIX_G, TS, 256), row_map),
             pl.BlockSpec((MIX_G, TS, 256), row_map)]
    specs += [pl.BlockSpec(p.shape, const) for p in params]
    return pl.pallas_call(
        _mixer_kernel,
        grid=(batch // MIX_G, seq // TS),
        in_specs=specs,
        out_specs=pl.BlockSpec((MIX_G, TS, D_MIX), row_map),
        out_shape=jax.ShapeDtypeStruct((batch, seq, D_MIX), BF16),
        scratch_shapes=[pltpu.VMEM((MIX_G, 256, 256), F32),
                        pltpu.VMEM((MIX_G, SSD_BC, SSD_W), F32),
                        pltpu.VMEM((MIX_G, GLA_W, GLA_QK), F32),
                        pltpu.VMEM((MIX_G, TS + 8, SSD_XBC), F32),
                        pltpu.VMEM((MIX_G, TS, SSD_XBC), F32)],
        compiler_params=pltpu.CompilerParams(dimension_semantics=("arbitrary", "arbitrary"),
                                             vmem_limit_bytes=VMEM_LIMIT),
        name="mixer",
    )(proj, cos_t, sin_t, *params)


def _layer_norm(y, g, b):
    mu = jnp.mean(y, axis=-1, keepdims=True)
    yc = y - mu
    var = jnp.mean(yc * yc, axis=-1, keepdims=True)
    return yc * lax.rsqrt(var + LN_EPS) * g + b


def _post_kernel(h_ref, x_ref, wout_ref, g_ref, b_ref, wrh_ref, wrl_ref, br_ref,
                 x1_ref, x1p_ref, mi_ref, mf_ref, cnt_ref, carry):
    tm = TM_POST

    @pl.when(pl.program_id(0) == 0)
    def _():
        carry[...] = jnp.zeros_like(carry)

    mix = _dot(h_ref[...], wout_ref[...])
    x1 = _layer_norm(DEEPNORM_ALPHA * x_ref[...] + mix, g_ref[...], b_ref[...])
    x1_ref[...] = x1
    x1p_ref[...] = _pack_rows(x1)

    x_hi = x1.astype(BF16)
    x_lo = (x1 - x_hi.astype(F32)).astype(BF16)
    logits = (_dot(x_hi, wrh_ref[...]) + _dot(x_lo, wrh_ref[...]) + _dot(x_hi, wrl_ref[...])
              + br_ref[...])
    lane_i = _iota((tm, LANES), 1)
    lane = lane_i.astype(F32)
    work = logits
    vals, idxs = [], []
    multi = jnp.zeros((tm, LANES), F32)
    for _ in range(TOP_K):
        m = jnp.max(work, axis=-1, keepdims=True)
        idx = jnp.min(jnp.where(work == m, lane, float(LANES)), axis=-1, keepdims=True)
        hit = lane == idx
        multi = multi + hit.astype(F32)
        work = jnp.where(hit, -jnp.inf, work)
        vals.append(m)
        idxs.append(idx)
    exps = [jnp.exp(v - vals[0]) for v in vals]
    denom = exps[0] + exps[1] + exps[2] + exps[3]
    gates = [e / denom for e in exps]

    before = (_iota((tm, tm), 0) > _iota((tm, tm), 1)).astype(BF16)
    prior = _dot(before, multi.astype(BF16)) + carry[...]
    mi = jnp.zeros((tm, LANES), F32)
    mf = jnp.zeros((tm, LANES), F32)
    for kk in range(TOP_K):
        rank = jnp.sum(jnp.where(lane == idxs[kk], prior, 0.0), axis=-1, keepdims=True)
        mi = jnp.where(lane_i == kk, idxs[kk], mi)
        mi = jnp.where(lane_i == TOP_K + kk, rank, mi)
        mf = jnp.where(lane_i == kk, gates[kk], mf)
    mi_ref[...] = jnp.transpose(mi)[0:2 * TOP_K, :].astype(jnp.int32)
    mf_ref[...] = mf
    carry[...] = carry[...] + jnp.sum(multi, axis=0, keepdims=True)
    cnt_ref[...] = jnp.broadcast_to(carry[...], cnt_ref.shape)


def _post(h, x2, w_out_b, ln_g, ln_b, wr_hi, wr_lo, br_p):
    t = x2.shape[0]
    tm = TM_POST
    row = lambda i: (i, 0)
    const = lambda i: (0, 0)
    return pl.pallas_call(
        _post_kernel,
        grid=(t // tm,),
        in_specs=[pl.BlockSpec((tm, D_MIX), row), pl.BlockSpec((tm, D_MODEL), row),
                  pl.BlockSpec((D_MIX, D_MODEL), const), pl.BlockSpec((1, D_MODEL), const),
                  pl.BlockSpec((1, D_MODEL), const), pl.BlockSpec((D_MODEL, LANES), const),
                  pl.BlockSpec((D_MODEL, LANES), const), pl.BlockSpec((1, LANES), const)],
        out_specs=[pl.BlockSpec((tm, D_MODEL), row), pl.BlockSpec((tm, ROW_WORDS), row),
                   pl.BlockSpec((2 * TOP_K, tm), lambda i: (0, i)), pl.BlockSpec((tm, LANES), row),
                   pl.BlockSpec((8, LANES), const)],
        out_shape=[jax.ShapeDtypeStruct((t, D_MODEL), F32), jax.ShapeDtypeStruct((t, ROW_WORDS), jnp.int32),
                   jax.ShapeDtypeStruct((2 * TOP_K, t), jnp.int32), jax.ShapeDtypeStruct((t, LANES), F32),
                   jax.ShapeDtypeStruct((8, LANES), F32)],
        scratch_shapes=[pltpu.VMEM((1, LANES), F32)],
        compiler_params=pltpu.CompilerParams(dimension_semantics=("arbitrary",),
                                             vmem_limit_bytes=VMEM_LIMIT),
        name="outproj_ln_router",
    )(h, x2, w_out_b, ln_g, ln_b, wr_hi, wr_lo, br_p)


def _ffn_kernel(be_ref, nv_ref, x_ref, wg_ref, bg_ref, wu_ref, bu_ref, wd_ref, bd_ref, o_ref,
                wg_b, wu_b, wd_b):
    i = pl.program_id(0)
    valid = i < nv_ref[0]
    e = be_ref[i]
    prev = be_ref[jnp.maximum(i - 1, 0)]
    fresh = jnp.logical_or(i == 0, e != prev)

    @pl.when(jnp.logical_and(valid, fresh))
    def _():
        wg_b[...] = wg_ref[0, 0].astype(BF16)
        wu_b[...] = wu_ref[0, 0].astype(BF16)
        wd_b[...] = wd_ref[0, 0].astype(BF16)

    @pl.when(valid)
    def _():
        x = _unpack_rows(x_ref[...]).astype(BF16)
        hg = jnp.minimum(_dot(x, wg_b[...]) + bg_ref[0, 0], SWIGLU_LIMIT)
        hu = jnp.clip(_dot(x, wu_b[...]) + bu_ref[0, 0], -SWIGLU_LIMIT, SWIGLU_LIMIT)
        hh = (hu + 1.0) * hg * jax.nn.sigmoid(SWIGLU_ALPHA * hg)
        o_ref[...] = _pack_rows(_dot(hh.astype(BF16), wd_b[...]) + bd_ref[0, 0])


def _ffn(layer, block_expert, n_valid, xin, wg, bg, wu, bu, wd, bd):
    cap = xin.shape[0]
    nb = cap // BM

    def blk(i, be, nv):
        return jnp.maximum(jnp.minimum(i, nv[0] - 1), 0)

    row = lambda i, be, nv: (blk(i, be, nv), 0)
    wmap = lambda i, be, nv: (layer, be[blk(i, be, nv)], 0, 0)
    grid_spec = pltpu.PrefetchScalarGridSpec(
        num_scalar_prefetch=2,
        grid=(nb,),
        in_specs=[pl.BlockSpec((BM, ROW_WORDS), row),
                  pl.BlockSpec((1, 1, D_MODEL, D_FF), wmap), pl.BlockSpec((1, 1, 1, D_FF), wmap),
                  pl.BlockSpec((1, 1, D_MODEL, D_FF), wmap), pl.BlockSpec((1, 1, 1, D_FF), wmap),
                  pl.BlockSpec((1, 1, D_FF, D_MODEL), wmap), pl.BlockSpec((1, 1, 1, D_MODEL), wmap)],
        out_specs=pl.BlockSpec((BM, ROW_WORDS), row),
        scratch_shapes=[pltpu.VMEM((D_MODEL, D_FF), BF16), pltpu.VMEM((D_MODEL, D_FF), BF16),
                        pltpu.VMEM((D_FF, D_MODEL), BF16)],
    )
    return pl.pallas_call(
        _ffn_kernel,
        grid_spec=grid_spec,
        out_shape=jax.ShapeDtypeStruct((cap, ROW_WORDS), jnp.int32),
        compiler_params=pltpu.CompilerParams(dimension_semantics=("arbitrary",),
                                             vmem_limit_bytes=VMEM_LIMIT),
        name="expert_ffn",
    )(block_expert, n_valid, xin, wg, bg, wu, bu, wd, bd)


def _sc_gather(table, idx3):
    nw, n_chunks, ch = idx3.shape
    width = table.shape[1]
    per_worker = n_chunks * ch
    mesh = plsc.VectorSubcoreMesh(core_axis_name="c", subcore_axis_name="s")
    n_cores = mesh.num_cores
    assert nw == n_cores * mesh.num_subcores and n_chunks % 2 == 0 and ch == SC_CHUNK

    def body(table_hbm, idx_hbm, out_hbm, idx_v, rows0, rows1, sem_g0, sem_g1, sem_w0, sem_w1):
        wid = lax.axis_index("s") * n_cores + lax.axis_index("c")
        base = wid * per_worker
        pltpu.sync_copy(idx_hbm.at[wid], idx_v)

        @pl.loop(0, n_chunks, step=2)
        def _(c):
            g0 = pltpu.async_copy(table_hbm.at[idx_v.at[c]], rows0, sem_g0)
            g1 = pltpu.async_copy(table_hbm.at[idx_v.at[c + 1]], rows1, sem_g1)
            g0.wait()
            w0 = pltpu.async_copy(rows0, out_hbm.at[pl.ds(base + c * ch, ch)], sem_w0)
            g1.wait()
            w1 = pltpu.async_copy(rows1, out_hbm.at[pl.ds(base + (c + 1) * ch, ch)], sem_w1)
            w0.wait()
            w1.wait()

    return pl.kernel(
        body,
        out_type=jax.ShapeDtypeStruct((nw * per_worker, width), table.dtype),
        mesh=mesh,
        scratch_types=[pltpu.VMEM((n_chunks, ch), jnp.int32),
                       pltpu.VMEM((ch, width), table.dtype), pltpu.VMEM((ch, width), table.dtype),
                       pltpu.SemaphoreType.DMA, pltpu.SemaphoreType.DMA,
                       pltpu.SemaphoreType.DMA, pltpu.SemaphoreType.DMA],
        name="sc_row_gather",
    )(table, idx3)


def _sc_scatter(rows, idx3, n_out):
    nw, n_lists, ch = idx3.shape
    n_chunks = n_lists // TOP_K
    width = rows.shape[1]
    per_worker = n_chunks * ch
    mesh = plsc.VectorSubcoreMesh(core_axis_name="c", subcore_axis_name="s")
    n_cores = mesh.num_cores
    assert nw == n_cores * mesh.num_subcores and n_chunks % 2 == 0 and ch == SC_CHUNK
    assert nw * per_worker == rows.shape[0]

    def body(rows_hbm, idx_hbm, out_hbm, idx_v, buf0, buf1, sem_r0, sem_r1, sem_w0, sem_w1):
        wid = lax.axis_index("s") * n_cores + lax.axis_index("c")
        base = wid * per_worker
        pltpu.sync_copy(idx_hbm.at[wid], idx_v)

        @pl.loop(0, n_chunks, step=2)
        def _(c):
            r0 = pltpu.async_copy(rows_hbm.at[pl.ds(base + c * ch, ch)], buf0, sem_r0)
            r1 = pltpu.async_copy(rows_hbm.at[pl.ds(base + (c + 1) * ch, ch)], buf1, sem_r1)
            r0.wait()
            w0 = [pltpu.async_copy(buf0, out_hbm.at[idx_v.at[c * TOP_K + k]], sem_w0) for k in range(TOP_K)]
            r1.wait()
            w1 = [pltpu.async_copy(buf1, out_hbm.at[idx_v.at[(c + 1) * TOP_K + k]], sem_w1) for k in range(TOP_K)]
            for w in w0 + w1:
                w.wait()

    return pl.kernel(
        body,
        out_type=jax.ShapeDtypeStruct((n_out, width), rows.dtype),
        mesh=mesh,
        scratch_types=[pltpu.VMEM((n_lists, ch), jnp.int32),
                       pltpu.VMEM((ch, width), rows.dtype), pltpu.VMEM((ch, width), rows.dtype),
                       pltpu.SemaphoreType.DMA, pltpu.SemaphoreType.DMA,
                       pltpu.SemaphoreType.DMA, pltpu.SemaphoreType.DMA],
        name="sc_row_scatter",
    )(rows, idx3)


def _combine_kernel(x_ref, rows_ref, mf_ref, g_ref, b_ref, o_ref):
    mf = mf_ref[...]
    y = DEEPNORM_ALPHA * x_ref[...]
    for kk in range(TOP_K):
        y = y + mf[:, kk:kk + 1] * _unpack_rows(rows_ref[kk])
    o_ref[...] = _layer_norm(y, g_ref[...], b_ref[...])


def _combine(x1, rows, mf, ln_g, ln_b):
    t = x1.shape[0]
    tm = TM_COMB
    row = lambda i: (i, 0)
    const = lambda i: (0, 0)
    return pl.pallas_call(
        _combine_kernel,
        grid=(t // tm,),
        in_specs=[pl.BlockSpec((tm, D_MODEL), row),
                  pl.BlockSpec((TOP_K, tm, ROW_WORDS), lambda i: (0, i, 0)),
                  pl.BlockSpec((tm, LANES), row),
                  pl.BlockSpec((1, D_MODEL), const), pl.BlockSpec((1, D_MODEL), const)],
        out_specs=pl.BlockSpec((tm, D_MODEL), row),
        out_shape=jax.ShapeDtypeStruct((t, D_MODEL), F32),
        compiler_params=pltpu.CompilerParams(dimension_semantics=("arbitrary",),
                                             vmem_limit_bytes=VMEM_LIMIT),
        name="combine_ln",
    )(x1, rows, mf, ln_g, ln_b)


def _relayout_w_in(w):
    widths = (256, 256, 256, 256, SSD_W, SSD_XBC, SSD_HEADS, GLA_QK, GLA_QK, GLA_W, GLA_RANK, GLA_W)
    offs = [0]
    for wd in widths:
        offs.append(offs[-1] + wd)
    parts = [w[:, offs[i]:offs[i + 1]] for i in range(len(widths))]
    parts[6] = jnp.repeat(parts[6], SSD_HEAD_DIM, axis=1)
    parts[10] = jnp.pad(parts[10], ((0, 0), (0, LANES - GLA_RANK)))
    return jnp.concatenate(parts, axis=1).astype(BF16)


def _rep_heads(p):
    return jnp.repeat(p, SSD_HEAD_DIM)[None, :]


def kernel(x, positions, w_in, w_out, ret_norm_w, ssd_conv_w, ssd_conv_b, ssd_dt_bias, ssd_a_log, ssd_d,
           ssd_norm_w, gla_w_gk2, gla_b_gk2, gla_norm_w, ln1_g, ln1_b, w_router, b_router, w_gate, b_gate,
           w_up, b_up, w_down, b_down, ln2_g, ln2_b):
    batch, seq, d = x.shape
    t = batch * seq
    depth = w_in.shape[0]
    assert d == D_MODEL and seq % TS == 0 and t % TM_PROJ == 0 and t % TM_COMB == 0
    n_assign = t * TOP_K
    nb = n_assign // BM + N_EXPERTS
    cap = nb * BM

    cos_t, sin_t = _rope_tables(positions.reshape(t, 1))
    x2 = x.reshape(t, d)

    for l in range(depth):
        proj = _inproj(x2, _relayout_w_in(w_in[l]))
        params = (ret_norm_w[l][None, :], ssd_conv_w[l], ssd_conv_b[l][None, :], _rep_heads(ssd_dt_bias[l]),
                  _rep_heads(ssd_a_log[l]), _rep_heads(ssd_d[l]), ssd_norm_w[l][None, :],
                  jnp.pad(gla_w_gk2[l], ((0, LANES - GLA_RANK), (0, 0))), gla_b_gk2[l][None, :],
                  gla_norm_w[l][None, :])
        h = _mixer(proj.reshape(batch, seq, NP), cos_t.reshape(batch, seq, -1), sin_t.reshape(batch, seq, -1),
                   params, batch, seq).reshape(t, D_MIX)

        wr_p = jnp.pad(w_router[l], ((0, 0), (0, LANES - N_EXPERTS)))
        br_p = jnp.pad(b_router[l], (0, LANES - N_EXPERTS), constant_values=NEG_BIG)[None, :]
        wr_hi = wr_p.astype(BF16)
        wr_lo = (wr_p - wr_hi.astype(F32)).astype(BF16)
        x1, x1p, mi, mf, cnt = _post(h, x2, w_out[l].astype(BF16), ln1_g[l][None, :], ln1_b[l][None, :],
                                     wr_hi, wr_lo, br_p)

        counts = cnt[0, :N_EXPERTS].astype(jnp.int32)
        padded = (counts + BM - 1) // BM * BM
        end_padded = jnp.cumsum(padded)
        start_padded = end_padded - padded
        top_idx = mi[:TOP_K]
        start_of = jnp.sum(jnp.where(top_idx[None] == jnp.arange(N_EXPERTS, dtype=jnp.int32)[:, None, None],
                                     start_padded[:, None, None], 0), axis=0)
        dest = start_of + mi[TOP_K:]
        block_start = jnp.arange(nb, dtype=jnp.int32) * BM
        block_expert = jnp.minimum(jnp.sum((end_padded[None, :] <= block_start[:, None]).astype(jnp.int32), axis=1),
                                   N_EXPERTS - 1)
        n_valid = (end_padded[-1:] // BM).astype(jnp.int32)

        scatter_idx = dest.reshape(TOP_K, SC_WORKERS, -1, SC_CHUNK).transpose(1, 2, 0, 3)
        xin = _sc_scatter(x1p, scatter_idx.reshape(SC_WORKERS, -1, SC_CHUNK), cap)
        yb = _ffn(l, block_expert, n_valid, xin, w_gate, b_gate[:, :, None, :], w_up, b_up[:, :, None, :],
                  w_down, b_down[:, :, None, :])
        rows = _sc_gather(yb, dest.reshape(SC_WORKERS, -1, SC_CHUNK))
        x2 = _combine(x1, rows.reshape(TOP_K, t, ROW_WORDS), mf, ln2_g[l][None, :], ln2_b[l][None, :])
    return x2.reshape(batch, seq, d)
```

```python
import functools
import math

import jax
import jax.numpy as jnp
from jax import lax
from jax.experimental import pallas as pl
from jax.experimental.pallas import tpu as pltpu
from jax.experimental.pallas import tpu_sc as plsc

F32 = jnp.float32
BF16 = jnp.bfloat16

D_MODEL = 1024
CHUNK = 64
RET_HEADS, RET_DK, RET_DV = 4, 64, 64
RET_W = RET_HEADS * RET_DV
SSD_HEADS, SSD_HEAD_DIM, SSD_STATE, SSD_GROUPS, SSD_CONV = 8, 64, 64, 2, 4
SSD_W = SSD_HEADS * SSD_HEAD_DIM
SSD_BC = SSD_GROUPS * SSD_STATE
SSD_XBC = SSD_W + 2 * SSD_BC
GLA_HEADS, GLA_DK, GLA_DV, GLA_RANK, GLA_TEMP = 4, 32, 64, 16, 16.0
GLA_QK = GLA_HEADS * GLA_DK
GLA_W = GLA_HEADS * GLA_DV
D_MIX = RET_W + SSD_W + GLA_W
N_EXPERTS, TOP_K, D_FF = 32, 4, 1024
SWIGLU_LIMIT, SWIGLU_ALPHA = 7.0, 1.702
ROPE_BASE = 10000.0
LN_EPS, NORM_EPS = 1e-5, 1e-6
DEPTH = 2
DEEPNORM_ALPHA = (2.0 * DEPTH) ** 0.25

LANES = 128
NEG_BIG = -1e30
VMEM_LIMIT = 56 * 1024 * 1024

_SEGS = (("rq", 256), ("rk", 256), ("rv", 256), ("rg", 256), ("sz", SSD_W), ("sxbc", SSD_XBC),
         ("sdt", SSD_W), ("gq", 128), ("gk", 128), ("gv", 256), ("ggk", 128), ("gg", 256))
COL = {}
_off = 0
for _n, _w in _SEGS:
    COL[_n] = _off
    _off += _w
NP = _off

TS = 256
MIX_G = 4
TM_PROJ = 512
TM_POST = 512
TM_COMB = 512
BM = 512
ROW_WORDS = D_MODEL // 2
SC_WORKERS = 32
SC_CHUNK = 64


def _dot(a, b, dims=(((1,), (0,)), ((), ())), precision=None):
    return lax.dot_general(a, b, dims, precision=precision, preferred_element_type=F32)


_NT = (((1,), (1,)), ((), ()))
_TN = (((0,), (0,)), ((), ()))


def _iota(shape, dim):
    return lax.broadcasted_iota(jnp.int32, shape, dim)


def _vdiv(x, n):
    assert n & (n - 1) == 0
    return lax.shift_right_logical(x, n.bit_length() - 1)


def _vmod(x, n):
    assert n & (n - 1) == 0
    return jnp.bitwise_and(x, n - 1)


def _silu(x):
    return x * jax.nn.sigmoid(x)


def _softplus(x):
    return jnp.maximum(x, 0.0) + jnp.log(1.0 + jnp.exp(-jnp.abs(x)))


def _pack_rows(x):
    w = x.shape[1] // 2
    lo = lax.bitcast_convert_type(x[:, :w].astype(BF16).astype(F32), jnp.uint32)
    hi = lax.bitcast_convert_type(x[:, w:].astype(BF16).astype(F32), jnp.uint32)
    return lax.bitcast_convert_type(lax.shift_right_logical(lo, jnp.uint32(16)) | hi, jnp.int32)


def _unpack_rows(words):
    u = lax.bitcast_convert_type(words, jnp.uint32)
    a = lax.bitcast_convert_type(lax.shift_left(u, jnp.uint32(16)), F32)
    b = lax.bitcast_convert_type(u & jnp.uint32(0xFFFF0000), F32)
    return jnp.concatenate([a, b], axis=-1)


def _seg_sum64(x):
    first = _iota((1, LANES), 1) < 64
    outs = []
    for j in range(x.shape[-1] // LANES):
        blk = x[:, j * LANES:(j + 1) * LANES]
        lo = jnp.sum(jnp.where(first, blk, 0.0), axis=-1, keepdims=True)
        hi = jnp.sum(jnp.where(first, 0.0, blk), axis=-1, keepdims=True)
        outs.append(jnp.where(first, lo, hi))
    return jnp.concatenate(outs, axis=-1)


def _block_mask(shape, row_blk, col_blk):
    keep = _vdiv(_iota(shape, 0), row_blk) == _vdiv(_iota(shape, 1), col_blk)
    return jnp.where(keep, 1.0, 0.0).astype(BF16)


def _block_diag(x, mask):
    reps = mask.shape[0] // x.shape[0]
    return jnp.concatenate([x.astype(BF16)] * reps, axis=0) * mask


def _cumsum_rows(tri, x):
    hi = x.astype(BF16)
    lo = (x - hi.astype(F32)).astype(BF16)
    return _dot(tri, hi) + _dot(tri, lo)


def _rope_kernel(pos_ref, cos_ref, sin_ref):
    lane = _iota((1, RET_HEADS * RET_DK), 1)
    half = RET_DK // 2
    k = _vmod(lane, half).astype(F32)
    inv_freq = jnp.exp(k * (-math.log(ROPE_BASE) / half))
    ang = pos_ref[...].astype(F32) * inv_freq
    first = _vmod(lane, RET_DK) < half
    cos_ref[...] = jnp.cos(ang)
    sin_ref[...] = jnp.where(first, -1.0, 1.0) * jnp.sin(ang)


def _rope_tables(pos_col):
    t = pos_col.shape[0]
    tm = 512
    w = RET_HEADS * RET_DK
    return pl.pallas_call(
        _rope_kernel,
        grid=(t // tm,),
        in_specs=[pl.BlockSpec((tm, 1), lambda i: (i, 0))],
        out_specs=[pl.BlockSpec((tm, w), lambda i: (i, 0))] * 2,
        out_shape=[jax.ShapeDtypeStruct((t, w), F32)] * 2,
        compiler_params=pltpu.CompilerParams(dimension_semantics=("arbitrary",)),
        name="rope_tables",
    )(pos_col)


def _inproj_kernel(x_ref, w_ref, o_ref):
    o_ref[...] = _dot(x_ref[...].astype(BF16), w_ref[...])


def _inproj(x2, w_p):
    t = x2.shape[0]
    return pl.pallas_call(
        _inproj_kernel,
        grid=(t // TM_PROJ,),
        in_specs=[pl.BlockSpec((TM_PROJ, D_MODEL), lambda i: (i, 0)),
                  pl.BlockSpec((D_MODEL, NP), lambda i: (0, 0))],
        out_specs=pl.BlockSpec((TM_PROJ, NP), lambda i: (i, 0)),
        out_shape=jax.ShapeDtypeStruct((t, NP), F32),
        compiler_params=pltpu.CompilerParams(dimension_semantics=("arbitrary",),
                                             vmem_limit_bytes=VMEM_LIMIT),
        name="inproj",
    )(x2, w_p)


def _mixer_kernel(proj_ref, cos_ref, sin_ref, retw_ref, convw_ref, convb_ref, dtb_ref, alog_ref,
                  dskip_ref, ssdw_ref, wgk_ref, bgk_ref, glaw_ref, h_ref,
                  ret_s, ssd_s, gla_s, stage, xact, m_heads, m_groups, m_gla):
    C = CHUNK

    m_heads[...] = _block_mask(m_heads.shape, C, 64)
    m_groups[...] = _block_mask(m_groups.shape, C * SSD_HEADS // SSD_GROUPS, SSD_STATE)
    m_gla[...] = _block_mask(m_gla.shape, C, GLA_DK)

    @pl.when(pl.program_id(1) == 0)
    def _():
        ret_s[...] = jnp.zeros_like(ret_s)
        ssd_s[...] = jnp.zeros_like(ssd_s)
        gla_s[...] = jnp.zeros_like(gla_s)
        for g in range(MIX_G):
            stage[g, 0:8, :] = jnp.zeros((8, SSD_XBC), F32)

    for g in range(MIX_G):
        stage[g, 8:8 + TS, :] = proj_ref[g, :, COL["sxbc"]:COL["sxbc"] + SSD_XBC]
        acc = convb_ref[...] + convw_ref[0:1, :] * stage[g, 5:5 + TS, :]
        for j in range(1, SSD_CONV):
            acc = acc + convw_ref[j:j + 1, :] * stage[g, 5 + j:5 + j + TS, :]
        xact[g] = _silu(acc)
        stage[g, 0:8, :] = stage[g, TS:TS + 8, :]

    lane256 = _iota((1, 256), 1)
    head = _vdiv(lane256, 64).astype(F32)
    log_gamma = jnp.log(1.0 - jnp.exp((-5.0 - head) * math.log(2.0)))
    row = _iota((C, 1), 0).astype(F32)
    dist = row - _vmod(lane256, 64).astype(F32)
    ret_intra = jnp.where(dist >= 0, jnp.exp(log_gamma * jnp.maximum(dist, 0.0)), 0.0)
    ret_qdec = jnp.exp(log_gamma * (row + 1.0))
    ret_kdec = jnp.exp(log_gamma * (C - 1.0 - row))
    ret_cdec = jnp.exp(log_gamma * C)
    first_half = _vmod(lane256, RET_DK) < (RET_DK // 2)

    tri = jnp.where(_iota((C, C), 0) >= _iota((C, C), 1), 1.0, 0.0).astype(BF16)
    causal4 = _iota((C, 256), 0) >= _vmod(_iota((C, 256), 1), 64)
    causal8 = _iota((C, 512), 0) >= _vmod(_iota((C, 512), 1), 64)
    eye8 = _iota((C, 512), 0) == _vmod(_iota((C, 512), 1), 64)

    a_neg = -jnp.exp(alog_ref[...])

    def rot(t, cos, sin):
        sw = jnp.where(first_half, pltpu.roll(t, 256 - 32, 1), pltpu.roll(t, 32, 1))
        return t * cos + sw * sin

    def chunk_of(sq, r0):
        def seg(name, width):
            return proj_ref[sq, pl.ds(r0, C), COL[name]:COL[name] + width]

        cos = cos_ref[sq, pl.ds(r0, C), :]
        sin = sin_ref[sq, pl.ds(r0, C), :]
        q = rot(seg("rq", 256), cos, sin)
        k = rot(seg("rk", 256), cos, sin) * (RET_DK ** -0.5)
        v = seg("rv", 256)
        vb = v.astype(BF16)
        kbd = _block_diag(k, m_heads[...])
        scores = _dot(q.astype(BF16), kbd, _NT) * ret_intra
        vbd = _block_diag(v, m_heads[...])
        s_prev = ret_s[sq]
        o = _dot(scores.astype(BF16), vbd) + _dot((q * ret_qdec).astype(BF16), s_prev.astype(BF16))
        contrib = _dot((k * ret_kdec).astype(BF16), vb, _TN)
        keep = _vdiv(_iota((256, 256), 0), RET_DK) == _vdiv(_iota((256, 256), 1), RET_DV)
        ret_s[sq] = jnp.where(keep, ret_cdec * s_prev + contrib, 0.0)
        mu = _seg_sum64(o) * (1.0 / RET_DV)
        oc = o - mu
        var = _seg_sum64(oc * oc) * (1.0 / RET_DV)
        o = oc * lax.rsqrt(var + LN_EPS) * retw_ref[...]
        h_ref[sq, pl.ds(r0, C), 0:RET_W] = (_silu(seg("rg", 256)) * o).astype(BF16)

        xs = xact[sq, pl.ds(r0, C), 0:SSD_W]
        bm = xact[sq, pl.ds(r0, C), SSD_W:SSD_W + SSD_BC]
        cm = xact[sq, pl.ds(r0, C), SSD_W + SSD_BC:SSD_XBC]
        cmb = cm.astype(BF16)
        dt = _softplus(seg("sdt", SSD_W) + dtb_ref[...])
        acum = _cumsum_rows(tri, dt * a_neg)
        arow = jnp.sum(jnp.where(eye8, acum, 0.0), axis=0, keepdims=True)
        decay = jnp.exp(jnp.where(causal8, acum - arow, NEG_BIG))
        b8 = _block_diag(bm, m_groups[...])
        cb = _dot(cmb, b8, _NT)
        m = (cb * decay).astype(BF16)
        xdt = xs * dt
        s2 = ssd_s[sq]
        half = SSD_W // SSD_GROUPS
        ys = []
        for g in range(SSD_GROUPS):
            xbd = _block_diag(xdt[:, g * half:(g + 1) * half], m_heads[...])
            ys.append(_dot(m[:, g * half:(g + 1) * half], xbd))
        y = jnp.concatenate(ys, axis=-1)
        y = y + _dot(cmb, s2.astype(BF16)) * jnp.exp(acum)
        y = y + dskip_ref[...] * xs
        a_last = acum[C - 1:C, :]
        sd = jnp.exp(a_last - acum)
        contrib_s = _dot(bm.astype(BF16), (xdt * sd).astype(BF16), _TN)
        keep_s = _vdiv(_iota(s2.shape, 0), SSD_STATE) == _vdiv(_iota(s2.shape, 1), half)
        ssd_s[sq] = jnp.where(keep_s, s2 * jnp.exp(a_last) + contrib_s, 0.0)
        yz = y * _silu(seg("sz", SSD_W))
        outs = []
        for g in range(SSD_GROUPS):
            blk = yz[:, g * half:(g + 1) * half]
            ms = jnp.mean(blk * blk, axis=-1, keepdims=True)
            outs.append(blk * lax.rsqrt(ms + NORM_EPS))
        h_ref[sq, pl.ds(r0, C), RET_W:RET_W + SSD_W] = (jnp.concatenate(outs, axis=-1) * ssdw_ref[...]).astype(BF16)

        gq = seg("gq", GLA_QK) * (GLA_DK ** -0.5)
        gkk = seg("gk", GLA_QK)
        gv = seg("gv", GLA_W)
        gkl = _dot(seg("ggk", 128).astype(BF16), wgk_ref[...].astype(BF16)) + bgk_ref[...]
        log_a = -_softplus(-gkl) * (1.0 / GLA_TEMP)
        b = _cumsum_rows(tri, log_a)
        q_t = (gq * jnp.exp(b)).astype(BF16)
        k_t = gkk * jnp.exp(-b)
        kbd_g = _block_diag(k_t, m_gla[...])
        att = jnp.where(causal4, _dot(q_t, kbd_g, _NT), 0.0)
        vbd_g = _block_diag(gv, m_heads[...])
        st = gla_s[sq]
        og = _dot(att.astype(BF16), vbd_g) + _dot(q_t, st.astype(BF16), _NT)
        b_last = b[C - 1:C, :]
        kd = (gkk * jnp.exp(b_last - b)).astype(BF16)
        contrib_g = _dot(gv.astype(BF16), kd, _TN)
        keep_g = _vdiv(_iota(st.shape, 0), GLA_DV) == _vdiv(_iota(st.shape, 1), GLA_DK)
        gla_s[sq] = jnp.where(keep_g, st * jnp.exp(b_last) + contrib_g, 0.0)
        ms = _seg_sum64(og * og) * (1.0 / GLA_DV)
        og = og * lax.rsqrt(ms + NORM_EPS) * glaw_ref[...]
        h_ref[sq, pl.ds(r0, C), RET_W + SSD_W:D_MIX] = (_silu(seg("gg", GLA_W)) * og).astype(BF16)

    def chunk(c, carry):
        r0 = pl.multiple_of(c * C, C)
        for sq in range(MIX_G):
            chunk_of(sq, r0)
        return carry

    lax.fori_loop(0, TS // C, chunk, 0)


def _mixer(proj, cos_t, sin_t, params, batch, seq):
    assert batch % MIX_G == 0
    row_map = lambda b, s: (b, s, 0)
    const = lambda b, s: (0, 0)
    specs = [pl.BlockSpec((MIX_G, TS, NP), row_map),
             pl.BlockSpec((MIX_G, TS, 256), row_map),
             pl.BlockSpec((MIX_G, TS, 256), row_map)]
    specs += [pl.BlockSpec(p.shape, const) for p in params]
    return pl.pallas_call(
        _mixer_kernel,
        grid=(batch // MIX_G, seq // TS),
        in_specs=specs,
        out_specs=pl.BlockSpec((MIX_G, TS, D_MIX), row_map),
        out_shape=jax.ShapeDtypeStruct((batch, seq, D_MIX), BF16),
        scratch_shapes=[pltpu.VMEM((MIX_G, 256, 256), F32),
                        pltpu.VMEM((MIX_G, SSD_BC, SSD_W), F32),
                        pltpu.VMEM((MIX_G, GLA_W, GLA_QK), F32),
                        pltpu.VMEM((MIX_G, TS + 8, SSD_XBC), F32),
                        pltpu.VMEM((MIX_G, TS, SSD_XBC), F32),
                        pltpu.VMEM((RET_HEADS * CHUNK, 256), BF16),
                        pltpu.VMEM((SSD_HEADS * CHUNK, SSD_BC), BF16),
                        pltpu.VMEM((GLA_HEADS * CHUNK, GLA_QK), BF16)],
        compiler_params=pltpu.CompilerParams(dimension_semantics=("arbitrary", "arbitrary"),
                                             vmem_limit_bytes=VMEM_LIMIT),
        name="mixer",
    )(proj, cos_t, sin_t, *params)


def _layer_norm(y, g, b):
    mu = jnp.mean(y, axis=-1, keepdims=True)
    yc = y - mu
    var = jnp.mean(yc * yc, axis=-1, keepdims=True)
    return yc * lax.rsqrt(var + LN_EPS) * g + b


def _post_kernel(h_ref, x_ref, wout_ref, g_ref, b_ref, wrh_ref, wrl_ref, br_ref,
                 x1_ref, x1p_ref, mi_ref, mf_ref, cnt_ref, carry):
    tm = TM_POST

    @pl.when(pl.program_id(0) == 0)
    def _():
        carry[...] = jnp.zeros_like(carry)

    mix = _dot(h_ref[...], wout_ref[...])
    x1 = _layer_norm(DEEPNORM_ALPHA * x_ref[...] + mix, g_ref[...], b_ref[...])
    x1_ref[...] = x1
    x1p_ref[...] = _pack_rows(x1)

    x_hi = x1.astype(BF16)
    x_lo = (x1 - x_hi.astype(F32)).astype(BF16)
    logits = (_dot(x_hi, wrh_ref[...]) + _dot(x_lo, wrh_ref[...]) + _dot(x_hi, wrl_ref[...])
              + br_ref[...])
    lane_i = _iota((tm, LANES), 1)
    lane = lane_i.astype(F32)
    work = logits
    vals, idxs = [], []
    multi = jnp.zeros((tm, LANES), F32)
    for _ in range(TOP_K):
        m = jnp.max(work, axis=-1, keepdims=True)
        idx = jnp.min(jnp.where(work == m, lane, float(LANES)), axis=-1, keepdims=True)
        hit = lane == idx
        multi = multi + hit.astype(F32)
        work = jnp.where(hit, -jnp.inf, work)
        vals.append(m)
        idxs.append(idx)
    exps = [jnp.exp(v - vals[0]) for v in vals]
    denom = exps[0] + exps[1] + exps[2] + exps[3]
    gates = [e / denom for e in exps]

    before = (_iota((tm, tm), 0) > _iota((tm, tm), 1)).astype(BF16)
    prior = _dot(before, multi.astype(BF16)) + carry[...]
    mi = jnp.zeros((tm, LANES), F32)
    mf = jnp.zeros((tm, LANES), F32)
    for kk in range(TOP_K):
        rank = jnp.sum(jnp.where(lane == idxs[kk], prior, 0.0), axis=-1, keepdims=True)
        mi = jnp.where(lane_i == kk, idxs[kk], mi)
        mi = jnp.where(lane_i == TOP_K + kk, rank, mi)
        mf = jnp.where(lane_i == kk, gates[kk], mf)
    mi_ref[...] = jnp.transpose(mi)[0:2 * TOP_K, :].astype(jnp.int32)
    mf_ref[...] = mf
    carry[...] = carry[...] + jnp.sum(multi, axis=0, keepdims=True)
    cnt_ref[...] = jnp.broadcast_to(carry[...], cnt_ref.shape)


def _post(h, x2, w_out_b, ln_g, ln_b, wr_hi, wr_lo, br_p):
    t = x2.shape[0]
    tm = TM_POST
    row = lambda i: (i, 0)
    const = lambda i: (0, 0)
    return pl.pallas_call(
        _post_kernel,
        grid=(t // tm,),
        in_specs=[pl.BlockSpec((tm, D_MIX), row), pl.BlockSpec((tm, D_MODEL), row),
                  pl.BlockSpec((D_MIX, D_MODEL), const), pl.BlockSpec((1, D_MODEL), const),
                  pl.BlockSpec((1, D_MODEL), const), pl.BlockSpec((D_MODEL, LANES), const),
                  pl.BlockSpec((D_MODEL, LANES), const), pl.BlockSpec((1, LANES), const)],
        out_specs=[pl.BlockSpec((tm, D_MODEL), row), pl.BlockSpec((tm, ROW_WORDS), row),
                   pl.BlockSpec((2 * TOP_K, tm), lambda i: (0, i)), pl.BlockSpec((tm, LANES), row),
                   pl.BlockSpec((8, LANES), const)],
        out_shape=[jax.ShapeDtypeStruct((t, D_MODEL), F32), jax.ShapeDtypeStruct((t, ROW_WORDS), jnp.int32),
                   jax.ShapeDtypeStruct((2 * TOP_K, t), jnp.int32), jax.ShapeDtypeStruct((t, LANES), F32),
                   jax.ShapeDtypeStruct((8, LANES), F32)],
        scratch_shapes=[pltpu.VMEM((1, LANES), F32)],
        compiler_params=pltpu.CompilerParams(dimension_semantics=("arbitrary",),
                                             vmem_limit_bytes=VMEM_LIMIT),
        name="outproj_ln_router",
    )(h, x2, w_out_b, ln_g, ln_b, wr_hi, wr_lo, br_p)


def _ffn_kernel(be_ref, nv_ref, x_ref, wg_ref, bg_ref, wu_ref, bu_ref, wd_ref, bd_ref, o_ref,
                wg_b, wu_b, wd_b):
    i = pl.program_id(0)
    valid = i < nv_ref[0]
    e = be_ref[i]
    prev = be_ref[jnp.maximum(i - 1, 0)]
    fresh = jnp.logical_or(i == 0, e != prev)

    @pl.when(jnp.logical_and(valid, fresh))
    def _():
        wg_b[...] = wg_ref[0, 0].astype(BF16)
        wu_b[...] = wu_ref[0, 0].astype(BF16)
        wd_b[...] = wd_ref[0, 0].astype(BF16)

    @pl.when(valid)
    def _():
        x = _unpack_rows(x_ref[...]).astype(BF16)
        hg = jnp.minimum(_dot(x, wg_b[...]) + bg_ref[0, 0], SWIGLU_LIMIT)
        hu = jnp.clip(_dot(x, wu_b[...]) + bu_ref[0, 0], -SWIGLU_LIMIT, SWIGLU_LIMIT)
        hh = (hu + 1.0) * hg * jax.nn.sigmoid(SWIGLU_ALPHA * hg)
        o_ref[...] = _pack_rows(_dot(hh.astype(BF16), wd_b[...]) + bd_ref[0, 0])


def _ffn(layer, block_expert, n_valid, xin, wg, bg, wu, bu, wd, bd):
    cap = xin.shape[0]
    nb = cap // BM

    def blk(i, be, nv):
        return jnp.maximum(jnp.minimum(i, nv[0] - 1), 0)

    row = lambda i, be, nv: (blk(i, be, nv), 0)
    wmap = lambda i, be, nv: (layer, be[blk(i, be, nv)], 0, 0)
    grid_spec = pltpu.PrefetchScalarGridSpec(
        num_scalar_prefetch=2,
        grid=(nb,),
        in_specs=[pl.BlockSpec((BM, ROW_WORDS), row),
                  pl.BlockSpec((1, 1, D_MODEL, D_FF), wmap), pl.BlockSpec((1, 1, 1, D_FF), wmap),
                  pl.BlockSpec((1, 1, D_MODEL, D_FF), wmap), pl.BlockSpec((1, 1, 1, D_FF), wmap),
                  pl.BlockSpec((1, 1, D_FF, D_MODEL), wmap), pl.BlockSpec((1, 1, 1, D_MODEL), wmap)],
        out_specs=pl.BlockSpec((BM, ROW_WORDS), row),
        scratch_shapes=[pltpu.VMEM((D_MODEL, D_FF), BF16), pltpu.VMEM((D_MODEL, D_FF), BF16),
                        pltpu.VMEM((D_FF, D_MODEL), BF16)],
    )
    return pl.pallas_call(
        _ffn_kernel,
        grid_spec=grid_spec,
        out_shape=jax.ShapeDtypeStruct((cap, ROW_WORDS), jnp.int32),
        compiler_params=pltpu.CompilerParams(dimension_semantics=("arbitrary",),
                                             vmem_limit_bytes=VMEM_LIMIT),
        name="expert_ffn",
    )(block_expert, n_valid, xin, wg, bg, wu, bu, wd, bd)


def _sc_gather(table, idx3):
    nw, n_chunks, ch = idx3.shape
    width = table.shape[1]
    per_worker = n_chunks * ch
    mesh = plsc.VectorSubcoreMesh(core_axis_name="c", subcore_axis_name="s")
    n_cores = mesh.num_cores
    assert nw == n_cores * mesh.num_subcores and n_chunks % 2 == 0 and ch == SC_CHUNK

    def body(table_hbm, idx_hbm, out_hbm, idx_v, rows0, rows1, sem_g0, sem_g1, sem_w0, sem_w1):
        wid = lax.axis_index("s") * n_cores + lax.axis_index("c")
        base = wid * per_worker
        pltpu.sync_copy(idx_hbm.at[wid], idx_v)

        @pl.loop(0, n_chunks, step=2)
        def _(c):
            g0 = pltpu.async_copy(table_hbm.at[idx_v.at[c]], rows0, sem_g0)
            g1 = pltpu.async_copy(table_hbm.at[idx_v.at[c + 1]], rows1, sem_g1)
            g0.wait()
            w0 = pltpu.async_copy(rows0, out_hbm.at[pl.ds(base + c * ch, ch)], sem_w0)
            g1.wait()
            w1 = pltpu.async_copy(rows1, out_hbm.at[pl.ds(base + (c + 1) * ch, ch)], sem_w1)
            w0.wait()
            w1.wait()

    return pl.kernel(
        body,
        out_type=jax.ShapeDtypeStruct((nw * per_worker, width), table.dtype),
        mesh=mesh,
        scratch_types=[pltpu.VMEM((n_chunks, ch), jnp.int32),
                       pltpu.VMEM((ch, width), table.dtype), pltpu.VMEM((ch, width), table.dtype),
                       pltpu.SemaphoreType.DMA, pltpu.SemaphoreType.DMA,
                       pltpu.SemaphoreType.DMA, pltpu.SemaphoreType.DMA],
        name="sc_row_gather",
    )(table, idx3)


def _sc_scatter(rows, idx3, n_out):
    nw, n_lists, ch = idx3.shape
    n_chunks = n_lists // TOP_K
    width = rows.shape[1]
    per_worker = n_chunks * ch
    mesh = plsc.VectorSubcoreMesh(core_axis_name="c", subcore_axis_name="s")
    n_cores = mesh.num_cores
    assert nw == n_cores * mesh.num_subcores and n_chunks % 2 == 0 and ch == SC_CHUNK
    assert nw * per_worker == rows.shape[0]

    def body(rows_hbm, idx_hbm, out_hbm, idx_v, buf0, buf1, sem_r0, sem_r1, sem_w0, sem_w1):
        wid = lax.axis_index("s") * n_cores + lax.axis_index("c")
        base = wid * per_worker
        pltpu.sync_copy(idx_hbm.at[wid], idx_v)

        @pl.loop(0, n_chunks, step=2)
        def _(c):
            r0 = pltpu.async_copy(rows_hbm.at[pl.ds(base + c * ch, ch)], buf0, sem_r0)
            r1 = pltpu.async_copy(rows_hbm.at[pl.ds(base + (c + 1) * ch, ch)], buf1, sem_r1)
            r0.wait()
            w0 = [pltpu.async_copy(buf0, out_hbm.at[idx_v.at[c * TOP_K + k]], sem_w0) for k in range(TOP_K)]
            r1.wait()
            w1 = [pltpu.async_copy(buf1, out_hbm.at[idx_v.at[(c + 1) * TOP_K + k]], sem_w1) for k in range(TOP_K)]
            for w in w0 + w1:
                w.wait()

    return pl.kernel(
        body,
        out_type=jax.ShapeDtypeStruct((n_out, width), rows.dtype),
        mesh=mesh,
        scratch_types=[pltpu.VMEM((n_lists, ch), jnp.int32),
                       pltpu.VMEM((ch, width), rows.dtype), pltpu.VMEM((ch, width), rows.dtype),
                       pltpu.SemaphoreType.DMA, pltpu.SemaphoreType.DMA,
                       pltpu.SemaphoreType.DMA, pltpu.SemaphoreType.DMA],
        name="sc_row_scatter",
    )(rows, idx3)


def _combine_kernel(x_ref, rows_ref, mf_ref, g_ref, b_ref, o_ref):
    mf = mf_ref[...]
    y = DEEPNORM_ALPHA * x_ref[...]
    for kk in range(TOP_K):
        y = y + mf[:, kk:kk + 1] * _unpack_rows(rows_ref[kk])
    o_ref[...] = _layer_norm(y, g_ref[...], b_ref[...])


def _combine(x1, rows, mf, ln_g, ln_b):
    t = x1.shape[0]
    tm = TM_COMB
    row = lambda i: (i, 0)
    const = lambda i: (0, 0)
    return pl.pallas_call(
        _combine_kernel,
        grid=(t // tm,),
        in_specs=[pl.BlockSpec((tm, D_MODEL), row),
                  pl.BlockSpec((TOP_K, tm, ROW_WORDS), lambda i: (0, i, 0)),
                  pl.BlockSpec((tm, LANES), row),
                  pl.BlockSpec((1, D_MODEL), const), pl.BlockSpec((1, D_MODEL), const)],
        out_specs=pl.BlockSpec((tm, D_MODEL), row),
        out_shape=jax.ShapeDtypeStruct((t, D_MODEL), F32),
        compiler_params=pltpu.CompilerParams(dimension_semantics=("arbitrary",),
                                             vmem_limit_bytes=VMEM_LIMIT),
        name="combine_ln",
    )(x1, rows, mf, ln_g, ln_b)


def _relayout_w_in(w):
    widths = (256, 256, 256, 256, SSD_W, SSD_XBC, SSD_HEADS, GLA_QK, GLA_QK, GLA_W, GLA_RANK, GLA_W)
    offs = [0]
    for wd in widths:
        offs.append(offs[-1] + wd)
    parts = [w[:, offs[i]:offs[i + 1]] for i in range(len(widths))]
    parts[6] = jnp.repeat(parts[6], SSD_HEAD_DIM, axis=1)
    parts[10] = jnp.pad(parts[10], ((0, 0), (0, LANES - GLA_RANK)))
    return jnp.concatenate(parts, axis=1).astype(BF16)


def _rep_heads(p):
    return jnp.repeat(p, SSD_HEAD_DIM)[None, :]


def kernel(x, positions, w_in, w_out, ret_norm_w, ssd_conv_w, ssd_conv_b, ssd_dt_bias, ssd_a_log, ssd_d,
           ssd_norm_w, gla_w_gk2, gla_b_gk2, gla_norm_w, ln1_g, ln1_b, w_router, b_router, w_gate, b_gate,
           w_up, b_up, w_down, b_down, ln2_g, ln2_b):
    batch, seq, d = x.shape
    t = batch * seq
    depth = w_in.shape[0]
    assert d == D_MODEL and seq % TS == 0 and t % TM_PROJ == 0 and t % TM_COMB == 0
    n_assign = t * TOP_K
    nb = n_assign // BM + N_EXPERTS
    cap = nb * BM

    cos_t, sin_t = _rope_tables(positions.reshape(t, 1))
    x2 = x.reshape(t, d)

    for l in range(depth):
        proj = _inproj(x2, _relayout_w_in(w_in[l]))
        params = (ret_norm_w[l][None, :], ssd_conv_w[l], ssd_conv_b[l][None, :], _rep_heads(ssd_dt_bias[l]),
                  _rep_heads(ssd_a_log[l]), _rep_heads(ssd_d[l]), ssd_norm_w[l][None, :],
                  jnp.pad(gla_w_gk2[l], ((0, LANES - GLA_RANK), (0, 0))), gla_b_gk2[l][None, :],
                  gla_norm_w[l][None, :])
        h = _mixer(proj.reshape(batch, seq, NP), cos_t.reshape(batch, seq, -1), sin_t.reshape(batch, seq, -1),
                   params, batch, seq).reshape(t, D_MIX)

        wr_p = jnp.pad(w_router[l], ((0, 0), (0, LANES - N_EXPERTS)))
        br_p = jnp.pad(b_router[l], (0, LANES - N_EXPERTS), constant_values=NEG_BIG)[None, :]
        wr_hi = wr_p.astype(BF16)
        wr_lo = (wr_p - wr_hi.astype(F32)).astype(BF16)
        x1, x1p, mi, mf, cnt = _post(h, x2, w_out[l].astype(BF16), ln1_g[l][None, :], ln1_b[l][None, :],
                                     wr_hi, wr_lo, br_p)

        counts = cnt[0, :N_EXPERTS].astype(jnp.int32)
        padded = (counts + BM - 1) // BM * BM
        end_padded = jnp.cumsum(padded)
        start_padded = end_padded - padded
        top_idx = mi[:TOP_K]
        start_of = jnp.sum(jnp.where(top_idx[None] == jnp.arange(N_EXPERTS, dtype=jnp.int32)[:, None, None],
                                     start_padded[:, None, None], 0), axis=0)
        dest = start_of + mi[TOP_K:]
        block_start = jnp.arange(nb, dtype=jnp.int32) * BM
        block_expert = jnp.minimum(jnp.sum((end_padded[None, :] <= block_start[:, None]).astype(jnp.int32), axis=1),
                                   N_EXPERTS - 1)
        n_valid = (end_padded[-1:] // BM).astype(jnp.int32)

        scatter_idx = dest.reshape(TOP_K, SC_WORKERS, -1, SC_CHUNK).transpose(1, 2, 0, 3)
        xin = _sc_scatter(x1p, scatter_idx.reshape(SC_WORKERS, -1, SC_CHUNK), cap)
        yb = _ffn(l, block_expert, n_valid, xin, w_gate, b_gate[:, :, None, :], w_up, b_up[:, :, None, :],
                  w_down, b_down[:, :, None, :])
        rows = _sc_gather(yb, dest.reshape(SC_WORKERS, -1, SC_CHUNK))
        x2 = _combine(x1, rows.reshape(TOP_K, t, ROW_WORDS), mf, ln2_g[l][None, :], ln2_b[l][None, :])
    return x2.reshape(batch, seq, d)
```

```python
import functools
import math

import jax
import jax.numpy as jnp
from jax import lax
from jax.experimental import pallas as pl
from jax.experimental.pallas import tpu as pltpu
from jax.experimental.pallas import tpu_sc as plsc

F32 = jnp.float32
BF16 = jnp.bfloat16

D_MODEL = 1024
CHUNK = 64
RET_HEADS, RET_DK, RET_DV = 4, 64, 64
RET_W = RET_HEADS * RET_DV
SSD_HEADS, SSD_HEAD_DIM, SSD_STATE, SSD_GROUPS, SSD_CONV = 8, 64, 64, 2, 4
SSD_W = SSD_HEADS * SSD_HEAD_DIM
SSD_BC = SSD_GROUPS * SSD_STATE
SSD_XBC = SSD_W + 2 * SSD_BC
GLA_HEADS, GLA_DK, GLA_DV, GLA_RANK, GLA_TEMP = 4, 32, 64, 16, 16.0
GLA_QK = GLA_HEADS * GLA_DK
GLA_W = GLA_HEADS * GLA_DV
D_MIX = RET_W + SSD_W + GLA_W
N_EXPERTS, TOP_K, D_FF = 32, 4, 1024
SWIGLU_LIMIT, SWIGLU_ALPHA = 7.0, 1.702
ROPE_BASE = 10000.0
LN_EPS, NORM_EPS = 1e-5, 1e-6
DEPTH = 2
DEEPNORM_ALPHA = (2.0 * DEPTH) ** 0.25

LANES = 128
NEG_BIG = -1e30
VMEM_LIMIT = 56 * 1024 * 1024

_SEGS = (("rq", 256), ("rk", 256), ("rv", 256), ("rg", 256), ("sz", SSD_W), ("sxbc", SSD_XBC),
         ("sdt", SSD_W), ("gq", 128), ("gk", 128), ("gv", 256), ("ggk", 128), ("gg", 256))
COL = {}
_off = 0
for _n, _w in _SEGS:
    COL[_n] = _off
    _off += _w
NP = _off

MIX_G = 4
PROJ_SLAB = 512
TM_POST = 512
TM_COMB = 512
BM = 512
ROW_WORDS = D_MODEL // 2
SC_WORKERS = 32
SC_CHUNK = 64


def _dot(a, b, dims=(((1,), (0,)), ((), ())), precision=None):
    return lax.dot_general(a, b, dims, precision=precision, preferred_element_type=F32)


_NT = (((1,), (1,)), ((), ()))
_TN = (((0,), (0,)), ((), ()))


def _iota(shape, dim):
    return lax.broadcasted_iota(jnp.int32, shape, dim)


def _vdiv(x, n):
    assert n & (n - 1) == 0
    return lax.shift_right_logical(x, n.bit_length() - 1)


def _vmod(x, n):
    assert n & (n - 1) == 0
    return jnp.bitwise_and(x, n - 1)


def _silu(x):
    return x * jax.nn.sigmoid(x)


def _softplus(x):
    return jnp.maximum(x, 0.0) + jnp.log(1.0 + jnp.exp(-jnp.abs(x)))


def _pack_rows(x):
    w = x.shape[1] // 2
    lo = lax.bitcast_convert_type(x[:, :w].astype(BF16).astype(F32), jnp.uint32)
    hi = lax.bitcast_convert_type(x[:, w:].astype(BF16).astype(F32), jnp.uint32)
    return lax.bitcast_convert_type(lax.shift_right_logical(lo, jnp.uint32(16)) | hi, jnp.int32)


def _unpack_rows(words):
    u = lax.bitcast_convert_type(words, jnp.uint32)
    a = lax.bitcast_convert_type(lax.shift_left(u, jnp.uint32(16)), F32)
    b = lax.bitcast_convert_type(u & jnp.uint32(0xFFFF0000), F32)
    return jnp.concatenate([a, b], axis=-1)


def _seg_sum64(x):
    first = _iota((1, LANES), 1) < 64
    outs = []
    for j in range(x.shape[-1] // LANES):
        blk = x[:, j * LANES:(j + 1) * LANES]
        lo = jnp.sum(jnp.where(first, blk, 0.0), axis=-1, keepdims=True)
        hi = jnp.sum(jnp.where(first, 0.0, blk), axis=-1, keepdims=True)
        outs.append(jnp.where(first, lo, hi))
    return jnp.concatenate(outs, axis=-1)


def _block_mask(shape, row_blk, col_blk):
    keep = _vdiv(_iota(shape, 0), row_blk) == _vdiv(_iota(shape, 1), col_blk)
    return jnp.where(keep, 1.0, 0.0).astype(BF16)


def _block_diag(x, mask):
    reps = mask.shape[0] // x.shape[0]
    return jnp.concatenate([x.astype(BF16)] * reps, axis=0) * mask


def _cumsum_rows(tri, x):
    hi = x.astype(BF16)
    lo = (x - hi.astype(F32)).astype(BF16)
    return _dot(tri, hi) + _dot(tri, lo)


def _rope_kernel(pos_ref, cos_ref, sin_ref):
    lane = _iota((1, RET_HEADS * RET_DK), 1)
    half = RET_DK // 2
    k = _vmod(lane, half).astype(F32)
    inv_freq = jnp.exp(k * (-math.log(ROPE_BASE) / half))
    ang = pos_ref[...].astype(F32) * inv_freq
    first = _vmod(lane, RET_DK) < half
    cos_ref[...] = jnp.cos(ang)
    sin_ref[...] = jnp.where(first, -1.0, 1.0) * jnp.sin(ang)


def _rope_tables(pos_col):
    t = pos_col.shape[0]
    tm = 512
    w = RET_HEADS * RET_DK
    return pl.pallas_call(
        _rope_kernel,
        grid=(t // tm,),
        in_specs=[pl.BlockSpec((tm, 1), lambda i: (i, 0))],
        out_specs=[pl.BlockSpec((tm, w), lambda i: (i, 0))] * 2,
        out_shape=[jax.ShapeDtypeStruct((t, w), F32)] * 2,
        compiler_params=pltpu.CompilerParams(dimension_semantics=("arbitrary",)),
        name="rope_tables",
    )(pos_col)


def _mixproj_kernel(xn_ref, w_ref, cos_ref, sin_ref, retw_ref, convw_ref, convb_ref, dtb_ref, alog_ref,
                    dskip_ref, ssdw_ref, wgk_ref, bgk_ref, glaw_ref, h_ref,
                    proj_a, proj_b, ret_s, ssd_s, gla_s, stage, m_heads, m_groups, m_gla, *, chunks_per_seq):
    C = CHUNK
    i = pl.program_id(0)
    cur = jnp.maximum(i - 1, 0)

    @pl.when(i == 0)
    def _():
        m_heads[...] = _block_mask(m_heads.shape, C, 64)
        m_groups[...] = _block_mask(m_groups.shape, C * SSD_HEADS // SSD_GROUPS, SSD_STATE)
        m_gla[...] = _block_mask(m_gla.shape, C, GLA_DK)
        proj_b[...] = jnp.zeros_like(proj_b)

    @pl.when(lax.rem(cur, chunks_per_seq) == 0)
    def _():
        ret_s[...] = jnp.zeros_like(ret_s)
        ssd_s[...] = jnp.zeros_like(ssd_s)
        gla_s[...] = jnp.zeros_like(gla_s)
        for g in range(MIX_G):
            stage[g, 0:8, :] = jnp.zeros((8, SSD_XBC), F32)

    def conv_act(sq, pref):
        stage[sq, 8:8 + C, :] = pref[sq * C:(sq + 1) * C, COL["sxbc"]:COL["sxbc"] + SSD_XBC]
        acc = convb_ref[...] + convw_ref[0:1, :] * stage[sq, 5:5 + C, :]
        for j in range(1, SSD_CONV):
            acc = acc + convw_ref[j:j + 1, :] * stage[sq, 5 + j:5 + j + C, :]
        stage[sq, 0:8, :] = stage[sq, C:C + 8, :]
        return _silu(acc)

    lane256 = _iota((1, 256), 1)
    head = _vdiv(lane256, 64).astype(F32)
    log_gamma = jnp.log(1.0 - jnp.exp((-5.0 - head) * math.log(2.0)))
    row = _iota((C, 1), 0).astype(F32)
    dist = row - _vmod(lane256, 64).astype(F32)
    ret_intra = jnp.where(dist >= 0, jnp.exp(log_gamma * jnp.maximum(dist, 0.0)), 0.0)
    ret_qdec = jnp.exp(log_gamma * (row + 1.0))
    ret_kdec = jnp.exp(log_gamma * (C - 1.0 - row))
    ret_cdec = jnp.exp(log_gamma * C)
    first_half = _vmod(lane256, RET_DK) < (RET_DK // 2)

    tri = jnp.where(_iota((C, C), 0) >= _iota((C, C), 1), 1.0, 0.0).astype(BF16)
    causal4 = _iota((C, 256), 0) >= _vmod(_iota((C, 256), 1), 64)
    causal8 = _iota((C, 512), 0) >= _vmod(_iota((C, 512), 1), 64)
    eye8 = _iota((C, 512), 0) == _vmod(_iota((C, 512), 1), 64)

    a_neg = -jnp.exp(alog_ref[...])

    def rot(t, cos, sin):
        sw = jnp.where(first_half, pltpu.roll(t, 256 - 32, 1), pltpu.roll(t, 32, 1))
        return t * cos + sw * sin

    def chunk_of(sq, pref):
        def seg(name, width):
            return pref[sq * C:(sq + 1) * C, COL[name]:COL[name] + width]

        xact = conv_act(sq, pref)
        yield

        cos = cos_ref[sq]
        sin = sin_ref[sq]
        q = rot(seg("rq", 256), cos, sin)
        k = rot(seg("rk", 256), cos, sin) * (RET_DK ** -0.5)
        v = seg("rv", 256)
        vb = v.astype(BF16)
        kbd = _block_diag(k, m_heads[...])
        scores = _dot(q.astype(BF16), kbd, _NT) * ret_intra
        vbd = _block_diag(v, m_heads[...])
        yield
        s_prev = ret_s[sq]
        o = _dot(scores.astype(BF16), vbd) + _dot((q * ret_qdec).astype(BF16), s_prev.astype(BF16))
        contrib = _dot((k * ret_kdec).astype(BF16), vb, _TN)
        yield
        keep = _vdiv(_iota((256, 256), 0), RET_DK) == _vdiv(_iota((256, 256), 1), RET_DV)
        ret_s[sq] = jnp.where(keep, ret_cdec * s_prev + contrib, 0.0)
        mu = _seg_sum64(o) * (1.0 / RET_DV)
        oc = o - mu
        var = _seg_sum64(oc * oc) * (1.0 / RET_DV)
        o = oc * lax.rsqrt(var + LN_EPS) * retw_ref[...]
        h_ref[sq, :, 0:RET_W] = (_silu(seg("rg", 256)) * o).astype(BF16)
        yield

        xs = xact[:, 0:SSD_W]
        bm = xact[:, SSD_W:SSD_W + SSD_BC]
        cm = xact[:, SSD_W + SSD_BC:SSD_XBC]
        cmb = cm.astype(BF16)
        dt = _softplus(seg("sdt", SSD_W) + dtb_ref[...])
        acum = _cumsum_rows(tri, dt * a_neg)
        yield
        arow = jnp.sum(jnp.where(eye8, acum, 0.0), axis=0, keepdims=True)
        decay = jnp.exp(jnp.where(causal8, acum - arow, NEG_BIG))
        b8 = _block_diag(bm, m_groups[...])
        cb = _dot(cmb, b8, _NT)
        yield
        m = (cb * decay).astype(BF16)
        xdt = xs * dt
        s2 = ssd_s[sq]
        half = SSD_W // SSD_GROUPS
        ys = []
        for g in range(SSD_GROUPS):
            xbd = _block_diag(xdt[:, g * half:(g + 1) * half], m_heads[...])
            ys.append(_dot(m[:, g * half:(g + 1) * half], xbd))
        y = jnp.concatenate(ys, axis=-1)
        y = y + _dot(cmb, s2.astype(BF16)) * jnp.exp(acum)
        y = y + dskip_ref[...] * xs
        a_last = acum[C - 1:C, :]
        sd = jnp.exp(a_last - acum)
        contrib_s = _dot(bm.astype(BF16), (xdt * sd).astype(BF16), _TN)
        yield
        keep_s = _vdiv(_iota(s2.shape, 0), SSD_STATE) == _vdiv(_iota(s2.shape, 1), half)
        ssd_s[sq] = jnp.where(keep_s, s2 * jnp.exp(a_last) + contrib_s, 0.0)
        yz = y * _silu(seg("sz", SSD_W))
        outs = []
        for g in range(SSD_GROUPS):
            blk = yz[:, g * half:(g + 1) * half]
            ms = jnp.mean(blk * blk, axis=-1, keepdims=True)
            outs.append(blk * lax.rsqrt(ms + NORM_EPS))
        h_ref[sq, :, RET_W:RET_W + SSD_W] = (jnp.concatenate(outs, axis=-1) * ssdw_ref[...]).astype(BF16)

        yield

        gq = seg("gq", GLA_QK) * (GLA_DK ** -0.5)
        gkk = seg("gk", GLA_QK)
        gv = seg("gv", GLA_W)
        gkl = _dot(seg("ggk", 128).astype(BF16), wgk_ref[...].astype(BF16)) + bgk_ref[...]
        yield
        log_a = -_softplus(-gkl) * (1.0 / GLA_TEMP)
        b = _cumsum_rows(tri, log_a)
        yield
        q_t = (gq * jnp.exp(b)).astype(BF16)
        k_t = gkk * jnp.exp(-b)
        kbd_g = _block_diag(k_t, m_gla[...])
        att = jnp.where(causal4, _dot(q_t, kbd_g, _NT), 0.0)
        yield
        vbd_g = _block_diag(gv, m_heads[...])
        st = gla_s[sq]
        og = _dot(att.astype(BF16), vbd_g) + _dot(q_t, st.astype(BF16), _NT)
        b_last = b[C - 1:C, :]
        kd = (gkk * jnp.exp(b_last - b)).astype(BF16)
        contrib_g = _dot(gv.astype(BF16), kd, _TN)
        yield
        keep_g = _vdiv(_iota(st.shape, 0), GLA_DV) == _vdiv(_iota(st.shape, 1), GLA_DK)
        gla_s[sq] = jnp.where(keep_g, st * jnp.exp(b_last) + contrib_g, 0.0)
        ms = _seg_sum64(og * og) * (1.0 / GLA_DV)
        og = og * lax.rsqrt(ms + NORM_EPS) * glaw_ref[...]
        h_ref[sq, :, RET_W + SSD_W:D_MIX] = (_silu(seg("gg", GLA_W)) * og).astype(BF16)

    def step(p_read, p_write):
        xb = xn_ref[...].reshape(MIX_G * C, D_MODEL).astype(BF16)
        edges = list(range(0, NP, PROJ_SLAB)) + [NP]
        slabs = list(zip(edges[:-1], edges[1:]))

        live = [chunk_of(sq, p_read) for sq in range(MIX_G)]
        while live or slabs:
            live = [g for g in live if next(g, "done") != "done"]
            if slabs:
                lo, hi = slabs.pop(0)
                p_write[:, lo:hi] = _dot(xb, w_ref[:, lo:hi])

    @pl.when(lax.rem(i, 2) == 0)
    def _():
        step(proj_b, proj_a)

    @pl.when(lax.rem(i, 2) == 1)
    def _():
        step(proj_a, proj_b)


def _mixproj(x3, w_p, cos_t, sin_t, params):
    batch, seq, _ = x3.shape
    assert batch % MIX_G == 0 and seq % CHUNK == 0
    cps = seq // CHUNK
    n = (batch // MIX_G) * cps

    def cur_map(i):
        c = jnp.clip(i - 1, 0, n - 1)
        return (c // cps, c % cps, 0)

    def next_map(i):
        c = jnp.minimum(i, n - 1)
        return (c // cps, c % cps, 0)

    const = lambda i: (0, 0)
    specs = [pl.BlockSpec((MIX_G, CHUNK, D_MODEL), next_map),
             pl.BlockSpec((D_MODEL, NP), const),
             pl.BlockSpec((MIX_G, CHUNK, 256), cur_map),
             pl.BlockSpec((MIX_G, CHUNK, 256), cur_map)]
    specs += [pl.BlockSpec(p.shape, const) for p in params]
    return pl.pallas_call(
        functools.partial(_mixproj_kernel, chunks_per_seq=cps),
        grid=(n + 1,),
        in_specs=specs,
        out_specs=pl.BlockSpec((MIX_G, CHUNK, D_MIX), cur_map),
        out_shape=jax.ShapeDtypeStruct((batch, seq, D_MIX), BF16),
        scratch_shapes=[pltpu.VMEM((MIX_G * CHUNK, NP), F32),
                        pltpu.VMEM((MIX_G * CHUNK, NP), F32),
                        pltpu.VMEM((MIX_G, 256, 256), F32),
                        pltpu.VMEM((MIX_G, SSD_BC, SSD_W), F32),
                        pltpu.VMEM((MIX_G, GLA_W, GLA_QK), F32),
                        pltpu.VMEM((MIX_G, CHUNK + 8, SSD_XBC), F32),
                        pltpu.VMEM((RET_HEADS * CHUNK, 256), BF16),
                        pltpu.VMEM((SSD_HEADS * CHUNK, SSD_BC), BF16),
                        pltpu.VMEM((GLA_HEADS * CHUNK, GLA_QK), BF16)],
        compiler_params=pltpu.CompilerParams(dimension_semantics=("arbitrary",),
                                             vmem_limit_bytes=VMEM_LIMIT),
        name="inproj_mixer",
    )(x3, w_p, cos_t, sin_t, *params)


def _layer_norm(y, g, b):
    mu = jnp.mean(y, axis=-1, keepdims=True)
    yc = y - mu
    var = jnp.mean(yc * yc, axis=-1, keepdims=True)
    return yc * lax.rsqrt(var + LN_EPS) * g + b


def _post_kernel(h_ref, x_ref, wout_ref, g_ref, b_ref, wrh_ref, wrl_ref, br_ref,
                 x1_ref, x1p_ref, mi_ref, mf_ref, cnt_ref, carry):
    tm = TM_POST

    @pl.when(pl.program_id(0) == 0)
    def _():
        carry[...] = jnp.zeros_like(carry)

    mix = _dot(h_ref[...], wout_ref[...])
    x1 = _layer_norm(DEEPNORM_ALPHA * x_ref[...] + mix, g_ref[...], b_ref[...])
    x1_ref[...] = x1
    x1p_ref[...] = _pack_rows(x1)

    x_hi = x1.astype(BF16)
    x_lo = (x1 - x_hi.astype(F32)).astype(BF16)
    logits = (_dot(x_hi, wrh_ref[...]) + _dot(x_lo, wrh_ref[...]) + _dot(x_hi, wrl_ref[...])
              + br_ref[...])
    lane_i = _iota((tm, LANES), 1)
    lane = lane_i.astype(F32)
    work = logits
    vals, idxs = [], []
    multi = jnp.zeros((tm, LANES), F32)
    for _ in range(TOP_K):
        m = jnp.max(work, axis=-1, keepdims=True)
        idx = jnp.min(jnp.where(work == m, lane, float(LANES)), axis=-1, keepdims=True)
        hit = lane == idx
        multi = multi + hit.astype(F32)
        work = jnp.where(hit, -jnp.inf, work)
        vals.append(m)
        idxs.append(idx)
    exps = [jnp.exp(v - vals[0]) for v in vals]
    denom = exps[0] + exps[1] + exps[2] + exps[3]
    gates = [e / denom for e in exps]

    before = (_iota((tm, tm), 0) > _iota((tm, tm), 1)).astype(BF16)
    prior = _dot(before, multi.astype(BF16)) + carry[...]
    mi = jnp.zeros((tm, LANES), F32)
    mf = jnp.zeros((tm, LANES), F32)
    for kk in range(TOP_K):
        rank = jnp.sum(jnp.where(lane == idxs[kk], prior, 0.0), axis=-1, keepdims=True)
        mi = jnp.where(lane_i == kk, idxs[kk], mi)
        mi = jnp.where(lane_i == TOP_K + kk, rank, mi)
        mf = jnp.where(lane_i == kk, gates[kk], mf)
    mi_ref[...] = jnp.transpose(mi)[0:2 * TOP_K, :].astype(jnp.int32)
    mf_ref[...] = mf
    carry[...] = carry[...] + jnp.sum(multi, axis=0, keepdims=True)
    cnt_ref[...] = jnp.broadcast_to(carry[...], cnt_ref.shape)


def _post(h, x2, w_out_b, ln_g, ln_b, wr_hi, wr_lo, br_p):
    t = x2.shape[0]
    tm = TM_POST
    row = lambda i: (i, 0)
    const = lambda i: (0, 0)
    return pl.pallas_call(
        _post_kernel,
        grid=(t // tm,),
        in_specs=[pl.BlockSpec((tm, D_MIX), row), pl.BlockSpec((tm, D_MODEL), row),
                  pl.BlockSpec((D_MIX, D_MODEL), const), pl.BlockSpec((1, D_MODEL), const),
                  pl.BlockSpec((1, D_MODEL), const), pl.BlockSpec((D_MODEL, LANES), const),
                  pl.BlockSpec((D_MODEL, LANES), const), pl.BlockSpec((1, LANES), const)],
        out_specs=[pl.BlockSpec((tm, D_MODEL), row), pl.BlockSpec((tm, ROW_WORDS), row),
                   pl.BlockSpec((2 * TOP_K, tm), lambda i: (0, i)), pl.BlockSpec((tm, LANES), row),
                   pl.BlockSpec((8, LANES), const)],
        out_shape=[jax.ShapeDtypeStruct((t, D_MODEL), F32), jax.ShapeDtypeStruct((t, ROW_WORDS), jnp.int32),
                   jax.ShapeDtypeStruct((2 * TOP_K, t), jnp.int32), jax.ShapeDtypeStruct((t, LANES), F32),
                   jax.ShapeDtypeStruct((8, LANES), F32)],
        scratch_shapes=[pltpu.VMEM((1, LANES), F32)],
        compiler_params=pltpu.CompilerParams(dimension_semantics=("arbitrary",),
                                             vmem_limit_bytes=VMEM_LIMIT),
        name="outproj_ln_router",
    )(h, x2, w_out_b, ln_g, ln_b, wr_hi, wr_lo, br_p)


def _ffn_kernel(be_ref, nv_ref, x_ref, wg_ref, bg_ref, wu_ref, bu_ref, wd_ref, bd_ref, o_ref,
                wg_b, wu_b, wd_b):
    i = pl.program_id(0)
    valid = i < nv_ref[0]
    e = be_ref[i]
    prev = be_ref[jnp.maximum(i - 1, 0)]
    fresh = jnp.logical_or(i == 0, e != prev)

    @pl.when(jnp.logical_and(valid, fresh))
    def _():
        wg_b[...] = wg_ref[0, 0].astype(BF16)
        wu_b[...] = wu_ref[0, 0].astype(BF16)
        wd_b[...] = wd_ref[0, 0].astype(BF16)

    @pl.when(valid)
    def _():
        x = _unpack_rows(x_ref[...]).astype(BF16)
        hg = jnp.minimum(_dot(x, wg_b[...]) + bg_ref[0, 0], SWIGLU_LIMIT)
        hu = jnp.clip(_dot(x, wu_b[...]) + bu_ref[0, 0], -SWIGLU_LIMIT, SWIGLU_LIMIT)
        hh = (hu + 1.0) * hg * jax.nn.sigmoid(SWIGLU_ALPHA * hg)
        o_ref[...] = _pack_rows(_dot(hh.astype(BF16), wd_b[...]) + bd_ref[0, 0])


def _ffn(layer, block_expert, n_valid, xin, wg, bg, wu, bu, wd, bd):
    cap = xin.shape[0]
    nb = cap // BM

    def blk(i, be, nv):
        return jnp.maximum(jnp.minimum(i, nv[0] - 1), 0)

    row = lambda i, be, nv: (blk(i, be, nv), 0)
    wmap = lambda i, be, nv: (layer, be[blk(i, be, nv)], 0, 0)
    grid_spec = pltpu.PrefetchScalarGridSpec(
        num_scalar_prefetch=2,
        grid=(nb,),
        in_specs=[pl.BlockSpec((BM, ROW_WORDS), row),
                  pl.BlockSpec((1, 1, D_MODEL, D_FF), wmap), pl.BlockSpec((1, 1, 1, D_FF), wmap),
                  pl.BlockSpec((1, 1, D_MODEL, D_FF), wmap), pl.BlockSpec((1, 1, 1, D_FF), wmap),
                  pl.BlockSpec((1, 1, D_FF, D_MODEL), wmap), pl.BlockSpec((1, 1, 1, D_MODEL), wmap)],
        out_specs=pl.BlockSpec((BM, ROW_WORDS), row),
        scratch_shapes=[pltpu.VMEM((D_MODEL, D_FF), BF16), pltpu.VMEM((D_MODEL, D_FF), BF16),
                        pltpu.VMEM((D_FF, D_MODEL), BF16)],
    )
    return pl.pallas_call(
        _ffn_kernel,
        grid_spec=grid_spec,
        out_shape=jax.ShapeDtypeStruct((cap, ROW_WORDS), jnp.int32),
        compiler_params=pltpu.CompilerParams(dimension_semantics=("arbitrary",),
                                             vmem_limit_bytes=VMEM_LIMIT),
        name="expert_ffn",
    )(block_expert, n_valid, xin, wg, bg, wu, bu, wd, bd)


def _sc_gather(table, idx3):
    nw, n_chunks, ch = idx3.shape
    width = table.shape[1]
    per_worker = n_chunks * ch
    mesh = plsc.VectorSubcoreMesh(core_axis_name="c", subcore_axis_name="s")
    n_cores = mesh.num_cores
    assert nw == n_cores * mesh.num_subcores and n_chunks % 2 == 0 and ch == SC_CHUNK

    def body(table_hbm, idx_hbm, out_hbm, idx_v, rows0, rows1, sem_g0, sem_g1, sem_w0, sem_w1):
        wid = lax.axis_index("s") * n_cores + lax.axis_index("c")
        base = wid * per_worker
        pltpu.sync_copy(idx_hbm.at[wid], idx_v)

        @pl.loop(0, n_chunks, step=2)
        def _(c):
            g0 = pltpu.async_copy(table_hbm.at[idx_v.at[c]], rows0, sem_g0)
            g1 = pltpu.async_copy(table_hbm.at[idx_v.at[c + 1]], rows1, sem_g1)
            g0.wait()
            w0 = pltpu.async_copy(rows0, out_hbm.at[pl.ds(base + c * ch, ch)], sem_w0)
            g1.wait()
            w1 = pltpu.async_copy(rows1, out_hbm.at[pl.ds(base + (c + 1) * ch, ch)], sem_w1)
            w0.wait()
            w1.wait()

    return pl.kernel(
        body,
        out_type=jax.ShapeDtypeStruct((nw * per_worker, width), table.dtype),
        mesh=mesh,
        scratch_types=[pltpu.VMEM((n_chunks, ch), jnp.int32),
                       pltpu.VMEM((ch, width), table.dtype), pltpu.VMEM((ch, width), table.dtype),
                       pltpu.SemaphoreType.DMA, pltpu.SemaphoreType.DMA,
                       pltpu.SemaphoreType.DMA, pltpu.SemaphoreType.DMA],
        name="sc_row_gather",
    )(table, idx3)


def _sc_scatter(rows, idx3, n_out):
    nw, n_lists, ch = idx3.shape
    n_chunks = n_lists // TOP_K
    width = rows.shape[1]
    per_worker = n_chunks * ch
    mesh = plsc.VectorSubcoreMesh(core_axis_name="c", subcore_axis_name="s")
    n_cores = mesh.num_cores
    assert nw == n_cores * mesh.num_subcores and n_chunks % 2 == 0 and ch == SC_CHUNK
    assert nw * per_worker == rows.shape[0]

    def body(rows_hbm, idx_hbm, out_hbm, idx_v, buf0, buf1, sem_r0, sem_r1, sem_w0, sem_w1):
        wid = lax.axis_index("s") * n_cores + lax.axis_index("c")
        base = wid * per_worker
        pltpu.sync_copy(idx_hbm.at[wid], idx_v)

        @pl.loop(0, n_chunks, step=2)
        def _(c):
            r0 = pltpu.async_copy(rows_hbm.at[pl.ds(base + c * ch, ch)], buf0, sem_r0)
            r1 = pltpu.async_copy(rows_hbm.at[pl.ds(base + (c + 1) * ch, ch)], buf1, sem_r1)
            r0.wait()
            w0 = [pltpu.async_copy(buf0, out_hbm.at[idx_v.at[c * TOP_K + k]], sem_w0) for k in range(TOP_K)]
            r1.wait()
            w1 = [pltpu.async_copy(buf1, out_hbm.at[idx_v.at[(c + 1) * TOP_K + k]], sem_w1) for k in range(TOP_K)]
            for w in w0 + w1:
                w.wait()

    return pl.kernel(
        body,
        out_type=jax.ShapeDtypeStruct((n_out, width), rows.dtype),
        mesh=mesh,
        scratch_types=[pltpu.VMEM((n_lists, ch), jnp.int32),
                       pltpu.VMEM((ch, width), rows.dtype), pltpu.VMEM((ch, width), rows.dtype),
                       pltpu.SemaphoreType.DMA, pltpu.SemaphoreType.DMA,
                       pltpu.SemaphoreType.DMA, pltpu.SemaphoreType.DMA],
        name="sc_row_scatter",
    )(rows, idx3)


def _combine_kernel(x_ref, rows_ref, mf_ref, g_ref, b_ref, o_ref):
    mf = mf_ref[...]
    y = DEEPNORM_ALPHA * x_ref[...]
    for kk in range(TOP_K):
        y = y + mf[:, kk:kk + 1] * _unpack_rows(rows_ref[kk])
    o_ref[...] = _layer_norm(y, g_ref[...], b_ref[...])


def _combine(x1, rows, mf, ln_g, ln_b):
    t = x1.shape[0]
    tm = TM_COMB
    row = lambda i: (i, 0)
    const = lambda i: (0, 0)
    return pl.pallas_call(
        _combine_kernel,
        grid=(t // tm,),
        in_specs=[pl.BlockSpec((tm, D_MODEL), row),
                  pl.BlockSpec((TOP_K, tm, ROW_WORDS), lambda i: (0, i, 0)),
                  pl.BlockSpec((tm, LANES), row),
                  pl.BlockSpec((1, D_MODEL), const), pl.BlockSpec((1, D_MODEL), const)],
        out_specs=pl.BlockSpec((tm, D_MODEL), row),
        out_shape=jax.ShapeDtypeStruct((t, D_MODEL), F32),
        compiler_params=pltpu.CompilerParams(dimension_semantics=("arbitrary",),
                                             vmem_limit_bytes=VMEM_LIMIT),
        name="combine_ln",
    )(x1, rows, mf, ln_g, ln_b)


def _relayout_w_in(w):
    widths = (256, 256, 256, 256, SSD_W, SSD_XBC, SSD_HEADS, GLA_QK, GLA_QK, GLA_W, GLA_RANK, GLA_W)
    offs = [0]
    for wd in widths:
        offs.append(offs[-1] + wd)
    parts = [w[:, offs[i]:offs[i + 1]] for i in range(len(widths))]
    parts[6] = jnp.repeat(parts[6], SSD_HEAD_DIM, axis=1)
    parts[10] = jnp.pad(parts[10], ((0, 0), (0, LANES - GLA_RANK)))
    return jnp.concatenate(parts, axis=1).astype(BF16)


def _rep_heads(p):
    return jnp.repeat(p, SSD_HEAD_DIM)[None, :]


def kernel(x, positions, w_in, w_out, ret_norm_w, ssd_conv_w, ssd_conv_b, ssd_dt_bias, ssd_a_log, ssd_d,
           ssd_norm_w, gla_w_gk2, gla_b_gk2, gla_norm_w, ln1_g, ln1_b, w_router, b_router, w_gate, b_gate,
           w_up, b_up, w_down, b_down, ln2_g, ln2_b):
    batch, seq, d = x.shape
    t = batch * seq
    depth = w_in.shape[0]
    assert d == D_MODEL and t % TM_POST == 0 and t % TM_COMB == 0
    n_assign = t * TOP_K
    nb = n_assign // BM + N_EXPERTS
    cap = nb * BM

    cos_t, sin_t = _rope_tables(positions.reshape(t, 1))
    cos_t = cos_t.reshape(batch, seq, -1)
    sin_t = sin_t.reshape(batch, seq, -1)
    x2 = x.reshape(t, d)

    for l in range(depth):
        params = (ret_norm_w[l][None, :], ssd_conv_w[l], ssd_conv_b[l][None, :], _rep_heads(ssd_dt_bias[l]),
                  _rep_heads(ssd_a_log[l]), _rep_heads(ssd_d[l]), ssd_norm_w[l][None, :],
                  jnp.pad(gla_w_gk2[l], ((0, LANES - GLA_RANK), (0, 0))), gla_b_gk2[l][None, :],
                  gla_norm_w[l][None, :])
        h = _mixproj(x2.reshape(batch, seq, d), _relayout_w_in(w_in[l]), cos_t, sin_t, params).reshape(t, D_MIX)

        wr_p = jnp.pad(w_router[l], ((0, 0), (0, LANES - N_EXPERTS)))
        br_p = jnp.pad(b_router[l], (0, LANES - N_EXPERTS), constant_values=NEG_BIG)[None, :]
        wr_hi = wr_p.astype(BF16)
        wr_lo = (wr_p - wr_hi.astype(F32)).astype(BF16)
        x1, x1p, mi, mf, cnt = _post(h, x2, w_out[l].astype(BF16), ln1_g[l][None, :], ln1_b[l][None, :],
                                     wr_hi, wr_lo, br_p)

        counts = cnt[0, :N_EXPERTS].astype(jnp.int32)
        padded = (counts + BM - 1) // BM * BM
        end_padded = jnp.cumsum(padded)
        start_padded = end_padded - padded
        top_idx = mi[:TOP_K]
        start_of = jnp.sum(jnp.where(top_idx[None] == jnp.arange(N_EXPERTS, dtype=jnp.int32)[:, None, None],
                                     start_padded[:, None, None], 0), axis=0)
        dest = start_of + mi[TOP_K:]
        block_start = jnp.arange(nb, dtype=jnp.int32) * BM
        block_expert = jnp.minimum(jnp.sum((end_padded[None, :] <= block_start[:, None]).astype(jnp.int32), axis=1),
                                   N_EXPERTS - 1)
        n_valid = (end_padded[-1:] // BM).astype(jnp.int32)

        scatter_idx = dest.reshape(TOP_K, SC_WORKERS, -1, SC_CHUNK).transpose(1, 2, 0, 3)
        xin = _sc_scatter(x1p, scatter_idx.reshape(SC_WORKERS, -1, SC_CHUNK), cap)
        yb = _ffn(l, block_expert, n_valid, xin, w_gate, b_gate[:, :, None, :], w_up, b_up[:, :, None, :],
                  w_down, b_down[:, :, None, :])
        rows = _sc_gather(yb, dest.reshape(SC_WORKERS, -1, SC_CHUNK))
        x2 = _combine(x1, rows.reshape(TOP_K, t, ROW_WORDS), mf, ln2_g[l][None, :], ln2_b[l][None, :])
    return x2.reshape(batch, seq, d)
```

```python
import functools
import math

import jax
import jax.numpy as jnp
from jax import lax
from jax.experimental import pallas as pl
from jax.experimental.pallas import tpu as pltpu
from jax.experimental.pallas import tpu_sc as plsc

F32 = jnp.float32
BF16 = jnp.bfloat16

D_MODEL = 1024
CHUNK = 64
RET_HEADS, RET_DK, RET_DV = 4, 64, 64
RET_W = RET_HEADS * RET_DV
SSD_HEADS, SSD_HEAD_DIM, SSD_STATE, SSD_GROUPS, SSD_CONV = 8, 64, 64, 2, 4
SSD_W = SSD_HEADS * SSD_HEAD_DIM
SSD_BC = SSD_GROUPS * SSD_STATE
SSD_XBC = SSD_W + 2 * SSD_BC
GLA_HEADS, GLA_DK, GLA_DV, GLA_RANK, GLA_TEMP = 4, 32, 64, 16, 16.0
GLA_QK = GLA_HEADS * GLA_DK
GLA_W = GLA_HEADS * GLA_DV
D_MIX = RET_W + SSD_W + GLA_W
N_EXPERTS, TOP_K, D_FF = 32, 4, 1024
SWIGLU_LIMIT, SWIGLU_ALPHA = 7.0, 1.702
ROPE_BASE = 10000.0
LN_EPS, NORM_EPS = 1e-5, 1e-6
DEPTH = 2
DEEPNORM_ALPHA = (2.0 * DEPTH) ** 0.25

LANES = 128
ROPE_W = LANES
NEG_BIG = -1e30
VMEM_LIMIT = 56 * 1024 * 1024

_SEGS = (("rq", 256), ("rk", 256), ("rv", 256), ("rg", 256), ("sz", SSD_W), ("sxbc", SSD_XBC),
         ("sdt", SSD_W), ("gq", 128), ("gk", 128), ("gv", 256), ("ggk", 128), ("gg", 256))
COL = {}
_off = 0
for _n, _w in _SEGS:
    COL[_n] = _off
    _off += _w
NP = _off

MIX_G = 4
PROJ_SLAB = 512
TM_POST = 1024
POST_SPLIT = 4
TM_COMB = 512
BM = 512
ROW_WORDS = D_MODEL // 2
SC_WORKERS = 32
SC_CHUNK = 64


def _dot(a, b, dims=(((1,), (0,)), ((), ())), precision=None):
    return lax.dot_general(a, b, dims, precision=precision, preferred_element_type=F32)


_NT = (((1,), (1,)), ((), ()))
_TN = (((0,), (0,)), ((), ()))


def _iota(shape, dim):
    return lax.broadcasted_iota(jnp.int32, shape, dim)


def _vdiv(x, n):
    assert n & (n - 1) == 0
    return lax.shift_right_logical(x, n.bit_length() - 1)


def _vmod(x, n):
    assert n & (n - 1) == 0
    return jnp.bitwise_and(x, n - 1)


def _silu(x):
    return x * jax.nn.sigmoid(x)


def _softplus(x):
    return jnp.maximum(x, 0.0) + jnp.log(1.0 + jnp.exp(-jnp.abs(x)))


def _pack_rows(x):
    w = x.shape[1] // 2
    lo = lax.bitcast_convert_type(x[:, :w].astype(BF16).astype(F32), jnp.uint32)
    hi = lax.bitcast_convert_type(x[:, w:].astype(BF16).astype(F32), jnp.uint32)
    return lax.bitcast_convert_type(lax.shift_right_logical(lo, jnp.uint32(16)) | hi, jnp.int32)


def _unpack_rows(words):
    u = lax.bitcast_convert_type(words, jnp.uint32)
    a = lax.bitcast_convert_type(lax.shift_left(u, jnp.uint32(16)), F32)
    b = lax.bitcast_convert_type(u & jnp.uint32(0xFFFF0000), F32)
    return jnp.concatenate([a, b], axis=-1)


def _seg_sum64(x):
    first = _iota((1, LANES), 1) < 64
    outs = []
    for j in range(x.shape[-1] // LANES):
        blk = x[:, j * LANES:(j + 1) * LANES]
        lo = jnp.sum(jnp.where(first, blk, 0.0), axis=-1, keepdims=True)
        hi = jnp.sum(jnp.where(first, 0.0, blk), axis=-1, keepdims=True)
        outs.append(jnp.where(first, lo, hi))
    return jnp.concatenate(outs, axis=-1)


def _block_mask(shape, row_blk, col_blk):
    keep = _vdiv(_iota(shape, 0), row_blk) == _vdiv(_iota(shape, 1), col_blk)
    return jnp.where(keep, 1.0, 0.0).astype(BF16)


def _block_diag(x, mask):
    reps = mask.shape[0] // x.shape[0]
    return jnp.concatenate([x.astype(BF16)] * reps, axis=0) * mask


def _cumsum_rows(tri, x):
    hi = x.astype(BF16)
    lo = (x - hi.astype(F32)).astype(BF16)
    return _dot(tri, hi) + _dot(tri, lo)


def _rope_kernel(pos_ref, cos_ref, sin_ref):
    lane = _iota((1, ROPE_W), 1)
    half = RET_DK // 2
    k = _vmod(lane, half).astype(F32)
    inv_freq = jnp.exp(k * (-math.log(ROPE_BASE) / half))
    ang = pos_ref[...].astype(F32) * inv_freq
    first = _vmod(lane, RET_DK) < half
    cos_ref[...] = jnp.cos(ang)
    sin_ref[...] = jnp.where(first, -1.0, 1.0) * jnp.sin(ang)


def _rope_tables(pos_col):
    t = pos_col.shape[0]
    tm = 512
    w = ROPE_W
    return pl.pallas_call(
        _rope_kernel,
        grid=(t // tm,),
        in_specs=[pl.BlockSpec((tm, 1), lambda i: (i, 0))],
        out_specs=[pl.BlockSpec((tm, w), lambda i: (i, 0))] * 2,
        out_shape=[jax.ShapeDtypeStruct((t, w), F32)] * 2,
        compiler_params=pltpu.CompilerParams(dimension_semantics=("arbitrary",)),
        name="rope_tables",
    )(pos_col)


def _mixproj_kernel(xn_ref, w_ref, cos_ref, sin_ref, retw_ref, convw_ref, convb_ref, dtb_ref, alog_ref,
                    dskip_ref, ssdw_ref, wgk_ref, bgk_ref, glaw_ref, h_ref,
                    proj_a, proj_b, ret_s, ssd_s, gla_s, stage, m_heads, m_groups, m_gla, *, chunks_per_seq):
    C = CHUNK
    i = pl.program_id(0)
    cur = jnp.maximum(i - 1, 0)

    @pl.when(i == 0)
    def _():
        m_heads[...] = _block_mask(m_heads.shape, C, 64)
        m_groups[...] = _block_mask(m_groups.shape, C * SSD_HEADS // SSD_GROUPS, SSD_STATE)
        m_gla[...] = _block_mask(m_gla.shape, C, GLA_DK)
        proj_b[...] = jnp.zeros_like(proj_b)

    @pl.when(lax.rem(cur, chunks_per_seq) == 0)
    def _():
        ret_s[...] = jnp.zeros_like(ret_s)
        ssd_s[...] = jnp.zeros_like(ssd_s)
        gla_s[...] = jnp.zeros_like(gla_s)
        for g in range(MIX_G):
            stage[g, 0:8, :] = jnp.zeros((8, SSD_XBC), F32)

    def conv_act(sq, pref):
        stage[sq, 8:8 + C, :] = pref[sq * C:(sq + 1) * C, COL["sxbc"]:COL["sxbc"] + SSD_XBC]
        acc = convb_ref[...] + convw_ref[0:1, :] * stage[sq, 5:5 + C, :]
        for j in range(1, SSD_CONV):
            acc = acc + convw_ref[j:j + 1, :] * stage[sq, 5 + j:5 + j + C, :]
        stage[sq, 0:8, :] = stage[sq, C:C + 8, :]
        return _silu(acc)

    lane256 = _iota((1, 256), 1)
    head = _vdiv(lane256, 64).astype(F32)
    log_gamma = jnp.log(1.0 - jnp.exp((-5.0 - head) * math.log(2.0)))
    row = _iota((C, 1), 0).astype(F32)
    dist = row - _vmod(lane256, 64).astype(F32)
    ret_intra = jnp.where(dist >= 0, jnp.exp(log_gamma * jnp.maximum(dist, 0.0)), 0.0)
    ret_qdec = jnp.exp(log_gamma * (row + 1.0))
    ret_kdec = jnp.exp(log_gamma * (C - 1.0 - row))
    ret_cdec = jnp.exp(log_gamma * C)
    first_half = _vmod(lane256, RET_DK) < (RET_DK // 2)

    tri = jnp.where(_iota((C, C), 0) >= _iota((C, C), 1), 1.0, 0.0).astype(BF16)
    causal4 = _iota((C, 256), 0) >= _vmod(_iota((C, 256), 1), 64)
    causal8 = _iota((C, 512), 0) >= _vmod(_iota((C, 512), 1), 64)
    eye8 = _iota((C, 512), 0) == _vmod(_iota((C, 512), 1), 64)

    a_neg = -jnp.exp(alog_ref[...])

    def rot(t, cos, sin):
        sw = jnp.where(first_half, pltpu.roll(t, 256 - 32, 1), pltpu.roll(t, 32, 1))
        return t * cos + sw * sin

    def chunk_of(sq, pref):
        def seg(name, width):
            return pref[sq * C:(sq + 1) * C, COL[name]:COL[name] + width]

        xact = conv_act(sq, pref)
        yield

        reps = RET_HEADS * RET_DK // ROPE_W
        cos = jnp.concatenate([cos_ref[sq]] * reps, axis=-1)
        sin = jnp.concatenate([sin_ref[sq]] * reps, axis=-1)
        q = rot(seg("rq", 256), cos, sin)
        k = rot(seg("rk", 256), cos, sin) * (RET_DK ** -0.5)
        v = seg("rv", 256)
        vb = v.astype(BF16)
        kbd = _block_diag(k, m_heads[...])
        scores = _dot(q.astype(BF16), kbd, _NT) * ret_intra
        vbd = _block_diag(v, m_heads[...])
        yield
        s_prev = ret_s[sq]
        o = _dot(scores.astype(BF16), vbd) + _dot((q * ret_qdec).astype(BF16), s_prev.astype(BF16))
        contrib = _dot((k * ret_kdec).astype(BF16), vb, _TN)
        yield
        keep = _vdiv(_iota((256, 256), 0), RET_DK) == _vdiv(_iota((256, 256), 1), RET_DV)
        ret_s[sq] = jnp.where(keep, ret_cdec * s_prev + contrib, 0.0)
        mu = _seg_sum64(o) * (1.0 / RET_DV)
        oc = o - mu
        var = _seg_sum64(oc * oc) * (1.0 / RET_DV)
        o = oc * lax.rsqrt(var + LN_EPS) * retw_ref[...]
        h_ref[sq, :, 0:RET_W] = (_silu(seg("rg", 256)) * o).astype(BF16)
        yield

        xs = xact[:, 0:SSD_W]
        bm = xact[:, SSD_W:SSD_W + SSD_BC]
        cm = xact[:, SSD_W + SSD_BC:SSD_XBC]
        cmb = cm.astype(BF16)
        dt = _softplus(seg("sdt", SSD_W) + dtb_ref[...])
        acum = _cumsum_rows(tri, dt * a_neg)
        yield
        arow = jnp.sum(jnp.where(eye8, acum, 0.0), axis=0, keepdims=True)
        decay = jnp.exp(jnp.where(causal8, acum - arow, NEG_BIG))
        b8 = _block_diag(bm, m_groups[...])
        cb = _dot(cmb, b8, _NT)
        yield
        m = (cb * decay).astype(BF16)
        xdt = xs * dt
        s2 = ssd_s[sq]
        half = SSD_W // SSD_GROUPS
        ys = []
        for g in range(SSD_GROUPS):
            xbd = _block_diag(xdt[:, g * half:(g + 1) * half], m_heads[...])
            ys.append(_dot(m[:, g * half:(g + 1) * half], xbd))
        y = jnp.concatenate(ys, axis=-1)
        y = y + _dot(cmb, s2.astype(BF16)) * jnp.exp(acum)
        y = y + dskip_ref[...] * xs
        a_last = acum[C - 1:C, :]
        sd = jnp.exp(a_last - acum)
        contrib_s = _dot(bm.astype(BF16), (xdt * sd).astype(BF16), _TN)
        yield
        keep_s = _vdiv(_iota(s2.shape, 0), SSD_STATE) == _vdiv(_iota(s2.shape, 1), half)
        ssd_s[sq] = jnp.where(keep_s, s2 * jnp.exp(a_last) + contrib_s, 0.0)
        yz = y * _silu(seg("sz", SSD_W))
        outs = []
        for g in range(SSD_GROUPS):
            blk = yz[:, g * half:(g + 1) * half]
            ms = jnp.mean(blk * blk, axis=-1, keepdims=True)
            outs.append(blk * lax.rsqrt(ms + NORM_EPS))
        h_ref[sq, :, RET_W:RET_W + SSD_W] = (jnp.concatenate(outs, axis=-1) * ssdw_ref[...]).astype(BF16)

        yield

        gq = seg("gq", GLA_QK) * (GLA_DK ** -0.5)
        gkk = seg("gk", GLA_QK)
        gv = seg("gv", GLA_W)
        gkl = _dot(seg("ggk", 128).astype(BF16), wgk_ref[...].astype(BF16)) + bgk_ref[...]
        yield
        log_a = -_softplus(-gkl) * (1.0 / GLA_TEMP)
        b = _cumsum_rows(tri, log_a)
        yield
        q_t = (gq * jnp.exp(b)).astype(BF16)
        k_t = gkk * jnp.exp(-b)
        kbd_g = _block_diag(k_t, m_gla[...])
        att = jnp.where(causal4, _dot(q_t, kbd_g, _NT), 0.0)
        yield
        vbd_g = _block_diag(gv, m_heads[...])
        st = gla_s[sq]
        og = _dot(att.astype(BF16), vbd_g) + _dot(q_t, st.astype(BF16), _NT)
        b_last = b[C - 1:C, :]
        kd = (gkk * jnp.exp(b_last - b)).astype(BF16)
        contrib_g = _dot(gv.astype(BF16), kd, _TN)
        yield
        keep_g = _vdiv(_iota(st.shape, 0), GLA_DV) == _vdiv(_iota(st.shape, 1), GLA_DK)
        gla_s[sq] = jnp.where(keep_g, st * jnp.exp(b_last) + contrib_g, 0.0)
        ms = _seg_sum64(og * og) * (1.0 / GLA_DV)
        og = og * lax.rsqrt(ms + NORM_EPS) * glaw_ref[...]
        h_ref[sq, :, RET_W + SSD_W:D_MIX] = (_silu(seg("gg", GLA_W)) * og).astype(BF16)

    def step(p_read, p_write):
        xb = xn_ref[...].reshape(MIX_G * C, D_MODEL).astype(BF16)
        edges = list(range(0, NP, PROJ_SLAB)) + [NP]
        slabs = list(zip(edges[:-1], edges[1:]))

        live = [chunk_of(sq, p_read) for sq in range(MIX_G)]
        while live or slabs:
            live = [g for g in live if next(g, "done") != "done"]
            if slabs:
                lo, hi = slabs.pop(0)
                p_write[:, lo:hi] = _dot(xb, w_ref[:, lo:hi])

    @pl.when(lax.rem(i, 2) == 0)
    def _():
        step(proj_b, proj_a)

    @pl.when(lax.rem(i, 2) == 1)
    def _():
        step(proj_a, proj_b)


def _mixproj(x3, w_p, cos_t, sin_t, params):
    batch, seq, _ = x3.shape
    assert batch % MIX_G == 0 and seq % CHUNK == 0
    cps = seq // CHUNK
    n = (batch // MIX_G) * cps

    def cur_map(i):
        c = jnp.clip(i - 1, 0, n - 1)
        return (c // cps, c % cps, 0)

    def next_map(i):
        c = jnp.minimum(i, n - 1)
        return (c // cps, c % cps, 0)

    const = lambda i: (0, 0)
    specs = [pl.BlockSpec((MIX_G, CHUNK, D_MODEL), next_map),
             pl.BlockSpec((D_MODEL, NP), const),
             pl.BlockSpec((MIX_G, CHUNK, ROPE_W), cur_map),
             pl.BlockSpec((MIX_G, CHUNK, ROPE_W), cur_map)]
    specs += [pl.BlockSpec(p.shape, const) for p in params]
    return pl.pallas_call(
        functools.partial(_mixproj_kernel, chunks_per_seq=cps),
        grid=(n + 1,),
        in_specs=specs,
        out_specs=pl.BlockSpec((MIX_G, CHUNK, D_MIX), cur_map),
        out_shape=jax.ShapeDtypeStruct((batch, seq, D_MIX), BF16),
        scratch_shapes=[pltpu.VMEM((MIX_G * CHUNK, NP), F32),
                        pltpu.VMEM((MIX_G * CHUNK, NP), F32),
                        pltpu.VMEM((MIX_G, 256, 256), F32),
                        pltpu.VMEM((MIX_G, SSD_BC, SSD_W), F32),
                        pltpu.VMEM((MIX_G, GLA_W, GLA_QK), F32),
                        pltpu.VMEM((MIX_G, CHUNK + 8, SSD_XBC), F32),
                        pltpu.VMEM((RET_HEADS * CHUNK, 256), BF16),
                        pltpu.VMEM((SSD_HEADS * CHUNK, SSD_BC), BF16),
                        pltpu.VMEM((GLA_HEADS * CHUNK, GLA_QK), BF16)],
        compiler_params=pltpu.CompilerParams(dimension_semantics=("arbitrary",),
                                             vmem_limit_bytes=VMEM_LIMIT),
        name="inproj_mixer",
    )(x3, w_p, cos_t, sin_t, *params)


def _layer_norm(y, g, b):
    mu = jnp.mean(y, axis=-1, keepdims=True)
    yc = y - mu
    var = jnp.mean(yc * yc, axis=-1, keepdims=True)
    return yc * lax.rsqrt(var + LN_EPS) * g + b


def _post_kernel(h_ref, x_ref, wout_ref, g_ref, b_ref, wrh_ref, wrl_ref, br_ref,
                 x1_ref, x1p_ref, mi_ref, mf_ref, cnt_ref, carry):
    sub = TM_POST // POST_SPLIT

    @pl.when(pl.program_id(0) == 0)
    def _():
        carry[...] = jnp.zeros_like(carry)

    lane_i = _iota((sub, LANES), 1)
    lane = lane_i.astype(F32)
    found = {}

    def sub_tile(part):
        rows = slice(part * sub, (part + 1) * sub)
        mix = _dot(h_ref[rows, :], wout_ref[...])
        yield
        x1 = _layer_norm(DEEPNORM_ALPHA * x_ref[rows, :] + mix, g_ref[...], b_ref[...])
        x1_ref[rows, :] = x1
        x1p_ref[rows, :] = _pack_rows(x1)
        x_hi = x1.astype(BF16)
        x_lo = (x1 - x_hi.astype(F32)).astype(BF16)
        logits = (_dot(x_hi, wrh_ref[...]) + _dot(x_lo, wrh_ref[...]) + _dot(x_hi, wrl_ref[...])
                  + br_ref[...])
        yield
        work = logits
        vals, idxs = [], []
        multi = jnp.zeros((sub, LANES), F32)
        for _ in range(TOP_K):
            m = jnp.max(work, axis=-1, keepdims=True)
            idx = jnp.min(jnp.where(work == m, lane, float(LANES)), axis=-1, keepdims=True)
            hit = lane == idx
            multi = multi + hit.astype(F32)
            work = jnp.where(hit, -jnp.inf, work)
            vals.append(m)
            idxs.append(idx)
            yield
        exps = [jnp.exp(v - vals[0]) for v in vals]
        denom = exps[0] + exps[1] + exps[2] + exps[3]
        gates = [e / denom for e in exps]
        before = (_iota((sub, sub), 0) > _iota((sub, sub), 1)).astype(BF16)
        found[part] = (idxs, gates, _dot(before, multi.astype(BF16)), multi)

    live = [sub_tile(part) for part in range(POST_SPLIT)]
    while live:
        live = [g for g in live if next(g, "done") != "done"]

    base = carry[...]
    for part in range(POST_SPLIT):
        idxs, gates, prior_local, multi = found[part]
        prior = prior_local + base
        mi = jnp.zeros((sub, LANES), F32)
        mf = jnp.zeros((sub, LANES), F32)
        for kk in range(TOP_K):
            rank = jnp.sum(jnp.where(lane == idxs[kk], prior, 0.0), axis=-1, keepdims=True)
            mi = jnp.where(lane_i == kk, idxs[kk], mi)
            mi = jnp.where(lane_i == TOP_K + kk, rank, mi)
            mf = jnp.where(lane_i == kk, gates[kk], mf)
        mi_ref[:, part * sub:(part + 1) * sub] = jnp.transpose(mi)[0:2 * TOP_K, :].astype(jnp.int32)
        mf_ref[part * sub:(part + 1) * sub, :] = mf
        base = base + jnp.sum(multi, axis=0, keepdims=True)
    carry[...] = base
    cnt_ref[...] = jnp.broadcast_to(base, cnt_ref.shape)


def _post(h, x2, w_out_b, ln_g, ln_b, wr_hi, wr_lo, br_p):
    t = x2.shape[0]
    tm = TM_POST
    row = lambda i: (i, 0)
    const = lambda i: (0, 0)
    return pl.pallas_call(
        _post_kernel,
        grid=(t // tm,),
        in_specs=[pl.BlockSpec((tm, D_MIX), row), pl.BlockSpec((tm, D_MODEL), row),
                  pl.BlockSpec((D_MIX, D_MODEL), const), pl.BlockSpec((1, D_MODEL), const),
                  pl.BlockSpec((1, D_MODEL), const), pl.BlockSpec((D_MODEL, LANES), const),
                  pl.BlockSpec((D_MODEL, LANES), const), pl.BlockSpec((1, LANES), const)],
        out_specs=[pl.BlockSpec((tm, D_MODEL), row), pl.BlockSpec((tm, ROW_WORDS), row),
                   pl.BlockSpec((2 * TOP_K, tm), lambda i: (0, i)), pl.BlockSpec((tm, LANES), row),
                   pl.BlockSpec((8, LANES), const)],
        out_shape=[jax.ShapeDtypeStruct((t, D_MODEL), F32), jax.ShapeDtypeStruct((t, ROW_WORDS), jnp.int32),
                   jax.ShapeDtypeStruct((2 * TOP_K, t), jnp.int32), jax.ShapeDtypeStruct((t, LANES), F32),
                   jax.ShapeDtypeStruct((8, LANES), F32)],
        scratch_shapes=[pltpu.VMEM((1, LANES), F32)],
        compiler_params=pltpu.CompilerParams(dimension_semantics=("arbitrary",),
                                             vmem_limit_bytes=VMEM_LIMIT),
        name="outproj_ln_router",
    )(h, x2, w_out_b, ln_g, ln_b, wr_hi, wr_lo, br_p)


def _ffn_kernel(be_ref, nv_ref, x_ref, wg_ref, bg_ref, wu_ref, bu_ref, wd_ref, bd_ref, o_ref,
                wg_b, wu_b, wd_b):
    i = pl.program_id(0)
    valid = i < nv_ref[0]
    e = be_ref[i]
    prev = be_ref[jnp.maximum(i - 1, 0)]
    fresh = jnp.logical_or(i == 0, e != prev)

    @pl.when(jnp.logical_and(valid, fresh))
    def _():
        wg_b[...] = wg_ref[0, 0].astype(BF16)
        wu_b[...] = wu_ref[0, 0].astype(BF16)
        wd_b[...] = wd_ref[0, 0].astype(BF16)

    @pl.when(valid)
    def _():
        x = _unpack_rows(x_ref[...]).astype(BF16)
        hg = jnp.minimum(_dot(x, wg_b[...]) + bg_ref[0, 0], SWIGLU_LIMIT)
        hu = jnp.clip(_dot(x, wu_b[...]) + bu_ref[0, 0], -SWIGLU_LIMIT, SWIGLU_LIMIT)
        hh = (hu + 1.0) * hg * jax.nn.sigmoid(SWIGLU_ALPHA * hg)
        o_ref[...] = _pack_rows(_dot(hh.astype(BF16), wd_b[...]) + bd_ref[0, 0])


def _ffn(layer, block_expert, n_valid, xin, wg, bg, wu, bu, wd, bd):
    cap = xin.shape[0]
    nb = cap // BM

    def blk(i, be, nv):
        return jnp.maximum(jnp.minimum(i, nv[0] - 1), 0)

    row = lambda i, be, nv: (blk(i, be, nv), 0)
    wmap = lambda i, be, nv: (layer, be[blk(i, be, nv)], 0, 0)
    grid_spec = pltpu.PrefetchScalarGridSpec(
        num_scalar_prefetch=2,
        grid=(nb,),
        in_specs=[pl.BlockSpec((BM, ROW_WORDS), row),
                  pl.BlockSpec((1, 1, D_MODEL, D_FF), wmap), pl.BlockSpec((1, 1, 1, D_FF), wmap),
                  pl.BlockSpec((1, 1, D_MODEL, D_FF), wmap), pl.BlockSpec((1, 1, 1, D_FF), wmap),
                  pl.BlockSpec((1, 1, D_FF, D_MODEL), wmap), pl.BlockSpec((1, 1, 1, D_MODEL), wmap)],
        out_specs=pl.BlockSpec((BM, ROW_WORDS), row),
        scratch_shapes=[pltpu.VMEM((D_MODEL, D_FF), BF16), pltpu.VMEM((D_MODEL, D_FF), BF16),
                        pltpu.VMEM((D_FF, D_MODEL), BF16)],
    )
    return pl.pallas_call(
        _ffn_kernel,
        grid_spec=grid_spec,
        out_shape=jax.ShapeDtypeStruct((cap, ROW_WORDS), jnp.int32),
        compiler_params=pltpu.CompilerParams(dimension_semantics=("arbitrary",),
                                             vmem_limit_bytes=VMEM_LIMIT),
        name="expert_ffn",
    )(block_expert, n_valid, xin, wg, bg, wu, bu, wd, bd)


def _sc_gather(table, idx3):
    nw, n_chunks, ch = idx3.shape
    width = table.shape[1]
    per_worker = n_chunks * ch
    mesh = plsc.VectorSubcoreMesh(core_axis_name="c", subcore_axis_name="s")
    n_cores = mesh.num_cores
    assert nw == n_cores * mesh.num_subcores and n_chunks % 2 == 0 and ch == SC_CHUNK

    def body(table_hbm, idx_hbm, out_hbm, idx_v, rows0, rows1, sem_g0, sem_g1, sem_w0, sem_w1):
        wid = lax.axis_index("s") * n_cores + lax.axis_index("c")
        base = wid * per_worker
        pltpu.sync_copy(idx_hbm.at[wid], idx_v)

        @pl.loop(0, n_chunks, step=2)
        def _(c):
            g0 = pltpu.async_copy(table_hbm.at[idx_v.at[c]], rows0, sem_g0)
            g1 = pltpu.async_copy(table_hbm.at[idx_v.at[c + 1]], rows1, sem_g1)
            g0.wait()
            w0 = pltpu.async_copy(rows0, out_hbm.at[pl.ds(base + c * ch, ch)], sem_w0)
            g1.wait()
            w1 = pltpu.async_copy(rows1, out_hbm.at[pl.ds(base + (c + 1) * ch, ch)], sem_w1)
            w0.wait()
            w1.wait()

    return pl.kernel(
        body,
        out_type=jax.ShapeDtypeStruct((nw * per_worker, width), table.dtype),
        mesh=mesh,
        scratch_types=[pltpu.VMEM((n_chunks, ch), jnp.int32),
                       pltpu.VMEM((ch, width), table.dtype), pltpu.VMEM((ch, width), table.dtype),
                       pltpu.SemaphoreType.DMA, pltpu.SemaphoreType.DMA,
                       pltpu.SemaphoreType.DMA, pltpu.SemaphoreType.DMA],
        name="sc_row_gather",
    )(table, idx3)


def _sc_scatter(rows, idx3, n_out):
    nw, n_lists, ch = idx3.shape
    n_chunks = n_lists // TOP_K
    width = rows.shape[1]
    per_worker = n_chunks * ch
    mesh = plsc.VectorSubcoreMesh(core_axis_name="c", subcore_axis_name="s")
    n_cores = mesh.num_cores
    assert nw == n_cores * mesh.num_subcores and n_chunks % 2 == 0 and ch == SC_CHUNK
    assert nw * per_worker == rows.shape[0]

    def body(rows_hbm, idx_hbm, out_hbm, idx_v, buf0, buf1, sem_r0, sem_r1, sem_w0, sem_w1):
        wid = lax.axis_index("s") * n_cores + lax.axis_index("c")
        base = wid * per_worker
        pltpu.sync_copy(idx_hbm.at[wid], idx_v)

        @pl.loop(0, n_chunks, step=2)
        def _(c):
            r0 = pltpu.async_copy(rows_hbm.at[pl.ds(base + c * ch, ch)], buf0, sem_r0)
            r1 = pltpu.async_copy(rows_hbm.at[pl.ds(base + (c + 1) * ch, ch)], buf1, sem_r1)
            r0.wait()
            w0 = [pltpu.async_copy(buf0, out_hbm.at[idx_v.at[c * TOP_K + k]], sem_w0) for k in range(TOP_K)]
            r1.wait()
            w1 = [pltpu.async_copy(buf1, out_hbm.at[idx_v.at[(c + 1) * TOP_K + k]], sem_w1) for k in range(TOP_K)]
            for w in w0 + w1:
                w.wait()

    return pl.kernel(
        body,
        out_type=jax.ShapeDtypeStruct((n_out, width), rows.dtype),
        mesh=mesh,
        scratch_types=[pltpu.VMEM((n_lists, ch), jnp.int32),
                       pltpu.VMEM((ch, width), rows.dtype), pltpu.VMEM((ch, width), rows.dtype),
                       pltpu.SemaphoreType.DMA, pltpu.SemaphoreType.DMA,
                       pltpu.SemaphoreType.DMA, pltpu.SemaphoreType.DMA],
        name="sc_row_scatter",
    )(rows, idx3)


def _combine_kernel(x_ref, rows_ref, mf_ref, g_ref, b_ref, o_ref):
    mf = mf_ref[...]
    y = DEEPNORM_ALPHA * x_ref[...]
    for kk in range(TOP_K):
        y = y + mf[:, kk:kk + 1] * _unpack_rows(rows_ref[kk])
    o_ref[...] = _layer_norm(y, g_ref[...], b_ref[...])


def _combine(x1, rows, mf, ln_g, ln_b):
    t = x1.shape[0]
    tm = TM_COMB
    row = lambda i: (i, 0)
    const = lambda i: (0, 0)
    return pl.pallas_call(
        _combine_kernel,
        grid=(t // tm,),
        in_specs=[pl.BlockSpec((tm, D_MODEL), row),
                  pl.BlockSpec((TOP_K, tm, ROW_WORDS), lambda i: (0, i, 0)),
                  pl.BlockSpec((tm, LANES), row),
                  pl.BlockSpec((1, D_MODEL), const), pl.BlockSpec((1, D_MODEL), const)],
        out_specs=pl.BlockSpec((tm, D_MODEL), row),
        out_shape=jax.ShapeDtypeStruct((t, D_MODEL), F32),
        compiler_params=pltpu.CompilerParams(dimension_semantics=("arbitrary",),
                                             vmem_limit_bytes=VMEM_LIMIT),
        name="combine_ln",
    )(x1, rows, mf, ln_g, ln_b)


def _relayout_w_in(w):
    widths = (256, 256, 256, 256, SSD_W, SSD_XBC, SSD_HEADS, GLA_QK, GLA_QK, GLA_W, GLA_RANK, GLA_W)
    offs = [0]
    for wd in widths:
        offs.append(offs[-1] + wd)
    parts = [w[:, offs[i]:offs[i + 1]] for i in range(len(widths))]
    parts[6] = jnp.repeat(parts[6], SSD_HEAD_DIM, axis=1)
    parts[10] = jnp.pad(parts[10], ((0, 0), (0, LANES - GLA_RANK)))
    return jnp.concatenate(parts, axis=1).astype(BF16)


def _rep_heads(p):
    return jnp.repeat(p, SSD_HEAD_DIM)[None, :]


def kernel(x, positions, w_in, w_out, ret_norm_w, ssd_conv_w, ssd_conv_b, ssd_dt_bias, ssd_a_log, ssd_d,
           ssd_norm_w, gla_w_gk2, gla_b_gk2, gla_norm_w, ln1_g, ln1_b, w_router, b_router, w_gate, b_gate,
           w_up, b_up, w_down, b_down, ln2_g, ln2_b):
    batch, seq, d = x.shape
    t = batch * seq
    depth = w_in.shape[0]
    assert d == D_MODEL and t % TM_POST == 0 and t % TM_COMB == 0
    n_assign = t * TOP_K
    nb = n_assign // BM + N_EXPERTS
    cap = nb * BM

    cos_t, sin_t = _rope_tables(positions.reshape(t, 1))
    cos_t = cos_t.reshape(batch, seq, -1)
    sin_t = sin_t.reshape(batch, seq, -1)
    x2 = x.reshape(t, d)

    for l in range(depth):
        params = (ret_norm_w[l][None, :], ssd_conv_w[l], ssd_conv_b[l][None, :], _rep_heads(ssd_dt_bias[l]),
                  _rep_heads(ssd_a_log[l]), _rep_heads(ssd_d[l]), ssd_norm_w[l][None, :],
                  jnp.pad(gla_w_gk2[l], ((0, LANES - GLA_RANK), (0, 0))), gla_b_gk2[l][None, :],
                  gla_norm_w[l][None, :])
        h = _mixproj(x2.reshape(batch, seq, d), _relayout_w_in(w_in[l]), cos_t, sin_t, params).reshape(t, D_MIX)

        wr_p = jnp.pad(w_router[l], ((0, 0), (0, LANES - N_EXPERTS)))
        br_p = jnp.pad(b_router[l], (0, LANES - N_EXPERTS), constant_values=NEG_BIG)[None, :]
        wr_hi = wr_p.astype(BF16)
        wr_lo = (wr_p - wr_hi.astype(F32)).astype(BF16)
        x1, x1p, mi, mf, cnt = _post(h, x2, w_out[l].astype(BF16), ln1_g[l][None, :], ln1_b[l][None, :],
                                     wr_hi, wr_lo, br_p)

        counts = cnt[0, :N_EXPERTS].astype(jnp.int32)
        padded = (counts + BM - 1) // BM * BM
        end_padded = jnp.cumsum(padded)
        start_padded = end_padded - padded
        top_idx = mi[:TOP_K]
        start_of = jnp.sum(jnp.where(top_idx[None] == jnp.arange(N_EXPERTS, dtype=jnp.int32)[:, None, None],
                                     start_padded[:, None, None], 0), axis=0)
        dest = start_of + mi[TOP_K:]
        block_start = jnp.arange(nb, dtype=jnp.int32) * BM
        block_expert = jnp.minimum(jnp.sum((end_padded[None, :] <= block_start[:, None]).astype(jnp.int32), axis=1),
                                   N_EXPERTS - 1)
        n_valid = (end_padded[-1:] // BM).astype(jnp.int32)

        scatter_idx = dest.reshape(TOP_K, SC_WORKERS, -1, SC_CHUNK).transpose(1, 2, 0, 3)
        xin = _sc_scatter(x1p, scatter_idx.reshape(SC_WORKERS, -1, SC_CHUNK), cap)
        yb = _ffn(l, block_expert, n_valid, xin, w_gate, b_gate[:, :, None, :], w_up, b_up[:, :, None, :],
                  w_down, b_down[:, :, None, :])
        rows = _sc_gather(yb, dest.reshape(SC_WORKERS, -1, SC_CHUNK))
        x2 = _combine(x1, rows.reshape(TOP_K, t, ROW_WORDS), mf, ln2_g[l][None, :], ln2_b[l][None, :])
    return x2.reshape(batch, seq, d)
```

```python
import functools
import math

import jax
import jax.numpy as jnp
from jax import lax
from jax.experimental import pallas as pl
from jax.experimental.pallas import tpu as pltpu
from jax.experimental.pallas import tpu_sc as plsc

F32 = jnp.float32
BF16 = jnp.bfloat16

D_MODEL = 1024
CHUNK = 64
RET_HEADS, RET_DK, RET_DV = 4, 64, 64
RET_W = RET_HEADS * RET_DV
SSD_HEADS, SSD_HEAD_DIM, SSD_STATE, SSD_GROUPS, SSD_CONV = 8, 64, 64, 2, 4
SSD_W = SSD_HEADS * SSD_HEAD_DIM
SSD_BC = SSD_GROUPS * SSD_STATE
SSD_XBC = SSD_W + 2 * SSD_BC
GLA_HEADS, GLA_DK, GLA_DV, GLA_RANK, GLA_TEMP = 4, 32, 64, 16, 16.0
GLA_QK = GLA_HEADS * GLA_DK
GLA_W = GLA_HEADS * GLA_DV
D_MIX = RET_W + SSD_W + GLA_W
N_EXPERTS, TOP_K, D_FF = 32, 4, 1024
SWIGLU_LIMIT, SWIGLU_ALPHA = 7.0, 1.702
ROPE_BASE = 10000.0
LN_EPS, NORM_EPS = 1e-5, 1e-6
DEPTH = 2
DEEPNORM_ALPHA = (2.0 * DEPTH) ** 0.25

LANES = 128
ROPE_W = LANES
NEG_BIG = -1e30
VMEM_LIMIT = 56 * 1024 * 1024

_SEGS = (("rq", 256), ("rk", 256), ("rv", 256), ("rg", 256), ("sz", SSD_W), ("sxbc", SSD_XBC),
         ("sdt", SSD_W), ("gq", 128), ("gk", 128), ("gv", 256), ("ggk", 128), ("gg", 256))
COL = {}
_off = 0
for _n, _w in _SEGS:
    COL[_n] = _off
    _off += _w
NP = _off

MIX_G = 4
PROJ_SLAB = 512
TM_POST = 1024
POST_SPLIT = 4
TM_COMB = 512
COMB_PARTS = 2
BM = 512
FFN_SPLIT = 2
ROW_WORDS = D_MODEL // 2
SC_WORKERS = 32
SC_CHUNK = 64


def _dot(a, b, dims=(((1,), (0,)), ((), ())), precision=None):
    return lax.dot_general(a, b, dims, precision=precision, preferred_element_type=F32)


_NT = (((1,), (1,)), ((), ()))
_TN = (((0,), (0,)), ((), ()))


def _iota(shape, dim):
    return lax.broadcasted_iota(jnp.int32, shape, dim)


def _vdiv(x, n):
    assert n & (n - 1) == 0
    return lax.shift_right_logical(x, n.bit_length() - 1)


def _vmod(x, n):
    assert n & (n - 1) == 0
    return jnp.bitwise_and(x, n - 1)


def _silu(x):
    return x * jax.nn.sigmoid(x)


def _softplus(x):
    return jnp.maximum(x, 0.0) + jnp.log(1.0 + jnp.exp(-jnp.abs(x)))


def _pack_rows(x):
    w = x.shape[1] // 2
    lo = lax.bitcast_convert_type(x[:, :w].astype(BF16).astype(F32), jnp.uint32)
    hi = lax.bitcast_convert_type(x[:, w:].astype(BF16).astype(F32), jnp.uint32)
    return lax.bitcast_convert_type(lax.shift_right_logical(lo, jnp.uint32(16)) | hi, jnp.int32)


def _unpack_rows(words):
    u = lax.bitcast_convert_type(words, jnp.uint32)
    a = lax.bitcast_convert_type(lax.shift_left(u, jnp.uint32(16)), F32)
    b = lax.bitcast_convert_type(u & jnp.uint32(0xFFFF0000), F32)
    return jnp.concatenate([a, b], axis=-1)


def _seg_sum64(x):
    first = _iota((1, LANES), 1) < 64
    outs = []
    for j in range(x.shape[-1] // LANES):
        blk = x[:, j * LANES:(j + 1) * LANES]
        lo = jnp.sum(jnp.where(first, blk, 0.0), axis=-1, keepdims=True)
        hi = jnp.sum(jnp.where(first, 0.0, blk), axis=-1, keepdims=True)
        outs.append(jnp.where(first, lo, hi))
    return jnp.concatenate(outs, axis=-1)


def _block_mask(shape, row_blk, col_blk):
    keep = _vdiv(_iota(shape, 0), row_blk) == _vdiv(_iota(shape, 1), col_blk)
    return jnp.where(keep, 1.0, 0.0).astype(BF16)


def _block_diag(x, mask):
    reps = mask.shape[0] // x.shape[0]
    return jnp.concatenate([x.astype(BF16)] * reps, axis=0) * mask


def _cumsum_rows(tri, x):
    hi = x.astype(BF16)
    lo = (x - hi.astype(F32)).astype(BF16)
    return _dot(tri, hi) + _dot(tri, lo)


def _rope_kernel(pos_ref, cos_ref, sin_ref):
    lane = _iota((1, ROPE_W), 1)
    half = RET_DK // 2
    k = _vmod(lane, half).astype(F32)
    inv_freq = jnp.exp(k * (-math.log(ROPE_BASE) / half))
    ang = pos_ref[...].astype(F32) * inv_freq
    first = _vmod(lane, RET_DK) < half
    cos_ref[...] = jnp.cos(ang)
    sin_ref[...] = jnp.where(first, -1.0, 1.0) * jnp.sin(ang)


def _rope_tables(pos_col):
    t = pos_col.shape[0]
    tm = 512
    w = ROPE_W
    return pl.pallas_call(
        _rope_kernel,
        grid=(t // tm,),
        in_specs=[pl.BlockSpec((tm, 1), lambda i: (i, 0))],
        out_specs=[pl.BlockSpec((tm, w), lambda i: (i, 0))] * 2,
        out_shape=[jax.ShapeDtypeStruct((t, w), F32)] * 2,
        compiler_params=pltpu.CompilerParams(dimension_semantics=("arbitrary",)),
        name="rope_tables",
    )(pos_col)


def _mixproj_kernel(xn_ref, w_ref, cos_ref, sin_ref, retw_ref, convw_ref, convb_ref, dtb_ref, alog_ref,
                    dskip_ref, ssdw_ref, wgk_ref, bgk_ref, glaw_ref, h_ref,
                    proj_a, proj_b, ret_s, ssd_s, gla_s, stage, m_heads, m_groups, m_gla, *, chunks_per_seq):
    C = CHUNK
    i = pl.program_id(0)
    cur = jnp.maximum(i - 1, 0)

    @pl.when(i == 0)
    def _():
        m_heads[...] = _block_mask(m_heads.shape, C, 64)
        m_groups[...] = _block_mask(m_groups.shape, C * SSD_HEADS // SSD_GROUPS, SSD_STATE)
        m_gla[...] = _block_mask(m_gla.shape, C, GLA_DK)
        proj_b[...] = jnp.zeros_like(proj_b)

    @pl.when(lax.rem(cur, chunks_per_seq) == 0)
    def _():
        ret_s[...] = jnp.zeros_like(ret_s)
        ssd_s[...] = jnp.zeros_like(ssd_s)
        gla_s[...] = jnp.zeros_like(gla_s)
        for g in range(MIX_G):
            stage[g, 0:8, :] = jnp.zeros((8, SSD_XBC), F32)

    def conv_act(sq, pref):
        stage[sq, 8:8 + C, :] = pref[sq * C:(sq + 1) * C, COL["sxbc"]:COL["sxbc"] + SSD_XBC]
        acc = convb_ref[...] + convw_ref[0:1, :] * stage[sq, 5:5 + C, :]
        for j in range(1, SSD_CONV):
            acc = acc + convw_ref[j:j + 1, :] * stage[sq, 5 + j:5 + j + C, :]
        stage[sq, 0:8, :] = stage[sq, C:C + 8, :]
        return _silu(acc)

    lane256 = _iota((1, 256), 1)
    head = _vdiv(lane256, 64).astype(F32)
    log_gamma = jnp.log(1.0 - jnp.exp((-5.0 - head) * math.log(2.0)))
    row = _iota((C, 1), 0).astype(F32)
    dist = row - _vmod(lane256, 64).astype(F32)
    ret_intra = jnp.where(dist >= 0, jnp.exp(log_gamma * jnp.maximum(dist, 0.0)), 0.0)
    ret_qdec = jnp.exp(log_gamma * (row + 1.0))
    ret_kdec = jnp.exp(log_gamma * (C - 1.0 - row))
    ret_cdec = jnp.exp(log_gamma * C)
    first_half = _vmod(lane256, RET_DK) < (RET_DK // 2)

    tri = jnp.where(_iota((C, C), 0) >= _iota((C, C), 1), 1.0, 0.0).astype(BF16)
    causal4 = _iota((C, 256), 0) >= _vmod(_iota((C, 256), 1), 64)
    causal8 = _iota((C, 512), 0) >= _vmod(_iota((C, 512), 1), 64)
    eye8 = _iota((C, 512), 0) == _vmod(_iota((C, 512), 1), 64)

    a_neg = -jnp.exp(alog_ref[...])

    def rot(t, cos, sin):
        sw = jnp.where(first_half, pltpu.roll(t, 256 - 32, 1), pltpu.roll(t, 32, 1))
        return t * cos + sw * sin

    def chunk_of(sq, pref):
        def seg(name, width):
            return pref[sq * C:(sq + 1) * C, COL[name]:COL[name] + width]

        xact = conv_act(sq, pref)
        yield

        reps = RET_HEADS * RET_DK // ROPE_W
        cos = jnp.concatenate([cos_ref[sq]] * reps, axis=-1)
        sin = jnp.concatenate([sin_ref[sq]] * reps, axis=-1)
        q = rot(seg("rq", 256), cos, sin)
        k = rot(seg("rk", 256), cos, sin) * (RET_DK ** -0.5)
        v = seg("rv", 256)
        vb = v.astype(BF16)
        kbd = _block_diag(k, m_heads[...])
        scores = _dot(q.astype(BF16), kbd, _NT) * ret_intra
        vbd = _block_diag(v, m_heads[...])
        yield
        s_prev = ret_s[sq]
        o = _dot(scores.astype(BF16), vbd) + _dot((q * ret_qdec).astype(BF16), s_prev.astype(BF16))
        contrib = _dot((k * ret_kdec).astype(BF16), vb, _TN)
        yield
        keep = _vdiv(_iota((256, 256), 0), RET_DK) == _vdiv(_iota((256, 256), 1), RET_DV)
        ret_s[sq] = jnp.where(keep, ret_cdec * s_prev + contrib, 0.0)
        mu = _seg_sum64(o) * (1.0 / RET_DV)
        oc = o - mu
        var = _seg_sum64(oc * oc) * (1.0 / RET_DV)
        o = oc * lax.rsqrt(var + LN_EPS) * retw_ref[...]
        h_ref[sq, :, 0:RET_W] = (_silu(seg("rg", 256)) * o).astype(BF16)
        yield

        xs = xact[:, 0:SSD_W]
        bm = xact[:, SSD_W:SSD_W + SSD_BC]
        cm = xact[:, SSD_W + SSD_BC:SSD_XBC]
        cmb = cm.astype(BF16)
        dt = _softplus(seg("sdt", SSD_W) + dtb_ref[...])
        acum = _cumsum_rows(tri, dt * a_neg)
        yield
        arow = jnp.sum(jnp.where(eye8, acum, 0.0), axis=0, keepdims=True)
        decay = jnp.exp(jnp.where(causal8, acum - arow, NEG_BIG))
        b8 = _block_diag(bm, m_groups[...])
        cb = _dot(cmb, b8, _NT)
        yield
        m = (cb * decay).astype(BF16)
        xdt = xs * dt
        s2 = ssd_s[sq]
        half = SSD_W // SSD_GROUPS
        ys = []
        for g in range(SSD_GROUPS):
            xbd = _block_diag(xdt[:, g * half:(g + 1) * half], m_heads[...])
            ys.append(_dot(m[:, g * half:(g + 1) * half], xbd))
        y = jnp.concatenate(ys, axis=-1)
        y = y + _dot(cmb, s2.astype(BF16)) * jnp.exp(acum)
        y = y + dskip_ref[...] * xs
        a_last = acum[C - 1:C, :]
        sd = jnp.exp(a_last - acum)
        contrib_s = _dot(bm.astype(BF16), (xdt * sd).astype(BF16), _TN)
        yield
        keep_s = _vdiv(_iota(s2.shape, 0), SSD_STATE) == _vdiv(_iota(s2.shape, 1), half)
        ssd_s[sq] = jnp.where(keep_s, s2 * jnp.exp(a_last) + contrib_s, 0.0)
        yz = y * _silu(seg("sz", SSD_W))
        outs = []
        for g in range(SSD_GROUPS):
            blk = yz[:, g * half:(g + 1) * half]
            ms = jnp.mean(blk * blk, axis=-1, keepdims=True)
            outs.append(blk * lax.rsqrt(ms + NORM_EPS))
        h_ref[sq, :, RET_W:RET_W + SSD_W] = (jnp.concatenate(outs, axis=-1) * ssdw_ref[...]).astype(BF16)

        yield

        gq = seg("gq", GLA_QK) * (GLA_DK ** -0.5)
        gkk = seg("gk", GLA_QK)
        gv = seg("gv", GLA_W)
        gkl = _dot(seg("ggk", 128).astype(BF16), wgk_ref[...].astype(BF16)) + bgk_ref[...]
        yield
        log_a = -_softplus(-gkl) * (1.0 / GLA_TEMP)
        b = _cumsum_rows(tri, log_a)
        yield
        q_t = (gq * jnp.exp(b)).astype(BF16)
        k_t = gkk * jnp.exp(-b)
        kbd_g = _block_diag(k_t, m_gla[...])
        att = jnp.where(causal4, _dot(q_t, kbd_g, _NT), 0.0)
        yield
        vbd_g = _block_diag(gv, m_heads[...])
        st = gla_s[sq]
        og = _dot(att.astype(BF16), vbd_g) + _dot(q_t, st.astype(BF16), _NT)
        b_last = b[C - 1:C, :]
        kd = (gkk * jnp.exp(b_last - b)).astype(BF16)
        contrib_g = _dot(gv.astype(BF16), kd, _TN)
        yield
        keep_g = _vdiv(_iota(st.shape, 0), GLA_DV) == _vdiv(_iota(st.shape, 1), GLA_DK)
        gla_s[sq] = jnp.where(keep_g, st * jnp.exp(b_last) + contrib_g, 0.0)
        ms = _seg_sum64(og * og) * (1.0 / GLA_DV)
        og = og * lax.rsqrt(ms + NORM_EPS) * glaw_ref[...]
        h_ref[sq, :, RET_W + SSD_W:D_MIX] = (_silu(seg("gg", GLA_W)) * og).astype(BF16)

    def step(p_read, p_write):
        xb = xn_ref[...].reshape(MIX_G * C, D_MODEL).astype(BF16)
        edges = list(range(0, NP, PROJ_SLAB)) + [NP]
        slabs = list(zip(edges[:-1], edges[1:]))

        live = [chunk_of(sq, p_read) for sq in range(MIX_G)]
        while live or slabs:
            live = [g for g in live if next(g, "done") != "done"]
            if slabs:
                lo, hi = slabs.pop(0)
                p_write[:, lo:hi] = _dot(xb, w_ref[:, lo:hi])

    @pl.when(lax.rem(i, 2) == 0)
    def _():
        step(proj_b, proj_a)

    @pl.when(lax.rem(i, 2) == 1)
    def _():
        step(proj_a, proj_b)


def _mixproj(x3, w_p, cos_t, sin_t, params):
    batch, seq, _ = x3.shape
    assert batch % MIX_G == 0 and seq % CHUNK == 0
    cps = seq // CHUNK
    n = (batch // MIX_G) * cps

    def cur_map(i):
        c = jnp.clip(i - 1, 0, n - 1)
        return (c // cps, c % cps, 0)

    def next_map(i):
        c = jnp.minimum(i, n - 1)
        return (c // cps, c % cps, 0)

    const = lambda i: (0, 0)
    specs = [pl.BlockSpec((MIX_G, CHUNK, D_MODEL), next_map),
             pl.BlockSpec((D_MODEL, NP), const),
             pl.BlockSpec((MIX_G, CHUNK, ROPE_W), cur_map),
             pl.BlockSpec((MIX_G, CHUNK, ROPE_W), cur_map)]
    specs += [pl.BlockSpec(p.shape, const) for p in params]
    return pl.pallas_call(
        functools.partial(_mixproj_kernel, chunks_per_seq=cps),
        grid=(n + 1,),
        in_specs=specs,
        out_specs=pl.BlockSpec((MIX_G, CHUNK, D_MIX), cur_map),
        out_shape=jax.ShapeDtypeStruct((batch, seq, D_MIX), BF16),
        scratch_shapes=[pltpu.VMEM((MIX_G * CHUNK, NP), F32),
                        pltpu.VMEM((MIX_G * CHUNK, NP), F32),
                        pltpu.VMEM((MIX_G, 256, 256), F32),
                        pltpu.VMEM((MIX_G, SSD_BC, SSD_W), F32),
                        pltpu.VMEM((MIX_G, GLA_W, GLA_QK), F32),
                        pltpu.VMEM((MIX_G, CHUNK + 8, SSD_XBC), F32),
                        pltpu.VMEM((RET_HEADS * CHUNK, 256), BF16),
                        pltpu.VMEM((SSD_HEADS * CHUNK, SSD_BC), BF16),
                        pltpu.VMEM((GLA_HEADS * CHUNK, GLA_QK), BF16)],
        compiler_params=pltpu.CompilerParams(dimension_semantics=("arbitrary",),
                                             vmem_limit_bytes=VMEM_LIMIT),
        name="inproj_mixer",
    )(x3, w_p, cos_t, sin_t, *params)


def _layer_norm(y, g, b):
    mu = jnp.mean(y, axis=-1, keepdims=True)
    yc = y - mu
    var = jnp.mean(yc * yc, axis=-1, keepdims=True)
    return yc * lax.rsqrt(var + LN_EPS) * g + b


def _post_kernel(h_ref, x_ref, wout_ref, g_ref, b_ref, wrh_ref, wrl_ref, br_ref,
                 x1_ref, x1p_ref, mi_ref, mf_ref, cnt_ref, carry):
    sub = TM_POST // POST_SPLIT

    @pl.when(pl.program_id(0) == 0)
    def _():
        carry[...] = jnp.zeros_like(carry)

    lane_i = _iota((sub, LANES), 1)
    lane = lane_i.astype(F32)
    found = {}

    def sub_tile(part):
        rows = slice(part * sub, (part + 1) * sub)
        mix = _dot(h_ref[rows, :], wout_ref[...])
        yield
        x1 = _layer_norm(DEEPNORM_ALPHA * x_ref[rows, :] + mix, g_ref[...], b_ref[...])
        x1_ref[rows, :] = x1
        x1p_ref[rows, :] = _pack_rows(x1)
        x_hi = x1.astype(BF16)
        x_lo = (x1 - x_hi.astype(F32)).astype(BF16)
        logits = (_dot(x_hi, wrh_ref[...]) + _dot(x_lo, wrh_ref[...]) + _dot(x_hi, wrl_ref[...])
                  + br_ref[...])
        yield
        work = logits
        vals, idxs = [], []
        multi = jnp.zeros((sub, LANES), F32)
        for _ in range(TOP_K):
            m = jnp.max(work, axis=-1, keepdims=True)
            idx = jnp.min(jnp.where(work == m, lane, float(LANES)), axis=-1, keepdims=True)
            hit = lane == idx
            multi = multi + hit.astype(F32)
            work = jnp.where(hit, -jnp.inf, work)
            vals.append(m)
            idxs.append(idx)
            yield
        exps = [jnp.exp(v - vals[0]) for v in vals]
        denom = exps[0] + exps[1] + exps[2] + exps[3]
        gates = [e / denom for e in exps]
        before = (_iota((sub, sub), 0) > _iota((sub, sub), 1)).astype(BF16)
        found[part] = (idxs, gates, _dot(before, multi.astype(BF16)), multi)

    live = [sub_tile(part) for part in range(POST_SPLIT)]
    while live:
        live = [g for g in live if next(g, "done") != "done"]

    base = carry[...]
    for part in range(POST_SPLIT):
        idxs, gates, prior_local, multi = found[part]
        prior = prior_local + base
        mi = jnp.zeros((sub, LANES), F32)
        mf = jnp.zeros((sub, LANES), F32)
        for kk in range(TOP_K):
            rank = jnp.sum(jnp.where(lane == idxs[kk], prior, 0.0), axis=-1, keepdims=True)
            mi = jnp.where(lane_i == kk, idxs[kk], mi)
            mi = jnp.where(lane_i == TOP_K + kk, rank, mi)
            mf = jnp.where(lane_i == kk, gates[kk], mf)
        mi_ref[:, part * sub:(part + 1) * sub] = jnp.transpose(mi)[0:2 * TOP_K, :].astype(jnp.int32)
        mf_ref[part * sub:(part + 1) * sub, :] = mf
        base = base + jnp.sum(multi, axis=0, keepdims=True)
    carry[...] = base
    cnt_ref[...] = jnp.broadcast_to(base, cnt_ref.shape)


def _post(h, x2, w_out_b, ln_g, ln_b, wr_hi, wr_lo, br_p):
    t = x2.shape[0]
    tm = TM_POST
    row = lambda i: (i, 0)
    const = lambda i: (0, 0)
    return pl.pallas_call(
        _post_kernel,
        grid=(t // tm,),
        in_specs=[pl.BlockSpec((tm, D_MIX), row), pl.BlockSpec((tm, D_MODEL), row),
                  pl.BlockSpec((D_MIX, D_MODEL), const), pl.BlockSpec((1, D_MODEL), const),
                  pl.BlockSpec((1, D_MODEL), const), pl.BlockSpec((D_MODEL, LANES), const),
                  pl.BlockSpec((D_MODEL, LANES), const), pl.BlockSpec((1, LANES), const)],
        out_specs=[pl.BlockSpec((tm, D_MODEL), row), pl.BlockSpec((tm, ROW_WORDS), row),
                   pl.BlockSpec((2 * TOP_K, tm), lambda i: (0, i)), pl.BlockSpec((tm, LANES), row),
                   pl.BlockSpec((8, LANES), const)],
        out_shape=[jax.ShapeDtypeStruct((t, D_MODEL), F32), jax.ShapeDtypeStruct((t, ROW_WORDS), jnp.int32),
                   jax.ShapeDtypeStruct((2 * TOP_K, t), jnp.int32), jax.ShapeDtypeStruct((t, LANES), F32),
                   jax.ShapeDtypeStruct((8, LANES), F32)],
        scratch_shapes=[pltpu.VMEM((1, LANES), F32)],
        compiler_params=pltpu.CompilerParams(dimension_semantics=("arbitrary",),
                                             vmem_limit_bytes=VMEM_LIMIT),
        name="outproj_ln_router",
    )(h, x2, w_out_b, ln_g, ln_b, wr_hi, wr_lo, br_p)


def _ffn_kernel(be_ref, nv_ref, x_ref, wg_ref, bg_ref, wu_ref, bu_ref, wd_ref, bd_ref, o_ref,
                wg_b, wu_b, wd_b):
    i = pl.program_id(0)
    valid = i < nv_ref[0]
    e = be_ref[i]
    prev = be_ref[jnp.maximum(i - 1, 0)]
    fresh = jnp.logical_or(i == 0, e != prev)

    @pl.when(jnp.logical_and(valid, fresh))
    def _():
        wg_b[...] = wg_ref[0, 0].astype(BF16)
        wu_b[...] = wu_ref[0, 0].astype(BF16)
        wd_b[...] = wd_ref[0, 0].astype(BF16)

    def row_part(part):
        sub = BM // FFN_SPLIT
        rows = slice(part * sub, (part + 1) * sub)
        x = _unpack_rows(x_ref[rows, :]).astype(BF16)
        hg = jnp.minimum(_dot(x, wg_b[...]) + bg_ref[0, 0], SWIGLU_LIMIT)
        hu = jnp.clip(_dot(x, wu_b[...]) + bu_ref[0, 0], -SWIGLU_LIMIT, SWIGLU_LIMIT)
        yield
        hh = (hu + 1.0) * hg * jax.nn.sigmoid(SWIGLU_ALPHA * hg)
        y = _dot(hh.astype(BF16), wd_b[...]) + bd_ref[0, 0]
        yield
        o_ref[rows, :] = _pack_rows(y)

    @pl.when(valid)
    def _():
        live = [row_part(part) for part in range(FFN_SPLIT)]
        while live:
            live = [g for g in live if next(g, "done") != "done"]


def _ffn(layer, block_expert, n_valid, xin, wg, bg, wu, bu, wd, bd):
    cap = xin.shape[0]
    nb = cap // BM

    def blk(i, be, nv):
        return jnp.maximum(jnp.minimum(i, nv[0] - 1), 0)

    row = lambda i, be, nv: (blk(i, be, nv), 0)
    wmap = lambda i, be, nv: (layer, be[blk(i, be, nv)], 0, 0)
    grid_spec = pltpu.PrefetchScalarGridSpec(
        num_scalar_prefetch=2,
        grid=(nb,),
        in_specs=[pl.BlockSpec((BM, ROW_WORDS), row),
                  pl.BlockSpec((1, 1, D_MODEL, D_FF), wmap), pl.BlockSpec((1, 1, 1, D_FF), wmap),
                  pl.BlockSpec((1, 1, D_MODEL, D_FF), wmap), pl.BlockSpec((1, 1, 1, D_FF), wmap),
                  pl.BlockSpec((1, 1, D_FF, D_MODEL), wmap), pl.BlockSpec((1, 1, 1, D_MODEL), wmap)],
        out_specs=pl.BlockSpec((BM, ROW_WORDS), row),
        scratch_shapes=[pltpu.VMEM((D_MODEL, D_FF), BF16), pltpu.VMEM((D_MODEL, D_FF), BF16),
                        pltpu.VMEM((D_FF, D_MODEL), BF16)],
    )
    return pl.pallas_call(
        _ffn_kernel,
        grid_spec=grid_spec,
        out_shape=jax.ShapeDtypeStruct((cap, ROW_WORDS), jnp.int32),
        compiler_params=pltpu.CompilerParams(dimension_semantics=("arbitrary",),
                                             vmem_limit_bytes=VMEM_LIMIT),
        name="expert_ffn",
    )(block_expert, n_valid, xin, wg, bg, wu, bu, wd, bd)


def _sc_gather(table, idx3):
    nw, n_chunks, ch = idx3.shape
    width = table.shape[1]
    per_worker = n_chunks * ch
    mesh = plsc.VectorSubcoreMesh(core_axis_name="c", subcore_axis_name="s")
    n_cores = mesh.num_cores
    assert nw == n_cores * mesh.num_subcores and n_chunks % 2 == 0 and ch == SC_CHUNK

    def body(table_hbm, idx_hbm, out_hbm, idx_v, rows0, rows1, sem_g0, sem_g1, sem_w0, sem_w1):
        wid = lax.axis_index("s") * n_cores + lax.axis_index("c")
        base = wid * per_worker
        pltpu.sync_copy(idx_hbm.at[wid], idx_v)

        @pl.loop(0, n_chunks, step=2)
        def _(c):
            g0 = pltpu.async_copy(table_hbm.at[idx_v.at[c]], rows0, sem_g0)
            g1 = pltpu.async_copy(table_hbm.at[idx_v.at[c + 1]], rows1, sem_g1)
            g0.wait()
            w0 = pltpu.async_copy(rows0, out_hbm.at[pl.ds(base + c * ch, ch)], sem_w0)
            g1.wait()
            w1 = pltpu.async_copy(rows1, out_hbm.at[pl.ds(base + (c + 1) * ch, ch)], sem_w1)
            w0.wait()
            w1.wait()

    return pl.kernel(
        body,
        out_type=jax.ShapeDtypeStruct((nw * per_worker, width), table.dtype),
        mesh=mesh,
        scratch_types=[pltpu.VMEM((n_chunks, ch), jnp.int32),
                       pltpu.VMEM((ch, width), table.dtype), pltpu.VMEM((ch, width), table.dtype),
                       pltpu.SemaphoreType.DMA, pltpu.SemaphoreType.DMA,
                       pltpu.SemaphoreType.DMA, pltpu.SemaphoreType.DMA],
        name="sc_row_gather",
    )(table, idx3)


def _sc_scatter(rows, idx3, n_out):
    nw, n_lists, ch = idx3.shape
    n_chunks = n_lists // TOP_K
    width = rows.shape[1]
    per_worker = n_chunks * ch
    mesh = plsc.VectorSubcoreMesh(core_axis_name="c", subcore_axis_name="s")
    n_cores = mesh.num_cores
    assert nw == n_cores * mesh.num_subcores and n_chunks % 2 == 0 and ch == SC_CHUNK
    assert nw * per_worker == rows.shape[0]

    def body(rows_hbm, idx_hbm, out_hbm, idx_v, buf0, buf1, sem_r0, sem_r1, sem_w0, sem_w1):
        wid = lax.axis_index("s") * n_cores + lax.axis_index("c")
        base = wid * per_worker
        pltpu.sync_copy(idx_hbm.at[wid], idx_v)

        @pl.loop(0, n_chunks, step=2)
        def _(c):
            r0 = pltpu.async_copy(rows_hbm.at[pl.ds(base + c * ch, ch)], buf0, sem_r0)
            r1 = pltpu.async_copy(rows_hbm.at[pl.ds(base + (c + 1) * ch, ch)], buf1, sem_r1)
            r0.wait()
            w0 = [pltpu.async_copy(buf0, out_hbm.at[idx_v.at[c * TOP_K + k]], sem_w0) for k in range(TOP_K)]
            r1.wait()
            w1 = [pltpu.async_copy(buf1, out_hbm.at[idx_v.at[(c + 1) * TOP_K + k]], sem_w1) for k in range(TOP_K)]
            for w in w0 + w1:
                w.wait()

    return pl.kernel(
        body,
        out_type=jax.ShapeDtypeStruct((n_out, width), rows.dtype),
        mesh=mesh,
        scratch_types=[pltpu.VMEM((n_lists, ch), jnp.int32),
                       pltpu.VMEM((ch, width), rows.dtype), pltpu.VMEM((ch, width), rows.dtype),
                       pltpu.SemaphoreType.DMA, pltpu.SemaphoreType.DMA,
                       pltpu.SemaphoreType.DMA, pltpu.SemaphoreType.DMA],
        name="sc_row_scatter",
    )(rows, idx3)


def _combine_kernel(x_ref, rows_ref, mf_ref, g_ref, b_ref, *rest):
    o_ref = rest[-1]
    mf = mf_ref[...]
    y = DEEPNORM_ALPHA * x_ref[...]
    for kk in range(TOP_K):
        y = y + mf[:, kk:kk + 1] * _unpack_rows(rows_ref[kk])
    o_ref[...] = _layer_norm(y, g_ref[...], b_ref[...])


def _combine(x1, rows, mf, ln_g, ln_b, part, n_parts, prev):
    t = x1.shape[0]
    tm = TM_COMB
    steps = t // n_parts // tm
    row = lambda i: (part * steps + i, 0)
    const = lambda i: (0, 0)
    in_specs = [pl.BlockSpec((tm, D_MODEL), row),
                pl.BlockSpec((TOP_K, tm, ROW_WORDS), lambda i: (0, i, 0)),
                pl.BlockSpec((tm, LANES), row),
                pl.BlockSpec((1, D_MODEL), const), pl.BlockSpec((1, D_MODEL), const)]
    args = [x1, rows, mf, ln_g, ln_b]
    aliases = {}
    if prev is not None:
        in_specs.append(pl.BlockSpec(memory_space=pl.ANY))
        args.append(prev)
        aliases = {len(args) - 1: 0}
    return pl.pallas_call(
        _combine_kernel,
        grid=(steps,),
        in_specs=in_specs,
        out_specs=pl.BlockSpec((tm, D_MODEL), row),
        out_shape=jax.ShapeDtypeStruct((t, D_MODEL), F32),
        input_output_aliases=aliases,
        compiler_params=pltpu.CompilerParams(dimension_semantics=("arbitrary",),
                                             vmem_limit_bytes=VMEM_LIMIT),
        name="combine_ln",
    )(*args)


def _relayout_w_in(w):
    widths = (256, 256, 256, 256, SSD_W, SSD_XBC, SSD_HEADS, GLA_QK, GLA_QK, GLA_W, GLA_RANK, GLA_W)
    offs = [0]
    for wd in widths:
        offs.append(offs[-1] + wd)
    parts = [w[:, offs[i]:offs[i + 1]] for i in range(len(widths))]
    parts[6] = jnp.repeat(parts[6], SSD_HEAD_DIM, axis=1)
    parts[10] = jnp.pad(parts[10], ((0, 0), (0, LANES - GLA_RANK)))
    return jnp.concatenate(parts, axis=1).astype(BF16)


def _rep_heads(p):
    return jnp.repeat(p, SSD_HEAD_DIM)[None, :]


def kernel(x, positions, w_in, w_out, ret_norm_w, ssd_conv_w, ssd_conv_b, ssd_dt_bias, ssd_a_log, ssd_d,
           ssd_norm_w, gla_w_gk2, gla_b_gk2, gla_norm_w, ln1_g, ln1_b, w_router, b_router, w_gate, b_gate,
           w_up, b_up, w_down, b_down, ln2_g, ln2_b):
    batch, seq, d = x.shape
    t = batch * seq
    depth = w_in.shape[0]
    assert d == D_MODEL and t % TM_POST == 0 and t % TM_COMB == 0
    n_assign = t * TOP_K
    nb = n_assign // BM + N_EXPERTS
    cap = nb * BM

    cos_t, sin_t = _rope_tables(positions.reshape(t, 1))
    cos_t = cos_t.reshape(batch, seq, -1)
    sin_t = sin_t.reshape(batch, seq, -1)
    x2 = x.reshape(t, d)

    for l in range(depth):
        params = (ret_norm_w[l][None, :], ssd_conv_w[l], ssd_conv_b[l][None, :], _rep_heads(ssd_dt_bias[l]),
                  _rep_heads(ssd_a_log[l]), _rep_heads(ssd_d[l]), ssd_norm_w[l][None, :],
                  jnp.pad(gla_w_gk2[l], ((0, LANES - GLA_RANK), (0, 0))), gla_b_gk2[l][None, :],
                  gla_norm_w[l][None, :])
        h = _mixproj(x2.reshape(batch, seq, d), _relayout_w_in(w_in[l]), cos_t, sin_t, params).reshape(t, D_MIX)

        wr_p = jnp.pad(w_router[l], ((0, 0), (0, LANES - N_EXPERTS)))
        br_p = jnp.pad(b_router[l], (0, LANES - N_EXPERTS), constant_values=NEG_BIG)[None, :]
        wr_hi = wr_p.astype(BF16)
        wr_lo = (wr_p - wr_hi.astype(F32)).astype(BF16)
        x1, x1p, mi, mf, cnt = _post(h, x2, w_out[l].astype(BF16), ln1_g[l][None, :], ln1_b[l][None, :],
                                     wr_hi, wr_lo, br_p)

        counts = cnt[0, :N_EXPERTS].astype(jnp.int32)
        padded = (counts + BM - 1) // BM * BM
        end_padded = jnp.cumsum(padded)
        start_padded = end_padded - padded
        top_idx = mi[:TOP_K]
        start_of = jnp.sum(jnp.where(top_idx[None] == jnp.arange(N_EXPERTS, dtype=jnp.int32)[:, None, None],
                                     start_padded[:, None, None], 0), axis=0)
        dest = start_of + mi[TOP_K:]
        block_start = jnp.arange(nb, dtype=jnp.int32) * BM
        block_expert = jnp.minimum(jnp.sum((end_padded[None, :] <= block_start[:, None]).astype(jnp.int32), axis=1),
                                   N_EXPERTS - 1)
        n_valid = (end_padded[-1:] // BM).astype(jnp.int32)

        scatter_idx = dest.reshape(TOP_K, SC_WORKERS, -1, SC_CHUNK).transpose(1, 2, 0, 3)
        xin = _sc_scatter(x1p, scatter_idx.reshape(SC_WORKERS, -1, SC_CHUNK), cap)
        yb = _ffn(l, block_expert, n_valid, xin, w_gate, b_gate[:, :, None, :], w_up, b_up[:, :, None, :],
                  w_down, b_down[:, :, None, :])
        tp = t // COMB_PARTS
        x2 = None
        for part in range(COMB_PARTS):
            idx = dest[:, part * tp:(part + 1) * tp].reshape(SC_WORKERS, -1, SC_CHUNK)
            rows = _sc_gather(yb, idx).reshape(TOP_K, tp, ROW_WORDS)
            x2 = _combine(x1, rows, mf, ln2_g[l][None, :], ln2_b[l][None, :], part, COMB_PARTS, x2)
    return x2.reshape(batch, seq, d)
```

```python
import functools
import math

import jax
import jax.numpy as jnp
from jax import lax
from jax.experimental import pallas as pl
from jax.experimental.pallas import tpu as pltpu
from jax.experimental.pallas import tpu_sc as plsc

F32 = jnp.float32
BF16 = jnp.bfloat16

D_MODEL = 1024
CHUNK = 64
RET_HEADS, RET_DK, RET_DV = 4, 64, 64
RET_W = RET_HEADS * RET_DV
SSD_HEADS, SSD_HEAD_DIM, SSD_STATE, SSD_GROUPS, SSD_CONV = 8, 64, 64, 2, 4
SSD_W = SSD_HEADS * SSD_HEAD_DIM
SSD_BC = SSD_GROUPS * SSD_STATE
SSD_XBC = SSD_W + 2 * SSD_BC
GLA_HEADS, GLA_DK, GLA_DV, GLA_RANK, GLA_TEMP = 4, 32, 64, 16, 16.0
GLA_QK = GLA_HEADS * GLA_DK
GLA_W = GLA_HEADS * GLA_DV
D_MIX = RET_W + SSD_W + GLA_W
N_EXPERTS, TOP_K, D_FF = 32, 4, 1024
SWIGLU_LIMIT, SWIGLU_ALPHA = 7.0, 1.702
ROPE_BASE = 10000.0
LN_EPS, NORM_EPS = 1e-5, 1e-6
DEPTH = 2
DEEPNORM_ALPHA = (2.0 * DEPTH) ** 0.25

LANES = 128
ROPE_W = LANES
NEG_BIG = -1e30
VMEM_LIMIT = 56 * 1024 * 1024

_SEGS = (("rq", 256), ("rk", 256), ("rv", 256), ("rg", 256), ("sz", SSD_W), ("sxbc", SSD_XBC),
         ("sdt", 128), ("gq", 128), ("gk", 128), ("gv", 256), ("ggk", 128), ("gg", 256))
COL = {}
_off = 0
for _n, _w in _SEGS:
    COL[_n] = _off
    _off += _w
NP = _off

MIX_G = 8
PROJ_SLAB = 512
TM_POST = 1024
POST_SPLIT = 4
TM_COMB = 512
COMB_PARTS = 1
BM = 512
FFN_SPLIT = 1
ROW_WORDS = D_MODEL // 2
SC_WORKERS = 32
SC_CHUNK = 64


def _dot(a, b, dims=(((1,), (0,)), ((), ())), precision=None):
    return lax.dot_general(a, b, dims, precision=precision, preferred_element_type=F32)


_NT = (((1,), (1,)), ((), ()))
_TN = (((0,), (0,)), ((), ()))


def _iota(shape, dim):
    return lax.broadcasted_iota(jnp.int32, shape, dim)


def _vdiv(x, n):
    assert n & (n - 1) == 0
    return lax.shift_right_logical(x, n.bit_length() - 1)


def _vmod(x, n):
    assert n & (n - 1) == 0
    return jnp.bitwise_and(x, n - 1)


def _silu(x):
    return x * jax.nn.sigmoid(x)


def _softplus(x):
    return jnp.maximum(x, 0.0) + jnp.log(1.0 + jnp.exp(-jnp.abs(x)))


def _pack_rows(x):
    w = x.shape[1] // 2
    lo = lax.bitcast_convert_type(x[:, :w].astype(BF16).astype(F32), jnp.uint32)
    hi = lax.bitcast_convert_type(x[:, w:].astype(BF16).astype(F32), jnp.uint32)
    return lax.bitcast_convert_type(lax.shift_right_logical(lo, jnp.uint32(16)) | hi, jnp.int32)


def _unpack_rows(words):
    u = lax.bitcast_convert_type(words, jnp.uint32)
    a = lax.bitcast_convert_type(lax.shift_left(u, jnp.uint32(16)), F32)
    b = lax.bitcast_convert_type(u & jnp.uint32(0xFFFF0000), F32)
    return jnp.concatenate([a, b], axis=-1)


def _seg_sum64(x):
    first = _iota((1, LANES), 1) < 64
    outs = []
    for j in range(x.shape[-1] // LANES):
        blk = x[:, j * LANES:(j + 1) * LANES]
        lo = jnp.sum(jnp.where(first, blk, 0.0), axis=-1, keepdims=True)
        hi = jnp.sum(jnp.where(first, 0.0, blk), axis=-1, keepdims=True)
        outs.append(jnp.where(first, lo, hi))
    return jnp.concatenate(outs, axis=-1)


def _block_mask(shape, row_blk, col_blk):
    keep = _vdiv(_iota(shape, 0), row_blk) == _vdiv(_iota(shape, 1), col_blk)
    return jnp.where(keep, 1.0, 0.0).astype(BF16)


def _block_diag(x, mask):
    reps = mask.shape[0] // x.shape[0]
    return jnp.concatenate([x.astype(BF16)] * reps, axis=0) * mask


def _expand_heads(x, expand):
    hi = x.astype(BF16)
    r1 = x - hi.astype(F32)
    mid = r1.astype(BF16)
    lo = (r1 - mid.astype(F32)).astype(BF16)
    return _dot(hi, expand) + _dot(mid, expand) + _dot(lo, expand)


def _cumsum_rows(tri, x):
    hi = x.astype(BF16)
    lo = (x - hi.astype(F32)).astype(BF16)
    return _dot(tri, hi) + _dot(tri, lo)


def _rope_kernel(pos_ref, cos_ref, sin_ref):
    lane = _iota((1, ROPE_W), 1)
    half = RET_DK // 2
    k = _vmod(lane, half).astype(F32)
    inv_freq = jnp.exp(k * (-math.log(ROPE_BASE) / half))
    ang = pos_ref[...].astype(F32) * inv_freq
    first = _vmod(lane, RET_DK) < half
    cos_ref[...] = jnp.cos(ang)
    sin_ref[...] = jnp.where(first, -1.0, 1.0) * jnp.sin(ang)


def _rope_tables(pos_col):
    t = pos_col.shape[0]
    tm = 512
    w = ROPE_W
    return pl.pallas_call(
        _rope_kernel,
        grid=(t // tm,),
        in_specs=[pl.BlockSpec((tm, 1), lambda i: (i, 0))],
        out_specs=[pl.BlockSpec((tm, w), lambda i: (i, 0))] * 2,
        out_shape=[jax.ShapeDtypeStruct((t, w), F32)] * 2,
        compiler_params=pltpu.CompilerParams(dimension_semantics=("arbitrary",)),
        name="rope_tables",
    )(pos_col)


def _mixproj_kernel(xn_ref, w_ref, cos_ref, sin_ref, retw_ref, convw_ref, convb_ref, dtb_ref, alog_ref,
                    dskip_ref, ssdw_ref, wgk_ref, bgk_ref, glaw_ref, h_ref,
                    proj_a, proj_b, ret_s, ssd_s, gla_s, stage, m_heads, m_groups, m_gla, m_expand, *,
                    chunks_per_seq):
    C = CHUNK
    i = pl.program_id(0)
    cur = jnp.maximum(i - 1, 0)

    @pl.when(i == 0)
    def _():
        m_heads[...] = _block_mask(m_heads.shape, C, 64)
        m_groups[...] = _block_mask(m_groups.shape, C * SSD_HEADS // SSD_GROUPS, SSD_STATE)
        m_gla[...] = _block_mask(m_gla.shape, C, GLA_DK)
        m_expand[...] = _block_mask(m_expand.shape, 1, SSD_HEAD_DIM)
        proj_b[...] = jnp.zeros_like(proj_b)

    @pl.when(lax.rem(cur, chunks_per_seq) == 0)
    def _():
        ret_s[...] = jnp.zeros_like(ret_s)
        ssd_s[...] = jnp.zeros_like(ssd_s)
        gla_s[...] = jnp.zeros_like(gla_s)
        for g in range(MIX_G):
            stage[g, 0:8, :] = jnp.zeros((8, SSD_XBC), F32)

    def conv_act(sq, pref):
        stage[sq, 8:8 + C, :] = pref[sq * C:(sq + 1) * C, COL["sxbc"]:COL["sxbc"] + SSD_XBC]
        acc = convb_ref[...] + convw_ref[0:1, :] * stage[sq, 5:5 + C, :]
        for j in range(1, SSD_CONV):
            acc = acc + convw_ref[j:j + 1, :] * stage[sq, 5 + j:5 + j + C, :]
        stage[sq, 0:8, :] = stage[sq, C:C + 8, :]
        return _silu(acc)

    lane256 = _iota((1, 256), 1)
    head = _vdiv(lane256, 64).astype(F32)
    log_gamma = jnp.log(1.0 - jnp.exp((-5.0 - head) * math.log(2.0)))
    row = _iota((C, 1), 0).astype(F32)
    dist = row - _vmod(lane256, 64).astype(F32)
    ret_intra = jnp.where(dist >= 0, jnp.exp(log_gamma * jnp.maximum(dist, 0.0)), 0.0)
    ret_qdec = jnp.exp(log_gamma * (row + 1.0))
    ret_kdec = jnp.exp(log_gamma * (C - 1.0 - row))
    ret_cdec = jnp.exp(log_gamma * C)
    first_half = _vmod(lane256, RET_DK) < (RET_DK // 2)

    tri = jnp.where(_iota((C, C), 0) >= _iota((C, C), 1), 1.0, 0.0).astype(BF16)
    causal4 = _iota((C, 256), 0) >= _vmod(_iota((C, 256), 1), 64)
    causal8 = _iota((C, 512), 0) >= _vmod(_iota((C, 512), 1), 64)
    eye8 = _iota((C, 512), 0) == _vmod(_iota((C, 512), 1), 64)

    a_neg = -jnp.exp(alog_ref[...])

    def rot(t, cos, sin):
        sw = jnp.where(first_half, pltpu.roll(t, 256 - 32, 1), pltpu.roll(t, 32, 1))
        return t * cos + sw * sin

    def chunk_of(sq, pref):
        def seg(name, width):
            return pref[sq * C:(sq + 1) * C, COL[name]:COL[name] + width]

        xact = conv_act(sq, pref)
        yield

        reps = RET_HEADS * RET_DK // ROPE_W
        cos = jnp.concatenate([cos_ref[sq]] * reps, axis=-1)
        sin = jnp.concatenate([sin_ref[sq]] * reps, axis=-1)
        q = rot(seg("rq", 256), cos, sin)
        k = rot(seg("rk", 256), cos, sin) * (RET_DK ** -0.5)
        v = seg("rv", 256)
        vb = v.astype(BF16)
        kbd = _block_diag(k, m_heads[...])
        scores = _dot(q.astype(BF16), kbd, _NT) * ret_intra
        vbd = _block_diag(v, m_heads[...])
        yield
        s_prev = ret_s[sq]
        o = _dot(scores.astype(BF16), vbd) + _dot((q * ret_qdec).astype(BF16), s_prev.astype(BF16))
        contrib = _dot((k * ret_kdec).astype(BF16), vb, _TN)
        yield
        keep = _vdiv(_iota((256, 256), 0), RET_DK) == _vdiv(_iota((256, 256), 1), RET_DV)
        ret_s[sq] = jnp.where(keep, ret_cdec * s_prev + contrib, 0.0)
        mu = _seg_sum64(o) * (1.0 / RET_DV)
        oc = o - mu
        var = _seg_sum64(oc * oc) * (1.0 / RET_DV)
        o = oc * lax.rsqrt(var + LN_EPS) * retw_ref[...]
        h_ref[sq, :, 0:RET_W] = (_silu(seg("rg", 256)) * o).astype(BF16)
        yield

        xs = xact[:, 0:SSD_W]
        bm = xact[:, SSD_W:SSD_W + SSD_BC]
        cm = xact[:, SSD_W + SSD_BC:SSD_XBC]
        cmb = cm.astype(BF16)
        dt_c = _softplus(seg("sdt", LANES) + dtb_ref[...])
        acum_c = _cumsum_rows(tri, dt_c * a_neg)
        both = _expand_heads(jnp.concatenate([dt_c, acum_c], axis=0), m_expand[...])
        dt = both[0:C, :]
        acum = both[C:2 * C, :]
        yield
        arow = jnp.sum(jnp.where(eye8, acum, 0.0), axis=0, keepdims=True)
        decay = jnp.exp(jnp.where(causal8, acum - arow, NEG_BIG))
        b8 = _block_diag(bm, m_groups[...])
        cb = _dot(cmb, b8, _NT)
        yield
        m = (cb * decay).astype(BF16)
        xdt = xs * dt
        s2 = ssd_s[sq]
        half = SSD_W // SSD_GROUPS
        ys = []
        for g in range(SSD_GROUPS):
            xbd = _block_diag(xdt[:, g * half:(g + 1) * half], m_heads[...])
            ys.append(_dot(m[:, g * half:(g + 1) * half], xbd))
        y = jnp.concatenate(ys, axis=-1)
        y = y + _dot(cmb, s2.astype(BF16)) * jnp.exp(acum)
        y = y + dskip_ref[...] * xs
        a_last = acum[C - 1:C, :]
        sd = jnp.exp(a_last - acum)
        contrib_s = _dot(bm.astype(BF16), (xdt * sd).astype(BF16), _TN)
        yield
        keep_s = _vdiv(_iota(s2.shape, 0), SSD_STATE) == _vdiv(_iota(s2.shape, 1), half)
        ssd_s[sq] = jnp.where(keep_s, s2 * jnp.exp(a_last) + contrib_s, 0.0)
        yz = y * _silu(seg("sz", SSD_W))
        outs = []
        for g in range(SSD_GROUPS):
            blk = yz[:, g * half:(g + 1) * half]
            ms = jnp.mean(blk * blk, axis=-1, keepdims=True)
            outs.append(blk * lax.rsqrt(ms + NORM_EPS))
        h_ref[sq, :, RET_W:RET_W + SSD_W] = (jnp.concatenate(outs, axis=-1) * ssdw_ref[...]).astype(BF16)

        yield

        gq = seg("gq", GLA_QK) * (GLA_DK ** -0.5)
        gkk = seg("gk", GLA_QK)
        gv = seg("gv", GLA_W)
        gkl = _dot(seg("ggk", 128).astype(BF16), wgk_ref[...].astype(BF16)) + bgk_ref[...]
        yield
        log_a = -_softplus(-gkl) * (1.0 / GLA_TEMP)
        b = _cumsum_rows(tri, log_a)
        yield
        q_t = (gq * jnp.exp(b)).astype(BF16)
        k_t = gkk * jnp.exp(-b)
        kbd_g = _block_diag(k_t, m_gla[...])
        att = jnp.where(causal4, _dot(q_t, kbd_g, _NT), 0.0)
        yield
        vbd_g = _block_diag(gv, m_heads[...])
        st = gla_s[sq]
        og = _dot(att.astype(BF16), vbd_g) + _dot(q_t, st.astype(BF16), _NT)
        b_last = b[C - 1:C, :]
        kd = (gkk * jnp.exp(b_last - b)).astype(BF16)
        contrib_g = _dot(gv.astype(BF16), kd, _TN)
        yield
        keep_g = _vdiv(_iota(st.shape, 0), GLA_DV) == _vdiv(_iota(st.shape, 1), GLA_DK)
        gla_s[sq] = jnp.where(keep_g, st * jnp.exp(b_last) + contrib_g, 0.0)
        ms = _seg_sum64(og * og) * (1.0 / GLA_DV)
        og = og * lax.rsqrt(ms + NORM_EPS) * glaw_ref[...]
        h_ref[sq, :, RET_W + SSD_W:D_MIX] = (_silu(seg("gg", GLA_W)) * og).astype(BF16)

    def step(p_read, p_write):
        xb = xn_ref[...].reshape(MIX_G * C, D_MODEL).astype(BF16)
        edges = list(range(0, NP, PROJ_SLAB)) + [NP]
        slabs = list(zip(edges[:-1], edges[1:]))

        live = [chunk_of(sq, p_read) for sq in range(MIX_G)]
        while live or slabs:
            live = [g for g in live if next(g, "done") != "done"]
            if slabs:
                lo, hi = slabs.pop(0)
                p_write[:, lo:hi] = _dot(xb, w_ref[:, lo:hi])

    @pl.when(lax.rem(i, 2) == 0)
    def _():
        step(proj_b, proj_a)

    @pl.when(lax.rem(i, 2) == 1)
    def _():
        step(proj_a, proj_b)


def _mixproj(x3, w_p, cos_t, sin_t, params):
    batch, seq, _ = x3.shape
    assert batch % MIX_G == 0 and seq % CHUNK == 0
    cps = seq // CHUNK
    n = (batch // MIX_G) * cps

    def cur_map(i):
        c = jnp.clip(i - 1, 0, n - 1)
        return (c // cps, c % cps, 0)

    def next_map(i):
        c = jnp.minimum(i, n - 1)
        return (c // cps, c % cps, 0)

    const = lambda i: (0, 0)
    specs = [pl.BlockSpec((MIX_G, CHUNK, D_MODEL), next_map),
             pl.BlockSpec((D_MODEL, NP), const),
             pl.BlockSpec((MIX_G, CHUNK, ROPE_W), cur_map),
             pl.BlockSpec((MIX_G, CHUNK, ROPE_W), cur_map)]
    specs += [pl.BlockSpec(p.shape, const) for p in params]
    return pl.pallas_call(
        functools.partial(_mixproj_kernel, chunks_per_seq=cps),
        grid=(n + 1,),
        in_specs=specs,
        out_specs=pl.BlockSpec((MIX_G, CHUNK, D_MIX), cur_map),
        out_shape=jax.ShapeDtypeStruct((batch, seq, D_MIX), BF16),
        scratch_shapes=[pltpu.VMEM((MIX_G * CHUNK, NP), F32),
                        pltpu.VMEM((MIX_G * CHUNK, NP), F32),
                        pltpu.VMEM((MIX_G, 256, 256), F32),
                        pltpu.VMEM((MIX_G, SSD_BC, SSD_W), F32),
                        pltpu.VMEM((MIX_G, GLA_W, GLA_QK), F32),
                        pltpu.VMEM((MIX_G, CHUNK + 8, SSD_XBC), F32),
                        pltpu.VMEM((RET_HEADS * CHUNK, 256), BF16),
                        pltpu.VMEM((SSD_HEADS * CHUNK, SSD_BC), BF16),
                        pltpu.VMEM((GLA_HEADS * CHUNK, GLA_QK), BF16),
                        pltpu.VMEM((LANES, SSD_W), BF16)],
        compiler_params=pltpu.CompilerParams(dimension_semantics=("arbitrary",),
                                             vmem_limit_bytes=VMEM_LIMIT),
        name="inproj_mixer",
    )(x3, w_p, cos_t, sin_t, *params)


def _layer_norm(y, g, b):
    mu = jnp.mean(y, axis=-1, keepdims=True)
    yc = y - mu
    var = jnp.mean(yc * yc, axis=-1, keepdims=True)
    return yc * lax.rsqrt(var + LN_EPS) * g + b


def _post_kernel(h_ref, x_ref, wout_ref, g_ref, b_ref, wrh_ref, wrl_ref, br_ref,
                 x1_ref, x1p_ref, mi_ref, mf_ref, cnt_ref, carry):
    sub = TM_POST // POST_SPLIT

    @pl.when(pl.program_id(0) == 0)
    def _():
        carry[...] = jnp.zeros_like(carry)

    lane_i = _iota((sub, LANES), 1)
    lane = lane_i.astype(F32)
    found = {}

    def sub_tile(part):
        rows = slice(part * sub, (part + 1) * sub)
        mix = _dot(h_ref[rows, :], wout_ref[...])
        yield
        x1 = _layer_norm(DEEPNORM_ALPHA * x_ref[rows, :] + mix, g_ref[...], b_ref[...])
        x1_ref[rows, :] = x1
        x1p_ref[rows, :] = _pack_rows(x1)
        x_hi = x1.astype(BF16)
        x_lo = (x1 - x_hi.astype(F32)).astype(BF16)
        logits = (_dot(x_hi, wrh_ref[...]) + _dot(x_lo, wrh_ref[...]) + _dot(x_hi, wrl_ref[...])
                  + br_ref[...])
        yield
        work = logits
        vals, idxs = [], []
        multi = jnp.zeros((sub, LANES), F32)
        for _ in range(TOP_K):
            m = jnp.max(work, axis=-1, keepdims=True)
            idx = jnp.min(jnp.where(work == m, lane, float(LANES)), axis=-1, keepdims=True)
            hit = lane == idx
            multi = multi + hit.astype(F32)
            work = jnp.where(hit, -jnp.inf, work)
            vals.append(m)
            idxs.append(idx)
            yield
        exps = [jnp.exp(v - vals[0]) for v in vals]
        denom = exps[0] + exps[1] + exps[2] + exps[3]
        gates = [e / denom for e in exps]
        before = (_iota((sub, sub), 0) > _iota((sub, sub), 1)).astype(BF16)
        found[part] = (idxs, gates, _dot(before, multi.astype(BF16)), multi)

    live = [sub_tile(part) for part in range(POST_SPLIT)]
    while live:
        live = [g for g in live if next(g, "done") != "done"]

    base = carry[...]
    for part in range(POST_SPLIT):
        idxs, gates, prior_local, multi = found[part]
        prior = prior_local + base
        mi = jnp.zeros((sub, LANES), F32)
        mf = jnp.zeros((sub, LANES), F32)
        for kk in range(TOP_K):
            rank = jnp.sum(jnp.where(lane == idxs[kk], prior, 0.0), axis=-1, keepdims=True)
            mi = jnp.where(lane_i == kk, idxs[kk], mi)
            mi = jnp.where(lane_i == TOP_K + kk, rank, mi)
            mf = jnp.where(lane_i == kk, gates[kk], mf)
        mi_ref[:, part * sub:(part + 1) * sub] = jnp.transpose(mi)[0:2 * TOP_K, :].astype(jnp.int32)
        mf_ref[part * sub:(part + 1) * sub, :] = mf
        base = base + jnp.sum(multi, axis=0, keepdims=True)
    carry[...] = base
    cnt_ref[...] = jnp.broadcast_to(base, cnt_ref.shape)


def _post(h, x2, w_out_b, ln_g, ln_b, wr_hi, wr_lo, br_p):
    t = x2.shape[0]
    tm = TM_POST
    row = lambda i: (i, 0)
    const = lambda i: (0, 0)
    return pl.pallas_call(
        _post_kernel,
        grid=(t // tm,),
        in_specs=[pl.BlockSpec((tm, D_MIX), row), pl.BlockSpec((tm, D_MODEL), row),
                  pl.BlockSpec((D_MIX, D_MODEL), const), pl.BlockSpec((1, D_MODEL), const),
                  pl.BlockSpec((1, D_MODEL), const), pl.BlockSpec((D_MODEL, LANES), const),
                  pl.BlockSpec((D_MODEL, LANES), const), pl.BlockSpec((1, LANES), const)],
        out_specs=[pl.BlockSpec((tm, D_MODEL), row), pl.BlockSpec((tm, ROW_WORDS), row),
                   pl.BlockSpec((2 * TOP_K, tm), lambda i: (0, i)), pl.BlockSpec((tm, LANES), row),
                   pl.BlockSpec((8, LANES), const)],
        out_shape=[jax.ShapeDtypeStruct((t, D_MODEL), F32), jax.ShapeDtypeStruct((t, ROW_WORDS), jnp.int32),
                   jax.ShapeDtypeStruct((2 * TOP_K, t), jnp.int32), jax.ShapeDtypeStruct((t, LANES), F32),
                   jax.ShapeDtypeStruct((8, LANES), F32)],
        scratch_shapes=[pltpu.VMEM((1, LANES), F32)],
        compiler_params=pltpu.CompilerParams(dimension_semantics=("arbitrary",),
                                             vmem_limit_bytes=VMEM_LIMIT),
        name="outproj_ln_router",
    )(h, x2, w_out_b, ln_g, ln_b, wr_hi, wr_lo, br_p)


def _ffn_kernel(be_ref, nv_ref, x_ref, wg_ref, bg_ref, wu_ref, bu_ref, wd_ref, bd_ref, o_ref,
                wg_b, wu_b, wd_b):
    i = pl.program_id(0)
    valid = i < nv_ref[0]
    e = be_ref[i]
    prev = be_ref[jnp.maximum(i - 1, 0)]
    fresh = jnp.logical_or(i == 0, e != prev)

    @pl.when(jnp.logical_and(valid, fresh))
    def _():
        wg_b[...] = wg_ref[0, 0].astype(BF16)
        wu_b[...] = wu_ref[0, 0].astype(BF16)
        wd_b[...] = wd_ref[0, 0].astype(BF16)

    def row_part(part):
        sub = BM // FFN_SPLIT
        rows = slice(part * sub, (part + 1) * sub)
        x = _unpack_rows(x_ref[rows, :]).astype(BF16)
        hg = jnp.minimum(_dot(x, wg_b[...]) + bg_ref[0, 0], SWIGLU_LIMIT)
        hu = jnp.clip(_dot(x, wu_b[...]) + bu_ref[0, 0], -SWIGLU_LIMIT, SWIGLU_LIMIT)
        yield
        hh = (hu + 1.0) * hg * jax.nn.sigmoid(SWIGLU_ALPHA * hg)
        y = _dot(hh.astype(BF16), wd_b[...]) + bd_ref[0, 0]
        yield
        o_ref[rows, :] = _pack_rows(y)

    @pl.when(valid)
    def _():
        live = [row_part(part) for part in range(FFN_SPLIT)]
        while live:
            live = [g for g in live if next(g, "done") != "done"]


def _ffn(layer, block_expert, n_valid, xin, wg, bg, wu, bu, wd, bd):
    cap = xin.shape[0]
    nb = cap // BM

    def blk(i, be, nv):
        return jnp.maximum(jnp.minimum(i, nv[0] - 1), 0)

    row = lambda i, be, nv: (blk(i, be, nv), 0)
    wmap = lambda i, be, nv: (layer, be[blk(i, be, nv)], 0, 0)
    grid_spec = pltpu.PrefetchScalarGridSpec(
        num_scalar_prefetch=2,
        grid=(nb,),
        in_specs=[pl.BlockSpec((BM, ROW_WORDS), row),
                  pl.BlockSpec((1, 1, D_MODEL, D_FF), wmap), pl.BlockSpec((1, 1, 1, D_FF), wmap),
                  pl.BlockSpec((1, 1, D_MODEL, D_FF), wmap), pl.BlockSpec((1, 1, 1, D_FF), wmap),
                  pl.BlockSpec((1, 1, D_FF, D_MODEL), wmap), pl.BlockSpec((1, 1, 1, D_MODEL), wmap)],
        out_specs=pl.BlockSpec((BM, ROW_WORDS), row),
        scratch_shapes=[pltpu.VMEM((D_MODEL, D_FF), BF16), pltpu.VMEM((D_MODEL, D_FF), BF16),
                        pltpu.VMEM((D_FF, D_MODEL), BF16)],
    )
    return pl.pallas_call(
        _ffn_kernel,
        grid_spec=grid_spec,
        out_shape=jax.ShapeDtypeStruct((cap, ROW_WORDS), jnp.int32),
        compiler_params=pltpu.CompilerParams(dimension_semantics=("arbitrary",),
                                             vmem_limit_bytes=VMEM_LIMIT),
        name="expert_ffn",
    )(block_expert, n_valid, xin, wg, bg, wu, bu, wd, bd)


def _sc_gather(table, idx3):
    nw, n_chunks, ch = idx3.shape
    width = table.shape[1]
    per_worker = n_chunks * ch
    mesh = plsc.VectorSubcoreMesh(core_axis_name="c", subcore_axis_name="s")
    n_cores = mesh.num_cores
    assert nw == n_cores * mesh.num_subcores and n_chunks % 2 == 0 and ch == SC_CHUNK

    def body(table_hbm, idx_hbm, out_hbm, idx_v, rows0, rows1, sem_g0, sem_g1, sem_w0, sem_w1):
        wid = lax.axis_index("s") * n_cores + lax.axis_index("c")
        base = wid * per_worker
        pltpu.sync_copy(idx_hbm.at[wid], idx_v)

        @pl.loop(0, n_chunks, step=2)
        def _(c):
            g0 = pltpu.async_copy(table_hbm.at[idx_v.at[c]], rows0, sem_g0)
            g1 = pltpu.async_copy(table_hbm.at[idx_v.at[c + 1]], rows1, sem_g1)
            g0.wait()
            w0 = pltpu.async_copy(rows0, out_hbm.at[pl.ds(base + c * ch, ch)], sem_w0)
            g1.wait()
            w1 = pltpu.async_copy(rows1, out_hbm.at[pl.ds(base + (c + 1) * ch, ch)], sem_w1)
            w0.wait()
            w1.wait()

    return pl.kernel(
        body,
        out_type=jax.ShapeDtypeStruct((nw * per_worker, width), table.dtype),
        mesh=mesh,
        scratch_types=[pltpu.VMEM((n_chunks, ch), jnp.int32),
                       pltpu.VMEM((ch, width), table.dtype), pltpu.VMEM((ch, width), table.dtype),
                       pltpu.SemaphoreType.DMA, pltpu.SemaphoreType.DMA,
                       pltpu.SemaphoreType.DMA, pltpu.SemaphoreType.DMA],
        name="sc_row_gather",
    )(table, idx3)


def _sc_scatter(rows, idx3, n_out):
    nw, n_lists, ch = idx3.shape
    n_chunks = n_lists // TOP_K
    width = rows.shape[1]
    per_worker = n_chunks * ch
    mesh = plsc.VectorSubcoreMesh(core_axis_name="c", subcore_axis_name="s")
    n_cores = mesh.num_cores
    assert nw == n_cores * mesh.num_subcores and n_chunks % 2 == 0 and ch == SC_CHUNK
    assert nw * per_worker == rows.shape[0]

    def body(rows_hbm, idx_hbm, out_hbm, idx_v, buf0, buf1, sem_r0, sem_r1, sem_w0, sem_w1):
        wid = lax.axis_index("s") * n_cores + lax.axis_index("c")
        base = wid * per_worker
        pltpu.sync_copy(idx_hbm.at[wid], idx_v)

        @pl.loop(0, n_chunks, step=2)
        def _(c):
            r0 = pltpu.async_copy(rows_hbm.at[pl.ds(base + c * ch, ch)], buf0, sem_r0)
            r1 = pltpu.async_copy(rows_hbm.at[pl.ds(base + (c + 1) * ch, ch)], buf1, sem_r1)
            r0.wait()
            w0 = [pltpu.async_copy(buf0, out_hbm.at[idx_v.at[c * TOP_K + k]], sem_w0) for k in range(TOP_K)]
            r1.wait()
            w1 = [pltpu.async_copy(buf1, out_hbm.at[idx_v.at[(c + 1) * TOP_K + k]], sem_w1) for k in range(TOP_K)]
            for w in w0 + w1:
                w.wait()

    return pl.kernel(
        body,
        out_type=jax.ShapeDtypeStruct((n_out, width), rows.dtype),
        mesh=mesh,
        scratch_types=[pltpu.VMEM((n_lists, ch), jnp.int32),
                       pltpu.VMEM((ch, width), rows.dtype), pltpu.VMEM((ch, width), rows.dtype),
                       pltpu.SemaphoreType.DMA, pltpu.SemaphoreType.DMA,
                       pltpu.SemaphoreType.DMA, pltpu.SemaphoreType.DMA],
        name="sc_row_scatter",
    )(rows, idx3)


def _combine_kernel(x_ref, rows_ref, mf_ref, g_ref, b_ref, *rest):
    o_ref = rest[-1]
    mf = mf_ref[...]
    y = DEEPNORM_ALPHA * x_ref[...]
    for kk in range(TOP_K):
        y = y + mf[:, kk:kk + 1] * _unpack_rows(rows_ref[kk])
    o_ref[...] = _layer_norm(y, g_ref[...], b_ref[...])


def _combine(x1, rows, mf, ln_g, ln_b, part, n_parts, prev):
    t = x1.shape[0]
    tm = TM_COMB
    steps = t // n_parts // tm
    row = lambda i: (part * steps + i, 0)
    const = lambda i: (0, 0)
    in_specs = [pl.BlockSpec((tm, D_MODEL), row),
                pl.BlockSpec((TOP_K, tm, ROW_WORDS), lambda i: (0, i, 0)),
                pl.BlockSpec((tm, LANES), row),
                pl.BlockSpec((1, D_MODEL), const), pl.BlockSpec((1, D_MODEL), const)]
    args = [x1, rows, mf, ln_g, ln_b]
    aliases = {}
    if prev is not None:
        in_specs.append(pl.BlockSpec(memory_space=pl.ANY))
        args.append(prev)
        aliases = {len(args) - 1: 0}
    return pl.pallas_call(
        _combine_kernel,
        grid=(steps,),
        in_specs=in_specs,
        out_specs=pl.BlockSpec((tm, D_MODEL), row),
        out_shape=jax.ShapeDtypeStruct((t, D_MODEL), F32),
        input_output_aliases=aliases,
        compiler_params=pltpu.CompilerParams(dimension_semantics=("arbitrary",),
                                             vmem_limit_bytes=VMEM_LIMIT),
        name="combine_ln",
    )(*args)


def _relayout_w_in(w):
    widths = (256, 256, 256, 256, SSD_W, SSD_XBC, SSD_HEADS, GLA_QK, GLA_QK, GLA_W, GLA_RANK, GLA_W)
    offs = [0]
    for wd in widths:
        offs.append(offs[-1] + wd)
    parts = [w[:, offs[i]:offs[i + 1]] for i in range(len(widths))]
    parts[6] = jnp.pad(parts[6], ((0, 0), (0, LANES - SSD_HEADS)))
    parts[10] = jnp.pad(parts[10], ((0, 0), (0, LANES - GLA_RANK)))
    return jnp.concatenate(parts, axis=1).astype(BF16)


def _rep_heads(p):
    return jnp.repeat(p, SSD_HEAD_DIM)[None, :]


def _pad_heads(p):
    return jnp.pad(p, (0, LANES - SSD_HEADS))[None, :]


def kernel(x, positions, w_in, w_out, ret_norm_w, ssd_conv_w, ssd_conv_b, ssd_dt_bias, ssd_a_log, ssd_d,
           ssd_norm_w, gla_w_gk2, gla_b_gk2, gla_norm_w, ln1_g, ln1_b, w_router, b_router, w_gate, b_gate,
           w_up, b_up, w_down, b_down, ln2_g, ln2_b):
    batch, seq, d = x.shape
    t = batch * seq
    depth = w_in.shape[0]
    assert d == D_MODEL and t % TM_POST == 0 and t % TM_COMB == 0
    n_assign = t * TOP_K
    nb = n_assign // BM + N_EXPERTS
    cap = nb * BM

    cos_t, sin_t = _rope_tables(positions.reshape(t, 1))
    cos_t = cos_t.reshape(batch, seq, -1)
    sin_t = sin_t.reshape(batch, seq, -1)
    x2 = x.reshape(t, d)

    for l in range(depth):
        params = (ret_norm_w[l][None, :], ssd_conv_w[l], ssd_conv_b[l][None, :], _pad_heads(ssd_dt_bias[l]),
                  _pad_heads(ssd_a_log[l]), _rep_heads(ssd_d[l]), ssd_norm_w[l][None, :],
                  jnp.pad(gla_w_gk2[l], ((0, LANES - GLA_RANK), (0, 0))), gla_b_gk2[l][None, :],
                  gla_norm_w[l][None, :])
        h = _mixproj(x2.reshape(batch, seq, d), _relayout_w_in(w_in[l]), cos_t, sin_t, params).reshape(t, D_MIX)

        wr_p = jnp.pad(w_router[l], ((0, 0), (0, LANES - N_EXPERTS)))
        br_p = jnp.pad(b_router[l], (0, LANES - N_EXPERTS), constant_values=NEG_BIG)[None, :]
        wr_hi = wr_p.astype(BF16)
        wr_lo = (wr_p - wr_hi.astype(F32)).astype(BF16)
        x1, x1p, mi, mf, cnt = _post(h, x2, w_out[l].astype(BF16), ln1_g[l][None, :], ln1_b[l][None, :],
                                     wr_hi, wr_lo, br_p)

        counts = cnt[0, :N_EXPERTS].astype(jnp.int32)
        padded = (counts + BM - 1) // BM * BM
        end_padded = jnp.cumsum(padded)
        start_padded = end_padded - padded
        top_idx = mi[:TOP_K]
        start_of = jnp.sum(jnp.where(top_idx[None] == jnp.arange(N_EXPERTS, dtype=jnp.int32)[:, None, None],
                                     start_padded[:, None, None], 0), axis=0)
        dest = start_of + mi[TOP_K:]
        block_start = jnp.arange(nb, dtype=jnp.int32) * BM
        block_expert = jnp.minimum(jnp.sum((end_padded[None, :] <= block_start[:, None]).astype(jnp.int32), axis=1),
                                   N_EXPERTS - 1)
        n_valid = (end_padded[-1:] // BM).astype(jnp.int32)

        scatter_idx = dest.reshape(TOP_K, SC_WORKERS, -1, SC_CHUNK).transpose(1, 2, 0, 3)
        xin = _sc_scatter(x1p, scatter_idx.reshape(SC_WORKERS, -1, SC_CHUNK), cap)
        yb = _ffn(l, block_expert, n_valid, xin, w_gate, b_gate[:, :, None, :], w_up, b_up[:, :, None, :],
                  w_down, b_down[:, :, None, :])
        tp = t // COMB_PARTS
        x2 = None
        for part in range(COMB_PARTS):
            idx = dest[:, part * tp:(part + 1) * tp].reshape(SC_WORKERS, -1, SC_CHUNK)
            rows = _sc_gather(yb, idx).reshape(TOP_K, tp, ROW_WORDS)
            x2 = _combine(x1, rows, mf, ln2_g[l][None, :], ln2_b[l][None, :], part, COMB_PARTS, x2)
    return x2.reshape(batch, seq, d)
```

```python
import functools
import math

import jax
import jax.numpy as jnp
from jax import lax
from jax.experimental import pallas as pl
from jax.experimental.pallas import tpu as pltpu
from jax.experimental.pallas import tpu_sc as plsc

F32 = jnp.float32
BF16 = jnp.bfloat16

D_MODEL = 1024
CHUNK = 64
RET_HEADS, RET_DK, RET_DV = 4, 64, 64
RET_W = RET_HEADS * RET_DV
SSD_HEADS, SSD_HEAD_DIM, SSD_STATE, SSD_GROUPS, SSD_CONV = 8, 64, 64, 2, 4
SSD_W = SSD_HEADS * SSD_HEAD_DIM
SSD_BC = SSD_GROUPS * SSD_STATE
SSD_XBC = SSD_W + 2 * SSD_BC
GLA_HEADS, GLA_DK, GLA_DV, GLA_RANK, GLA_TEMP = 4, 32, 64, 16, 16.0
GLA_QK = GLA_HEADS * GLA_DK
GLA_W = GLA_HEADS * GLA_DV
D_MIX = RET_W + SSD_W + GLA_W
N_EXPERTS, TOP_K, D_FF = 32, 4, 1024
SWIGLU_LIMIT, SWIGLU_ALPHA = 7.0, 1.702
ROPE_BASE = 10000.0
LN_EPS, NORM_EPS = 1e-5, 1e-6
DEPTH = 2
DEEPNORM_ALPHA = (2.0 * DEPTH) ** 0.25

LANES = 128
ROPE_W = LANES
NEG_BIG = -1e30
VMEM_LIMIT = 56 * 1024 * 1024

_SEGS = (("rq", 256), ("rk", 256), ("rv", 256), ("rg", 256), ("sz", SSD_W), ("sxbc", SSD_XBC),
         ("sdt", 128), ("gq", 128), ("gk", 128), ("gv", 256), ("ggk", 128), ("gg", 256))
COL = {}
_off = 0
for _n, _w in _SEGS:
    COL[_n] = _off
    _off += _w
NP = _off

MIX_G = 8
PROJ_SLAB = 512
TM_POST = 1024
POST_SPLIT = 4
TM_COMB = 512
BM = 512
ROW_WORDS = D_MODEL // 2
SC_WORKERS = 32
SC_CHUNK = 64


def _dot(a, b, dims=(((1,), (0,)), ((), ())), precision=None):
    return lax.dot_general(a, b, dims, precision=precision, preferred_element_type=F32)


_NT = (((1,), (1,)), ((), ()))
_TN = (((0,), (0,)), ((), ()))


def _iota(shape, dim):
    return lax.broadcasted_iota(jnp.int32, shape, dim)


def _vdiv(x, n):
    assert n & (n - 1) == 0
    return lax.shift_right_logical(x, n.bit_length() - 1)


def _vmod(x, n):
    assert n & (n - 1) == 0
    return jnp.bitwise_and(x, n - 1)


def _silu(x):
    return x * jax.nn.sigmoid(x)


def _softplus(x):
    return jnp.maximum(x, 0.0) + jnp.log(1.0 + jnp.exp(-jnp.abs(x)))


def _pack_rows(x):
    w = x.shape[1] // 2
    lo = lax.bitcast_convert_type(x[:, :w].astype(BF16).astype(F32), jnp.uint32)
    hi = lax.bitcast_convert_type(x[:, w:].astype(BF16).astype(F32), jnp.uint32)
    return lax.bitcast_convert_type(lax.shift_right_logical(lo, jnp.uint32(16)) | hi, jnp.int32)


def _unpack_rows(words):
    u = lax.bitcast_convert_type(words, jnp.uint32)
    a = lax.bitcast_convert_type(lax.shift_left(u, jnp.uint32(16)), F32)
    b = lax.bitcast_convert_type(u & jnp.uint32(0xFFFF0000), F32)
    return jnp.concatenate([a, b], axis=-1)


def _seg_sum64(x):
    first = _iota((1, LANES), 1) < 64
    outs = []
    for j in range(x.shape[-1] // LANES):
        blk = x[:, j * LANES:(j + 1) * LANES]
        lo = jnp.sum(jnp.where(first, blk, 0.0), axis=-1, keepdims=True)
        hi = jnp.sum(jnp.where(first, 0.0, blk), axis=-1, keepdims=True)
        outs.append(jnp.where(first, lo, hi))
    return jnp.concatenate(outs, axis=-1)


def _block_mask(shape, row_blk, col_blk):
    keep = _vdiv(_iota(shape, 0), row_blk) == _vdiv(_iota(shape, 1), col_blk)
    return jnp.where(keep, 1.0, 0.0).astype(BF16)


def _block_diag(x, mask):
    reps = mask.shape[0] // x.shape[0]
    return jnp.concatenate([x.astype(BF16)] * reps, axis=0) * mask


def _expand_heads(x, expand):
    hi = x.astype(BF16)
    r1 = x - hi.astype(F32)
    mid = r1.astype(BF16)
    lo = (r1 - mid.astype(F32)).astype(BF16)
    return _dot(hi, expand) + _dot(mid, expand) + _dot(lo, expand)


def _cumsum_rows(tri, x):
    hi = x.astype(BF16)
    lo = (x - hi.astype(F32)).astype(BF16)
    return _dot(tri, hi) + _dot(tri, lo)


def _rope_kernel(pos_ref, cos_ref, sin_ref):
    lane = _iota((1, ROPE_W), 1)
    half = RET_DK // 2
    k = _vmod(lane, half).astype(F32)
    inv_freq = jnp.exp(k * (-math.log(ROPE_BASE) / half))
    ang = pos_ref[...].astype(F32) * inv_freq
    lower = lane < RET_DK
    both = jnp.cos(jnp.where(lower, ang, ang - math.pi / 2))
    swapped = pltpu.roll(both, RET_DK, 1)
    first = _vmod(lane, RET_DK) < half
    cos_ref[...] = jnp.where(lower, both, swapped)
    sin_ref[...] = jnp.where(first, -1.0, 1.0) * jnp.where(lower, swapped, both)


def _rope_tables(pos_col):
    t = pos_col.shape[0]
    tm = 512
    w = ROPE_W
    return pl.pallas_call(
        _rope_kernel,
        grid=(t // tm,),
        in_specs=[pl.BlockSpec((tm, 1), lambda i: (i, 0))],
        out_specs=[pl.BlockSpec((tm, w), lambda i: (i, 0))] * 2,
        out_shape=[jax.ShapeDtypeStruct((t, w), F32)] * 2,
        compiler_params=pltpu.CompilerParams(dimension_semantics=("arbitrary",)),
        name="rope_tables",
    )(pos_col)


def _mixproj_kernel(xn_ref, w_ref, cos_ref, sin_ref, retw_ref, convw_ref, convb_ref, dtb_ref, alog_ref,
                    dskip_ref, ssdw_ref, wgk_ref, bgk_ref, glaw_ref, h_ref,
                    proj_a, proj_b, ret_s, ssd_s, gla_s, stage, m_heads, m_groups, m_gla, m_expand, *,
                    chunks_per_seq):
    C = CHUNK
    i = pl.program_id(0)
    cur = jnp.maximum(i - 1, 0)

    @pl.when(i == 0)
    def _():
        m_heads[...] = _block_mask(m_heads.shape, C, 64)
        m_groups[...] = _block_mask(m_groups.shape, C * SSD_HEADS // SSD_GROUPS, SSD_STATE)
        m_gla[...] = _block_mask(m_gla.shape, C, GLA_DK)
        m_expand[...] = _block_mask(m_expand.shape, 1, SSD_HEAD_DIM)
        proj_b[...] = jnp.zeros_like(proj_b)

    @pl.when(lax.rem(cur, chunks_per_seq) == 0)
    def _():
        ret_s[...] = jnp.zeros_like(ret_s)
        ssd_s[...] = jnp.zeros_like(ssd_s)
        gla_s[...] = jnp.zeros_like(gla_s)
        for g in range(MIX_G):
            stage[g, 0:8, :] = jnp.zeros((8, SSD_XBC), F32)

    def conv_act(sq, pref):
        stage[sq, 8:8 + C, :] = pref[sq * C:(sq + 1) * C, COL["sxbc"]:COL["sxbc"] + SSD_XBC]
        acc = convb_ref[...] + convw_ref[0:1, :] * stage[sq, 5:5 + C, :]
        for j in range(1, SSD_CONV):
            acc = acc + convw_ref[j:j + 1, :] * stage[sq, 5 + j:5 + j + C, :]
        stage[sq, 0:8, :] = stage[sq, C:C + 8, :]
        return _silu(acc)

    lane256 = _iota((1, 256), 1)
    head = _vdiv(lane256, 64).astype(F32)
    log_gamma = jnp.log(1.0 - jnp.exp((-5.0 - head) * math.log(2.0)))
    row = _iota((C, 1), 0).astype(F32)
    dist = row - _vmod(lane256, 64).astype(F32)
    ret_intra = jnp.where(dist >= 0, jnp.exp(log_gamma * jnp.maximum(dist, 0.0)), 0.0)
    ret_qdec = jnp.exp(log_gamma * (row + 1.0))
    ret_kdec = jnp.exp(log_gamma * (C - 1.0 - row))
    ret_cdec = jnp.exp(log_gamma * C)
    first_half = _vmod(lane256, RET_DK) < (RET_DK // 2)

    tri = jnp.where(_iota((C, C), 0) >= _iota((C, C), 1), 1.0, 0.0).astype(BF16)
    causal4 = _iota((C, 256), 0) >= _vmod(_iota((C, 256), 1), 64)
    causal8 = _iota((C, 512), 0) >= _vmod(_iota((C, 512), 1), 64)
    eye8 = _iota((C, 512), 0) == _vmod(_iota((C, 512), 1), 64)

    a_neg = -jnp.exp(alog_ref[...])

    def rot(t, cos, sin):
        sw = jnp.where(first_half, pltpu.roll(t, 256 - 32, 1), pltpu.roll(t, 32, 1))
        return t * cos + sw * sin

    def chunk_of(sq, pref):
        def seg(name, width):
            return pref[sq * C:(sq + 1) * C, COL[name]:COL[name] + width]

        xact = conv_act(sq, pref)
        yield

        reps = RET_HEADS * RET_DK // ROPE_W
        cos = jnp.concatenate([cos_ref[sq]] * reps, axis=-1)
        sin = jnp.concatenate([sin_ref[sq]] * reps, axis=-1)
        q = rot(seg("rq", 256), cos, sin)
        k = rot(seg("rk", 256), cos, sin) * (RET_DK ** -0.5)
        v = seg("rv", 256)
        vb = v.astype(BF16)
        kbd = _block_diag(k, m_heads[...])
        scores = _dot(q.astype(BF16), kbd, _NT) * ret_intra
        vbd = _block_diag(v, m_heads[...])
        yield
        s_prev = ret_s[sq]
        o = _dot(scores.astype(BF16), vbd) + _dot((q * ret_qdec).astype(BF16), s_prev.astype(BF16))
        contrib = _dot((k * ret_kdec).astype(BF16), vb, _TN)
        yield
        keep = _vdiv(_iota((256, 256), 0), RET_DK) == _vdiv(_iota((256, 256), 1), RET_DV)
        ret_s[sq] = jnp.where(keep, ret_cdec * s_prev + contrib, 0.0)
        mu = _seg_sum64(o) * (1.0 / RET_DV)
        oc = o - mu
        var = _seg_sum64(oc * oc) * (1.0 / RET_DV)
        o = oc * lax.rsqrt(var + LN_EPS) * retw_ref[...]
        h_ref[sq, :, 0:RET_W] = (_silu(seg("rg", 256)) * o).astype(BF16)
        yield

        xs = xact[:, 0:SSD_W]
        bm = xact[:, SSD_W:SSD_W + SSD_BC]
        cm = xact[:, SSD_W + SSD_BC:SSD_XBC]
        cmb = cm.astype(BF16)
        dt_c = _softplus(seg("sdt", LANES) + dtb_ref[...])
        acum_c = _cumsum_rows(tri, dt_c * a_neg)
        both = _expand_heads(jnp.concatenate([dt_c, acum_c], axis=0), m_expand[...])
        dt = both[0:C, :]
        acum = both[C:2 * C, :]
        yield
        arow = jnp.sum(jnp.where(eye8, acum, 0.0), axis=0, keepdims=True)
        decay = jnp.exp(jnp.where(causal8, acum - arow, NEG_BIG))
        b8 = _block_diag(bm, m_groups[...])
        cb = _dot(cmb, b8, _NT)
        yield
        m = (cb * decay).astype(BF16)
        xdt = xs * dt
        s2 = ssd_s[sq]
        half = SSD_W // SSD_GROUPS
        ys = []
        for g in range(SSD_GROUPS):
            xbd = _block_diag(xdt[:, g * half:(g + 1) * half], m_heads[...])
            ys.append(_dot(m[:, g * half:(g + 1) * half], xbd))
        y = jnp.concatenate(ys, axis=-1)
        y = y + _dot(cmb, s2.astype(BF16)) * jnp.exp(acum)
        y = y + dskip_ref[...] * xs
        a_last = acum[C - 1:C, :]
        sd = jnp.exp(a_last - acum)
        contrib_s = _dot(bm.astype(BF16), (xdt * sd).astype(BF16), _TN)
        yield
        keep_s = _vdiv(_iota(s2.shape, 0), SSD_STATE) == _vdiv(_iota(s2.shape, 1), half)
        ssd_s[sq] = jnp.where(keep_s, s2 * jnp.exp(a_last) + contrib_s, 0.0)
        yz = y * _silu(seg("sz", SSD_W))
        outs = []
        for g in range(SSD_GROUPS):
            blk = yz[:, g * half:(g + 1) * half]
            ms = jnp.mean(blk * blk, axis=-1, keepdims=True)
            outs.append(blk * lax.rsqrt(ms + NORM_EPS))
        h_ref[sq, :, RET_W:RET_W + SSD_W] = (jnp.concatenate(outs, axis=-1) * ssdw_ref[...]).astype(BF16)

        yield

        gq = seg("gq", GLA_QK) * (GLA_DK ** -0.5)
        gkk = seg("gk", GLA_QK)
        gv = seg("gv", GLA_W)
        gkl = _dot(seg("ggk", 128).astype(BF16), wgk_ref[...].astype(BF16)) + bgk_ref[...]
        yield
        log_a = -_softplus(-gkl) * (1.0 / GLA_TEMP)
        b = _cumsum_rows(tri, log_a)
        yield
        q_t = (gq * jnp.exp(b)).astype(BF16)
        k_t = gkk * jnp.exp(-b)
        kbd_g = _block_diag(k_t, m_gla[...])
        att = jnp.where(causal4, _dot(q_t, kbd_g, _NT), 0.0)
        yield
        vbd_g = _block_diag(gv, m_heads[...])
        st = gla_s[sq]
        og = _dot(att.astype(BF16), vbd_g) + _dot(q_t, st.astype(BF16), _NT)
        b_last = b[C - 1:C, :]
        kd = (gkk * jnp.exp(b_last - b)).astype(BF16)
        contrib_g = _dot(gv.astype(BF16), kd, _TN)
        yield
        keep_g = _vdiv(_iota(st.shape, 0), GLA_DV) == _vdiv(_iota(st.shape, 1), GLA_DK)
        gla_s[sq] = jnp.where(keep_g, st * jnp.exp(b_last) + contrib_g, 0.0)
        ms = _seg_sum64(og * og) * (1.0 / GLA_DV)
        og = og * lax.rsqrt(ms + NORM_EPS) * glaw_ref[...]
        h_ref[sq, :, RET_W + SSD_W:D_MIX] = (_silu(seg("gg", GLA_W)) * og).astype(BF16)

    def step(p_read, p_write):
        xb = xn_ref[...].reshape(MIX_G * C, D_MODEL).astype(BF16)
        edges = list(range(0, NP, PROJ_SLAB)) + [NP]
        slabs = list(zip(edges[:-1], edges[1:]))

        live = [chunk_of(sq, p_read) for sq in range(MIX_G)]
        while live or slabs:
            live = [g for g in live if next(g, "done") != "done"]
            if slabs:
                lo, hi = slabs.pop(0)
                p_write[:, lo:hi] = _dot(xb, w_ref[:, lo:hi])

    @pl.when(lax.rem(i, 2) == 0)
    def _():
        step(proj_b, proj_a)

    @pl.when(lax.rem(i, 2) == 1)
    def _():
        step(proj_a, proj_b)


def _mixproj(x3, w_p, cos_t, sin_t, params):
    batch, seq, _ = x3.shape
    assert batch % MIX_G == 0 and seq % CHUNK == 0
    cps = seq // CHUNK
    n = (batch // MIX_G) * cps

    def cur_map(i):
        c = jnp.clip(i - 1, 0, n - 1)
        return (c // cps, c % cps, 0)

    def next_map(i):
        c = jnp.minimum(i, n - 1)
        return (c // cps, c % cps, 0)

    const = lambda i: (0, 0)
    specs = [pl.BlockSpec((MIX_G, CHUNK, D_MODEL), next_map),
             pl.BlockSpec((D_MODEL, NP), const),
             pl.BlockSpec((MIX_G, CHUNK, ROPE_W), cur_map),
             pl.BlockSpec((MIX_G, CHUNK, ROPE_W), cur_map)]
    specs += [pl.BlockSpec(p.shape, const) for p in params]
    return pl.pallas_call(
        functools.partial(_mixproj_kernel, chunks_per_seq=cps),
        grid=(n + 1,),
        in_specs=specs,
        out_specs=pl.BlockSpec((MIX_G, CHUNK, D_MIX), cur_map),
        out_shape=jax.ShapeDtypeStruct((batch, seq, D_MIX), BF16),
        scratch_shapes=[pltpu.VMEM((MIX_G * CHUNK, NP), F32),
                        pltpu.VMEM((MIX_G * CHUNK, NP), F32),
                        pltpu.VMEM((MIX_G, 256, 256), F32),
                        pltpu.VMEM((MIX_G, SSD_BC, SSD_W), F32),
                        pltpu.VMEM((MIX_G, GLA_W, GLA_QK), F32),
                        pltpu.VMEM((MIX_G, CHUNK + 8, SSD_XBC), F32),
                        pltpu.VMEM((RET_HEADS * CHUNK, 256), BF16),
                        pltpu.VMEM((SSD_HEADS * CHUNK, SSD_BC), BF16),
                        pltpu.VMEM((GLA_HEADS * CHUNK, GLA_QK), BF16),
                        pltpu.VMEM((LANES, SSD_W), BF16)],
        compiler_params=pltpu.CompilerParams(dimension_semantics=("arbitrary",),
                                             vmem_limit_bytes=VMEM_LIMIT),
        name="inproj_mixer",
    )(x3, w_p, cos_t, sin_t, *params)


def _layer_norm(y, g, b):
    mu = jnp.mean(y, axis=-1, keepdims=True)
    yc = y - mu
    var = jnp.mean(yc * yc, axis=-1, keepdims=True)
    return yc * lax.rsqrt(var + LN_EPS) * g + b


def _post_kernel(h_ref, x_ref, wout_ref, g_ref, b_ref, wrh_ref, wrl_ref, br_ref,
                 x1_ref, x1p_ref, mi_ref, mf_ref, cnt_ref, carry):
    sub = TM_POST // POST_SPLIT

    @pl.when(pl.program_id(0) == 0)
    def _():
        carry[...] = jnp.zeros_like(carry)

    lane_i = _iota((sub, LANES), 1)
    lane = lane_i.astype(F32)
    found = {}

    def sub_tile(part):
        rows = slice(part * sub, (part + 1) * sub)
        mix = _dot(h_ref[rows, :], wout_ref[...])
        yield
        x1 = _layer_norm(DEEPNORM_ALPHA * x_ref[rows, :] + mix, g_ref[...], b_ref[...])
        x1_ref[rows, :] = x1
        x1p_ref[rows, :] = _pack_rows(x1)
        x_hi = x1.astype(BF16)
        x_lo = (x1 - x_hi.astype(F32)).astype(BF16)
        logits = (_dot(x_hi, wrh_ref[...]) + _dot(x_lo, wrh_ref[...]) + _dot(x_hi, wrl_ref[...])
                  + br_ref[...])
        yield
        work = logits
        vals, idxs = [], []
        multi = jnp.zeros((sub, LANES), F32)
        for _ in range(TOP_K):
            m = jnp.max(work, axis=-1, keepdims=True)
            idx = jnp.min(jnp.where(work == m, lane, float(LANES)), axis=-1, keepdims=True)
            hit = lane == idx
            multi = multi + hit.astype(F32)
            work = jnp.where(hit, -jnp.inf, work)
            vals.append(m)
            idxs.append(idx)
            yield
        exps = [jnp.exp(v - vals[0]) for v in vals]
        denom = exps[0] + exps[1] + exps[2] + exps[3]
        gates = [e / denom for e in exps]
        before = (_iota((sub, sub), 0) > _iota((sub, sub), 1)).astype(BF16)
        found[part] = (idxs, gates, _dot(before, multi.astype(BF16)), multi)

    live = [sub_tile(part) for part in range(POST_SPLIT)]
    while live:
        live = [g for g in live if next(g, "done") != "done"]

    base = carry[...]
    for part in range(POST_SPLIT):
        idxs, gates, prior_local, multi = found[part]
        prior = prior_local + base
        mi = jnp.zeros((sub, LANES), F32)
        mf = jnp.zeros((sub, LANES), F32)
        for kk in range(TOP_K):
            rank = jnp.sum(jnp.where(lane == idxs[kk], prior, 0.0), axis=-1, keepdims=True)
            mi = jnp.where(lane_i == kk, idxs[kk], mi)
            mi = jnp.where(lane_i == TOP_K + kk, rank, mi)
            mf = jnp.where(lane_i == kk, gates[kk], mf)
        mi_ref[:, part * sub:(part + 1) * sub] = jnp.transpose(mi)[0:2 * TOP_K, :].astype(jnp.int32)
        mf_ref[part * sub:(part + 1) * sub, :] = mf
        base = base + jnp.sum(multi, axis=0, keepdims=True)
    carry[...] = base
    cnt_ref[...] = jnp.broadcast_to(base, cnt_ref.shape)


def _post(h, x2, w_out_b, ln_g, ln_b, wr_hi, wr_lo, br_p):
    t = x2.shape[0]
    tm = TM_POST
    row = lambda i: (i, 0)
    const = lambda i: (0, 0)
    return pl.pallas_call(
        _post_kernel,
        grid=(t // tm,),
        in_specs=[pl.BlockSpec((tm, D_MIX), row), pl.BlockSpec((tm, D_MODEL), row),
                  pl.BlockSpec((D_MIX, D_MODEL), const), pl.BlockSpec((1, D_MODEL), const),
                  pl.BlockSpec((1, D_MODEL), const), pl.BlockSpec((D_MODEL, LANES), const),
                  pl.BlockSpec((D_MODEL, LANES), const), pl.BlockSpec((1, LANES), const)],
        out_specs=[pl.BlockSpec((tm, D_MODEL), row), pl.BlockSpec((tm, ROW_WORDS), row),
                   pl.BlockSpec((2 * TOP_K, tm), lambda i: (0, i)), pl.BlockSpec((tm, LANES), row),
                   pl.BlockSpec((8, LANES), const)],
        out_shape=[jax.ShapeDtypeStruct((t, D_MODEL), F32), jax.ShapeDtypeStruct((t, ROW_WORDS), jnp.int32),
                   jax.ShapeDtypeStruct((2 * TOP_K, t), jnp.int32), jax.ShapeDtypeStruct((t, LANES), F32),
                   jax.ShapeDtypeStruct((8, LANES), F32)],
        scratch_shapes=[pltpu.VMEM((1, LANES), F32)],
        compiler_params=pltpu.CompilerParams(dimension_semantics=("arbitrary",),
                                             vmem_limit_bytes=VMEM_LIMIT),
        name="outproj_ln_router",
    )(h, x2, w_out_b, ln_g, ln_b, wr_hi, wr_lo, br_p)


def _ffn_kernel(be_ref, nv_ref, x_ref, wg_ref, bg_ref, wu_ref, bu_ref, wd_ref, bd_ref, o_ref,
                wg_b, wu_b, wd_b):
    i = pl.program_id(0)
    valid = i < nv_ref[0]
    e = be_ref[i]
    prev = be_ref[jnp.maximum(i - 1, 0)]
    fresh = jnp.logical_or(i == 0, e != prev)

    @pl.when(jnp.logical_and(valid, fresh))
    def _():
        wg_b[...] = wg_ref[0, 0].astype(BF16)
        wu_b[...] = wu_ref[0, 0].astype(BF16)
        wd_b[...] = wd_ref[0, 0].astype(BF16)

    @pl.when(valid)
    def _():
        x = _unpack_rows(x_ref[...]).astype(BF16)
        hg = jnp.minimum(_dot(x, wg_b[...]) + bg_ref[0, 0], SWIGLU_LIMIT)
        hu = jnp.clip(_dot(x, wu_b[...]) + bu_ref[0, 0], -SWIGLU_LIMIT, SWIGLU_LIMIT)
        hh = (hu + 1.0) * hg * jax.nn.sigmoid(SWIGLU_ALPHA * hg)
        o_ref[...] = _pack_rows(_dot(hh.astype(BF16), wd_b[...]) + bd_ref[0, 0])


def _ffn(layer, block_expert, n_valid, xin, wg, bg, wu, bu, wd, bd):
    cap = xin.shape[0]
    nb = cap // BM

    def blk(i, be, nv):
        return jnp.maximum(jnp.minimum(i, nv[0] - 1), 0)

    row = lambda i, be, nv: (blk(i, be, nv), 0)
    wmap = lambda i, be, nv: (layer, be[blk(i, be, nv)], 0, 0)
    grid_spec = pltpu.PrefetchScalarGridSpec(
        num_scalar_prefetch=2,
        grid=(nb,),
        in_specs=[pl.BlockSpec((BM, ROW_WORDS), row),
                  pl.BlockSpec((1, 1, D_MODEL, D_FF), wmap), pl.BlockSpec((1, 1, 1, D_FF), wmap),
                  pl.BlockSpec((1, 1, D_MODEL, D_FF), wmap), pl.BlockSpec((1, 1, 1, D_FF), wmap),
                  pl.BlockSpec((1, 1, D_FF, D_MODEL), wmap), pl.BlockSpec((1, 1, 1, D_MODEL), wmap)],
        out_specs=pl.BlockSpec((BM, ROW_WORDS), row),
        scratch_shapes=[pltpu.VMEM((D_MODEL, D_FF), BF16), pltpu.VMEM((D_MODEL, D_FF), BF16),
                        pltpu.VMEM((D_FF, D_MODEL), BF16)],
    )
    return pl.pallas_call(
        _ffn_kernel,
        grid_spec=grid_spec,
        out_shape=jax.ShapeDtypeStruct((cap, ROW_WORDS), jnp.int32),
        compiler_params=pltpu.CompilerParams(dimension_semantics=("arbitrary",),
                                             vmem_limit_bytes=VMEM_LIMIT),
        name="expert_ffn",
    )(block_expert, n_valid, xin, wg, bg, wu, bu, wd, bd)


def _sc_gather(table, idx3):
    nw, n_chunks, ch = idx3.shape
    width = table.shape[1]
    per_worker = n_chunks * ch
    mesh = plsc.VectorSubcoreMesh(core_axis_name="c", subcore_axis_name="s")
    n_cores = mesh.num_cores
    assert nw == n_cores * mesh.num_subcores and n_chunks % 2 == 0 and ch == SC_CHUNK

    def body(table_hbm, idx_hbm, out_hbm, idx_v, rows0, rows1, sem_g0, sem_g1, sem_w0, sem_w1):
        wid = lax.axis_index("s") * n_cores + lax.axis_index("c")
        base = wid * per_worker
        pltpu.sync_copy(idx_hbm.at[wid], idx_v)

        @pl.loop(0, n_chunks, step=2)
        def _(c):
            g0 = pltpu.async_copy(table_hbm.at[idx_v.at[c]], rows0, sem_g0)
            g1 = pltpu.async_copy(table_hbm.at[idx_v.at[c + 1]], rows1, sem_g1)
            g0.wait()
            w0 = pltpu.async_copy(rows0, out_hbm.at[pl.ds(base + c * ch, ch)], sem_w0)
            g1.wait()
            w1 = pltpu.async_copy(rows1, out_hbm.at[pl.ds(base + (c + 1) * ch, ch)], sem_w1)
            w0.wait()
            w1.wait()

    return pl.kernel(
        body,
        out_type=jax.ShapeDtypeStruct((nw * per_worker, width), table.dtype),
        mesh=mesh,
        scratch_types=[pltpu.VMEM((n_chunks, ch), jnp.int32),
                       pltpu.VMEM((ch, width), table.dtype), pltpu.VMEM((ch, width), table.dtype),
                       pltpu.SemaphoreType.DMA, pltpu.SemaphoreType.DMA,
                       pltpu.SemaphoreType.DMA, pltpu.SemaphoreType.DMA],
        name="sc_row_gather",
    )(table, idx3)


def _sc_scatter(rows, idx3, n_out):
    nw, n_lists, ch = idx3.shape
    n_chunks = n_lists // TOP_K
    width = rows.shape[1]
    per_worker = n_chunks * ch
    mesh = plsc.VectorSubcoreMesh(core_axis_name="c", subcore_axis_name="s")
    n_cores = mesh.num_cores
    assert nw == n_cores * mesh.num_subcores and n_chunks % 2 == 0 and ch == SC_CHUNK
    assert nw * per_worker == rows.shape[0]

    def body(rows_hbm, idx_hbm, out_hbm, idx_v, buf0, buf1, sem_r0, sem_r1, sem_w0, sem_w1):
        wid = lax.axis_index("s") * n_cores + lax.axis_index("c")
        base = wid * per_worker
        pltpu.sync_copy(idx_hbm.at[wid], idx_v)

        @pl.loop(0, n_chunks, step=2)
        def _(c):
            r0 = pltpu.async_copy(rows_hbm.at[pl.ds(base + c * ch, ch)], buf0, sem_r0)
            r1 = pltpu.async_copy(rows_hbm.at[pl.ds(base + (c + 1) * ch, ch)], buf1, sem_r1)
            r0.wait()
            w0 = [pltpu.async_copy(buf0, out_hbm.at[idx_v.at[c * TOP_K + k]], sem_w0) for k in range(TOP_K)]
            r1.wait()
            w1 = [pltpu.async_copy(buf1, out_hbm.at[idx_v.at[(c + 1) * TOP_K + k]], sem_w1) for k in range(TOP_K)]
            for w in w0 + w1:
                w.wait()

    return pl.kernel(
        body,
        out_type=jax.ShapeDtypeStruct((n_out, width), rows.dtype),
        mesh=mesh,
        scratch_types=[pltpu.VMEM((n_lists, ch), jnp.int32),
                       pltpu.VMEM((ch, width), rows.dtype), pltpu.VMEM((ch, width), rows.dtype),
                       pltpu.SemaphoreType.DMA, pltpu.SemaphoreType.DMA,
                       pltpu.SemaphoreType.DMA, pltpu.SemaphoreType.DMA],
        name="sc_row_scatter",
    )(rows, idx3)


def _combine_kernel(x_ref, rows_ref, mf_ref, g_ref, b_ref, o_ref):
    mf = mf_ref[...]
    y = DEEPNORM_ALPHA * x_ref[...]
    for kk in range(TOP_K):
        y = y + mf[:, kk:kk + 1] * _unpack_rows(rows_ref[kk])
    o_ref[...] = _layer_norm(y, g_ref[...], b_ref[...])


def _combine(x1, rows, mf, ln_g, ln_b):
    t = x1.shape[0]
    tm = TM_COMB
    row = lambda i: (i, 0)
    const = lambda i: (0, 0)
    return pl.pallas_call(
        _combine_kernel,
        grid=(t // tm,),
        in_specs=[pl.BlockSpec((tm, D_MODEL), row),
                  pl.BlockSpec((TOP_K, tm, ROW_WORDS), lambda i: (0, i, 0)),
                  pl.BlockSpec((tm, LANES), row),
                  pl.BlockSpec((1, D_MODEL), const), pl.BlockSpec((1, D_MODEL), const)],
        out_specs=pl.BlockSpec((tm, D_MODEL), row),
        out_shape=jax.ShapeDtypeStruct((t, D_MODEL), F32),
        compiler_params=pltpu.CompilerParams(dimension_semantics=("arbitrary",),
                                             vmem_limit_bytes=VMEM_LIMIT),
        name="combine_ln",
    )(x1, rows, mf, ln_g, ln_b)


def _relayout_w_in(w):
    dt_end = 4 * 256 + SSD_W + SSD_XBC + SSD_HEADS
    gk_end = dt_end + 2 * GLA_QK + GLA_W + GLA_RANK
    zeros = lambda n: jnp.zeros(w.shape[:2] + (n,), w.dtype)
    parts = [w[..., :dt_end], zeros(LANES - SSD_HEADS), w[..., dt_end:gk_end], zeros(LANES - GLA_RANK),
             w[..., gk_end:]]
    return jnp.concatenate(parts, axis=-1).astype(BF16)


def _rep_heads(p):
    return jnp.repeat(p, SSD_HEAD_DIM, axis=-1)[:, None, :]


def _pad_heads(p):
    return jnp.pad(p, ((0, 0), (0, LANES - SSD_HEADS)))[:, None, :]


def kernel(x, positions, w_in, w_out, ret_norm_w, ssd_conv_w, ssd_conv_b, ssd_dt_bias, ssd_a_log, ssd_d,
           ssd_norm_w, gla_w_gk2, gla_b_gk2, gla_norm_w, ln1_g, ln1_b, w_router, b_router, w_gate, b_gate,
           w_up, b_up, w_down, b_down, ln2_g, ln2_b):
    batch, seq, d = x.shape
    t = batch * seq
    depth = w_in.shape[0]
    assert d == D_MODEL and t % TM_POST == 0 and t % TM_COMB == 0
    n_assign = t * TOP_K
    nb = n_assign // BM + N_EXPERTS
    cap = nb * BM

    cos_t, sin_t = _rope_tables(positions.reshape(t, 1))
    cos_t = cos_t.reshape(batch, seq, -1)
    sin_t = sin_t.reshape(batch, seq, -1)
    x2 = x.reshape(t, d)

    row = lambda p: p[:, None, :]
    w_in_p = _relayout_w_in(w_in)
    mix_params = (row(ret_norm_w), ssd_conv_w, row(ssd_conv_b), _pad_heads(ssd_dt_bias), _pad_heads(ssd_a_log),
                  _rep_heads(ssd_d), row(ssd_norm_w), jnp.pad(gla_w_gk2, ((0, 0), (0, LANES - GLA_RANK), (0, 0))),
                  row(gla_b_gk2), row(gla_norm_w))
    w_out_b = w_out.astype(BF16)
    wr_p = jnp.pad(w_router, ((0, 0), (0, 0), (0, LANES - N_EXPERTS)))
    wr_hi = wr_p.astype(BF16)
    wr_lo = (wr_p - wr_hi.astype(F32)).astype(BF16)
    br_p = row(jnp.pad(b_router, ((0, 0), (0, LANES - N_EXPERTS)), constant_values=NEG_BIG))
    ln1 = (row(ln1_g), row(ln1_b))
    ln2 = (row(ln2_g), row(ln2_b))
    bias4 = lambda b: b[:, :, None, :]
    expert_ids = jnp.arange(N_EXPERTS, dtype=jnp.int32)
    block_start = jnp.arange(nb, dtype=jnp.int32) * BM

    for l in range(depth):
        h = _mixproj(x2.reshape(batch, seq, d), w_in_p[l], cos_t, sin_t,
                     tuple(p[l] for p in mix_params)).reshape(t, D_MIX)
        x1, x1p, mi, mf, cnt = _post(h, x2, w_out_b[l], ln1[0][l], ln1[1][l], wr_hi[l], wr_lo[l], br_p[l])

        counts = cnt[0, :N_EXPERTS].astype(jnp.int32)
        padded = (counts + BM - 1) // BM * BM
        end_padded = jnp.cumsum(padded)
        start_padded = end_padded - padded
        top_idx = mi[:TOP_K]
        start_of = jnp.sum(jnp.where(top_idx[None] == expert_ids[:, None, None],
                                     start_padded[:, None, None], 0), axis=0)
        dest = start_of + mi[TOP_K:]
        block_expert = jnp.minimum(jnp.sum((end_padded[None, :] <= block_start[:, None]).astype(jnp.int32), axis=1),
                                   N_EXPERTS - 1)
        n_valid = (end_padded[-1:] // BM).astype(jnp.int32)

        scatter_idx = dest.reshape(TOP_K, SC_WORKERS, -1, SC_CHUNK).transpose(1, 2, 0, 3)
        xin = _sc_scatter(x1p, scatter_idx.reshape(SC_WORKERS, -1, SC_CHUNK), cap)
        yb = _ffn(l, block_expert, n_valid, xin, w_gate, bias4(b_gate), w_up, bias4(b_up), w_down, bias4(b_down))
        rows = _sc_gather(yb, dest.reshape(SC_WORKERS, -1, SC_CHUNK))
        x2 = _combine(x1, rows.reshape(TOP_K, t, ROW_WORDS), mf, ln2[0][l], ln2[1][l])
    return x2.reshape(batch, seq, d)
```

```python
import functools
import math

import jax
import jax.numpy as jnp
from jax import lax
from jax.experimental import pallas as pl
from jax.experimental.pallas import tpu as pltpu
from jax.experimental.pallas import tpu_sc as plsc

F32 = jnp.float32
BF16 = jnp.bfloat16

D_MODEL = 1024
CHUNK = 64
RET_HEADS, RET_DK, RET_DV = 4, 64, 64
RET_W = RET_HEADS * RET_DV
SSD_HEADS, SSD_HEAD_DIM, SSD_STATE, SSD_GROUPS, SSD_CONV = 8, 64, 64, 2, 4
SSD_W = SSD_HEADS * SSD_HEAD_DIM
SSD_BC = SSD_GROUPS * SSD_STATE
SSD_XBC = SSD_W + 2 * SSD_BC
GLA_HEADS, GLA_DK, GLA_DV, GLA_RANK, GLA_TEMP = 4, 32, 64, 16, 16.0
GLA_QK = GLA_HEADS * GLA_DK
GLA_W = GLA_HEADS * GLA_DV
D_MIX = RET_W + SSD_W + GLA_W
N_EXPERTS, TOP_K, D_FF = 32, 4, 1024
SWIGLU_LIMIT, SWIGLU_ALPHA = 7.0, 1.702
ROPE_BASE = 10000.0
LN_EPS, NORM_EPS = 1e-5, 1e-6
DEPTH = 2
DEEPNORM_ALPHA = (2.0 * DEPTH) ** 0.25

LANES = 128
ROPE_W = LANES
NEG_BIG = -1e30
VMEM_LIMIT = 56 * 1024 * 1024

_SEGS = (("rq", 256), ("rk", 256), ("rv", 256), ("rg", 256), ("sz", SSD_W), ("sxbc", SSD_XBC),
         ("sdt", 128), ("gq", 128), ("gk", 128), ("gv", 256), ("ggk", 128), ("gg", 256))
COL = {}
_off = 0
for _n, _w in _SEGS:
    COL[_n] = _off
    _off += _w
NP = _off

MIX_G = 8
PROJ_SLAB = 512
TM_POST = 1024
POST_SPLIT = 4
TM_COMB = 512
BM = 512
ROW_WORDS = D_MODEL // 2
SC_WORKERS = 32
SC_CHUNK = 64
SC_GATHER_CHUNK = 32
SC_GATHER_DEPTH = 4


def _dot(a, b, dims=(((1,), (0,)), ((), ())), precision=None):
    return lax.dot_general(a, b, dims, precision=precision, preferred_element_type=F32)


_NT = (((1,), (1,)), ((), ()))
_TN = (((0,), (0,)), ((), ()))


def _iota(shape, dim):
    return lax.broadcasted_iota(jnp.int32, shape, dim)


def _vdiv(x, n):
    assert n & (n - 1) == 0
    return lax.shift_right_logical(x, n.bit_length() - 1)


def _vmod(x, n):
    assert n & (n - 1) == 0
    return jnp.bitwise_and(x, n - 1)


def _silu(x):
    return x * jax.nn.sigmoid(x)


def _softplus(x):
    return jnp.maximum(x, 0.0) + jnp.log(1.0 + jnp.exp(-jnp.abs(x)))


def _pack_rows(x):
    w = x.shape[1] // 2
    lo = lax.bitcast_convert_type(x[:, :w].astype(BF16).astype(F32), jnp.uint32)
    hi = lax.bitcast_convert_type(x[:, w:].astype(BF16).astype(F32), jnp.uint32)
    return lax.bitcast_convert_type(lax.shift_right_logical(lo, jnp.uint32(16)) | hi, jnp.int32)


def _unpack_rows(words):
    u = lax.bitcast_convert_type(words, jnp.uint32)
    a = lax.bitcast_convert_type(lax.shift_left(u, jnp.uint32(16)), F32)
    b = lax.bitcast_convert_type(u & jnp.uint32(0xFFFF0000), F32)
    return jnp.concatenate([a, b], axis=-1)


def _seg_sum64(x):
    first = _iota((1, LANES), 1) < 64
    outs = []
    for j in range(x.shape[-1] // LANES):
        blk = x[:, j * LANES:(j + 1) * LANES]
        lo = jnp.sum(jnp.where(first, blk, 0.0), axis=-1, keepdims=True)
        hi = jnp.sum(jnp.where(first, 0.0, blk), axis=-1, keepdims=True)
        outs.append(jnp.where(first, lo, hi))
    return jnp.concatenate(outs, axis=-1)


def _block_mask(shape, row_blk, col_blk):
    keep = _vdiv(_iota(shape, 0), row_blk) == _vdiv(_iota(shape, 1), col_blk)
    return jnp.where(keep, 1.0, 0.0).astype(BF16)


def _block_diag(x, mask):
    reps = mask.shape[0] // x.shape[0]
    return jnp.concatenate([x.astype(BF16)] * reps, axis=0) * mask


def _expand_heads(x, expand):
    hi = x.astype(BF16)
    r1 = x - hi.astype(F32)
    mid = r1.astype(BF16)
    lo = (r1 - mid.astype(F32)).astype(BF16)
    return _dot(hi, expand) + _dot(mid, expand) + _dot(lo, expand)


def _cumsum_rows(tri, x):
    hi = x.astype(BF16)
    lo = (x - hi.astype(F32)).astype(BF16)
    return _dot(tri, hi) + _dot(tri, lo)


def _rope_kernel(pos_ref, cos_ref, sin_ref):
    lane = _iota((1, ROPE_W), 1)
    half = RET_DK // 2
    k = _vmod(lane, half).astype(F32)
    inv_freq = jnp.exp(k * (-math.log(ROPE_BASE) / half))
    ang = pos_ref[...].astype(F32) * inv_freq
    first = _vmod(lane, RET_DK) < half
    cos_ref[...] = jnp.cos(ang)
    sin_ref[...] = jnp.where(first, -1.0, 1.0) * jnp.sin(ang)


def _rope_tables(pos_col):
    t = pos_col.shape[0]
    tm = 512
    w = ROPE_W
    return pl.pallas_call(
        _rope_kernel,
        grid=(t // tm,),
        in_specs=[pl.BlockSpec((tm, 1), lambda i: (i, 0))],
        out_specs=[pl.BlockSpec((tm, w), lambda i: (i, 0))] * 2,
        out_shape=[jax.ShapeDtypeStruct((t, w), F32)] * 2,
        compiler_params=pltpu.CompilerParams(dimension_semantics=("arbitrary",)),
        name="rope_tables",
    )(pos_col)


def _mixproj_kernel(xn_ref, w_ref, cos_ref, sin_ref, retw_ref, convw_ref, convb_ref, dtb_ref, alog_ref,
                    dskip_ref, ssdw_ref, wgk_ref, bgk_ref, glaw_ref, h_ref,
                    proj_a, proj_b, ret_s, ssd_s, gla_s, stage, m_heads, m_groups, m_gla, m_expand, *,
                    chunks_per_seq):
    C = CHUNK
    i = pl.program_id(0)
    cur = jnp.maximum(i - 1, 0)

    @pl.when(i == 0)
    def _():
        m_heads[...] = _block_mask(m_heads.shape, C, 64)
        m_groups[...] = _block_mask(m_groups.shape, C * SSD_HEADS // SSD_GROUPS, SSD_STATE)
        m_gla[...] = _block_mask(m_gla.shape, C, GLA_DK)
        m_expand[...] = _block_mask(m_expand.shape, 1, SSD_HEAD_DIM)
        proj_b[...] = jnp.zeros_like(proj_b)

    @pl.when(lax.rem(cur, chunks_per_seq) == 0)
    def _():
        ret_s[...] = jnp.zeros_like(ret_s)
        ssd_s[...] = jnp.zeros_like(ssd_s)
        gla_s[...] = jnp.zeros_like(gla_s)
        for g in range(MIX_G):
            stage[g, 0:8, :] = jnp.zeros((8, SSD_XBC), F32)

    def conv_act(sq, pref):
        stage[sq, 8:8 + C, :] = pref[sq * C:(sq + 1) * C, COL["sxbc"]:COL["sxbc"] + SSD_XBC]
        acc = convb_ref[...] + convw_ref[0:1, :] * stage[sq, 5:5 + C, :]
        for j in range(1, SSD_CONV):
            acc = acc + convw_ref[j:j + 1, :] * stage[sq, 5 + j:5 + j + C, :]
        stage[sq, 0:8, :] = stage[sq, C:C + 8, :]
        return _silu(acc)

    lane256 = _iota((1, 256), 1)
    head = _vdiv(lane256, 64).astype(F32)
    log_gamma = jnp.log(1.0 - jnp.exp((-5.0 - head) * math.log(2.0)))
    row = _iota((C, 1), 0).astype(F32)
    dist = row - _vmod(lane256, 64).astype(F32)
    ret_intra = jnp.where(dist >= 0, jnp.exp(log_gamma * jnp.maximum(dist, 0.0)), 0.0)
    ret_qdec = jnp.exp(log_gamma * (row + 1.0))
    ret_kdec = jnp.exp(log_gamma * (C - 1.0 - row))
    ret_cdec = jnp.exp(log_gamma * C)
    first_half = _vmod(lane256, RET_DK) < (RET_DK // 2)

    tri = jnp.where(_iota((C, C), 0) >= _iota((C, C), 1), 1.0, 0.0).astype(BF16)
    causal4 = _iota((C, 256), 0) >= _vmod(_iota((C, 256), 1), 64)
    causal8 = _iota((C, 512), 0) >= _vmod(_iota((C, 512), 1), 64)
    eye8 = _iota((C, 512), 0) == _vmod(_iota((C, 512), 1), 64)

    a_neg = -jnp.exp(alog_ref[...])

    def rot(t, cos, sin):
        sw = jnp.where(first_half, pltpu.roll(t, 256 - 32, 1), pltpu.roll(t, 32, 1))
        return t * cos + sw * sin

    def chunk_of(sq, pref):
        def seg(name, width):
            return pref[sq * C:(sq + 1) * C, COL[name]:COL[name] + width]

        xact = conv_act(sq, pref)
        yield

        reps = RET_HEADS * RET_DK // ROPE_W
        cos = jnp.concatenate([cos_ref[sq]] * reps, axis=-1)
        sin = jnp.concatenate([sin_ref[sq]] * reps, axis=-1)
        q = rot(seg("rq", 256), cos, sin)
        k = rot(seg("rk", 256), cos, sin) * (RET_DK ** -0.5)
        v = seg("rv", 256)
        vb = v.astype(BF16)
        kbd = _block_diag(k, m_heads[...])
        scores = _dot(q.astype(BF16), kbd, _NT) * ret_intra
        vbd = _block_diag(v, m_heads[...])
        yield
        s_prev = ret_s[sq]
        o = _dot(scores.astype(BF16), vbd) + _dot((q * ret_qdec).astype(BF16), s_prev.astype(BF16))
        contrib = _dot((k * ret_kdec).astype(BF16), vb, _TN)
        yield
        keep = _vdiv(_iota((256, 256), 0), RET_DK) == _vdiv(_iota((256, 256), 1), RET_DV)
        ret_s[sq] = jnp.where(keep, ret_cdec * s_prev + contrib, 0.0)
        mu = _seg_sum64(o) * (1.0 / RET_DV)
        oc = o - mu
        var = _seg_sum64(oc * oc) * (1.0 / RET_DV)
        o = oc * lax.rsqrt(var + LN_EPS) * retw_ref[...]
        h_ref[sq, :, 0:RET_W] = (_silu(seg("rg", 256)) * o).astype(BF16)
        yield

        xs = xact[:, 0:SSD_W]
        bm = xact[:, SSD_W:SSD_W + SSD_BC]
        cm = xact[:, SSD_W + SSD_BC:SSD_XBC]
        cmb = cm.astype(BF16)
        dt_c = _softplus(seg("sdt", LANES) + dtb_ref[...])
        acum_c = _cumsum_rows(tri, dt_c * a_neg)
        both = _expand_heads(jnp.concatenate([dt_c, acum_c], axis=0), m_expand[...])
        dt = both[0:C, :]
        acum = both[C:2 * C, :]
        yield
        arow = jnp.sum(jnp.where(eye8, acum, 0.0), axis=0, keepdims=True)
        decay = jnp.exp(jnp.where(causal8, acum - arow, NEG_BIG))
        b8 = _block_diag(bm, m_groups[...])
        cb = _dot(cmb, b8, _NT)
        yield
        m = (cb * decay).astype(BF16)
        xdt = xs * dt
        s2 = ssd_s[sq]
        half = SSD_W // SSD_GROUPS
        ys = []
        for g in range(SSD_GROUPS):
            xbd = _block_diag(xdt[:, g * half:(g + 1) * half], m_heads[...])
            ys.append(_dot(m[:, g * half:(g + 1) * half], xbd))
        y = jnp.concatenate(ys, axis=-1)
        y = y + _dot(cmb, s2.astype(BF16)) * jnp.exp(acum)
        y = y + dskip_ref[...] * xs
        a_last = acum[C - 1:C, :]
        sd = jnp.exp(a_last - acum)
        contrib_s = _dot(bm.astype(BF16), (xdt * sd).astype(BF16), _TN)
        yield
        keep_s = _vdiv(_iota(s2.shape, 0), SSD_STATE) == _vdiv(_iota(s2.shape, 1), half)
        ssd_s[sq] = jnp.where(keep_s, s2 * jnp.exp(a_last) + contrib_s, 0.0)
        yz = y * _silu(seg("sz", SSD_W))
        outs = []
        for g in range(SSD_GROUPS):
            blk = yz[:, g * half:(g + 1) * half]
            ms = jnp.mean(blk * blk, axis=-1, keepdims=True)
            outs.append(blk * lax.rsqrt(ms + NORM_EPS))
        h_ref[sq, :, RET_W:RET_W + SSD_W] = (jnp.concatenate(outs, axis=-1) * ssdw_ref[...]).astype(BF16)

        yield

        gq = seg("gq", GLA_QK) * (GLA_DK ** -0.5)
        gkk = seg("gk", GLA_QK)
        gv = seg("gv", GLA_W)
        gkl = _dot(seg("ggk", 128).astype(BF16), wgk_ref[...].astype(BF16)) + bgk_ref[...]
        yield
        log_a = -_softplus(-gkl) * (1.0 / GLA_TEMP)
        b = _cumsum_rows(tri, log_a)
        yield
        q_t = (gq * jnp.exp(b)).astype(BF16)
        k_t = gkk * jnp.exp(-b)
        kbd_g = _block_diag(k_t, m_gla[...])
        att = jnp.where(causal4, _dot(q_t, kbd_g, _NT), 0.0)
        yield
        vbd_g = _block_diag(gv, m_heads[...])
        st = gla_s[sq]
        og = _dot(att.astype(BF16), vbd_g) + _dot(q_t, st.astype(BF16), _NT)
        b_last = b[C - 1:C, :]
        kd = (gkk * jnp.exp(b_last - b)).astype(BF16)
        contrib_g = _dot(gv.astype(BF16), kd, _TN)
        yield
        keep_g = _vdiv(_iota(st.shape, 0), GLA_DV) == _vdiv(_iota(st.shape, 1), GLA_DK)
        gla_s[sq] = jnp.where(keep_g, st * jnp.exp(b_last) + contrib_g, 0.0)
        ms = _seg_sum64(og * og) * (1.0 / GLA_DV)
        og = og * lax.rsqrt(ms + NORM_EPS) * glaw_ref[...]
        h_ref[sq, :, RET_W + SSD_W:D_MIX] = (_silu(seg("gg", GLA_W)) * og).astype(BF16)

    def step(p_read, p_write):
        xb = xn_ref[...].reshape(MIX_G * C, D_MODEL).astype(BF16)
        edges = list(range(0, NP, PROJ_SLAB)) + [NP]
        slabs = list(zip(edges[:-1], edges[1:]))

        live = [chunk_of(sq, p_read) for sq in range(MIX_G)]
        while live or slabs:
            live = [g for g in live if next(g, "done") != "done"]
            if slabs:
                lo, hi = slabs.pop(0)
                p_write[:, lo:hi] = _dot(xb, w_ref[:, lo:hi])

    @pl.when(lax.rem(i, 2) == 0)
    def _():
        step(proj_b, proj_a)

    @pl.when(lax.rem(i, 2) == 1)
    def _():
        step(proj_a, proj_b)


def _mixproj(x3, w_p, cos_t, sin_t, params):
    batch, seq, _ = x3.shape
    assert batch % MIX_G == 0 and seq % CHUNK == 0
    cps = seq // CHUNK
    n = (batch // MIX_G) * cps

    def cur_map(i):
        c = jnp.clip(i - 1, 0, n - 1)
        return (c // cps, c % cps, 0)

    def next_map(i):
        c = jnp.minimum(i, n - 1)
        return (c // cps, c % cps, 0)

    const = lambda i: (0, 0)
    specs = [pl.BlockSpec((MIX_G, CHUNK, D_MODEL), next_map),
             pl.BlockSpec((D_MODEL, NP), const),
             pl.BlockSpec((MIX_G, CHUNK, ROPE_W), cur_map),
             pl.BlockSpec((MIX_G, CHUNK, ROPE_W), cur_map)]
    specs += [pl.BlockSpec(p.shape, const) for p in params]
    return pl.pallas_call(
        functools.partial(_mixproj_kernel, chunks_per_seq=cps),
        grid=(n + 1,),
        in_specs=specs,
        out_specs=pl.BlockSpec((MIX_G, CHUNK, D_MIX), cur_map),
        out_shape=jax.ShapeDtypeStruct((batch, seq, D_MIX), BF16),
        scratch_shapes=[pltpu.VMEM((MIX_G * CHUNK, NP), F32),
                        pltpu.VMEM((MIX_G * CHUNK, NP), F32),
                        pltpu.VMEM((MIX_G, 256, 256), F32),
                        pltpu.VMEM((MIX_G, SSD_BC, SSD_W), F32),
                        pltpu.VMEM((MIX_G, GLA_W, GLA_QK), F32),
                        pltpu.VMEM((MIX_G, CHUNK + 8, SSD_XBC), F32),
                        pltpu.VMEM((RET_HEADS * CHUNK, 256), BF16),
                        pltpu.VMEM((SSD_HEADS * CHUNK, SSD_BC), BF16),
                        pltpu.VMEM((GLA_HEADS * CHUNK, GLA_QK), BF16),
                        pltpu.VMEM((LANES, SSD_W), BF16)],
        compiler_params=pltpu.CompilerParams(dimension_semantics=("arbitrary",),
                                             vmem_limit_bytes=VMEM_LIMIT),
        name="inproj_mixer",
    )(x3, w_p, cos_t, sin_t, *params)


def _layer_norm(y, g, b):
    mu = jnp.mean(y, axis=-1, keepdims=True)
    yc = y - mu
    var = jnp.mean(yc * yc, axis=-1, keepdims=True)
    return yc * lax.rsqrt(var + LN_EPS) * g + b


def _post_kernel(h_ref, x_ref, wout_ref, g_ref, b_ref, wrh_ref, wrl_ref, br_ref,
                 x1_ref, x1p_ref, mi_ref, mf_ref, cnt_ref, carry):
    sub = TM_POST // POST_SPLIT

    @pl.when(pl.program_id(0) == 0)
    def _():
        carry[...] = jnp.zeros_like(carry)

    lane_i = _iota((sub, LANES), 1)
    lane = lane_i.astype(F32)
    found = {}

    def sub_tile(part):
        rows = slice(part * sub, (part + 1) * sub)
        mix = _dot(h_ref[rows, :], wout_ref[...])
        yield
        x1 = _layer_norm(DEEPNORM_ALPHA * x_ref[rows, :] + mix, g_ref[...], b_ref[...])
        x1_ref[rows, :] = x1
        x1p_ref[rows, :] = _pack_rows(x1)
        x_hi = x1.astype(BF16)
        x_lo = (x1 - x_hi.astype(F32)).astype(BF16)
        logits = (_dot(x_hi, wrh_ref[...]) + _dot(x_lo, wrh_ref[...]) + _dot(x_hi, wrl_ref[...])
                  + br_ref[...])
        yield
        work = logits
        vals, idxs = [], []
        multi = jnp.zeros((sub, LANES), F32)
        for _ in range(TOP_K):
            m = jnp.max(work, axis=-1, keepdims=True)
            idx = jnp.min(jnp.where(work == m, lane, float(LANES)), axis=-1, keepdims=True)
            hit = lane == idx
            multi = multi + hit.astype(F32)
            work = jnp.where(hit, -jnp.inf, work)
            vals.append(m)
            idxs.append(idx)
            yield
        exps = [jnp.exp(v - vals[0]) for v in vals]
        denom = exps[0] + exps[1] + exps[2] + exps[3]
        gates = [e / denom for e in exps]
        before = (_iota((sub, sub), 0) > _iota((sub, sub), 1)).astype(BF16)
        found[part] = (idxs, gates, _dot(before, multi.astype(BF16)), multi)

    live = [sub_tile(part) for part in range(POST_SPLIT)]
    while live:
        live = [g for g in live if next(g, "done") != "done"]

    base = carry[...]
    for part in range(POST_SPLIT):
        idxs, gates, prior_local, multi = found[part]
        prior = prior_local + base
        mi = jnp.zeros((sub, LANES), F32)
        mf = jnp.zeros((sub, LANES), F32)
        for kk in range(TOP_K):
            rank = jnp.sum(jnp.where(lane == idxs[kk], prior, 0.0), axis=-1, keepdims=True)
            mi = jnp.where(lane_i == kk, idxs[kk], mi)
            mi = jnp.where(lane_i == TOP_K + kk, rank, mi)
            mf = jnp.where(lane_i == kk, gates[kk], mf)
        mi_ref[:, part * sub:(part + 1) * sub] = jnp.transpose(mi)[0:2 * TOP_K, :].astype(jnp.int32)
        mf_ref[part * sub:(part + 1) * sub, :] = mf
        base = base + jnp.sum(multi, axis=0, keepdims=True)
    carry[...] = base
    cnt_ref[...] = jnp.broadcast_to(base, cnt_ref.shape)


def _post(h, x2, w_out_b, ln_g, ln_b, wr_hi, wr_lo, br_p):
    t = x2.shape[0]
    tm = TM_POST
    row = lambda i: (i, 0)
    const = lambda i: (0, 0)
    return pl.pallas_call(
        _post_kernel,
        grid=(t // tm,),
        in_specs=[pl.BlockSpec((tm, D_MIX), row), pl.BlockSpec((tm, D_MODEL), row),
                  pl.BlockSpec((D_MIX, D_MODEL), const), pl.BlockSpec((1, D_MODEL), const),
                  pl.BlockSpec((1, D_MODEL), const), pl.BlockSpec((D_MODEL, LANES), const),
                  pl.BlockSpec((D_MODEL, LANES), const), pl.BlockSpec((1, LANES), const)],
        out_specs=[pl.BlockSpec((tm, D_MODEL), row), pl.BlockSpec((tm, ROW_WORDS), row),
                   pl.BlockSpec((2 * TOP_K, tm), lambda i: (0, i)), pl.BlockSpec((tm, LANES), row),
                   pl.BlockSpec((8, LANES), const)],
        out_shape=[jax.ShapeDtypeStruct((t, D_MODEL), F32), jax.ShapeDtypeStruct((t, ROW_WORDS), jnp.int32),
                   jax.ShapeDtypeStruct((2 * TOP_K, t), jnp.int32), jax.ShapeDtypeStruct((t, LANES), F32),
                   jax.ShapeDtypeStruct((8, LANES), F32)],
        scratch_shapes=[pltpu.VMEM((1, LANES), F32)],
        compiler_params=pltpu.CompilerParams(dimension_semantics=("arbitrary",),
                                             vmem_limit_bytes=VMEM_LIMIT),
        name="outproj_ln_router",
    )(h, x2, w_out_b, ln_g, ln_b, wr_hi, wr_lo, br_p)


def _ffn_kernel(be_ref, nv_ref, x_ref, wg_ref, bg_ref, wu_ref, bu_ref, wd_ref, bd_ref, o_ref,
                wg_b, wu_b, wd_b, y_done):
    i = pl.program_id(0)
    n = nv_ref[0]
    valid = i < n
    e = be_ref[jnp.minimum(i, be_ref.shape[0] - 1)]
    prev = be_ref[jnp.maximum(i - 1, 0)]
    fresh = jnp.logical_or(i == 0, e != prev)

    @pl.when(i == 0)
    def _():
        y_done[...] = jnp.zeros_like(y_done)

    @pl.when(jnp.logical_and(valid, fresh))
    def _():
        wg_b[...] = wg_ref[0, 0].astype(BF16)
        wu_b[...] = wu_ref[0, 0].astype(BF16)
        wd_b[...] = wd_ref[0, 0].astype(BF16)

    @pl.when(valid)
    def _():
        o_ref[...] = _pack_rows(y_done[...])
        x = _unpack_rows(x_ref[...]).astype(BF16)
        hg = jnp.minimum(_dot(x, wg_b[...]) + bg_ref[0, 0], SWIGLU_LIMIT)
        hu = jnp.clip(_dot(x, wu_b[...]) + bu_ref[0, 0], -SWIGLU_LIMIT, SWIGLU_LIMIT)
        hh = (hu + 1.0) * hg * jax.nn.sigmoid(SWIGLU_ALPHA * hg)
        y_done[...] = _dot(hh.astype(BF16), wd_b[...]) + bd_ref[0, 0]

    @pl.when(i == n)
    def _():
        o_ref[...] = _pack_rows(y_done[...])


def _ffn(layer, block_expert, n_valid, xin, wg, bg, wu, bu, wd, bd):
    cap = xin.shape[0]
    nb = cap // BM

    def blk(i, be, nv):
        return jnp.maximum(jnp.minimum(i, nv[0] - 1), 0)

    row = lambda i, be, nv: (blk(i, be, nv), 0)
    out_row = lambda i, be, nv: (blk(i - 1, be, nv), 0)
    wmap = lambda i, be, nv: (layer, be[blk(i, be, nv)], 0, 0)
    grid_spec = pltpu.PrefetchScalarGridSpec(
        num_scalar_prefetch=2,
        grid=(nb + 1,),
        in_specs=[pl.BlockSpec((BM, ROW_WORDS), row),
                  pl.BlockSpec((1, 1, D_MODEL, D_FF), wmap), pl.BlockSpec((1, 1, 1, D_FF), wmap),
                  pl.BlockSpec((1, 1, D_MODEL, D_FF), wmap), pl.BlockSpec((1, 1, 1, D_FF), wmap),
                  pl.BlockSpec((1, 1, D_FF, D_MODEL), wmap), pl.BlockSpec((1, 1, 1, D_MODEL), wmap)],
        out_specs=pl.BlockSpec((BM, ROW_WORDS), out_row),
        scratch_shapes=[pltpu.VMEM((D_MODEL, D_FF), BF16), pltpu.VMEM((D_MODEL, D_FF), BF16),
                        pltpu.VMEM((D_FF, D_MODEL), BF16), pltpu.VMEM((BM, D_MODEL), F32)],
    )
    return pl.pallas_call(
        _ffn_kernel,
        grid_spec=grid_spec,
        out_shape=jax.ShapeDtypeStruct((cap, ROW_WORDS), jnp.int32),
        compiler_params=pltpu.CompilerParams(dimension_semantics=("arbitrary",),
                                             vmem_limit_bytes=VMEM_LIMIT),
        name="expert_ffn",
    )(block_expert, n_valid, xin, wg, bg, wu, bu, wd, bd)


def _sc_gather(table, idx3):
    nw, n_chunks, ch = idx3.shape
    width = table.shape[1]
    per_worker = n_chunks * ch
    depth = SC_GATHER_DEPTH
    mesh = plsc.VectorSubcoreMesh(core_axis_name="c", subcore_axis_name="s")
    n_cores = mesh.num_cores
    assert nw == n_cores * mesh.num_subcores and n_chunks % depth == 0

    def body(table_hbm, idx_hbm, out_hbm, idx_v, *scratch):
        bufs, sem_g, sem_w = scratch[:depth], scratch[depth:2 * depth], scratch[2 * depth:]
        wid = lax.axis_index("s") * n_cores + lax.axis_index("c")
        base = wid * per_worker
        pltpu.sync_copy(idx_hbm.at[wid], idx_v)

        @pl.loop(0, n_chunks, step=depth)
        def _(c):
            gathers = [pltpu.async_copy(table_hbm.at[idx_v.at[c + j]], bufs[j], sem_g[j]) for j in range(depth)]
            writes = []
            for j in range(depth):
                gathers[j].wait()
                writes.append(pltpu.async_copy(bufs[j], out_hbm.at[pl.ds(base + (c + j) * ch, ch)], sem_w[j]))
            for w in writes:
                w.wait()

    return pl.kernel(
        body,
        out_type=jax.ShapeDtypeStruct((nw * per_worker, width), table.dtype),
        mesh=mesh,
        scratch_types=([pltpu.VMEM((n_chunks, ch), jnp.int32)]
                       + [pltpu.VMEM((ch, width), table.dtype)] * depth
                       + [pltpu.SemaphoreType.DMA] * (2 * depth)),
        name="sc_row_gather",
    )(table, idx3)


def _sc_scatter(rows, idx3, n_out):
    nw, n_lists, ch = idx3.shape
    n_chunks = n_lists // TOP_K
    width = rows.shape[1]
    per_worker = n_chunks * ch
    mesh = plsc.VectorSubcoreMesh(core_axis_name="c", subcore_axis_name="s")
    n_cores = mesh.num_cores
    assert nw == n_cores * mesh.num_subcores and n_chunks % 2 == 0 and ch == SC_CHUNK
    assert nw * per_worker == rows.shape[0]

    def body(rows_hbm, idx_hbm, out_hbm, idx_v, buf0, buf1, sem_r0, sem_r1, sem_w0, sem_w1):
        wid = lax.axis_index("s") * n_cores + lax.axis_index("c")
        base = wid * per_worker
        pltpu.sync_copy(idx_hbm.at[wid], idx_v)

        @pl.loop(0, n_chunks, step=2)
        def _(c):
            r0 = pltpu.async_copy(rows_hbm.at[pl.ds(base + c * ch, ch)], buf0, sem_r0)
            r1 = pltpu.async_copy(rows_hbm.at[pl.ds(base + (c + 1) * ch, ch)], buf1, sem_r1)
            r0.wait()
            w0 = [pltpu.async_copy(buf0, out_hbm.at[idx_v.at[c * TOP_K + k]], sem_w0) for k in range(TOP_K)]
            r1.wait()
            w1 = [pltpu.async_copy(buf1, out_hbm.at[idx_v.at[(c + 1) * TOP_K + k]], sem_w1) for k in range(TOP_K)]
            for w in w0 + w1:
                w.wait()

    return pl.kernel(
        body,
        out_type=jax.ShapeDtypeStruct((n_out, width), rows.dtype),
        mesh=mesh,
        scratch_types=[pltpu.VMEM((n_lists, ch), jnp.int32),
                       pltpu.VMEM((ch, width), rows.dtype), pltpu.VMEM((ch, width), rows.dtype),
                       pltpu.SemaphoreType.DMA, pltpu.SemaphoreType.DMA,
                       pltpu.SemaphoreType.DMA, pltpu.SemaphoreType.DMA],
        name="sc_row_scatter",
    )(rows, idx3)


def _combine_kernel(x_ref, rows_ref, mf_ref, g_ref, b_ref, o_ref):
    mf = mf_ref[...]
    y = DEEPNORM_ALPHA * x_ref[...]
    for kk in range(TOP_K):
        y = y + mf[:, kk:kk + 1] * _unpack_rows(rows_ref[kk])
    o_ref[...] = _layer_norm(y, g_ref[...], b_ref[...])


def _combine(x1, rows, mf, ln_g, ln_b):
    t = x1.shape[0]
    tm = TM_COMB
    row = lambda i: (i, 0)
    const = lambda i: (0, 0)
    return pl.pallas_call(
        _combine_kernel,
        grid=(t // tm,),
        in_specs=[pl.BlockSpec((tm, D_MODEL), row),
                  pl.BlockSpec((TOP_K, tm, ROW_WORDS), lambda i: (0, i, 0)),
                  pl.BlockSpec((tm, LANES), row),
                  pl.BlockSpec((1, D_MODEL), const), pl.BlockSpec((1, D_MODEL), const)],
        out_specs=pl.BlockSpec((tm, D_MODEL), row),
        out_shape=jax.ShapeDtypeStruct((t, D_MODEL), F32),
        compiler_params=pltpu.CompilerParams(dimension_semantics=("arbitrary",),
                                             vmem_limit_bytes=VMEM_LIMIT),
        name="combine_ln",
    )(x1, rows, mf, ln_g, ln_b)


def _relayout_w_in(w):
    dt_end = 4 * 256 + SSD_W + SSD_XBC + SSD_HEADS
    gk_end = dt_end + 2 * GLA_QK + GLA_W + GLA_RANK
    zeros = lambda n: jnp.zeros(w.shape[:2] + (n,), w.dtype)
    parts = [w[..., :dt_end], zeros(LANES - SSD_HEADS), w[..., dt_end:gk_end], zeros(LANES - GLA_RANK),
             w[..., gk_end:]]
    return jnp.concatenate(parts, axis=-1).astype(BF16)


def _rep_heads(p):
    return jnp.repeat(p, SSD_HEAD_DIM, axis=-1)[:, None, :]


def _pad_heads(p):
    return jnp.pad(p, ((0, 0), (0, LANES - SSD_HEADS)))[:, None, :]


def kernel(x, positions, w_in, w_out, ret_norm_w, ssd_conv_w, ssd_conv_b, ssd_dt_bias, ssd_a_log, ssd_d,
           ssd_norm_w, gla_w_gk2, gla_b_gk2, gla_norm_w, ln1_g, ln1_b, w_router, b_router, w_gate, b_gate,
           w_up, b_up, w_down, b_down, ln2_g, ln2_b):
    batch, seq, d = x.shape
    t = batch * seq
    depth = w_in.shape[0]
    assert d == D_MODEL and t % TM_POST == 0 and t % TM_COMB == 0
    n_assign = t * TOP_K
    nb = n_assign // BM + N_EXPERTS
    cap = nb * BM

    cos_t, sin_t = _rope_tables(positions.reshape(t, 1))
    cos_t = cos_t.reshape(batch, seq, -1)
    sin_t = sin_t.reshape(batch, seq, -1)
    x2 = x.reshape(t, d)

    row = lambda p: p[:, None, :]
    w_in_p = _relayout_w_in(w_in)
    mix_params = (row(ret_norm_w), ssd_conv_w, row(ssd_conv_b), _pad_heads(ssd_dt_bias), _pad_heads(ssd_a_log),
                  _rep_heads(ssd_d), row(ssd_norm_w), jnp.pad(gla_w_gk2, ((0, 0), (0, LANES - GLA_RANK), (0, 0))),
                  row(gla_b_gk2), row(gla_norm_w))
    w_out_b = w_out.astype(BF16)
    wr_p = jnp.pad(w_router, ((0, 0), (0, 0), (0, LANES - N_EXPERTS)))
    wr_hi = wr_p.astype(BF16)
    wr_lo = (wr_p - wr_hi.astype(F32)).astype(BF16)
    br_p = row(jnp.pad(b_router, ((0, 0), (0, LANES - N_EXPERTS)), constant_values=NEG_BIG))
    ln1 = (row(ln1_g), row(ln1_b))
    ln2 = (row(ln2_g), row(ln2_b))
    bias4 = lambda b: b[:, :, None, :]
    expert_ids = jnp.arange(N_EXPERTS, dtype=jnp.int32)
    block_start = jnp.arange(nb, dtype=jnp.int32) * BM

    for l in range(depth):
        h = _mixproj(x2.reshape(batch, seq, d), w_in_p[l], cos_t, sin_t,
                     tuple(p[l] for p in mix_params)).reshape(t, D_MIX)
        x1, x1p, mi, mf, cnt = _post(h, x2, w_out_b[l], ln1[0][l], ln1[1][l], wr_hi[l], wr_lo[l], br_p[l])

        counts = cnt[0, :N_EXPERTS].astype(jnp.int32)
        padded = (counts + BM - 1) // BM * BM
        end_padded = jnp.cumsum(padded)
        start_padded = end_padded - padded
        top_idx = mi[:TOP_K]
        start_of = jnp.sum(jnp.where(top_idx[None] == expert_ids[:, None, None],
                                     start_padded[:, None, None], 0), axis=0)
        dest = start_of + mi[TOP_K:]
        block_expert = jnp.minimum(jnp.sum((end_padded[None, :] <= block_start[:, None]).astype(jnp.int32), axis=1),
                                   N_EXPERTS - 1)
        n_valid = (end_padded[-1:] // BM).astype(jnp.int32)

        scatter_idx = dest.reshape(TOP_K, SC_WORKERS, -1, SC_CHUNK).transpose(1, 2, 0, 3)
        xin = _sc_scatter(x1p, scatter_idx.reshape(SC_WORKERS, -1, SC_CHUNK), cap)
        yb = _ffn(l, block_expert, n_valid, xin, w_gate, bias4(b_gate), w_up, bias4(b_up), w_down, bias4(b_down))
        rows = _sc_gather(yb, dest.reshape(SC_WORKERS, -1, SC_GATHER_CHUNK))
        x2 = _combine(x1, rows.reshape(TOP_K, t, ROW_WORDS), mf, ln2[0][l], ln2[1][l])
    return x2.reshape(batch, seq, d)
```

```python
import functools
import math

import jax
import jax.numpy as jnp
from jax import lax
from jax.experimental import pallas as pl
from jax.experimental.pallas import tpu as pltpu
from jax.experimental.pallas import tpu_sc as plsc

F32 = jnp.float32
BF16 = jnp.bfloat16

D_MODEL = 1024
CHUNK = 64
RET_HEADS, RET_DK, RET_DV = 4, 64, 64
RET_W = RET_HEADS * RET_DV
SSD_HEADS, SSD_HEAD_DIM, SSD_STATE, SSD_GROUPS, SSD_CONV = 8, 64, 64, 2, 4
SSD_W = SSD_HEADS * SSD_HEAD_DIM
SSD_BC = SSD_GROUPS * SSD_STATE
SSD_XBC = SSD_W + 2 * SSD_BC
GLA_HEADS, GLA_DK, GLA_DV, GLA_RANK, GLA_TEMP = 4, 32, 64, 16, 16.0
GLA_QK = GLA_HEADS * GLA_DK
GLA_W = GLA_HEADS * GLA_DV
D_MIX = RET_W + SSD_W + GLA_W
N_EXPERTS, TOP_K, D_FF = 32, 4, 1024
SWIGLU_LIMIT, SWIGLU_ALPHA = 7.0, 1.702
ROPE_BASE = 10000.0
LN_EPS, NORM_EPS = 1e-5, 1e-6
DEPTH = 2
DEEPNORM_ALPHA = (2.0 * DEPTH) ** 0.25

LANES = 128
ROPE_W = LANES
NEG_BIG = -1e30
VMEM_LIMIT = 56 * 1024 * 1024

_SEGS = (("rq", 256), ("rk", 256), ("rv", 256), ("rg", 256), ("sz", SSD_W), ("sxbc", SSD_XBC),
         ("sdt", 128), ("gq", 128), ("gk", 128), ("gv", 256), ("ggk", 128), ("gg", 256))
COL = {}
_off = 0
for _n, _w in _SEGS:
    COL[_n] = _off
    _off += _w
NP = _off

MIX_G = 8
PROJ_SLAB = 512
TM_POST = 1024
POST_SPLIT = 4
TM_COMB = 1024
TM_ROPE = 2048
BM = 512
ROW_WORDS = D_MODEL // 2
SC_WORKERS = 32
SC_CHUNK = 64


def _dot(a, b, dims=(((1,), (0,)), ((), ())), precision=None):
    return lax.dot_general(a, b, dims, precision=precision, preferred_element_type=F32)


_NT = (((1,), (1,)), ((), ()))
_TN = (((0,), (0,)), ((), ()))


def _iota(shape, dim):
    return lax.broadcasted_iota(jnp.int32, shape, dim)


def _vdiv(x, n):
    assert n & (n - 1) == 0
    return lax.shift_right_logical(x, n.bit_length() - 1)


def _vmod(x, n):
    assert n & (n - 1) == 0
    return jnp.bitwise_and(x, n - 1)


def _silu(x):
    return x * jax.nn.sigmoid(x)


def _softplus(x):
    return jnp.maximum(x, 0.0) + jnp.log(1.0 + jnp.exp(-jnp.abs(x)))


def _pack_rows(x):
    w = x.shape[1] // 2
    lo = lax.bitcast_convert_type(x[:, :w].astype(BF16).astype(F32), jnp.uint32)
    hi = lax.bitcast_convert_type(x[:, w:].astype(BF16).astype(F32), jnp.uint32)
    return lax.bitcast_convert_type(lax.shift_right_logical(lo, jnp.uint32(16)) | hi, jnp.int32)


def _unpack_rows(words):
    u = lax.bitcast_convert_type(words, jnp.uint32)
    a = lax.bitcast_convert_type(lax.shift_left(u, jnp.uint32(16)), F32)
    b = lax.bitcast_convert_type(u & jnp.uint32(0xFFFF0000), F32)
    return jnp.concatenate([a, b], axis=-1)


def _seg_sum64(x):
    first = _iota((1, LANES), 1) < 64
    outs = []
    for j in range(x.shape[-1] // LANES):
        blk = x[:, j * LANES:(j + 1) * LANES]
        lo = jnp.sum(jnp.where(first, blk, 0.0), axis=-1, keepdims=True)
        hi = jnp.sum(jnp.where(first, 0.0, blk), axis=-1, keepdims=True)
        outs.append(jnp.where(first, lo, hi))
    return jnp.concatenate(outs, axis=-1)


def _block_mask(shape, row_blk, col_blk):
    keep = _vdiv(_iota(shape, 0), row_blk) == _vdiv(_iota(shape, 1), col_blk)
    return jnp.where(keep, 1.0, 0.0).astype(BF16)


def _block_diag(x, mask):
    reps = mask.shape[0] // x.shape[0]
    return jnp.concatenate([x.astype(BF16)] * reps, axis=0) * mask


def _expand_heads(x, expand):
    hi = x.astype(BF16)
    r1 = x - hi.astype(F32)
    mid = r1.astype(BF16)
    lo = (r1 - mid.astype(F32)).astype(BF16)
    return _dot(hi, expand) + _dot(mid, expand) + _dot(lo, expand)


def _cumsum_rows(tri, x):
    hi = x.astype(BF16)
    lo = (x - hi.astype(F32)).astype(BF16)
    return _dot(tri, hi) + _dot(tri, lo)


def _rope_kernel(pos_ref, cos_ref, sin_ref):
    lane = _iota((1, ROPE_W), 1)
    half = RET_DK // 2
    k = _vmod(lane, half).astype(F32)
    inv_freq = jnp.exp(k * (-math.log(ROPE_BASE) / half))
    ang = pos_ref[...].astype(F32) * inv_freq
    first = _vmod(lane, RET_DK) < half
    cos_ref[...] = jnp.cos(ang)
    sin_ref[...] = jnp.where(first, -1.0, 1.0) * jnp.sin(ang)


def _rope_tables(pos_col):
    t = pos_col.shape[0]
    tm = TM_ROPE
    w = ROPE_W
    return pl.pallas_call(
        _rope_kernel,
        grid=(t // tm,),
        in_specs=[pl.BlockSpec((tm, 1), lambda i: (i, 0))],
        out_specs=[pl.BlockSpec((tm, w), lambda i: (i, 0))] * 2,
        out_shape=[jax.ShapeDtypeStruct((t, w), F32)] * 2,
        compiler_params=pltpu.CompilerParams(dimension_semantics=("arbitrary",)),
        name="rope_tables",
    )(pos_col)


def _mixproj_kernel(xn_ref, w_ref, cos_ref, sin_ref, retw_ref, convw_ref, convb_ref, dtb_ref, alog_ref,
                    dskip_ref, ssdw_ref, wgk_ref, bgk_ref, glaw_ref, h_ref,
                    proj_a, proj_b, ret_s, ssd_s, gla_s, stage, m_heads, m_groups, m_gla, m_expand, *,
                    chunks_per_seq):
    C = CHUNK
    i = pl.program_id(0)
    cur = jnp.maximum(i - 1, 0)

    @pl.when(i == 0)
    def _():
        m_heads[...] = _block_mask(m_heads.shape, C, 64)
        m_groups[...] = _block_mask(m_groups.shape, C * SSD_HEADS // SSD_GROUPS, SSD_STATE)
        m_gla[...] = _block_mask(m_gla.shape, C, GLA_DK)
        m_expand[...] = _block_mask(m_expand.shape, 1, SSD_HEAD_DIM)
        proj_b[...] = jnp.zeros_like(proj_b)

    @pl.when(lax.rem(cur, chunks_per_seq) == 0)
    def _():
        ret_s[...] = jnp.zeros_like(ret_s)
        ssd_s[...] = jnp.zeros_like(ssd_s)
        gla_s[...] = jnp.zeros_like(gla_s)
        for g in range(MIX_G):
            stage[g, 0:8, :] = jnp.zeros((8, SSD_XBC), F32)

    def conv_act(sq, pref):
        stage[sq, 8:8 + C, :] = pref[sq * C:(sq + 1) * C, COL["sxbc"]:COL["sxbc"] + SSD_XBC]
        acc = convb_ref[...] + convw_ref[0:1, :] * stage[sq, 5:5 + C, :]
        for j in range(1, SSD_CONV):
            acc = acc + convw_ref[j:j + 1, :] * stage[sq, 5 + j:5 + j + C, :]
        stage[sq, 0:8, :] = stage[sq, C:C + 8, :]
        return _silu(acc)

    lane256 = _iota((1, 256), 1)
    head = _vdiv(lane256, 64).astype(F32)
    log_gamma = jnp.log(1.0 - jnp.exp((-5.0 - head) * math.log(2.0)))
    row = _iota((C, 1), 0).astype(F32)
    dist = row - _vmod(lane256, 64).astype(F32)
    ret_intra = jnp.where(dist >= 0, jnp.exp(log_gamma * jnp.maximum(dist, 0.0)), 0.0)
    ret_qdec = jnp.exp(log_gamma * (row + 1.0))
    ret_kdec = jnp.exp(log_gamma * (C - 1.0 - row))
    ret_cdec = jnp.exp(log_gamma * C)
    first_half = _vmod(lane256, RET_DK) < (RET_DK // 2)

    tri = jnp.where(_iota((C, C), 0) >= _iota((C, C), 1), 1.0, 0.0).astype(BF16)
    causal4 = _iota((C, 256), 0) >= _vmod(_iota((C, 256), 1), 64)
    causal8 = _iota((C, 512), 0) >= _vmod(_iota((C, 512), 1), 64)
    eye8 = _iota((C, 512), 0) == _vmod(_iota((C, 512), 1), 64)

    a_neg = -jnp.exp(alog_ref[...])

    def rot(t, cos, sin):
        sw = jnp.where(first_half, pltpu.roll(t, 256 - 32, 1), pltpu.roll(t, 32, 1))
        return t * cos + sw * sin

    def chunk_of(sq, pref):
        def seg(name, width):
            return pref[sq * C:(sq + 1) * C, COL[name]:COL[name] + width]

        xact = conv_act(sq, pref)
        yield

        reps = RET_HEADS * RET_DK // ROPE_W
        cos = jnp.concatenate([cos_ref[sq]] * reps, axis=-1)
        sin = jnp.concatenate([sin_ref[sq]] * reps, axis=-1)
        q = rot(seg("rq", 256), cos, sin)
        k = rot(seg("rk", 256), cos, sin) * (RET_DK ** -0.5)
        v = seg("rv", 256)
        vb = v.astype(BF16)
        kbd = _block_diag(k, m_heads[...])
        scores = _dot(q.astype(BF16), kbd, _NT) * ret_intra
        vbd = _block_diag(v, m_heads[...])
        yield
        s_prev = ret_s[sq]
        o = _dot(scores.astype(BF16), vbd) + _dot((q * ret_qdec).astype(BF16), s_prev.astype(BF16))
        contrib = _dot((k * ret_kdec).astype(BF16), vb, _TN)
        yield
        keep = _vdiv(_iota((256, 256), 0), RET_DK) == _vdiv(_iota((256, 256), 1), RET_DV)
        ret_s[sq] = jnp.where(keep, ret_cdec * s_prev + contrib, 0.0)
        mu = _seg_sum64(o) * (1.0 / RET_DV)
        oc = o - mu
        var = _seg_sum64(oc * oc) * (1.0 / RET_DV)
        o = oc * lax.rsqrt(var + LN_EPS) * retw_ref[...]
        h_ref[sq, :, 0:RET_W] = (_silu(seg("rg", 256)) * o).astype(BF16)
        yield

        xs = xact[:, 0:SSD_W]
        bm = xact[:, SSD_W:SSD_W + SSD_BC]
        cm = xact[:, SSD_W + SSD_BC:SSD_XBC]
        cmb = cm.astype(BF16)
        dt_c = _softplus(seg("sdt", LANES) + dtb_ref[...])
        acum_c = _cumsum_rows(tri, dt_c * a_neg)
        both = _expand_heads(jnp.concatenate([dt_c, acum_c], axis=0), m_expand[...])
        dt = both[0:C, :]
        acum = both[C:2 * C, :]
        yield
        arow = jnp.sum(jnp.where(eye8, acum, 0.0), axis=0, keepdims=True)
        decay = jnp.exp(jnp.where(causal8, acum - arow, NEG_BIG))
        b8 = _block_diag(bm, m_groups[...])
        cb = _dot(cmb, b8, _NT)
        yield
        m = (cb * decay).astype(BF16)
        xdt = xs * dt
        s2 = ssd_s[sq]
        half = SSD_W // SSD_GROUPS
        ys = []
        for g in range(SSD_GROUPS):
            xbd = _block_diag(xdt[:, g * half:(g + 1) * half], m_heads[...])
            ys.append(_dot(m[:, g * half:(g + 1) * half], xbd))
        y = jnp.concatenate(ys, axis=-1)
        y = y + _dot(cmb, s2.astype(BF16)) * jnp.exp(acum)
        y = y + dskip_ref[...] * xs
        a_last = acum[C - 1:C, :]
        sd = jnp.exp(a_last - acum)
        contrib_s = _dot(bm.astype(BF16), (xdt * sd).astype(BF16), _TN)
        yield
        keep_s = _vdiv(_iota(s2.shape, 0), SSD_STATE) == _vdiv(_iota(s2.shape, 1), half)
        ssd_s[sq] = jnp.where(keep_s, s2 * jnp.exp(a_last) + contrib_s, 0.0)
        yz = y * _silu(seg("sz", SSD_W))
        outs = []
        for g in range(SSD_GROUPS):
            blk = yz[:, g * half:(g + 1) * half]
            ms = jnp.mean(blk * blk, axis=-1, keepdims=True)
            outs.append(blk * lax.rsqrt(ms + NORM_EPS))
        h_ref[sq, :, RET_W:RET_W + SSD_W] = (jnp.concatenate(outs, axis=-1) * ssdw_ref[...]).astype(BF16)

        yield

        gq = seg("gq", GLA_QK) * (GLA_DK ** -0.5)
        gkk = seg("gk", GLA_QK)
        gv = seg("gv", GLA_W)
        gkl = _dot(seg("ggk", 128).astype(BF16), wgk_ref[...].astype(BF16)) + bgk_ref[...]
        yield
        log_a = -_softplus(-gkl) * (1.0 / GLA_TEMP)
        b = _cumsum_rows(tri, log_a)
        yield
        q_t = (gq * jnp.exp(b)).astype(BF16)
        k_t = gkk * jnp.exp(-b)
        kbd_g = _block_diag(k_t, m_gla[...])
        att = jnp.where(causal4, _dot(q_t, kbd_g, _NT), 0.0)
        yield
        vbd_g = _block_diag(gv, m_heads[...])
        st = gla_s[sq]
        og = _dot(att.astype(BF16), vbd_g) + _dot(q_t, st.astype(BF16), _NT)
        b_last = b[C - 1:C, :]
        kd = (gkk * jnp.exp(b_last - b)).astype(BF16)
        contrib_g = _dot(gv.astype(BF16), kd, _TN)
        yield
        keep_g = _vdiv(_iota(st.shape, 0), GLA_DV) == _vdiv(_iota(st.shape, 1), GLA_DK)
        gla_s[sq] = jnp.where(keep_g, st * jnp.exp(b_last) + contrib_g, 0.0)
        ms = _seg_sum64(og * og) * (1.0 / GLA_DV)
        og = og * lax.rsqrt(ms + NORM_EPS) * glaw_ref[...]
        h_ref[sq, :, RET_W + SSD_W:D_MIX] = (_silu(seg("gg", GLA_W)) * og).astype(BF16)

    def step(p_read, p_write):
        xb = xn_ref[...].reshape(MIX_G * C, D_MODEL).astype(BF16)
        edges = list(range(0, NP, PROJ_SLAB)) + [NP]
        slabs = list(zip(edges[:-1], edges[1:]))

        live = [chunk_of(sq, p_read) for sq in range(MIX_G)]
        while live or slabs:
            live = [g for g in live if next(g, "done") != "done"]
            if slabs:
                lo, hi = slabs.pop(0)
                p_write[:, lo:hi] = _dot(xb, w_ref[:, lo:hi])

    @pl.when(lax.rem(i, 2) == 0)
    def _():
        step(proj_b, proj_a)

    @pl.when(lax.rem(i, 2) == 1)
    def _():
        step(proj_a, proj_b)


def _mixproj(x3, w_p, cos_t, sin_t, params):
    batch, seq, _ = x3.shape
    assert batch % MIX_G == 0 and seq % CHUNK == 0
    cps = seq // CHUNK
    n = (batch // MIX_G) * cps

    def cur_map(i):
        c = jnp.clip(i - 1, 0, n - 1)
        return (c // cps, c % cps, 0)

    def next_map(i):
        c = jnp.minimum(i, n - 1)
        return (c // cps, c % cps, 0)

    const = lambda i: (0, 0)
    specs = [pl.BlockSpec((MIX_G, CHUNK, D_MODEL), next_map),
             pl.BlockSpec((D_MODEL, NP), const),
             pl.BlockSpec((MIX_G, CHUNK, ROPE_W), cur_map),
             pl.BlockSpec((MIX_G, CHUNK, ROPE_W), cur_map)]
    specs += [pl.BlockSpec(p.shape, const) for p in params]
    return pl.pallas_call(
        functools.partial(_mixproj_kernel, chunks_per_seq=cps),
        grid=(n + 1,),
        in_specs=specs,
        out_specs=pl.BlockSpec((MIX_G, CHUNK, D_MIX), cur_map),
        out_shape=jax.ShapeDtypeStruct((batch, seq, D_MIX), BF16),
        scratch_shapes=[pltpu.VMEM((MIX_G * CHUNK, NP), F32),
                        pltpu.VMEM((MIX_G * CHUNK, NP), F32),
                        pltpu.VMEM((MIX_G, 256, 256), F32),
                        pltpu.VMEM((MIX_G, SSD_BC, SSD_W), F32),
                        pltpu.VMEM((MIX_G, GLA_W, GLA_QK), F32),
                        pltpu.VMEM((MIX_G, CHUNK + 8, SSD_XBC), F32),
                        pltpu.VMEM((RET_HEADS * CHUNK, 256), BF16),
                        pltpu.VMEM((SSD_HEADS * CHUNK, SSD_BC), BF16),
                        pltpu.VMEM((GLA_HEADS * CHUNK, GLA_QK), BF16),
                        pltpu.VMEM((LANES, SSD_W), BF16)],
        compiler_params=pltpu.CompilerParams(dimension_semantics=("arbitrary",),
                                             vmem_limit_bytes=VMEM_LIMIT),
        name="inproj_mixer",
    )(x3, w_p, cos_t, sin_t, *params)


def _layer_norm(y, g, b):
    mu = jnp.mean(y, axis=-1, keepdims=True)
    yc = y - mu
    var = jnp.mean(yc * yc, axis=-1, keepdims=True)
    return yc * lax.rsqrt(var + LN_EPS) * g + b


def _post_kernel(h_ref, x_ref, wout_ref, g_ref, b_ref, wrh_ref, wrl_ref, br_ref,
                 x1_ref, x1p_ref, mi_ref, mf_ref, cnt_ref, carry):
    sub = TM_POST // POST_SPLIT

    @pl.when(pl.program_id(0) == 0)
    def _():
        carry[...] = jnp.zeros_like(carry)

    lane_i = _iota((sub, LANES), 1)
    lane = lane_i.astype(F32)
    found = {}

    def sub_tile(part):
        rows = slice(part * sub, (part + 1) * sub)
        mix = _dot(h_ref[rows, :], wout_ref[...])
        yield
        x1 = _layer_norm(DEEPNORM_ALPHA * x_ref[rows, :] + mix, g_ref[...], b_ref[...])
        x1_ref[rows, :] = x1
        x1p_ref[rows, :] = _pack_rows(x1)
        x_hi = x1.astype(BF16)
        x_lo = (x1 - x_hi.astype(F32)).astype(BF16)
        logits = (_dot(x_hi, wrh_ref[...]) + _dot(x_lo, wrh_ref[...]) + _dot(x_hi, wrl_ref[...])
                  + br_ref[...])
        yield
        work = logits
        vals, idxs = [], []
        multi = jnp.zeros((sub, LANES), F32)
        for _ in range(TOP_K):
            m = jnp.max(work, axis=-1, keepdims=True)
            idx = jnp.min(jnp.where(work == m, lane, float(LANES)), axis=-1, keepdims=True)
            hit = lane == idx
            multi = multi + hit.astype(F32)
            work = jnp.where(hit, -jnp.inf, work)
            vals.append(m)
            idxs.append(idx)
            yield
        exps = [jnp.exp(v - vals[0]) for v in vals]
        denom = exps[0] + exps[1] + exps[2] + exps[3]
        gates = [e / denom for e in exps]
        before = (_iota((sub, sub), 0) > _iota((sub, sub), 1)).astype(BF16)
        found[part] = (idxs, gates, _dot(before, multi.astype(BF16)), multi)

    live = [sub_tile(part) for part in range(POST_SPLIT)]
    while live:
        live = [g for g in live if next(g, "done") != "done"]

    base = carry[...]
    for part in range(POST_SPLIT):
        idxs, gates, prior_local, multi = found[part]
        prior = prior_local + base
        mi = jnp.zeros((sub, LANES), F32)
        mf = jnp.zeros((sub, LANES), F32)
        for kk in range(TOP_K):
            rank = jnp.sum(jnp.where(lane == idxs[kk], prior, 0.0), axis=-1, keepdims=True)
            mi = jnp.where(lane_i == kk, idxs[kk], mi)
            mi = jnp.where(lane_i == TOP_K + kk, rank, mi)
            mf = jnp.where(lane_i == kk, gates[kk], mf)
        mi_ref[:, part * sub:(part + 1) * sub] = jnp.transpose(mi)[0:2 * TOP_K, :].astype(jnp.int32)
        mf_ref[part * sub:(part + 1) * sub, :] = mf
        base = base + jnp.sum(multi, axis=0, keepdims=True)
    carry[...] = base
    cnt_ref[...] = jnp.broadcast_to(base, cnt_ref.shape)


def _post(h, x2, w_out_b, ln_g, ln_b, wr_hi, wr_lo, br_p):
    t = x2.shape[0]
    tm = TM_POST
    row = lambda i: (i, 0)
    const = lambda i: (0, 0)
    return pl.pallas_call(
        _post_kernel,
        grid=(t // tm,),
        in_specs=[pl.BlockSpec((tm, D_MIX), row), pl.BlockSpec((tm, D_MODEL), row),
                  pl.BlockSpec((D_MIX, D_MODEL), const), pl.BlockSpec((1, D_MODEL), const),
                  pl.BlockSpec((1, D_MODEL), const), pl.BlockSpec((D_MODEL, LANES), const),
                  pl.BlockSpec((D_MODEL, LANES), const), pl.BlockSpec((1, LANES), const)],
        out_specs=[pl.BlockSpec((tm, D_MODEL), row), pl.BlockSpec((tm, ROW_WORDS), row),
                   pl.BlockSpec((2 * TOP_K, tm), lambda i: (0, i)), pl.BlockSpec((tm, LANES), row),
                   pl.BlockSpec((8, LANES), const)],
        out_shape=[jax.ShapeDtypeStruct((t, D_MODEL), F32), jax.ShapeDtypeStruct((t, ROW_WORDS), jnp.int32),
                   jax.ShapeDtypeStruct((2 * TOP_K, t), jnp.int32), jax.ShapeDtypeStruct((t, LANES), F32),
                   jax.ShapeDtypeStruct((8, LANES), F32)],
        scratch_shapes=[pltpu.VMEM((1, LANES), F32)],
        compiler_params=pltpu.CompilerParams(dimension_semantics=("arbitrary",),
                                             vmem_limit_bytes=VMEM_LIMIT),
        name="outproj_ln_router",
    )(h, x2, w_out_b, ln_g, ln_b, wr_hi, wr_lo, br_p)


def _ffn_kernel(be_ref, nv_ref, x_ref, wg_ref, bg_ref, wu_ref, bu_ref, wd_ref, bd_ref, o_ref,
                wg_b, wu_b, wd_b):
    i = pl.program_id(0)
    valid = i < nv_ref[0]
    e = be_ref[i]
    prev = be_ref[jnp.maximum(i - 1, 0)]
    fresh = jnp.logical_or(i == 0, e != prev)

    @pl.when(jnp.logical_and(valid, fresh))
    def _():
        wg_b[...] = wg_ref[0, 0].astype(BF16)
        wu_b[...] = wu_ref[0, 0].astype(BF16)
        wd_b[...] = wd_ref[0, 0].astype(BF16)

    @pl.when(valid)
    def _():
        x = _unpack_rows(x_ref[...]).astype(BF16)
        hg = jnp.minimum(_dot(x, wg_b[...]) + bg_ref[0, 0], SWIGLU_LIMIT)
        hu = jnp.clip(_dot(x, wu_b[...]) + bu_ref[0, 0], -SWIGLU_LIMIT, SWIGLU_LIMIT)
        hh = (hu + 1.0) * hg * jax.nn.sigmoid(SWIGLU_ALPHA * hg)
        o_ref[...] = _pack_rows(_dot(hh.astype(BF16), wd_b[...]) + bd_ref[0, 0])


def _ffn(layer, block_expert, n_valid, xin, wg, bg, wu, bu, wd, bd):
    cap = xin.shape[0]
    nb = cap // BM

    def blk(i, be, nv):
        return jnp.maximum(jnp.minimum(i, nv[0] - 1), 0)

    row = lambda i, be, nv: (blk(i, be, nv), 0)
    wmap = lambda i, be, nv: (layer, be[blk(i, be, nv)], 0, 0)
    grid_spec = pltpu.PrefetchScalarGridSpec(
        num_scalar_prefetch=2,
        grid=(nb,),
        in_specs=[pl.BlockSpec((BM, ROW_WORDS), row),
                  pl.BlockSpec((1, 1, D_MODEL, D_FF), wmap), pl.BlockSpec((1, 1, 1, D_FF), wmap),
                  pl.BlockSpec((1, 1, D_MODEL, D_FF), wmap), pl.BlockSpec((1, 1, 1, D_FF), wmap),
                  pl.BlockSpec((1, 1, D_FF, D_MODEL), wmap), pl.BlockSpec((1, 1, 1, D_MODEL), wmap)],
        out_specs=pl.BlockSpec((BM, ROW_WORDS), row),
        scratch_shapes=[pltpu.VMEM((D_MODEL, D_FF), BF16), pltpu.VMEM((D_MODEL, D_FF), BF16),
                        pltpu.VMEM((D_FF, D_MODEL), BF16)],
    )
    return pl.pallas_call(
        _ffn_kernel,
        grid_spec=grid_spec,
        out_shape=jax.ShapeDtypeStruct((cap, ROW_WORDS), jnp.int32),
        compiler_params=pltpu.CompilerParams(dimension_semantics=("arbitrary",),
                                             vmem_limit_bytes=VMEM_LIMIT),
        name="expert_ffn",
    )(block_expert, n_valid, xin, wg, bg, wu, bu, wd, bd)


def _sc_gather(table, idx3):
    nw, n_chunks, ch = idx3.shape
    width = table.shape[1]
    per_worker = n_chunks * ch
    mesh = plsc.VectorSubcoreMesh(core_axis_name="c", subcore_axis_name="s")
    n_cores = mesh.num_cores
    assert nw == n_cores * mesh.num_subcores and n_chunks % 2 == 0 and ch == SC_CHUNK

    def body(table_hbm, idx_hbm, out_hbm, idx_v, rows0, rows1, sem_g0, sem_g1, sem_w0, sem_w1):
        wid = lax.axis_index("s") * n_cores + lax.axis_index("c")
        base = wid * per_worker
        pltpu.sync_copy(idx_hbm.at[wid], idx_v)

        @pl.loop(0, n_chunks, step=2)
        def _(c):
            g0 = pltpu.async_copy(table_hbm.at[idx_v.at[c]], rows0, sem_g0)
            g1 = pltpu.async_copy(table_hbm.at[idx_v.at[c + 1]], rows1, sem_g1)
            g0.wait()
            w0 = pltpu.async_copy(rows0, out_hbm.at[pl.ds(base + c * ch, ch)], sem_w0)
            g1.wait()
            w1 = pltpu.async_copy(rows1, out_hbm.at[pl.ds(base + (c + 1) * ch, ch)], sem_w1)
            w0.wait()
            w1.wait()

    return pl.kernel(
        body,
        out_type=jax.ShapeDtypeStruct((nw * per_worker, width), table.dtype),
        mesh=mesh,
        scratch_types=[pltpu.VMEM((n_chunks, ch), jnp.int32),
                       pltpu.VMEM((ch, width), table.dtype), pltpu.VMEM((ch, width), table.dtype),
                       pltpu.SemaphoreType.DMA, pltpu.SemaphoreType.DMA,
                       pltpu.SemaphoreType.DMA, pltpu.SemaphoreType.DMA],
        name="sc_row_gather",
    )(table, idx3)


def _sc_scatter(rows, idx3, n_out):
    nw, n_lists, ch = idx3.shape
    n_chunks = n_lists // TOP_K
    width = rows.shape[1]
    per_worker = n_chunks * ch
    mesh = plsc.VectorSubcoreMesh(core_axis_name="c", subcore_axis_name="s")
    n_cores = mesh.num_cores
    assert nw == n_cores * mesh.num_subcores and n_chunks % 2 == 0 and ch == SC_CHUNK
    assert nw * per_worker == rows.shape[0]

    def body(rows_hbm, idx_hbm, out_hbm, idx_v, buf0, buf1, sem_r0, sem_r1, sem_w0, sem_w1):
        wid = lax.axis_index("s") * n_cores + lax.axis_index("c")
        base = wid * per_worker
        pltpu.sync_copy(idx_hbm.at[wid], idx_v)

        @pl.loop(0, n_chunks, step=2)
        def _(c):
            r0 = pltpu.async_copy(rows_hbm.at[pl.ds(base + c * ch, ch)], buf0, sem_r0)
            r1 = pltpu.async_copy(rows_hbm.at[pl.ds(base + (c + 1) * ch, ch)], buf1, sem_r1)
            r0.wait()
            w0 = [pltpu.async_copy(buf0, out_hbm.at[idx_v.at[c * TOP_K + k]], sem_w0) for k in range(TOP_K)]
            r1.wait()
            w1 = [pltpu.async_copy(buf1, out_hbm.at[idx_v.at[(c + 1) * TOP_K + k]], sem_w1) for k in range(TOP_K)]
            for w in w0 + w1:
                w.wait()

    return pl.kernel(
        body,
        out_type=jax.ShapeDtypeStruct((n_out, width), rows.dtype),
        mesh=mesh,
        scratch_types=[pltpu.VMEM((n_lists, ch), jnp.int32),
                       pltpu.VMEM((ch, width), rows.dtype), pltpu.VMEM((ch, width), rows.dtype),
                       pltpu.SemaphoreType.DMA, pltpu.SemaphoreType.DMA,
                       pltpu.SemaphoreType.DMA, pltpu.SemaphoreType.DMA],
        name="sc_row_scatter",
    )(rows, idx3)


def _combine_kernel(x_ref, rows_ref, mf_ref, g_ref, b_ref, o_ref):
    mf = mf_ref[...]
    y = DEEPNORM_ALPHA * x_ref[...]
    for kk in range(TOP_K):
        y = y + mf[:, kk:kk + 1] * _unpack_rows(rows_ref[kk])
    o_ref[...] = _layer_norm(y, g_ref[...], b_ref[...])


def _combine(x1, rows, mf, ln_g, ln_b):
    t = x1.shape[0]
    tm = TM_COMB
    row = lambda i: (i, 0)
    const = lambda i: (0, 0)
    return pl.pallas_call(
        _combine_kernel,
        grid=(t // tm,),
        in_specs=[pl.BlockSpec((tm, D_MODEL), row),
                  pl.BlockSpec((TOP_K, tm, ROW_WORDS), lambda i: (0, i, 0)),
                  pl.BlockSpec((tm, LANES), row),
                  pl.BlockSpec((1, D_MODEL), const), pl.BlockSpec((1, D_MODEL), const)],
        out_specs=pl.BlockSpec((tm, D_MODEL), row),
        out_shape=jax.ShapeDtypeStruct((t, D_MODEL), F32),
        compiler_params=pltpu.CompilerParams(dimension_semantics=("arbitrary",),
                                             vmem_limit_bytes=VMEM_LIMIT),
        name="combine_ln",
    )(x1, rows, mf, ln_g, ln_b)


def _relayout_w_in(w):
    widths = (256, 256, 256, 256, SSD_W, SSD_XBC, SSD_HEADS, GLA_QK, GLA_QK, GLA_W, GLA_RANK, GLA_W)
    offs = [0]
    for wd in widths:
        offs.append(offs[-1] + wd)
    parts = [w[:, offs[i]:offs[i + 1]] for i in range(len(widths))]
    parts[6] = jnp.pad(parts[6], ((0, 0), (0, LANES - SSD_HEADS)))
    parts[10] = jnp.pad(parts[10], ((0, 0), (0, LANES - GLA_RANK)))
    return jnp.concatenate(parts, axis=1).astype(BF16)


def _rep_heads(p):
    return jnp.repeat(p, SSD_HEAD_DIM)[None, :]


def _pad_heads(p):
    return jnp.pad(p, (0, LANES - SSD_HEADS))[None, :]


def kernel(x, positions, w_in, w_out, ret_norm_w, ssd_conv_w, ssd_conv_b, ssd_dt_bias, ssd_a_log, ssd_d,
           ssd_norm_w, gla_w_gk2, gla_b_gk2, gla_norm_w, ln1_g, ln1_b, w_router, b_router, w_gate, b_gate,
           w_up, b_up, w_down, b_down, ln2_g, ln2_b):
    batch, seq, d = x.shape
    t = batch * seq
    depth = w_in.shape[0]
    assert d == D_MODEL and t % TM_POST == 0 and t % TM_COMB == 0 and t % TM_ROPE == 0
    n_assign = t * TOP_K
    nb = n_assign // BM + N_EXPERTS
    cap = nb * BM

    cos_t, sin_t = _rope_tables(positions.reshape(t, 1))
    cos_t = cos_t.reshape(batch, seq, -1)
    sin_t = sin_t.reshape(batch, seq, -1)
    x2 = x.reshape(t, d)

    for l in range(depth):
        params = (ret_norm_w[l][None, :], ssd_conv_w[l], ssd_conv_b[l][None, :], _pad_heads(ssd_dt_bias[l]),
                  _pad_heads(ssd_a_log[l]), _rep_heads(ssd_d[l]), ssd_norm_w[l][None, :],
                  jnp.pad(gla_w_gk2[l], ((0, LANES - GLA_RANK), (0, 0))), gla_b_gk2[l][None, :],
                  gla_norm_w[l][None, :])
        h = _mixproj(x2.reshape(batch, seq, d), _relayout_w_in(w_in[l]), cos_t, sin_t, params).reshape(t, D_MIX)

        wr_p = jnp.pad(w_router[l], ((0, 0), (0, LANES - N_EXPERTS)))
        br_p = jnp.pad(b_router[l], (0, LANES - N_EXPERTS), constant_values=NEG_BIG)[None, :]
        wr_hi = wr_p.astype(BF16)
        wr_lo = (wr_p - wr_hi.astype(F32)).astype(BF16)
        x1, x1p, mi, mf, cnt = _post(h, x2, w_out[l].astype(BF16), ln1_g[l][None, :], ln1_b[l][None, :],
                                     wr_hi, wr_lo, br_p)

        counts = cnt[0, :N_EXPERTS].astype(jnp.int32)
        padded = (counts + BM - 1) // BM * BM
        end_padded = jnp.cumsum(padded)
        start_padded = end_padded - padded
        top_idx = mi[:TOP_K]
        start_of = jnp.sum(jnp.where(top_idx[None] == jnp.arange(N_EXPERTS, dtype=jnp.int32)[:, None, None],
                                     start_padded[:, None, None], 0), axis=0)
        dest = start_of + mi[TOP_K:]
        block_start = jnp.arange(nb, dtype=jnp.int32) * BM
        block_expert = jnp.minimum(jnp.sum((end_padded[None, :] <= block_start[:, None]).astype(jnp.int32), axis=1),
                                   N_EXPERTS - 1)
        n_valid = (end_padded[-1:] // BM).astype(jnp.int32)

        scatter_idx = dest.reshape(TOP_K, SC_WORKERS, -1, SC_CHUNK).transpose(1, 2, 0, 3)
        xin = _sc_scatter(x1p, scatter_idx.reshape(SC_WORKERS, -1, SC_CHUNK), cap)
        yb = _ffn(l, block_expert, n_valid, xin, w_gate, b_gate[:, :, None, :], w_up, b_up[:, :, None, :],
                  w_down, b_down[:, :, None, :])
        rows = _sc_gather(yb, dest.reshape(SC_WORKERS, -1, SC_CHUNK))
        x2 = _combine(x1, rows.reshape(TOP_K, t, ROW_WORDS), mf, ln2_g[l][None, :], ln2_b[l][None, :])
    return x2.reshape(batch, seq, d)
```

```python
import functools
import math

import jax
import jax.numpy as jnp
from jax import lax
from jax.experimental import pallas as pl
from jax.experimental.pallas import tpu as pltpu
from jax.experimental.pallas import tpu_sc as plsc

F32 = jnp.float32
BF16 = jnp.bfloat16

D_MODEL = 1024
CHUNK = 64
RET_HEADS, RET_DK, RET_DV = 4, 64, 64
RET_W = RET_HEADS * RET_DV
SSD_HEADS, SSD_HEAD_DIM, SSD_STATE, SSD_GROUPS, SSD_CONV = 8, 64, 64, 2, 4
SSD_W = SSD_HEADS * SSD_HEAD_DIM
SSD_BC = SSD_GROUPS * SSD_STATE
SSD_XBC = SSD_W + 2 * SSD_BC
GLA_HEADS, GLA_DK, GLA_DV, GLA_RANK, GLA_TEMP = 4, 32, 64, 16, 16.0
GLA_QK = GLA_HEADS * GLA_DK
GLA_W = GLA_HEADS * GLA_DV
D_MIX = RET_W + SSD_W + GLA_W
N_EXPERTS, TOP_K, D_FF = 32, 4, 1024
SWIGLU_LIMIT, SWIGLU_ALPHA = 7.0, 1.702
ROPE_BASE = 10000.0
LN_EPS, NORM_EPS = 1e-5, 1e-6
DEPTH = 2
DEEPNORM_ALPHA = (2.0 * DEPTH) ** 0.25

LANES = 128
ROPE_W = LANES
NEG_BIG = -1e30
VMEM_LIMIT = 56 * 1024 * 1024

_SEGS = (("rq", 256), ("rk", 256), ("rv", 256), ("rg", 256), ("sz", SSD_W), ("sxbc", SSD_XBC),
         ("sdt", 128), ("gq", 128), ("gk", 128), ("gv", 256), ("ggk", 128), ("gg", 256))
COL = {}
_off = 0
for _n, _w in _SEGS:
    COL[_n] = _off
    _off += _w
NP = _off

MIX_G = 8
PROJ_SLAB = 512
TM_POST = 1024
POST_SPLIT = 4
TM_COMB = 1024
TM_ROPE = 2048
BM = 512
ROW_WORDS = D_MODEL // 2
SC_WORKERS = 32
SC_CHUNK = 64


def _dot(a, b, dims=(((1,), (0,)), ((), ())), precision=None):
    return lax.dot_general(a, b, dims, precision=precision, preferred_element_type=F32)


_NT = (((1,), (1,)), ((), ()))
_TN = (((0,), (0,)), ((), ()))


def _iota(shape, dim):
    return lax.broadcasted_iota(jnp.int32, shape, dim)


def _vdiv(x, n):
    assert n & (n - 1) == 0
    return lax.shift_right_logical(x, n.bit_length() - 1)


def _vmod(x, n):
    assert n & (n - 1) == 0
    return jnp.bitwise_and(x, n - 1)


def _silu(x):
    return x * jax.nn.sigmoid(x)


def _softplus(x):
    return jnp.maximum(x, 0.0) + jnp.log(1.0 + jnp.exp(-jnp.abs(x)))


def _pack_rows(x):
    w = x.shape[1] // 2
    lo = lax.bitcast_convert_type(x[:, :w].astype(BF16).astype(F32), jnp.uint32)
    hi = lax.bitcast_convert_type(x[:, w:].astype(BF16).astype(F32), jnp.uint32)
    return lax.bitcast_convert_type(lax.shift_right_logical(lo, jnp.uint32(16)) | hi, jnp.int32)


def _unpack_rows(words):
    u = lax.bitcast_convert_type(words, jnp.uint32)
    a = lax.bitcast_convert_type(lax.shift_left(u, jnp.uint32(16)), F32)
    b = lax.bitcast_convert_type(u & jnp.uint32(0xFFFF0000), F32)
    return jnp.concatenate([a, b], axis=-1)


def _seg_sum64(x):
    first = _iota((1, LANES), 1) < 64
    outs = []
    for j in range(x.shape[-1] // LANES):
        blk = x[:, j * LANES:(j + 1) * LANES]
        lo = jnp.sum(jnp.where(first, blk, 0.0), axis=-1, keepdims=True)
        hi = jnp.sum(jnp.where(first, 0.0, blk), axis=-1, keepdims=True)
        outs.append(jnp.where(first, lo, hi))
    return jnp.concatenate(outs, axis=-1)


def _block_mask(shape, row_blk, col_blk):
    keep = _vdiv(_iota(shape, 0), row_blk) == _vdiv(_iota(shape, 1), col_blk)
    return jnp.where(keep, 1.0, 0.0).astype(BF16)


def _block_diag(x, mask):
    reps = mask.shape[0] // x.shape[0]
    return jnp.concatenate([x.astype(BF16)] * reps, axis=0) * mask


def _expand_heads(x, expand):
    hi = x.astype(BF16)
    r1 = x - hi.astype(F32)
    mid = r1.astype(BF16)
    lo = (r1 - mid.astype(F32)).astype(BF16)
    return _dot(hi, expand) + _dot(mid, expand) + _dot(lo, expand)


def _cumsum_rows(tri, x):
    hi = x.astype(BF16)
    lo = (x - hi.astype(F32)).astype(BF16)
    return _dot(tri, hi) + _dot(tri, lo)


def _rope_kernel(pos_ref, cos_ref, sin_ref):
    lane = _iota((1, ROPE_W), 1)
    half = RET_DK // 2
    k = _vmod(lane, half).astype(F32)
    inv_freq = jnp.exp(k * (-math.log(ROPE_BASE) / half))
    ang = pos_ref[...].astype(F32) * inv_freq
    first = _vmod(lane, RET_DK) < half
    cos_ref[...] = jnp.cos(ang)
    sin_ref[...] = jnp.where(first, -1.0, 1.0) * jnp.sin(ang)


def _rope_tables(pos_col):
    t = pos_col.shape[0]
    tm = TM_ROPE
    w = ROPE_W
    return pl.pallas_call(
        _rope_kernel,
        grid=(t // tm,),
        in_specs=[pl.BlockSpec((tm, 1), lambda i: (i, 0))],
        out_specs=[pl.BlockSpec((tm, w), lambda i: (i, 0))] * 2,
        out_shape=[jax.ShapeDtypeStruct((t, w), F32)] * 2,
        compiler_params=pltpu.CompilerParams(dimension_semantics=("arbitrary",)),
        name="rope_tables",
    )(pos_col)


def _mixproj_kernel(xn_ref, w_ref, cos_ref, sin_ref, retw_ref, convw_ref, convb_ref, dtb_ref, alog_ref,
                    dskip_ref, ssdw_ref, wgk_ref, bgk_ref, glaw_ref, h_ref,
                    proj_a, proj_b, ret_s, ssd_s, gla_s, stage, m_heads, m_groups, m_gla, m_expand, *,
                    chunks_per_seq):
    C = CHUNK
    i = pl.program_id(0)
    cur = jnp.maximum(i - 1, 0)

    @pl.when(i == 0)
    def _():
        m_heads[...] = _block_mask(m_heads.shape, C, 64)
        m_groups[...] = _block_mask(m_groups.shape, C * SSD_HEADS // SSD_GROUPS, SSD_STATE)
        m_gla[...] = _block_mask(m_gla.shape, C, GLA_DK)
        m_expand[...] = _block_mask(m_expand.shape, 1, SSD_HEAD_DIM)
        proj_b[...] = jnp.zeros_like(proj_b)

    @pl.when(lax.rem(cur, chunks_per_seq) == 0)
    def _():
        ret_s[...] = jnp.zeros_like(ret_s)
        ssd_s[...] = jnp.zeros_like(ssd_s)
        gla_s[...] = jnp.zeros_like(gla_s)
        for g in range(MIX_G):
            stage[g, 0:8, :] = jnp.zeros((8, SSD_XBC), F32)

    def conv_act(sq, pref):
        stage[sq, 8:8 + C, :] = pref[sq * C:(sq + 1) * C, COL["sxbc"]:COL["sxbc"] + SSD_XBC]
        acc = convb_ref[...] + convw_ref[0:1, :] * stage[sq, 5:5 + C, :]
        for j in range(1, SSD_CONV):
            acc = acc + convw_ref[j:j + 1, :] * stage[sq, 5 + j:5 + j + C, :]
        stage[sq, 0:8, :] = stage[sq, C:C + 8, :]
        return _silu(acc)

    lane256 = _iota((1, 256), 1)
    head = _vdiv(lane256, 64).astype(F32)
    log_gamma = jnp.log(1.0 - jnp.exp((-5.0 - head) * math.log(2.0)))
    row = _iota((C, 1), 0).astype(F32)
    dist = row - _vmod(lane256, 64).astype(F32)
    ret_intra = jnp.where(dist >= 0, jnp.exp(log_gamma * jnp.maximum(dist, 0.0)), 0.0)
    ret_qdec = jnp.exp(log_gamma * (row + 1.0))
    ret_kdec = jnp.exp(log_gamma * (C - 1.0 - row))
    ret_cdec = jnp.exp(log_gamma * C)
    first_half = _vmod(lane256, RET_DK) < (RET_DK // 2)

    tri = jnp.where(_iota((C, C), 0) >= _iota((C, C), 1), 1.0, 0.0).astype(BF16)
    causal4 = _iota((C, 256), 0) >= _vmod(_iota((C, 256), 1), 64)
    causal8 = _iota((C, 512), 0) >= _vmod(_iota((C, 512), 1), 64)
    eye8 = _iota((C, 512), 0) == _vmod(_iota((C, 512), 1), 64)

    a_neg = -jnp.exp(alog_ref[...])

    def rot(t, cos, sin):
        sw = jnp.where(first_half, pltpu.roll(t, 256 - 32, 1), pltpu.roll(t, 32, 1))
        return t * cos + sw * sin

    def chunk_of(sq, pref):
        def seg(name, width):
            return pref[sq * C:(sq + 1) * C, COL[name]:COL[name] + width]

        xact = conv_act(sq, pref)
        yield

        reps = RET_HEADS * RET_DK // ROPE_W
        cos = jnp.concatenate([cos_ref[sq]] * reps, axis=-1)
        sin = jnp.concatenate([sin_ref[sq]] * reps, axis=-1)
        q = rot(seg("rq", 256), cos, sin)
        k = rot(seg("rk", 256), cos, sin) * (RET_DK ** -0.5)
        v = seg("rv", 256)
        vb = v.astype(BF16)
        kbd = _block_diag(k, m_heads[...])
        scores = _dot(q.astype(BF16), kbd, _NT) * ret_intra
        vbd = _block_diag(v, m_heads[...])
        yield
        s_prev = ret_s[sq]
        o = _dot(scores.astype(BF16), vbd) + _dot((q * ret_qdec).astype(BF16), s_prev.astype(BF16))
        contrib = _dot((k * ret_kdec).astype(BF16), vb, _TN)
        yield
        keep = _vdiv(_iota((256, 256), 0), RET_DK) == _vdiv(_iota((256, 256), 1), RET_DV)
        ret_s[sq] = jnp.where(keep, ret_cdec * s_prev + contrib, 0.0)
        mu = _seg_sum64(o) * (1.0 / RET_DV)
        oc = o - mu
        var = _seg_sum64(oc * oc) * (1.0 / RET_DV)
        o = oc * lax.rsqrt(var + LN_EPS) * retw_ref[...]
        h_ref[sq, :, 0:RET_W] = (_silu(seg("rg", 256)) * o).astype(BF16)
        yield

        xs = xact[:, 0:SSD_W]
        bm = xact[:, SSD_W:SSD_W + SSD_BC]
        cm = xact[:, SSD_W + SSD_BC:SSD_XBC]
        cmb = cm.astype(BF16)
        dt_c = _softplus(seg("sdt", LANES) + dtb_ref[...])
        acum_c = _cumsum_rows(tri, dt_c * a_neg)
        both = _expand_heads(jnp.concatenate([dt_c, acum_c], axis=0), m_expand[...])
        dt = both[0:C, :]
        acum = both[C:2 * C, :]
        yield
        arow = jnp.sum(jnp.where(eye8, acum, 0.0), axis=0, keepdims=True)
        decay = jnp.exp(jnp.where(causal8, acum - arow, NEG_BIG))
        b8 = _block_diag(bm, m_groups[...])
        cb = _dot(cmb, b8, _NT)
        yield
        m = (cb * decay).astype(BF16)
        xdt = xs * dt
        s2 = ssd_s[sq]
        half = SSD_W // SSD_GROUPS
        ys = []
        for g in range(SSD_GROUPS):
            xbd = _block_diag(xdt[:, g * half:(g + 1) * half], m_heads[...])
            ys.append(_dot(m[:, g * half:(g + 1) * half], xbd))
        y = jnp.concatenate(ys, axis=-1)
        y = y + _dot(cmb, s2.astype(BF16)) * jnp.exp(acum)
        y = y + dskip_ref[...] * xs
        a_last = acum[C - 1:C, :]
        sd = jnp.exp(a_last - acum)
        contrib_s = _dot(bm.astype(BF16), (xdt * sd).astype(BF16), _TN)
        yield
        keep_s = _vdiv(_iota(s2.shape, 0), SSD_STATE) == _vdiv(_iota(s2.shape, 1), half)
        ssd_s[sq] = jnp.where(keep_s, s2 * jnp.exp(a_last) + contrib_s, 0.0)
        yz = y * _silu(seg("sz", SSD_W))
        outs = []
        for g in range(SSD_GROUPS):
            blk = yz[:, g * half:(g + 1) * half]
            ms = jnp.mean(blk * blk, axis=-1, keepdims=True)
            outs.append(blk * lax.rsqrt(ms + NORM_EPS))
        h_ref[sq, :, RET_W:RET_W + SSD_W] = (jnp.concatenate(outs, axis=-1) * ssdw_ref[...]).astype(BF16)

        yield

        gq = seg("gq", GLA_QK) * (GLA_DK ** -0.5)
        gkk = seg("gk", GLA_QK)
        gv = seg("gv", GLA_W)
        gkl = _dot(seg("ggk", 128).astype(BF16), wgk_ref[...].astype(BF16)) + bgk_ref[...]
        yield
        log_a = -_softplus(-gkl) * (1.0 / GLA_TEMP)
        b = _cumsum_rows(tri, log_a)
        yield
        q_t = (gq * jnp.exp(b)).astype(BF16)
        k_t = gkk * jnp.exp(-b)
        kbd_g = _block_diag(k_t, m_gla[...])
        att = jnp.where(causal4, _dot(q_t, kbd_g, _NT), 0.0)
        yield
        vbd_g = _block_diag(gv, m_heads[...])
        st = gla_s[sq]
        og = _dot(att.astype(BF16), vbd_g) + _dot(q_t, st.astype(BF16), _NT)
        b_last = b[C - 1:C, :]
        kd = (gkk * jnp.exp(b_last - b)).astype(BF16)
        contrib_g = _dot(gv.astype(BF16), kd, _TN)
        yield
        keep_g = _vdiv(_iota(st.shape, 0), GLA_DV) == _vdiv(_iota(st.shape, 1), GLA_DK)
        gla_s[sq] = jnp.where(keep_g, st * jnp.exp(b_last) + contrib_g, 0.0)
        ms = _seg_sum64(og * og) * (1.0 / GLA_DV)
        og = og * lax.rsqrt(ms + NORM_EPS) * glaw_ref[...]
        h_ref[sq, :, RET_W + SSD_W:D_MIX] = (_silu(seg("gg", GLA_W)) * og).astype(BF16)

    def step(p_read, p_write):
        xb = xn_ref[...].reshape(MIX_G * C, D_MODEL).astype(BF16)
        edges = list(range(0, NP, PROJ_SLAB)) + [NP]
        slabs = list(zip(edges[:-1], edges[1:]))

        live = [chunk_of(sq, p_read) for sq in range(MIX_G)]
        while live or slabs:
            live = [g for g in live if next(g, "done") != "done"]
            if slabs:
                lo, hi = slabs.pop(0)
                p_write[:, lo:hi] = _dot(xb, w_ref[:, lo:hi])

    @pl.when(lax.rem(i, 2) == 0)
    def _():
        step(proj_b, proj_a)

    @pl.when(lax.rem(i, 2) == 1)
    def _():
        step(proj_a, proj_b)


def _mixproj(x3, w_p, cos_t, sin_t, params):
    batch, seq, _ = x3.shape
    assert batch % MIX_G == 0 and seq % CHUNK == 0
    cps = seq // CHUNK
    n = (batch // MIX_G) * cps

    def cur_map(i):
        c = jnp.clip(i - 1, 0, n - 1)
        return (c // cps, c % cps, 0)

    def next_map(i):
        c = jnp.minimum(i, n - 1)
        return (c // cps, c % cps, 0)

    const = lambda i: (0, 0)
    specs = [pl.BlockSpec((MIX_G, CHUNK, D_MODEL), next_map),
             pl.BlockSpec((D_MODEL, NP), const),
             pl.BlockSpec((MIX_G, CHUNK, ROPE_W), cur_map),
             pl.BlockSpec((MIX_G, CHUNK, ROPE_W), cur_map)]
    specs += [pl.BlockSpec(p.shape, const) for p in params]
    return pl.pallas_call(
        functools.partial(_mixproj_kernel, chunks_per_seq=cps),
        grid=(n + 1,),
        in_specs=specs,
        out_specs=pl.BlockSpec((MIX_G, CHUNK, D_MIX), cur_map),
        out_shape=jax.ShapeDtypeStruct((batch, seq, D_MIX), BF16),
        scratch_shapes=[pltpu.VMEM((MIX_G * CHUNK, NP), F32),
                        pltpu.VMEM((MIX_G * CHUNK, NP), F32),
                        pltpu.VMEM((MIX_G, 256, 256), F32),
                        pltpu.VMEM((MIX_G, SSD_BC, SSD_W), F32),
                        pltpu.VMEM((MIX_G, GLA_W, GLA_QK), F32),
                        pltpu.VMEM((MIX_G, CHUNK + 8, SSD_XBC), F32),
                        pltpu.VMEM((RET_HEADS * CHUNK, 256), BF16),
                        pltpu.VMEM((SSD_HEADS * CHUNK, SSD_BC), BF16),
                        pltpu.VMEM((GLA_HEADS * CHUNK, GLA_QK), BF16),
                        pltpu.VMEM((LANES, SSD_W), BF16)],
        compiler_params=pltpu.CompilerParams(dimension_semantics=("arbitrary",),
                                             vmem_limit_bytes=VMEM_LIMIT),
        name="inproj_mixer",
    )(x3, w_p, cos_t, sin_t, *params)


def _layer_norm(y, g, b):
    mu = jnp.mean(y, axis=-1, keepdims=True)
    yc = y - mu
    var = jnp.mean(yc * yc, axis=-1, keepdims=True)
    return yc * lax.rsqrt(var + LN_EPS) * g + b


def _post_kernel(h_ref, x_ref, wout_ref, g_ref, b_ref, wrh_ref, wrl_ref, br_ref,
                 x1_ref, x1p_ref, mi_ref, mf_ref, cnt_ref, carry):
    sub = TM_POST // POST_SPLIT

    @pl.when(pl.program_id(0) == 0)
    def _():
        carry[...] = jnp.zeros_like(carry)

    lane_i = _iota((sub, LANES), 1)
    lane = lane_i.astype(F32)
    found = {}

    def sub_tile(part):
        rows = slice(part * sub, (part + 1) * sub)
        mix = _dot(h_ref[rows, :], wout_ref[...])
        yield
        x1 = _layer_norm(DEEPNORM_ALPHA * x_ref[rows, :] + mix, g_ref[...], b_ref[...])
        x1_ref[rows, :] = x1
        x1p_ref[rows, :] = _pack_rows(x1)
        x_hi = x1.astype(BF16)
        x_lo = (x1 - x_hi.astype(F32)).astype(BF16)
        logits = (_dot(x_hi, wrh_ref[...]) + _dot(x_lo, wrh_ref[...]) + _dot(x_hi, wrl_ref[...])
                  + br_ref[...])
        yield
        work = logits
        vals, idxs = [], []
        multi = jnp.zeros((sub, LANES), F32)
        for _ in range(TOP_K):
            m = jnp.max(work, axis=-1, keepdims=True)
            idx = jnp.min(jnp.where(work == m, lane, float(LANES)), axis=-1, keepdims=True)
            hit = lane == idx
            multi = multi + hit.astype(F32)
            work = jnp.where(hit, -jnp.inf, work)
            vals.append(m)
            idxs.append(idx)
            yield
        exps = [jnp.exp(v - vals[0]) for v in vals]
        denom = exps[0] + exps[1] + exps[2] + exps[3]
        gates = [e / denom for e in exps]
        before = (_iota((sub, sub), 0) > _iota((sub, sub), 1)).astype(BF16)
        found[part] = (idxs, gates, _dot(before, multi.astype(BF16)), multi)

    live = [sub_tile(part) for part in range(POST_SPLIT)]
    while live:
        live = [g for g in live if next(g, "done") != "done"]

    base = carry[...]
    for part in range(POST_SPLIT):
        idxs, gates, prior_local, multi = found[part]
        prior = prior_local + base
        mi = jnp.zeros((sub, LANES), F32)
        mf = jnp.zeros((sub, LANES), F32)
        for kk in range(TOP_K):
            rank = jnp.sum(jnp.where(lane == idxs[kk], prior, 0.0), axis=-1, keepdims=True)
            mi = jnp.where(lane_i == kk, idxs[kk], mi)
            mi = jnp.where(lane_i == TOP_K + kk, rank, mi)
            mf = jnp.where(lane_i == kk, gates[kk], mf)
        mi_ref[:, part * sub:(part + 1) * sub] = jnp.transpose(mi)[0:2 * TOP_K, :].astype(jnp.int32)
        mf_ref[part * sub:(part + 1) * sub, :] = mf
        base = base + jnp.sum(multi, axis=0, keepdims=True)
    carry[...] = base
    cnt_ref[...] = jnp.broadcast_to(base, cnt_ref.shape)


def _post(h, x2, w_out_b, ln_g, ln_b, wr_hi, wr_lo, br_p):
    t = x2.shape[0]
    tm = TM_POST
    row = lambda i: (i, 0)
    const = lambda i: (0, 0)
    return pl.pallas_call(
        _post_kernel,
        grid=(t // tm,),
        in_specs=[pl.BlockSpec((tm, D_MIX), row), pl.BlockSpec((tm, D_MODEL), row),
                  pl.BlockSpec((D_MIX, D_MODEL), const), pl.BlockSpec((1, D_MODEL), const),
                  pl.BlockSpec((1, D_MODEL), const), pl.BlockSpec((D_MODEL, LANES), const),
                  pl.BlockSpec((D_MODEL, LANES), const), pl.BlockSpec((1, LANES), const)],
        out_specs=[pl.BlockSpec((tm, D_MODEL), row), pl.BlockSpec((tm, ROW_WORDS), row),
                   pl.BlockSpec((2 * TOP_K, tm), lambda i: (0, i)), pl.BlockSpec((tm, LANES), row),
                   pl.BlockSpec((8, LANES), const)],
        out_shape=[jax.ShapeDtypeStruct((t, D_MODEL), F32), jax.ShapeDtypeStruct((t, ROW_WORDS), jnp.int32),
                   jax.ShapeDtypeStruct((2 * TOP_K, t), jnp.int32), jax.ShapeDtypeStruct((t, LANES), F32),
                   jax.ShapeDtypeStruct((8, LANES), F32)],
        scratch_shapes=[pltpu.VMEM((1, LANES), F32)],
        compiler_params=pltpu.CompilerParams(dimension_semantics=("arbitrary",),
                                             vmem_limit_bytes=VMEM_LIMIT),
        name="outproj_ln_router",
    )(h, x2, w_out_b, ln_g, ln_b, wr_hi, wr_lo, br_p)


def _ffn_kernel(be_ref, nv_ref, par_ref, ahead_ref, nx_ref, x_ref, wg_ref, bg_ref, wu_ref, bu_ref, wd_ref, bd_ref,
                o_ref, ga, ua, da, gb, ub, db):
    del nx_ref
    wa, wb = (ga, ua, da), (gb, ub, db)
    i = pl.program_id(0)
    b = jnp.clip(i - 1, 0, be_ref.shape[0] - 1)
    valid = jnp.logical_and(i >= 1, b < nv_ref[0])
    parity = par_ref[b]
    cast_ahead = ahead_ref[b]
    f32_weights = (wg_ref, wu_ref, wd_ref)

    def cast_into(dst, j):
        dst[j][...] = f32_weights[j][0, 0].astype(BF16)

    @pl.when(i == 0)
    def _():
        for j in range(3):
            cast_into(wa, j)

    def block(cur, nxt):
        x = _unpack_rows(x_ref[...]).astype(BF16)
        if nxt is not None:
            cast_into(nxt, 0)
        hg = jnp.minimum(_dot(x, cur[0][...]) + bg_ref[0, 0], SWIGLU_LIMIT)
        if nxt is not None:
            cast_into(nxt, 1)
        hu = jnp.clip(_dot(x, cur[1][...]) + bu_ref[0, 0], -SWIGLU_LIMIT, SWIGLU_LIMIT)
        hh = (hu + 1.0) * hg * jax.nn.sigmoid(SWIGLU_ALPHA * hg)
        if nxt is not None:
            cast_into(nxt, 2)
        o_ref[...] = _pack_rows(_dot(hh.astype(BF16), cur[2][...]) + bd_ref[0, 0])

    for par, cur, other in ((0, wa, wb), (1, wb, wa)):
        for ahead in (0, 1):
            @pl.when(valid & (parity == par) & (cast_ahead == ahead))
            def _(cur=cur, nxt=other if ahead else None):
                block(cur, nxt)


def _ffn(layer, block_expert, n_valid, end_padded, xin, wg, bg, wu, bu, wd, bd):
    cap = xin.shape[0]
    nb = cap // BM

    ids = jnp.arange(nb, dtype=jnp.int32)
    first = (ids == 0) | (block_expert != jnp.roll(block_expert, 1))
    parity = (jnp.cumsum(first.astype(jnp.int32)) - 1) & 1
    after_run = jnp.sum(jnp.where(block_expert[:, None] == jnp.arange(N_EXPERTS, dtype=jnp.int32)[None, :],
                                  end_padded[None, :], 0), axis=1) // BM
    has_next = after_run < n_valid[0]
    next_expert = jnp.where(has_next, block_expert[jnp.minimum(after_run, nb - 1)], block_expert)
    ahead = (first & has_next).astype(jnp.int32)

    def blk(i, nv):
        return jnp.clip(i - 1, 0, jnp.maximum(nv[0] - 1, 0))

    row = lambda i, be, nv, pa, ah, nx: (blk(i, nv), 0)
    bmap = lambda i, be, nv, pa, ah, nx: (layer, be[blk(i, nv)], 0, 0)
    wmap = lambda i, be, nv, pa, ah, nx: (layer, jnp.where(i == 0, be[0], nx[blk(i, nv)]), 0, 0)
    grid_spec = pltpu.PrefetchScalarGridSpec(
        num_scalar_prefetch=5,
        grid=(nb + 1,),
        in_specs=[pl.BlockSpec((BM, ROW_WORDS), row),
                  pl.BlockSpec((1, 1, D_MODEL, D_FF), wmap), pl.BlockSpec((1, 1, 1, D_FF), bmap),
                  pl.BlockSpec((1, 1, D_MODEL, D_FF), wmap), pl.BlockSpec((1, 1, 1, D_FF), bmap),
                  pl.BlockSpec((1, 1, D_FF, D_MODEL), wmap), pl.BlockSpec((1, 1, 1, D_MODEL), bmap)],
        out_specs=pl.BlockSpec((BM, ROW_WORDS), row),
        scratch_shapes=[pltpu.VMEM((D_MODEL, D_FF), BF16), pltpu.VMEM((D_MODEL, D_FF), BF16),
                        pltpu.VMEM((D_FF, D_MODEL), BF16)] * 2,
    )
    return pl.pallas_call(
        _ffn_kernel,
        grid_spec=grid_spec,
        out_shape=jax.ShapeDtypeStruct((cap, ROW_WORDS), jnp.int32),
        compiler_params=pltpu.CompilerParams(dimension_semantics=("arbitrary",),
                                             vmem_limit_bytes=VMEM_LIMIT),
        name="expert_ffn",
    )(block_expert, n_valid, parity, ahead, next_expert, xin, wg, bg, wu, bu, wd, bd)


def _sc_gather(table, idx3):
    nw, n_chunks, ch = idx3.shape
    width = table.shape[1]
    per_worker = n_chunks * ch
    mesh = plsc.VectorSubcoreMesh(core_axis_name="c", subcore_axis_name="s")
    n_cores = mesh.num_cores
    assert nw == n_cores * mesh.num_subcores and n_chunks % 2 == 0 and ch == SC_CHUNK

    def body(table_hbm, idx_hbm, out_hbm, idx_v, rows0, rows1, sem_g0, sem_g1, sem_w0, sem_w1):
        wid = lax.axis_index("s") * n_cores + lax.axis_index("c")
        base = wid * per_worker
        pltpu.sync_copy(idx_hbm.at[wid], idx_v)

        @pl.loop(0, n_chunks, step=2)
        def _(c):
            g0 = pltpu.async_copy(table_hbm.at[idx_v.at[c]], rows0, sem_g0)
            g1 = pltpu.async_copy(table_hbm.at[idx_v.at[c + 1]], rows1, sem_g1)
            g0.wait()
            w0 = pltpu.async_copy(rows0, out_hbm.at[pl.ds(base + c * ch, ch)], sem_w0)
            g1.wait()
            w1 = pltpu.async_copy(rows1, out_hbm.at[pl.ds(base + (c + 1) * ch, ch)], sem_w1)
            w0.wait()
            w1.wait()

    return pl.kernel(
        body,
        out_type=jax.ShapeDtypeStruct((nw * per_worker, width), table.dtype),
        mesh=mesh,
        scratch_types=[pltpu.VMEM((n_chunks, ch), jnp.int32),
                       pltpu.VMEM((ch, width), table.dtype), pltpu.VMEM((ch, width), table.dtype),
                       pltpu.SemaphoreType.DMA, pltpu.SemaphoreType.DMA,
                       pltpu.SemaphoreType.DMA, pltpu.SemaphoreType.DMA],
        name="sc_row_gather",
    )(table, idx3)


def _sc_scatter(rows, idx3, n_out):
    nw, n_lists, ch = idx3.shape
    n_chunks = n_lists // TOP_K
    width = rows.shape[1]
    per_worker = n_chunks * ch
    mesh = plsc.VectorSubcoreMesh(core_axis_name="c", subcore_axis_name="s")
    n_cores = mesh.num_cores
    assert nw == n_cores * mesh.num_subcores and n_chunks % 2 == 0 and ch == SC_CHUNK
    assert nw * per_worker == rows.shape[0]

    def body(rows_hbm, idx_hbm, out_hbm, idx_v, buf0, buf1, sem_r0, sem_r1, sem_w0, sem_w1):
        wid = lax.axis_index("s") * n_cores + lax.axis_index("c")
        base = wid * per_worker
        pltpu.sync_copy(idx_hbm.at[wid], idx_v)

        @pl.loop(0, n_chunks, step=2)
        def _(c):
            r0 = pltpu.async_copy(rows_hbm.at[pl.ds(base + c * ch, ch)], buf0, sem_r0)
            r1 = pltpu.async_copy(rows_hbm.at[pl.ds(base + (c + 1) * ch, ch)], buf1, sem_r1)
            r0.wait()
            w0 = [pltpu.async_copy(buf0, out_hbm.at[idx_v.at[c * TOP_K + k]], sem_w0) for k in range(TOP_K)]
            r1.wait()
            w1 = [pltpu.async_copy(buf1, out_hbm.at[idx_v.at[(c + 1) * TOP_K + k]], sem_w1) for k in range(TOP_K)]
            for w in w0 + w1:
                w.wait()

    return pl.kernel(
        body,
        out_type=jax.ShapeDtypeStruct((n_out, width), rows.dtype),
        mesh=mesh,
        scratch_types=[pltpu.VMEM((n_lists, ch), jnp.int32),
                       pltpu.VMEM((ch, width), rows.dtype), pltpu.VMEM((ch, width), rows.dtype),
                       pltpu.SemaphoreType.DMA, pltpu.SemaphoreType.DMA,
                       pltpu.SemaphoreType.DMA, pltpu.SemaphoreType.DMA],
        name="sc_row_scatter",
    )(rows, idx3)


def _combine_kernel(x_ref, rows_ref, mf_ref, g_ref, b_ref, o_ref):
    mf = mf_ref[...]
    y = DEEPNORM_ALPHA * x_ref[...]
    for kk in range(TOP_K):
        y = y + mf[:, kk:kk + 1] * _unpack_rows(rows_ref[kk])
    o_ref[...] = _layer_norm(y, g_ref[...], b_ref[...])


def _combine(x1, rows, mf, ln_g, ln_b):
    t = x1.shape[0]
    tm = TM_COMB
    row = lambda i: (i, 0)
    const = lambda i: (0, 0)
    return pl.pallas_call(
        _combine_kernel,
        grid=(t // tm,),
        in_specs=[pl.BlockSpec((tm, D_MODEL), row),
                  pl.BlockSpec((TOP_K, tm, ROW_WORDS), lambda i: (0, i, 0)),
                  pl.BlockSpec((tm, LANES), row),
                  pl.BlockSpec((1, D_MODEL), const), pl.BlockSpec((1, D_MODEL), const)],
        out_specs=pl.BlockSpec((tm, D_MODEL), row),
        out_shape=jax.ShapeDtypeStruct((t, D_MODEL), F32),
        compiler_params=pltpu.CompilerParams(dimension_semantics=("arbitrary",),
                                             vmem_limit_bytes=VMEM_LIMIT),
        name="combine_ln",
    )(x1, rows, mf, ln_g, ln_b)


def _relayout_w_in(w):
    widths = (256, 256, 256, 256, SSD_W, SSD_XBC, SSD_HEADS, GLA_QK, GLA_QK, GLA_W, GLA_RANK, GLA_W)
    offs = [0]
    for wd in widths:
        offs.append(offs[-1] + wd)
    parts = [w[:, offs[i]:offs[i + 1]] for i in range(len(widths))]
    parts[6] = jnp.pad(parts[6], ((0, 0), (0, LANES - SSD_HEADS)))
    parts[10] = jnp.pad(parts[10], ((0, 0), (0, LANES - GLA_RANK)))
    return jnp.concatenate(parts, axis=1).astype(BF16)


def _rep_heads(p):
    return jnp.repeat(p, SSD_HEAD_DIM)[None, :]


def _pad_heads(p):
    return jnp.pad(p, (0, LANES - SSD_HEADS))[None, :]


def kernel(x, positions, w_in, w_out, ret_norm_w, ssd_conv_w, ssd_conv_b, ssd_dt_bias, ssd_a_log, ssd_d,
           ssd_norm_w, gla_w_gk2, gla_b_gk2, gla_norm_w, ln1_g, ln1_b, w_router, b_router, w_gate, b_gate,
           w_up, b_up, w_down, b_down, ln2_g, ln2_b):
    batch, seq, d = x.shape
    t = batch * seq
    depth = w_in.shape[0]
    assert d == D_MODEL and t % TM_POST == 0 and t % TM_COMB == 0 and t % TM_ROPE == 0
    n_assign = t * TOP_K
    nb = n_assign // BM + N_EXPERTS
    cap = nb * BM

    cos_t, sin_t = _rope_tables(positions.reshape(t, 1))
    cos_t = cos_t.reshape(batch, seq, -1)
    sin_t = sin_t.reshape(batch, seq, -1)
    x2 = x.reshape(t, d)

    for l in range(depth):
        params = (ret_norm_w[l][None, :], ssd_conv_w[l], ssd_conv_b[l][None, :], _pad_heads(ssd_dt_bias[l]),
                  _pad_heads(ssd_a_log[l]), _rep_heads(ssd_d[l]), ssd_norm_w[l][None, :],
                  jnp.pad(gla_w_gk2[l], ((0, LANES - GLA_RANK), (0, 0))), gla_b_gk2[l][None, :],
                  gla_norm_w[l][None, :])
        h = _mixproj(x2.reshape(batch, seq, d), _relayout_w_in(w_in[l]), cos_t, sin_t, params).reshape(t, D_MIX)

        wr_p = jnp.pad(w_router[l], ((0, 0), (0, LANES - N_EXPERTS)))
        br_p = jnp.pad(b_router[l], (0, LANES - N_EXPERTS), constant_values=NEG_BIG)[None, :]
        wr_hi = wr_p.astype(BF16)
        wr_lo = (wr_p - wr_hi.astype(F32)).astype(BF16)
        x1, x1p, mi, mf, cnt = _post(h, x2, w_out[l].astype(BF16), ln1_g[l][None, :], ln1_b[l][None, :],
                                     wr_hi, wr_lo, br_p)

        counts = cnt[0, :N_EXPERTS].astype(jnp.int32)
        padded = (counts + BM - 1) // BM * BM
        end_padded = jnp.cumsum(padded)
        start_padded = end_padded - padded
        top_idx = mi[:TOP_K]
        start_of = jnp.sum(jnp.where(top_idx[None] == jnp.arange(N_EXPERTS, dtype=jnp.int32)[:, None, None],
                                     start_padded[:, None, None], 0), axis=0)
        dest = start_of + mi[TOP_K:]
        block_start = jnp.arange(nb, dtype=jnp.int32) * BM
        block_expert = jnp.minimum(jnp.sum((end_padded[None, :] <= block_start[:, None]).astype(jnp.int32), axis=1),
                                   N_EXPERTS - 1)
        n_valid = (end_padded[-1:] // BM).astype(jnp.int32)

        scatter_idx = dest.reshape(TOP_K, SC_WORKERS, -1, SC_CHUNK).transpose(1, 2, 0, 3)
        xin = _sc_scatter(x1p, scatter_idx.reshape(SC_WORKERS, -1, SC_CHUNK), cap)
        yb = _ffn(l, block_expert, n_valid, end_padded, xin, w_gate, b_gate[:, :, None, :], w_up, b_up[:, :, None, :],
                  w_down, b_down[:, :, None, :])
        rows = _sc_gather(yb, dest.reshape(SC_WORKERS, -1, SC_CHUNK))
        x2 = _combine(x1, rows.reshape(TOP_K, t, ROW_WORDS), mf, ln2_g[l][None, :], ln2_b[l][None, :])
    return x2.reshape(batch, seq, d)
```

```python
import functools
import math

import jax
import jax.numpy as jnp
from jax import lax
from jax.experimental import pallas as pl
from jax.experimental.pallas import tpu as pltpu
from jax.experimental.pallas import tpu_sc as plsc

F32 = jnp.float32
BF16 = jnp.bfloat16

D_MODEL = 1024
CHUNK = 64
RET_HEADS, RET_DK, RET_DV = 4, 64, 64
RET_W = RET_HEADS * RET_DV
SSD_HEADS, SSD_HEAD_DIM, SSD_STATE, SSD_GROUPS, SSD_CONV = 8, 64, 64, 2, 4
SSD_W = SSD_HEADS * SSD_HEAD_DIM
SSD_BC = SSD_GROUPS * SSD_STATE
SSD_XBC = SSD_W + 2 * SSD_BC
GLA_HEADS, GLA_DK, GLA_DV, GLA_RANK, GLA_TEMP = 4, 32, 64, 16, 16.0
GLA_QK = GLA_HEADS * GLA_DK
GLA_W = GLA_HEADS * GLA_DV
D_MIX = RET_W + SSD_W + GLA_W
N_EXPERTS, TOP_K, D_FF = 32, 4, 1024
SWIGLU_LIMIT, SWIGLU_ALPHA = 7.0, 1.702
ROPE_BASE = 10000.0
LN_EPS, NORM_EPS = 1e-5, 1e-6
DEPTH = 2
DEEPNORM_ALPHA = (2.0 * DEPTH) ** 0.25

LANES = 128
ROPE_W = LANES
NEG_BIG = -1e30
VMEM_LIMIT = 56 * 1024 * 1024

_SEGS = (("rq", 256), ("rk", 256), ("rv", 256), ("rg", 256), ("sz", SSD_W), ("sxbc", SSD_XBC),
         ("sdt", 128), ("gq", 128), ("gk", 128), ("gv", 256), ("ggk", 128), ("gg", 256))
COL = {}
_off = 0
for _n, _w in _SEGS:
    COL[_n] = _off
    _off += _w
NP = _off

MIX_G = 8
PROJ_SLAB = 512
TM_POST = 1024
POST_SPLIT = 4
TM_COMB = 1024
TM_ROPE = 2048
BM = 768
ROW_WORDS = D_MODEL // 2
SC_WORKERS = 32
SC_CHUNK = 64


def _dot(a, b, dims=(((1,), (0,)), ((), ())), precision=None):
    return lax.dot_general(a, b, dims, precision=precision, preferred_element_type=F32)


_NT = (((1,), (1,)), ((), ()))
_TN = (((0,), (0,)), ((), ()))


def _iota(shape, dim):
    return lax.broadcasted_iota(jnp.int32, shape, dim)


def _vdiv(x, n):
    assert n & (n - 1) == 0
    return lax.shift_right_logical(x, n.bit_length() - 1)


def _vmod(x, n):
    assert n & (n - 1) == 0
    return jnp.bitwise_and(x, n - 1)


def _silu(x):
    return x * jax.nn.sigmoid(x)


def _softplus(x):
    return jnp.maximum(x, 0.0) + jnp.log(1.0 + jnp.exp(-jnp.abs(x)))


def _pack_rows(x):
    w = x.shape[1] // 2
    lo = lax.bitcast_convert_type(x[:, :w].astype(BF16).astype(F32), jnp.uint32)
    hi = lax.bitcast_convert_type(x[:, w:].astype(BF16).astype(F32), jnp.uint32)
    return lax.bitcast_convert_type(lax.shift_right_logical(lo, jnp.uint32(16)) | hi, jnp.int32)


def _unpack_rows(words):
    u = lax.bitcast_convert_type(words, jnp.uint32)
    a = lax.bitcast_convert_type(lax.shift_left(u, jnp.uint32(16)), F32)
    b = lax.bitcast_convert_type(u & jnp.uint32(0xFFFF0000), F32)
    return jnp.concatenate([a, b], axis=-1)


def _seg_sum64(x):
    first = _iota((1, LANES), 1) < 64
    outs = []
    for j in range(x.shape[-1] // LANES):
        blk = x[:, j * LANES:(j + 1) * LANES]
        lo = jnp.sum(jnp.where(first, blk, 0.0), axis=-1, keepdims=True)
        hi = jnp.sum(jnp.where(first, 0.0, blk), axis=-1, keepdims=True)
        outs.append(jnp.where(first, lo, hi))
    return jnp.concatenate(outs, axis=-1)


def _block_mask(shape, row_blk, col_blk):
    keep = _vdiv(_iota(shape, 0), row_blk) == _vdiv(_iota(shape, 1), col_blk)
    return jnp.where(keep, 1.0, 0.0).astype(BF16)


def _block_diag(x, mask):
    reps = mask.shape[0] // x.shape[0]
    return jnp.concatenate([x.astype(BF16)] * reps, axis=0) * mask


def _expand_heads(x, expand):
    hi = x.astype(BF16)
    r1 = x - hi.astype(F32)
    mid = r1.astype(BF16)
    lo = (r1 - mid.astype(F32)).astype(BF16)
    return _dot(hi, expand) + _dot(mid, expand) + _dot(lo, expand)


def _cumsum_rows(tri, x):
    hi = x.astype(BF16)
    lo = (x - hi.astype(F32)).astype(BF16)
    return _dot(tri, hi) + _dot(tri, lo)


def _rope_kernel(pos_ref, cos_ref, sin_ref):
    lane = _iota((1, ROPE_W), 1)
    half = RET_DK // 2
    k = _vmod(lane, half).astype(F32)
    inv_freq = jnp.exp(k * (-math.log(ROPE_BASE) / half))
    ang = pos_ref[...].astype(F32) * inv_freq
    first = _vmod(lane, RET_DK) < half
    cos_ref[...] = jnp.cos(ang)
    sin_ref[...] = jnp.where(first, -1.0, 1.0) * jnp.sin(ang)


def _rope_tables(pos_col):
    t = pos_col.shape[0]
    tm = TM_ROPE
    w = ROPE_W
    return pl.pallas_call(
        _rope_kernel,
        grid=(t // tm,),
        in_specs=[pl.BlockSpec((tm, 1), lambda i: (i, 0))],
        out_specs=[pl.BlockSpec((tm, w), lambda i: (i, 0))] * 2,
        out_shape=[jax.ShapeDtypeStruct((t, w), F32)] * 2,
        compiler_params=pltpu.CompilerParams(dimension_semantics=("arbitrary",)),
        name="rope_tables",
    )(pos_col)


def _mixproj_kernel(xn_ref, w_ref, cos_ref, sin_ref, retw_ref, convw_ref, convb_ref, dtb_ref, alog_ref,
                    dskip_ref, ssdw_ref, wgk_ref, bgk_ref, glaw_ref, h_ref,
                    proj_a, proj_b, ret_s, ssd_s, gla_s, stage, m_heads, m_groups, m_gla, m_expand, *,
                    chunks_per_seq):
    C = CHUNK
    i = pl.program_id(0)
    cur = jnp.maximum(i - 1, 0)

    @pl.when(i == 0)
    def _():
        m_heads[...] = _block_mask(m_heads.shape, C, 64)
        m_groups[...] = _block_mask(m_groups.shape, C * SSD_HEADS // SSD_GROUPS, SSD_STATE)
        m_gla[...] = _block_mask(m_gla.shape, C, GLA_DK)
        m_expand[...] = _block_mask(m_expand.shape, 1, SSD_HEAD_DIM)
        proj_b[...] = jnp.zeros_like(proj_b)

    @pl.when(lax.rem(cur, chunks_per_seq) == 0)
    def _():
        ret_s[...] = jnp.zeros_like(ret_s)
        ssd_s[...] = jnp.zeros_like(ssd_s)
        gla_s[...] = jnp.zeros_like(gla_s)
        for g in range(MIX_G):
            stage[g, 0:8, :] = jnp.zeros((8, SSD_XBC), F32)

    def conv_act(sq, pref):
        stage[sq, 8:8 + C, :] = pref[sq * C:(sq + 1) * C, COL["sxbc"]:COL["sxbc"] + SSD_XBC]
        acc = convb_ref[...] + convw_ref[0:1, :] * stage[sq, 5:5 + C, :]
        for j in range(1, SSD_CONV):
            acc = acc + convw_ref[j:j + 1, :] * stage[sq, 5 + j:5 + j + C, :]
        stage[sq, 0:8, :] = stage[sq, C:C + 8, :]
        return _silu(acc)

    lane256 = _iota((1, 256), 1)
    head = _vdiv(lane256, 64).astype(F32)
    log_gamma = jnp.log(1.0 - jnp.exp((-5.0 - head) * math.log(2.0)))
    row = _iota((C, 1), 0).astype(F32)
    dist = row - _vmod(lane256, 64).astype(F32)
    ret_intra = jnp.where(dist >= 0, jnp.exp(log_gamma * jnp.maximum(dist, 0.0)), 0.0)
    ret_qdec = jnp.exp(log_gamma * (row + 1.0))
    ret_kdec = jnp.exp(log_gamma * (C - 1.0 - row))
    ret_cdec = jnp.exp(log_gamma * C)
    first_half = _vmod(lane256, RET_DK) < (RET_DK // 2)

    tri = jnp.where(_iota((C, C), 0) >= _iota((C, C), 1), 1.0, 0.0).astype(BF16)
    causal4 = _iota((C, 256), 0) >= _vmod(_iota((C, 256), 1), 64)
    causal8 = _iota((C, 512), 0) >= _vmod(_iota((C, 512), 1), 64)
    eye8 = _iota((C, 512), 0) == _vmod(_iota((C, 512), 1), 64)

    a_neg = -jnp.exp(alog_ref[...])

    def rot(t, cos, sin):
        sw = jnp.where(first_half, pltpu.roll(t, 256 - 32, 1), pltpu.roll(t, 32, 1))
        return t * cos + sw * sin

    def chunk_of(sq, pref):
        def seg(name, width):
            return pref[sq * C:(sq + 1) * C, COL[name]:COL[name] + width]

        xact = conv_act(sq, pref)
        yield

        reps = RET_HEADS * RET_DK // ROPE_W
        cos = jnp.concatenate([cos_ref[sq]] * reps, axis=-1)
        sin = jnp.concatenate([sin_ref[sq]] * reps, axis=-1)
        q = rot(seg("rq", 256), cos, sin)
        k = rot(seg("rk", 256), cos, sin) * (RET_DK ** -0.5)
        v = seg("rv", 256)
        vb = v.astype(BF16)
        kbd = _block_diag(k, m_heads[...])
        scores = _dot(q.astype(BF16), kbd, _NT) * ret_intra
        vbd = _block_diag(v, m_heads[...])
        yield
        s_prev = ret_s[sq]
        o = _dot(scores.astype(BF16), vbd) + _dot((q * ret_qdec).astype(BF16), s_prev.astype(BF16))
        contrib = _dot((k * ret_kdec).astype(BF16), vb, _TN)
        yield
        keep = _vdiv(_iota((256, 256), 0), RET_DK) == _vdiv(_iota((256, 256), 1), RET_DV)
        ret_s[sq] = jnp.where(keep, ret_cdec * s_prev + contrib, 0.0)
        mu = _seg_sum64(o) * (1.0 / RET_DV)
        oc = o - mu
        var = _seg_sum64(oc * oc) * (1.0 / RET_DV)
        o = oc * lax.rsqrt(var + LN_EPS) * retw_ref[...]
        h_ref[sq, :, 0:RET_W] = (_silu(seg("rg", 256)) * o).astype(BF16)
        yield

        xs = xact[:, 0:SSD_W]
        bm = xact[:, SSD_W:SSD_W + SSD_BC]
        cm = xact[:, SSD_W + SSD_BC:SSD_XBC]
        cmb = cm.astype(BF16)
        dt_c = _softplus(seg("sdt", LANES) + dtb_ref[...])
        acum_c = _cumsum_rows(tri, dt_c * a_neg)
        both = _expand_heads(jnp.concatenate([dt_c, acum_c], axis=0), m_expand[...])
        dt = both[0:C, :]
        acum = both[C:2 * C, :]
        yield
        arow = jnp.sum(jnp.where(eye8, acum, 0.0), axis=0, keepdims=True)
        decay = jnp.exp(jnp.where(causal8, acum - arow, NEG_BIG))
        b8 = _block_diag(bm, m_groups[...])
        cb = _dot(cmb, b8, _NT)
        yield
        m = (cb * decay).astype(BF16)
        xdt = xs * dt
        s2 = ssd_s[sq]
        half = SSD_W // SSD_GROUPS
        ys = []
        for g in range(SSD_GROUPS):
            xbd = _block_diag(xdt[:, g * half:(g + 1) * half], m_heads[...])
            ys.append(_dot(m[:, g * half:(g + 1) * half], xbd))
        y = jnp.concatenate(ys, axis=-1)
        y = y + _dot(cmb, s2.astype(BF16)) * jnp.exp(acum)
        y = y + dskip_ref[...] * xs
        a_last = acum[C - 1:C, :]
        sd = jnp.exp(a_last - acum)
        contrib_s = _dot(bm.astype(BF16), (xdt * sd).astype(BF16), _TN)
        yield
        keep_s = _vdiv(_iota(s2.shape, 0), SSD_STATE) == _vdiv(_iota(s2.shape, 1), half)
        ssd_s[sq] = jnp.where(keep_s, s2 * jnp.exp(a_last) + contrib_s, 0.0)
        yz = y * _silu(seg("sz", SSD_W))
        outs = []
        for g in range(SSD_GROUPS):
            blk = yz[:, g * half:(g + 1) * half]
            ms = jnp.mean(blk * blk, axis=-1, keepdims=True)
            outs.append(blk * lax.rsqrt(ms + NORM_EPS))
        h_ref[sq, :, RET_W:RET_W + SSD_W] = (jnp.concatenate(outs, axis=-1) * ssdw_ref[...]).astype(BF16)

        yield

        gq = seg("gq", GLA_QK) * (GLA_DK ** -0.5)
        gkk = seg("gk", GLA_QK)
        gv = seg("gv", GLA_W)
        gkl = _dot(seg("ggk", 128).astype(BF16), wgk_ref[...].astype(BF16)) + bgk_ref[...]
        yield
        log_a = -_softplus(-gkl) * (1.0 / GLA_TEMP)
        b = _cumsum_rows(tri, log_a)
        yield
        q_t = (gq * jnp.exp(b)).astype(BF16)
        k_t = gkk * jnp.exp(-b)
        kbd_g = _block_diag(k_t, m_gla[...])
        att = jnp.where(causal4, _dot(q_t, kbd_g, _NT), 0.0)
        yield
        vbd_g = _block_diag(gv, m_heads[...])
        st = gla_s[sq]
        og = _dot(att.astype(BF16), vbd_g) + _dot(q_t, st.astype(BF16), _NT)
        b_last = b[C - 1:C, :]
        kd = (gkk * jnp.exp(b_last - b)).astype(BF16)
        contrib_g = _dot(gv.astype(BF16), kd, _TN)
        yield
        keep_g = _vdiv(_iota(st.shape, 0), GLA_DV) == _vdiv(_iota(st.shape, 1), GLA_DK)
        gla_s[sq] = jnp.where(keep_g, st * jnp.exp(b_last) + contrib_g, 0.0)
        ms = _seg_sum64(og * og) * (1.0 / GLA_DV)
        og = og * lax.rsqrt(ms + NORM_EPS) * glaw_ref[...]
        h_ref[sq, :, RET_W + SSD_W:D_MIX] = (_silu(seg("gg", GLA_W)) * og).astype(BF16)

    def step(p_read, p_write):
        xb = xn_ref[...].reshape(MIX_G * C, D_MODEL).astype(BF16)
        edges = list(range(0, NP, PROJ_SLAB)) + [NP]
        slabs = list(zip(edges[:-1], edges[1:]))

        live = [chunk_of(sq, p_read) for sq in range(MIX_G)]
        while live or slabs:
            live = [g for g in live if next(g, "done") != "done"]
            if slabs:
                lo, hi = slabs.pop(0)
                p_write[:, lo:hi] = _dot(xb, w_ref[:, lo:hi])

    @pl.when(lax.rem(i, 2) == 0)
    def _():
        step(proj_b, proj_a)

    @pl.when(lax.rem(i, 2) == 1)
    def _():
        step(proj_a, proj_b)


def _mixproj(x3, w_p, cos_t, sin_t, params):
    batch, seq, _ = x3.shape
    assert batch % MIX_G == 0 and seq % CHUNK == 0
    cps = seq // CHUNK
    n = (batch // MIX_G) * cps

    def cur_map(i):
        c = jnp.clip(i - 1, 0, n - 1)
        return (c // cps, c % cps, 0)

    def next_map(i):
        c = jnp.minimum(i, n - 1)
        return (c // cps, c % cps, 0)

    const = lambda i: (0, 0)
    specs = [pl.BlockSpec((MIX_G, CHUNK, D_MODEL), next_map),
             pl.BlockSpec((D_MODEL, NP), const),
             pl.BlockSpec((MIX_G, CHUNK, ROPE_W), cur_map),
             pl.BlockSpec((MIX_G, CHUNK, ROPE_W), cur_map)]
    specs += [pl.BlockSpec(p.shape, const) for p in params]
    return pl.pallas_call(
        functools.partial(_mixproj_kernel, chunks_per_seq=cps),
        grid=(n + 1,),
        in_specs=specs,
        out_specs=pl.BlockSpec((MIX_G, CHUNK, D_MIX), cur_map),
        out_shape=jax.ShapeDtypeStruct((batch, seq, D_MIX), BF16),
        scratch_shapes=[pltpu.VMEM((MIX_G * CHUNK, NP), F32),
                        pltpu.VMEM((MIX_G * CHUNK, NP), F32),
                        pltpu.VMEM((MIX_G, 256, 256), F32),
                        pltpu.VMEM((MIX_G, SSD_BC, SSD_W), F32),
                        pltpu.VMEM((MIX_G, GLA_W, GLA_QK), F32),
                        pltpu.VMEM((MIX_G, CHUNK + 8, SSD_XBC), F32),
                        pltpu.VMEM((RET_HEADS * CHUNK, 256), BF16),
                        pltpu.VMEM((SSD_HEADS * CHUNK, SSD_BC), BF16),
                        pltpu.VMEM((GLA_HEADS * CHUNK, GLA_QK), BF16),
                        pltpu.VMEM((LANES, SSD_W), BF16)],
        compiler_params=pltpu.CompilerParams(dimension_semantics=("arbitrary",),
                                             vmem_limit_bytes=VMEM_LIMIT),
        name="inproj_mixer",
    )(x3, w_p, cos_t, sin_t, *params)


def _layer_norm(y, g, b):
    mu = jnp.mean(y, axis=-1, keepdims=True)
    yc = y - mu
    var = jnp.mean(yc * yc, axis=-1, keepdims=True)
    return yc * lax.rsqrt(var + LN_EPS) * g + b


def _post_kernel(h_ref, x_ref, wout_ref, g_ref, b_ref, wrh_ref, wrc_ref, br_ref,
                 x1_ref, x1p_ref, mi_ref, mf_ref, cnt_ref, carry):
    sub = TM_POST // POST_SPLIT

    @pl.when(pl.program_id(0) == 0)
    def _():
        carry[...] = jnp.zeros_like(carry)

    lane_i = _iota((sub, LANES), 1)
    lane = lane_i.astype(F32)
    found = {}

    def sub_tile(part):
        rows = slice(part * sub, (part + 1) * sub)
        mix = _dot(h_ref[rows, :], wout_ref[...])
        yield
        x1 = _layer_norm(DEEPNORM_ALPHA * x_ref[rows, :] + mix, g_ref[...], b_ref[...])
        x1_ref[rows, :] = x1
        x1p_ref[rows, :] = _pack_rows(x1)
        x_hi = x1.astype(BF16)
        x_lo = (x1 - x_hi.astype(F32)).astype(BF16)
        both = _dot(x_hi, wrc_ref[...])
        logits = both[:, :LANES] + both[:, LANES:] + _dot(x_lo, wrh_ref[...]) + br_ref[...]
        yield
        work = logits
        vals, idxs = [], []
        multi = jnp.zeros((sub, LANES), F32)
        for _ in range(TOP_K):
            m = jnp.max(work, axis=-1, keepdims=True)
            idx = jnp.min(jnp.where(work == m, lane, float(LANES)), axis=-1, keepdims=True)
            hit = lane == idx
            multi = multi + hit.astype(F32)
            work = jnp.where(hit, -jnp.inf, work)
            vals.append(m)
            idxs.append(idx)
            yield
        exps = [jnp.exp(v - vals[0]) for v in vals]
        denom = exps[0] + exps[1] + exps[2] + exps[3]
        gates = [e / denom for e in exps]
        before = (_iota((sub, sub), 0) > _iota((sub, sub), 1)).astype(BF16)
        found[part] = (idxs, gates, _dot(before, multi.astype(BF16)), multi)

    live = [sub_tile(part) for part in range(POST_SPLIT)]
    while live:
        live = [g for g in live if next(g, "done") != "done"]

    base = carry[...]
    for part in range(POST_SPLIT):
        idxs, gates, prior_local, multi = found[part]
        prior = prior_local + base
        mi = jnp.zeros((sub, LANES), F32)
        mf = jnp.zeros((sub, LANES), F32)
        for kk in range(TOP_K):
            rank = jnp.sum(jnp.where(lane == idxs[kk], prior, 0.0), axis=-1, keepdims=True)
            mi = jnp.where(lane_i == kk, idxs[kk], mi)
            mi = jnp.where(lane_i == TOP_K + kk, rank, mi)
            mf = jnp.where(lane_i == kk, gates[kk], mf)
        mi_ref[:, part * sub:(part + 1) * sub] = jnp.transpose(mi)[0:2 * TOP_K, :].astype(jnp.int32)
        mf_ref[part * sub:(part + 1) * sub, :] = mf
        base = base + jnp.sum(multi, axis=0, keepdims=True)
    carry[...] = base
    cnt_ref[...] = jnp.broadcast_to(base, cnt_ref.shape)


def _post(h, x2, w_out_b, ln_g, ln_b, wr_hi, wr_cat, br_p):
    t = x2.shape[0]
    tm = TM_POST
    row = lambda i: (i, 0)
    const = lambda i: (0, 0)
    return pl.pallas_call(
        _post_kernel,
        grid=(t // tm,),
        in_specs=[pl.BlockSpec((tm, D_MIX), row), pl.BlockSpec((tm, D_MODEL), row),
                  pl.BlockSpec((D_MIX, D_MODEL), const), pl.BlockSpec((1, D_MODEL), const),
                  pl.BlockSpec((1, D_MODEL), const), pl.BlockSpec((D_MODEL, LANES), const),
                  pl.BlockSpec((D_MODEL, 2 * LANES), const), pl.BlockSpec((1, LANES), const)],
        out_specs=[pl.BlockSpec((tm, D_MODEL), row), pl.BlockSpec((tm, ROW_WORDS), row),
                   pl.BlockSpec((2 * TOP_K, tm), lambda i: (0, i)), pl.BlockSpec((tm, LANES), row),
                   pl.BlockSpec((8, LANES), const)],
        out_shape=[jax.ShapeDtypeStruct((t, D_MODEL), F32), jax.ShapeDtypeStruct((t, ROW_WORDS), jnp.int32),
                   jax.ShapeDtypeStruct((2 * TOP_K, t), jnp.int32), jax.ShapeDtypeStruct((t, LANES), F32),
                   jax.ShapeDtypeStruct((8, LANES), F32)],
        scratch_shapes=[pltpu.VMEM((1, LANES), F32)],
        compiler_params=pltpu.CompilerParams(dimension_semantics=("arbitrary",),
                                             vmem_limit_bytes=VMEM_LIMIT),
        name="outproj_ln_router",
    )(h, x2, w_out_b, ln_g, ln_b, wr_hi, wr_cat, br_p)


def _ffn_kernel(be_ref, nv_ref, x_ref, wg_ref, bg_ref, wu_ref, bu_ref, wd_ref, bd_ref, o_ref,
                wg_b, wu_b, wd_b):
    i = pl.program_id(0)
    valid = i < nv_ref[0]
    e = be_ref[i]
    prev = be_ref[jnp.maximum(i - 1, 0)]
    fresh = jnp.logical_or(i == 0, e != prev)

    @pl.when(jnp.logical_and(valid, fresh))
    def _():
        wg_b[...] = wg_ref[0, 0].astype(BF16)
        wu_b[...] = wu_ref[0, 0].astype(BF16)
        wd_b[...] = wd_ref[0, 0].astype(BF16)

    @pl.when(valid)
    def _():
        x = _unpack_rows(x_ref[...]).astype(BF16)
        hg = jnp.minimum(_dot(x, wg_b[...]) + bg_ref[0, 0], SWIGLU_LIMIT)
        hu = jnp.clip(_dot(x, wu_b[...]) + bu_ref[0, 0], -SWIGLU_LIMIT, SWIGLU_LIMIT)
        hh = (hu + 1.0) * hg * jax.nn.sigmoid(SWIGLU_ALPHA * hg)
        o_ref[...] = _pack_rows(_dot(hh.astype(BF16), wd_b[...]) + bd_ref[0, 0])


def _ffn(layer, block_expert, n_valid, xin, wg, bg, wu, bu, wd, bd):
    cap = xin.shape[0]
    nb = cap // BM

    def blk(i, be, nv):
        return jnp.maximum(jnp.minimum(i, nv[0] - 1), 0)

    row = lambda i, be, nv: (blk(i, be, nv), 0)
    wmap = lambda i, be, nv: (layer, be[blk(i, be, nv)], 0, 0)
    grid_spec = pltpu.PrefetchScalarGridSpec(
        num_scalar_prefetch=2,
        grid=(nb,),
        in_specs=[pl.BlockSpec((BM, ROW_WORDS), row),
                  pl.BlockSpec((1, 1, D_MODEL, D_FF), wmap), pl.BlockSpec((1, 1, 1, D_FF), wmap),
                  pl.BlockSpec((1, 1, D_MODEL, D_FF), wmap), pl.BlockSpec((1, 1, 1, D_FF), wmap),
                  pl.BlockSpec((1, 1, D_FF, D_MODEL), wmap), pl.BlockSpec((1, 1, 1, D_MODEL), wmap)],
        out_specs=pl.BlockSpec((BM, ROW_WORDS), row),
        scratch_shapes=[pltpu.VMEM((D_MODEL, D_FF), BF16), pltpu.VMEM((D_MODEL, D_FF), BF16),
                        pltpu.VMEM((D_FF, D_MODEL), BF16)],
    )
    return pl.pallas_call(
        _ffn_kernel,
        grid_spec=grid_spec,
        out_shape=jax.ShapeDtypeStruct((cap, ROW_WORDS), jnp.int32),
        compiler_params=pltpu.CompilerParams(dimension_semantics=("arbitrary",),
                                             vmem_limit_bytes=VMEM_LIMIT),
        name="expert_ffn",
    )(block_expert, n_valid, xin, wg, bg, wu, bu, wd, bd)


def _sc_gather(table, idx3):
    nw, n_chunks, ch = idx3.shape
    width = table.shape[1]
    per_worker = n_chunks * ch
    mesh = plsc.VectorSubcoreMesh(core_axis_name="c", subcore_axis_name="s")
    n_cores = mesh.num_cores
    assert nw == n_cores * mesh.num_subcores and n_chunks % 2 == 0 and ch == SC_CHUNK

    def body(table_hbm, idx_hbm, out_hbm, idx_v, rows0, rows1, sem_g0, sem_g1, sem_w0, sem_w1):
        wid = lax.axis_index("s") * n_cores + lax.axis_index("c")
        base = wid * per_worker
        pltpu.sync_copy(idx_hbm.at[wid], idx_v)

        @pl.loop(0, n_chunks, step=2)
        def _(c):
            g0 = pltpu.async_copy(table_hbm.at[idx_v.at[c]], rows0, sem_g0)
            g1 = pltpu.async_copy(table_hbm.at[idx_v.at[c + 1]], rows1, sem_g1)
            g0.wait()
            w0 = pltpu.async_copy(rows0, out_hbm.at[pl.ds(base + c * ch, ch)], sem_w0)
            g1.wait()
            w1 = pltpu.async_copy(rows1, out_hbm.at[pl.ds(base + (c + 1) * ch, ch)], sem_w1)
            w0.wait()
            w1.wait()

    return pl.kernel(
        body,
        out_type=jax.ShapeDtypeStruct((nw * per_worker, width), table.dtype),
        mesh=mesh,
        scratch_types=[pltpu.VMEM((n_chunks, ch), jnp.int32),
                       pltpu.VMEM((ch, width), table.dtype), pltpu.VMEM((ch, width), table.dtype),
                       pltpu.SemaphoreType.DMA, pltpu.SemaphoreType.DMA,
                       pltpu.SemaphoreType.DMA, pltpu.SemaphoreType.DMA],
        name="sc_row_gather",
    )(table, idx3)


def _sc_scatter(rows, idx3, n_out):
    nw, n_lists, ch = idx3.shape
    n_chunks = n_lists // TOP_K
    width = rows.shape[1]
    per_worker = n_chunks * ch
    mesh = plsc.VectorSubcoreMesh(core_axis_name="c", subcore_axis_name="s")
    n_cores = mesh.num_cores
    assert nw == n_cores * mesh.num_subcores and n_chunks % 2 == 0 and ch == SC_CHUNK
    assert nw * per_worker == rows.shape[0]

    def body(rows_hbm, idx_hbm, out_hbm, idx_v, buf0, buf1, sem_r0, sem_r1, sem_w0, sem_w1):
        wid = lax.axis_index("s") * n_cores + lax.axis_index("c")
        base = wid * per_worker
        pltpu.sync_copy(idx_hbm.at[wid], idx_v)

        @pl.loop(0, n_chunks, step=2)
        def _(c):
            r0 = pltpu.async_copy(rows_hbm.at[pl.ds(base + c * ch, ch)], buf0, sem_r0)
            r1 = pltpu.async_copy(rows_hbm.at[pl.ds(base + (c + 1) * ch, ch)], buf1, sem_r1)
            r0.wait()
            w0 = [pltpu.async_copy(buf0, out_hbm.at[idx_v.at[c * TOP_K + k]], sem_w0) for k in range(TOP_K)]
            r1.wait()
            w1 = [pltpu.async_copy(buf1, out_hbm.at[idx_v.at[(c + 1) * TOP_K + k]], sem_w1) for k in range(TOP_K)]
            for w in w0 + w1:
                w.wait()

    return pl.kernel(
        body,
        out_type=jax.ShapeDtypeStruct((n_out, width), rows.dtype),
        mesh=mesh,
        scratch_types=[pltpu.VMEM((n_lists, ch), jnp.int32),
                       pltpu.VMEM((ch, width), rows.dtype), pltpu.VMEM((ch, width), rows.dtype),
                       pltpu.SemaphoreType.DMA, pltpu.SemaphoreType.DMA,
                       pltpu.SemaphoreType.DMA, pltpu.SemaphoreType.DMA],
        name="sc_row_scatter",
    )(rows, idx3)


def _combine_kernel(x_ref, rows_ref, mf_ref, g_ref, b_ref, o_ref):
    mf = mf_ref[...]
    y = DEEPNORM_ALPHA * x_ref[...]
    for kk in range(TOP_K):
        y = y + mf[:, kk:kk + 1] * _unpack_rows(rows_ref[kk])
    o_ref[...] = _layer_norm(y, g_ref[...], b_ref[...])


def _combine(x1, rows, mf, ln_g, ln_b):
    t = x1.shape[0]
    tm = TM_COMB
    row = lambda i: (i, 0)
    const = lambda i: (0, 0)
    return pl.pallas_call(
        _combine_kernel,
        grid=(t // tm,),
        in_specs=[pl.BlockSpec((tm, D_MODEL), row),
                  pl.BlockSpec((TOP_K, tm, ROW_WORDS), lambda i: (0, i, 0)),
                  pl.BlockSpec((tm, LANES), row),
                  pl.BlockSpec((1, D_MODEL), const), pl.BlockSpec((1, D_MODEL), const)],
        out_specs=pl.BlockSpec((tm, D_MODEL), row),
        out_shape=jax.ShapeDtypeStruct((t, D_MODEL), F32),
        compiler_params=pltpu.CompilerParams(dimension_semantics=("arbitrary",),
                                             vmem_limit_bytes=VMEM_LIMIT),
        name="combine_ln",
    )(x1, rows, mf, ln_g, ln_b)


def _relayout_w_in(w):
    widths = (256, 256, 256, 256, SSD_W, SSD_XBC, SSD_HEADS, GLA_QK, GLA_QK, GLA_W, GLA_RANK, GLA_W)
    offs = [0]
    for wd in widths:
        offs.append(offs[-1] + wd)
    parts = [w[:, offs[i]:offs[i + 1]] for i in range(len(widths))]
    parts[6] = jnp.pad(parts[6], ((0, 0), (0, LANES - SSD_HEADS)))
    parts[10] = jnp.pad(parts[10], ((0, 0), (0, LANES - GLA_RANK)))
    return jnp.concatenate(parts, axis=1).astype(BF16)


def _rep_heads(p):
    return jnp.repeat(p, SSD_HEAD_DIM)[None, :]


def _pad_heads(p):
    return jnp.pad(p, (0, LANES - SSD_HEADS))[None, :]


def kernel(x, positions, w_in, w_out, ret_norm_w, ssd_conv_w, ssd_conv_b, ssd_dt_bias, ssd_a_log, ssd_d,
           ssd_norm_w, gla_w_gk2, gla_b_gk2, gla_norm_w, ln1_g, ln1_b, w_router, b_router, w_gate, b_gate,
           w_up, b_up, w_down, b_down, ln2_g, ln2_b):
    batch, seq, d = x.shape
    t = batch * seq
    depth = w_in.shape[0]
    assert d == D_MODEL and t % TM_POST == 0 and t % TM_COMB == 0 and t % TM_ROPE == 0
    n_assign = t * TOP_K
    nb = pl.cdiv(n_assign, BM) + N_EXPERTS
    cap = nb * BM

    cos_t, sin_t = _rope_tables(positions.reshape(t, 1))
    cos_t = cos_t.reshape(batch, seq, -1)
    sin_t = sin_t.reshape(batch, seq, -1)
    x2 = x.reshape(t, d)

    for l in range(depth):
        params = (ret_norm_w[l][None, :], ssd_conv_w[l], ssd_conv_b[l][None, :], _pad_heads(ssd_dt_bias[l]),
                  _pad_heads(ssd_a_log[l]), _rep_heads(ssd_d[l]), ssd_norm_w[l][None, :],
                  jnp.pad(gla_w_gk2[l], ((0, LANES - GLA_RANK), (0, 0))), gla_b_gk2[l][None, :],
                  gla_norm_w[l][None, :])
        h = _mixproj(x2.reshape(batch, seq, d), _relayout_w_in(w_in[l]), cos_t, sin_t, params).reshape(t, D_MIX)

        wr_p = jnp.pad(w_router[l], ((0, 0), (0, LANES - N_EXPERTS)))
        br_p = jnp.pad(b_router[l], (0, LANES - N_EXPERTS), constant_values=NEG_BIG)[None, :]
        wr_hi = wr_p.astype(BF16)
        wr_lo = (wr_p - wr_hi.astype(F32)).astype(BF16)
        x1, x1p, mi, mf, cnt = _post(h, x2, w_out[l].astype(BF16), ln1_g[l][None, :], ln1_b[l][None, :],
                                     wr_hi, jnp.concatenate([wr_hi, wr_lo], axis=1), br_p)

        counts = cnt[0, :N_EXPERTS].astype(jnp.int32)
        padded = (counts + BM - 1) // BM * BM
        end_padded = jnp.cumsum(padded)
        start_padded = end_padded - padded
        top_idx = mi[:TOP_K]
        start_of = jnp.sum(jnp.where(top_idx[None] == jnp.arange(N_EXPERTS, dtype=jnp.int32)[:, None, None],
                                     start_padded[:, None, None], 0), axis=0)
        dest = start_of + mi[TOP_K:]
        block_start = jnp.arange(nb, dtype=jnp.int32) * BM
        block_expert = jnp.minimum(jnp.sum((end_padded[None, :] <= block_start[:, None]).astype(jnp.int32), axis=1),
                                   N_EXPERTS - 1)
        n_valid = (end_padded[-1:] // BM).astype(jnp.int32)

        scatter_idx = dest.reshape(TOP_K, SC_WORKERS, -1, SC_CHUNK).transpose(1, 2, 0, 3)
        xin = _sc_scatter(x1p, scatter_idx.reshape(SC_WORKERS, -1, SC_CHUNK), cap)
        yb = _ffn(l, block_expert, n_valid, xin, w_gate, b_gate[:, :, None, :], w_up, b_up[:, :, None, :],
                  w_down, b_down[:, :, None, :])
        rows = _sc_gather(yb, dest.reshape(SC_WORKERS, -1, SC_CHUNK))
        x2 = _combine(x1, rows.reshape(TOP_K, t, ROW_WORDS), mf, ln2_g[l][None, :], ln2_b[l][None, :])
    return x2.reshape(batch, seq, d)
```

```python
import functools
import math

import jax
import jax.numpy as jnp
from jax import lax
from jax.experimental import pallas as pl
from jax.experimental.pallas import tpu as pltpu
from jax.experimental.pallas import tpu_sc as plsc

F32 = jnp.float32
BF16 = jnp.bfloat16

D_MODEL = 1024
CHUNK = 64
RET_HEADS, RET_DK, RET_DV = 4, 64, 64
RET_W = RET_HEADS * RET_DV
SSD_HEADS, SSD_HEAD_DIM, SSD_STATE, SSD_GROUPS, SSD_CONV = 8, 64, 64, 2, 4
SSD_W = SSD_HEADS * SSD_HEAD_DIM
SSD_BC = SSD_GROUPS * SSD_STATE
SSD_XBC = SSD_W + 2 * SSD_BC
GLA_HEADS, GLA_DK, GLA_DV, GLA_RANK, GLA_TEMP = 4, 32, 64, 16, 16.0
GLA_QK = GLA_HEADS * GLA_DK
GLA_W = GLA_HEADS * GLA_DV
D_MIX = RET_W + SSD_W + GLA_W
N_EXPERTS, TOP_K, D_FF = 32, 4, 1024
SWIGLU_LIMIT, SWIGLU_ALPHA = 7.0, 1.702
ROPE_BASE = 10000.0
LN_EPS, NORM_EPS = 1e-5, 1e-6
DEPTH = 2
DEEPNORM_ALPHA = (2.0 * DEPTH) ** 0.25

LANES = 128
ROPE_W = LANES
NEG_BIG = -1e30
VMEM_LIMIT = 56 * 1024 * 1024

_SEGS = (("rq", 256), ("rk", 256), ("rv", 256), ("rg", 256), ("sz", SSD_W), ("sxbc", SSD_XBC),
         ("sdt", 128), ("gq", 128), ("gk", 128), ("gv", 256), ("ggk", 128), ("gg", 256))
COL = {}
_off = 0
for _n, _w in _SEGS:
    COL[_n] = _off
    _off += _w
NP = _off

MIX_G = 8
PROJ_SLAB = 512
TM_POST = 1024
POST_SPLIT = 4
TM_COMB = 1024
TM_ROPE = 2048
BM = 1024
ROW_WORDS = D_MODEL // 2
SC_WORKERS = 32
SC_CHUNK = 64


def _dot(a, b, dims=(((1,), (0,)), ((), ())), precision=None):
    return lax.dot_general(a, b, dims, precision=precision, preferred_element_type=F32)


_NT = (((1,), (1,)), ((), ()))
_TN = (((0,), (0,)), ((), ()))


def _iota(shape, dim):
    return lax.broadcasted_iota(jnp.int32, shape, dim)


def _vdiv(x, n):
    assert n & (n - 1) == 0
    return lax.shift_right_logical(x, n.bit_length() - 1)


def _vmod(x, n):
    assert n & (n - 1) == 0
    return jnp.bitwise_and(x, n - 1)


def _silu(x):
    return x * jax.nn.sigmoid(x)


def _softplus(x):
    return jnp.maximum(x, 0.0) + jnp.log(1.0 + jnp.exp(-jnp.abs(x)))


def _pack_rows(x):
    w = x.shape[1] // 2
    lo = lax.bitcast_convert_type(x[:, :w].astype(BF16).astype(F32), jnp.uint32)
    hi = lax.bitcast_convert_type(x[:, w:].astype(BF16).astype(F32), jnp.uint32)
    return lax.bitcast_convert_type(lax.shift_right_logical(lo, jnp.uint32(16)) | hi, jnp.int32)


def _unpack_rows(words):
    u = lax.bitcast_convert_type(words, jnp.uint32)
    a = lax.bitcast_convert_type(lax.shift_left(u, jnp.uint32(16)), F32)
    b = lax.bitcast_convert_type(u & jnp.uint32(0xFFFF0000), F32)
    return jnp.concatenate([a, b], axis=-1)


def _seg_sum64(x):
    first = _iota((1, LANES), 1) < 64
    outs = []
    for j in range(x.shape[-1] // LANES):
        blk = x[:, j * LANES:(j + 1) * LANES]
        lo = jnp.sum(jnp.where(first, blk, 0.0), axis=-1, keepdims=True)
        hi = jnp.sum(jnp.where(first, 0.0, blk), axis=-1, keepdims=True)
        outs.append(jnp.where(first, lo, hi))
    return jnp.concatenate(outs, axis=-1)


def _block_mask(shape, row_blk, col_blk):
    keep = _vdiv(_iota(shape, 0), row_blk) == _vdiv(_iota(shape, 1), col_blk)
    return jnp.where(keep, 1.0, 0.0).astype(BF16)


def _block_diag(x, mask):
    reps = mask.shape[0] // x.shape[0]
    return jnp.concatenate([x.astype(BF16)] * reps, axis=0) * mask


def _expand_heads(x, expand):
    hi = x.astype(BF16)
    r1 = x - hi.astype(F32)
    mid = r1.astype(BF16)
    lo = (r1 - mid.astype(F32)).astype(BF16)
    return _dot(hi, expand) + _dot(mid, expand) + _dot(lo, expand)


def _cumsum_rows(tri, x):
    hi = x.astype(BF16)
    lo = (x - hi.astype(F32)).astype(BF16)
    return _dot(tri, hi) + _dot(tri, lo)


def _rope_kernel(pos_ref, cos_ref, sin_ref):
    lane = _iota((1, ROPE_W), 1)
    half = RET_DK // 2
    k = _vmod(lane, half).astype(F32)
    inv_freq = jnp.exp(k * (-math.log(ROPE_BASE) / half))
    ang = pos_ref[...].astype(F32) * inv_freq
    first = _vmod(lane, RET_DK) < half
    cos_ref[...] = jnp.cos(ang)
    sin_ref[...] = jnp.where(first, -1.0, 1.0) * jnp.sin(ang)


def _rope_tables(pos_col):
    t = pos_col.shape[0]
    tm = TM_ROPE
    w = ROPE_W
    return pl.pallas_call(
        _rope_kernel,
        grid=(t // tm,),
        in_specs=[pl.BlockSpec((tm, 1), lambda i: (i, 0))],
        out_specs=[pl.BlockSpec((tm, w), lambda i: (i, 0))] * 2,
        out_shape=[jax.ShapeDtypeStruct((t, w), F32)] * 2,
        compiler_params=pltpu.CompilerParams(dimension_semantics=("arbitrary",)),
        name="rope_tables",
    )(pos_col)


def _mixproj_kernel(xn_ref, w_ref, cos_ref, sin_ref, retw_ref, convw_ref, convb_ref, dtb_ref, alog_ref,
                    dskip_ref, ssdw_ref, wgk_ref, bgk_ref, glaw_ref, h_ref,
                    proj_a, proj_b, ret_s, ssd_s, gla_s, stage, m_heads, m_groups, m_gla, m_expand, *,
                    chunks_per_seq):
    C = CHUNK
    i = pl.program_id(0)
    cur = jnp.maximum(i - 1, 0)

    @pl.when(i == 0)
    def _():
        m_heads[...] = _block_mask(m_heads.shape, C, 64)
        m_groups[...] = _block_mask(m_groups.shape, C * SSD_HEADS // SSD_GROUPS, SSD_STATE)
        m_gla[...] = _block_mask(m_gla.shape, C, GLA_DK)
        m_expand[...] = _block_mask(m_expand.shape, 1, SSD_HEAD_DIM)
        proj_b[...] = jnp.zeros_like(proj_b)

    @pl.when(lax.rem(cur, chunks_per_seq) == 0)
    def _():
        ret_s[...] = jnp.zeros_like(ret_s)
        ssd_s[...] = jnp.zeros_like(ssd_s)
        gla_s[...] = jnp.zeros_like(gla_s)
        for g in range(MIX_G):
            stage[g, 0:8, :] = jnp.zeros((8, SSD_XBC), F32)

    def conv_act(sq, pref):
        stage[sq, 8:8 + C, :] = pref[sq * C:(sq + 1) * C, COL["sxbc"]:COL["sxbc"] + SSD_XBC]
        acc = convb_ref[...] + convw_ref[0:1, :] * stage[sq, 5:5 + C, :]
        for j in range(1, SSD_CONV):
            acc = acc + convw_ref[j:j + 1, :] * stage[sq, 5 + j:5 + j + C, :]
        stage[sq, 0:8, :] = stage[sq, C:C + 8, :]
        return _silu(acc)

    lane256 = _iota((1, 256), 1)
    head = _vdiv(lane256, 64).astype(F32)
    log_gamma = jnp.log(1.0 - jnp.exp((-5.0 - head) * math.log(2.0)))
    row = _iota((C, 1), 0).astype(F32)
    dist = row - _vmod(lane256, 64).astype(F32)
    ret_intra = jnp.where(dist >= 0, jnp.exp(log_gamma * jnp.maximum(dist, 0.0)), 0.0)
    ret_qdec = jnp.exp(log_gamma * (row + 1.0))
    ret_kdec = jnp.exp(log_gamma * (C - 1.0 - row))
    ret_cdec = jnp.exp(log_gamma * C)
    first_half = _vmod(lane256, RET_DK) < (RET_DK // 2)

    tri = jnp.where(_iota((C, C), 0) >= _iota((C, C), 1), 1.0, 0.0).astype(BF16)
    causal4 = _iota((C, 256), 0) >= _vmod(_iota((C, 256), 1), 64)
    causal8 = _iota((C, 512), 0) >= _vmod(_iota((C, 512), 1), 64)
    eye8 = _iota((C, 512), 0) == _vmod(_iota((C, 512), 1), 64)

    a_neg = -jnp.exp(alog_ref[...])

    def rot(t, cos, sin):
        sw = jnp.where(first_half, pltpu.roll(t, 256 - 32, 1), pltpu.roll(t, 32, 1))
        return t * cos + sw * sin

    def chunk_of(sq, pref):
        def seg(name, width):
            return pref[sq * C:(sq + 1) * C, COL[name]:COL[name] + width]

        xact = conv_act(sq, pref)
        yield

        reps = RET_HEADS * RET_DK // ROPE_W
        cos = jnp.concatenate([cos_ref[sq]] * reps, axis=-1)
        sin = jnp.concatenate([sin_ref[sq]] * reps, axis=-1)
        q = rot(seg("rq", 256), cos, sin)
        k = rot(seg("rk", 256), cos, sin) * (RET_DK ** -0.5)
        v = seg("rv", 256)
        vb = v.astype(BF16)
        kbd = _block_diag(k, m_heads[...])
        scores = _dot(q.astype(BF16), kbd, _NT) * ret_intra
        vbd = _block_diag(v, m_heads[...])
        yield
        s_prev = ret_s[sq]
        o = _dot(scores.astype(BF16), vbd) + _dot((q * ret_qdec).astype(BF16), s_prev.astype(BF16))
        contrib = _dot((k * ret_kdec).astype(BF16), vb, _TN)
        yield
        keep = _vdiv(_iota((256, 256), 0), RET_DK) == _vdiv(_iota((256, 256), 1), RET_DV)
        ret_s[sq] = jnp.where(keep, ret_cdec * s_prev + contrib, 0.0)
        mu = _seg_sum64(o) * (1.0 / RET_DV)
        oc = o - mu
        var = _seg_sum64(oc * oc) * (1.0 / RET_DV)
        o = oc * lax.rsqrt(var + LN_EPS) * retw_ref[...]
        h_ref[sq, :, 0:RET_W] = (_silu(seg("rg", 256)) * o).astype(BF16)
        yield

        xs = xact[:, 0:SSD_W]
        bm = xact[:, SSD_W:SSD_W + SSD_BC]
        cm = xact[:, SSD_W + SSD_BC:SSD_XBC]
        cmb = cm.astype(BF16)
        dt_c = _softplus(seg("sdt", LANES) + dtb_ref[...])
        acum_c = _cumsum_rows(tri, dt_c * a_neg)
        both = _expand_heads(jnp.concatenate([dt_c, acum_c], axis=0), m_expand[...])
        dt = both[0:C, :]
        acum = both[C:2 * C, :]
        yield
        arow = jnp.sum(jnp.where(eye8, acum, 0.0), axis=0, keepdims=True)
        decay = jnp.exp(jnp.where(causal8, acum - arow, NEG_BIG))
        b8 = _block_diag(bm, m_groups[...])
        cb = _dot(cmb, b8, _NT)
        yield
        m = (cb * decay).astype(BF16)
        xdt = xs * dt
        s2 = ssd_s[sq]
        half = SSD_W // SSD_GROUPS
        ys = []
        for g in range(SSD_GROUPS):
            xbd = _block_diag(xdt[:, g * half:(g + 1) * half], m_heads[...])
            ys.append(_dot(m[:, g * half:(g + 1) * half], xbd))
        y = jnp.concatenate(ys, axis=-1)
        y = y + _dot(cmb, s2.astype(BF16)) * jnp.exp(acum)
        y = y + dskip_ref[...] * xs
        a_last = acum[C - 1:C, :]
        sd = jnp.exp(a_last - acum)
        contrib_s = _dot(bm.astype(BF16), (xdt * sd).astype(BF16), _TN)
        yield
        keep_s = _vdiv(_iota(s2.shape, 0), SSD_STATE) == _vdiv(_iota(s2.shape, 1), half)
        ssd_s[sq] = jnp.where(keep_s, s2 * jnp.exp(a_last) + contrib_s, 0.0)
        yz = y * _silu(seg("sz", SSD_W))
        outs = []
        for g in range(SSD_GROUPS):
            blk = yz[:, g * half:(g + 1) * half]
            ms = jnp.mean(blk * blk, axis=-1, keepdims=True)
            outs.append(blk * lax.rsqrt(ms + NORM_EPS))
        h_ref[sq, :, RET_W:RET_W + SSD_W] = (jnp.concatenate(outs, axis=-1) * ssdw_ref[...]).astype(BF16)

        yield

        gq = seg("gq", GLA_QK) * (GLA_DK ** -0.5)
        gkk = seg("gk", GLA_QK)
        gv = seg("gv", GLA_W)
        gkl = _dot(seg("ggk", 128).astype(BF16), wgk_ref[...].astype(BF16)) + bgk_ref[...]
        yield
        log_a = -_softplus(-gkl) * (1.0 / GLA_TEMP)
        b = _cumsum_rows(tri, log_a)
        yield
        q_t = (gq * jnp.exp(b)).astype(BF16)
        k_t = gkk * jnp.exp(-b)
        kbd_g = _block_diag(k_t, m_gla[...])
        att = jnp.where(causal4, _dot(q_t, kbd_g, _NT), 0.0)
        yield
        vbd_g = _block_diag(gv, m_heads[...])
        st = gla_s[sq]
        og = _dot(att.astype(BF16), vbd_g) + _dot(q_t, st.astype(BF16), _NT)
        b_last = b[C - 1:C, :]
        kd = (gkk * jnp.exp(b_last - b)).astype(BF16)
        contrib_g = _dot(gv.astype(BF16), kd, _TN)
        yield
        keep_g = _vdiv(_iota(st.shape, 0), GLA_DV) == _vdiv(_iota(st.shape, 1), GLA_DK)
        gla_s[sq] = jnp.where(keep_g, st * jnp.exp(b_last) + contrib_g, 0.0)
        ms = _seg_sum64(og * og) * (1.0 / GLA_DV)
        og = og * lax.rsqrt(ms + NORM_EPS) * glaw_ref[...]
        h_ref[sq, :, RET_W + SSD_W:D_MIX] = (_silu(seg("gg", GLA_W)) * og).astype(BF16)

    def step(p_read, p_write):
        xb = xn_ref[...].reshape(MIX_G * C, D_MODEL).astype(BF16)
        edges = list(range(0, NP, PROJ_SLAB)) + [NP]
        slabs = list(zip(edges[:-1], edges[1:]))

        live = [chunk_of(sq, p_read) for sq in range(MIX_G)]
        while live or slabs:
            live = [g for g in live if next(g, "done") != "done"]
            if slabs:
                lo, hi = slabs.pop(0)
                p_write[:, lo:hi] = _dot(xb, w_ref[:, lo:hi])

    @pl.when(lax.rem(i, 2) == 0)
    def _():
        step(proj_b, proj_a)

    @pl.when(lax.rem(i, 2) == 1)
    def _():
        step(proj_a, proj_b)


def _mixproj(x3, w_p, cos_t, sin_t, params):
    batch, seq, _ = x3.shape
    assert batch % MIX_G == 0 and seq % CHUNK == 0
    cps = seq // CHUNK
    n = (batch // MIX_G) * cps

    def cur_map(i):
        c = jnp.clip(i - 1, 0, n - 1)
        return (c // cps, c % cps, 0)

    def next_map(i):
        c = jnp.minimum(i, n - 1)
        return (c // cps, c % cps, 0)

    const = lambda i: (0, 0)
    specs = [pl.BlockSpec((MIX_G, CHUNK, D_MODEL), next_map),
             pl.BlockSpec((D_MODEL, NP), const),
             pl.BlockSpec((MIX_G, CHUNK, ROPE_W), cur_map),
             pl.BlockSpec((MIX_G, CHUNK, ROPE_W), cur_map)]
    specs += [pl.BlockSpec(p.shape, const) for p in params]
    return pl.pallas_call(
        functools.partial(_mixproj_kernel, chunks_per_seq=cps),
        grid=(n + 1,),
        in_specs=specs,
        out_specs=pl.BlockSpec((MIX_G, CHUNK, D_MIX), cur_map),
        out_shape=jax.ShapeDtypeStruct((batch, seq, D_MIX), BF16),
        scratch_shapes=[pltpu.VMEM((MIX_G * CHUNK, NP), F32),
                        pltpu.VMEM((MIX_G * CHUNK, NP), F32),
                        pltpu.VMEM((MIX_G, 256, 256), F32),
                        pltpu.VMEM((MIX_G, SSD_BC, SSD_W), F32),
                        pltpu.VMEM((MIX_G, GLA_W, GLA_QK), F32),
                        pltpu.VMEM((MIX_G, CHUNK + 8, SSD_XBC), F32),
                        pltpu.VMEM((RET_HEADS * CHUNK, 256), BF16),
                        pltpu.VMEM((SSD_HEADS * CHUNK, SSD_BC), BF16),
                        pltpu.VMEM((GLA_HEADS * CHUNK, GLA_QK), BF16),
                        pltpu.VMEM((LANES, SSD_W), BF16)],
        compiler_params=pltpu.CompilerParams(dimension_semantics=("arbitrary",),
                                             vmem_limit_bytes=VMEM_LIMIT),
        name="inproj_mixer",
    )(x3, w_p, cos_t, sin_t, *params)


def _layer_norm(y, g, b):
    mu = jnp.mean(y, axis=-1, keepdims=True)
    yc = y - mu
    var = jnp.mean(yc * yc, axis=-1, keepdims=True)
    return yc * lax.rsqrt(var + LN_EPS) * g + b


def _post_kernel(h_ref, x_ref, wout_ref, g_ref, b_ref, wrh_ref, wrc_ref, br_ref,
                 x1_ref, x1p_ref, mi_ref, mf_ref, cnt_ref, carry):
    sub = TM_POST // POST_SPLIT

    @pl.when(pl.program_id(0) == 0)
    def _():
        carry[...] = jnp.zeros_like(carry)

    lane_i = _iota((sub, LANES), 1)
    lane = lane_i.astype(F32)
    found = {}

    def sub_tile(part):
        rows = slice(part * sub, (part + 1) * sub)
        mix = _dot(h_ref[rows, :], wout_ref[...])
        yield
        x1 = _layer_norm(DEEPNORM_ALPHA * x_ref[rows, :] + mix, g_ref[...], b_ref[...])
        x1_ref[rows, :] = x1
        x1p_ref[rows, :] = _pack_rows(x1)
        x_hi = x1.astype(BF16)
        x_lo = (x1 - x_hi.astype(F32)).astype(BF16)
        both = _dot(x_hi, wrc_ref[...])
        logits = both[:, :LANES] + both[:, LANES:] + _dot(x_lo, wrh_ref[...]) + br_ref[...]
        yield
        work = logits
        vals, idxs = [], []
        multi = jnp.zeros((sub, LANES), F32)
        for _ in range(TOP_K):
            m = jnp.max(work, axis=-1, keepdims=True)
            idx = jnp.min(jnp.where(work == m, lane, float(LANES)), axis=-1, keepdims=True)
            hit = lane == idx
            multi = multi + hit.astype(F32)
            work = jnp.where(hit, -jnp.inf, work)
            vals.append(m)
            idxs.append(idx)
            yield
        exps = [jnp.exp(v - vals[0]) for v in vals]
        denom = exps[0] + exps[1] + exps[2] + exps[3]
        gates = [e / denom for e in exps]
        before = (_iota((sub, sub), 0) > _iota((sub, sub), 1)).astype(BF16)
        found[part] = (idxs, gates, _dot(before, multi.astype(BF16)), multi)

    live = [sub_tile(part) for part in range(POST_SPLIT)]
    while live:
        live = [g for g in live if next(g, "done") != "done"]

    base = carry[...]
    for part in range(POST_SPLIT):
        idxs, gates, prior_local, multi = found[part]
        prior = prior_local + base
        mi = jnp.zeros((sub, LANES), F32)
        mf = jnp.zeros((sub, LANES), F32)
        for kk in range(TOP_K):
            rank = jnp.sum(jnp.where(lane == idxs[kk], prior, 0.0), axis=-1, keepdims=True)
            mi = jnp.where(lane_i == kk, idxs[kk], mi)
            mi = jnp.where(lane_i == TOP_K + kk, rank, mi)
            mf = jnp.where(lane_i == kk, gates[kk], mf)
        mi_ref[:, part * sub:(part + 1) * sub] = jnp.transpose(mi)[0:2 * TOP_K, :].astype(jnp.int32)
        mf_ref[part * sub:(part + 1) * sub, :] = mf
        base = base + jnp.sum(multi, axis=0, keepdims=True)
    carry[...] = base
    cnt_ref[...] = jnp.broadcast_to(base, cnt_ref.shape)


def _post(h, x2, w_out_b, ln_g, ln_b, wr_hi, wr_cat, br_p):
    t = x2.shape[0]
    tm = TM_POST
    row = lambda i: (i, 0)
    const = lambda i: (0, 0)
    return pl.pallas_call(
        _post_kernel,
        grid=(t // tm,),
        in_specs=[pl.BlockSpec((tm, D_MIX), row), pl.BlockSpec((tm, D_MODEL), row),
                  pl.BlockSpec((D_MIX, D_MODEL), const), pl.BlockSpec((1, D_MODEL), const),
                  pl.BlockSpec((1, D_MODEL), const), pl.BlockSpec((D_MODEL, LANES), const),
                  pl.BlockSpec((D_MODEL, 2 * LANES), const), pl.BlockSpec((1, LANES), const)],
        out_specs=[pl.BlockSpec((tm, D_MODEL), row), pl.BlockSpec((tm, ROW_WORDS), row),
                   pl.BlockSpec((2 * TOP_K, tm), lambda i: (0, i)), pl.BlockSpec((tm, LANES), row),
                   pl.BlockSpec((8, LANES), const)],
        out_shape=[jax.ShapeDtypeStruct((t, D_MODEL), F32), jax.ShapeDtypeStruct((t, ROW_WORDS), jnp.int32),
                   jax.ShapeDtypeStruct((2 * TOP_K, t), jnp.int32), jax.ShapeDtypeStruct((t, LANES), F32),
                   jax.ShapeDtypeStruct((8, LANES), F32)],
        scratch_shapes=[pltpu.VMEM((1, LANES), F32)],
        compiler_params=pltpu.CompilerParams(dimension_semantics=("arbitrary",),
                                             vmem_limit_bytes=VMEM_LIMIT),
        name="outproj_ln_router",
    )(h, x2, w_out_b, ln_g, ln_b, wr_hi, wr_cat, br_p)


def _ffn_kernel(be_ref, nv_ref, x_ref, wg_ref, bg_ref, wu_ref, bu_ref, wd_ref, bd_ref, o_ref,
                wg_b, wu_b, wd_b):
    i = pl.program_id(0)
    valid = i < nv_ref[0]
    e = be_ref[i]
    prev = be_ref[jnp.maximum(i - 1, 0)]
    fresh = jnp.logical_or(i == 0, e != prev)

    @pl.when(jnp.logical_and(valid, fresh))
    def _():
        wg_b[...] = wg_ref[0, 0].astype(BF16)
        wu_b[...] = wu_ref[0, 0].astype(BF16)
        wd_b[...] = wd_ref[0, 0].astype(BF16)

    @pl.when(valid)
    def _():
        x = _unpack_rows(x_ref[...]).astype(BF16)
        hg = jnp.minimum(_dot(x, wg_b[...]) + bg_ref[0, 0], SWIGLU_LIMIT)
        hu = jnp.clip(_dot(x, wu_b[...]) + bu_ref[0, 0], -SWIGLU_LIMIT, SWIGLU_LIMIT)
        hh = (hu + 1.0) * hg * jax.nn.sigmoid(SWIGLU_ALPHA * hg)
        o_ref[...] = _pack_rows(_dot(hh.astype(BF16), wd_b[...]) + bd_ref[0, 0])


def _ffn(layer, block_expert, n_valid, xin, wg, bg, wu, bu, wd, bd):
    cap = xin.shape[0]
    nb = cap // BM

    def blk(i, be, nv):
        return jnp.maximum(jnp.minimum(i, nv[0] - 1), 0)

    row = lambda i, be, nv: (blk(i, be, nv), 0)
    wmap = lambda i, be, nv: (layer, be[blk(i, be, nv)], 0, 0)
    grid_spec = pltpu.PrefetchScalarGridSpec(
        num_scalar_prefetch=2,
        grid=(nb,),
        in_specs=[pl.BlockSpec((BM, ROW_WORDS), row),
                  pl.BlockSpec((1, 1, D_MODEL, D_FF), wmap), pl.BlockSpec((1, 1, 1, D_FF), wmap),
                  pl.BlockSpec((1, 1, D_MODEL, D_FF), wmap), pl.BlockSpec((1, 1, 1, D_FF), wmap),
                  pl.BlockSpec((1, 1, D_FF, D_MODEL), wmap), pl.BlockSpec((1, 1, 1, D_MODEL), wmap)],
        out_specs=pl.BlockSpec((BM, ROW_WORDS), row),
        scratch_shapes=[pltpu.VMEM((D_MODEL, D_FF), BF16), pltpu.VMEM((D_MODEL, D_FF), BF16),
                        pltpu.VMEM((D_FF, D_MODEL), BF16)],
    )
    return pl.pallas_call(
        _ffn_kernel,
        grid_spec=grid_spec,
        out_shape=jax.ShapeDtypeStruct((cap, ROW_WORDS), jnp.int32),
        compiler_params=pltpu.CompilerParams(dimension_semantics=("arbitrary",),
                                             vmem_limit_bytes=VMEM_LIMIT),
        name="expert_ffn",
    )(block_expert, n_valid, xin, wg, bg, wu, bu, wd, bd)


def _sc_gather(table, idx3):
    nw, n_chunks, ch = idx3.shape
    width = table.shape[1]
    per_worker = n_chunks * ch
    mesh = plsc.VectorSubcoreMesh(core_axis_name="c", subcore_axis_name="s")
    n_cores = mesh.num_cores
    assert nw == n_cores * mesh.num_subcores and n_chunks % 2 == 0 and ch == SC_CHUNK

    def body(table_hbm, idx_hbm, out_hbm, idx_v, rows0, rows1, sem_g0, sem_g1, sem_w0, sem_w1):
        wid = lax.axis_index("s") * n_cores + lax.axis_index("c")
        base = wid * per_worker
        pltpu.sync_copy(idx_hbm.at[wid], idx_v)

        @pl.loop(0, n_chunks, step=2)
        def _(c):
            g0 = pltpu.async_copy(table_hbm.at[idx_v.at[c]], rows0, sem_g0)
            g1 = pltpu.async_copy(table_hbm.at[idx_v.at[c + 1]], rows1, sem_g1)
            g0.wait()
            w0 = pltpu.async_copy(rows0, out_hbm.at[pl.ds(base + c * ch, ch)], sem_w0)
            g1.wait()
            w1 = pltpu.async_copy(rows1, out_hbm.at[pl.ds(base + (c + 1) * ch, ch)], sem_w1)
            w0.wait()
            w1.wait()

    return pl.kernel(
        body,
        out_type=jax.ShapeDtypeStruct((nw * per_worker, width), table.dtype),
        mesh=mesh,
        scratch_types=[pltpu.VMEM((n_chunks, ch), jnp.int32),
                       pltpu.VMEM((ch, width), table.dtype), pltpu.VMEM((ch, width), table.dtype),
                       pltpu.SemaphoreType.DMA, pltpu.SemaphoreType.DMA,
                       pltpu.SemaphoreType.DMA, pltpu.SemaphoreType.DMA],
        name="sc_row_gather",
    )(table, idx3)


def _sc_scatter(rows, idx3, n_out):
    nw, n_lists, ch = idx3.shape
    n_chunks = n_lists // TOP_K
    width = rows.shape[1]
    per_worker = n_chunks * ch
    mesh = plsc.VectorSubcoreMesh(core_axis_name="c", subcore_axis_name="s")
    n_cores = mesh.num_cores
    assert nw == n_cores * mesh.num_subcores and n_chunks % 2 == 0 and ch == SC_CHUNK
    assert nw * per_worker == rows.shape[0]

    def body(rows_hbm, idx_hbm, out_hbm, idx_v, buf0, buf1, sem_r0, sem_r1, sem_w0, sem_w1):
        wid = lax.axis_index("s") * n_cores + lax.axis_index("c")
        base = wid * per_worker
        pltpu.sync_copy(idx_hbm.at[wid], idx_v)

        @pl.loop(0, n_chunks, step=2)
        def _(c):
            r0 = pltpu.async_copy(rows_hbm.at[pl.ds(base + c * ch, ch)], buf0, sem_r0)
            r1 = pltpu.async_copy(rows_hbm.at[pl.ds(base + (c + 1) * ch, ch)], buf1, sem_r1)
            r0.wait()
            w0 = [pltpu.async_copy(buf0, out_hbm.at[idx_v.at[c * TOP_K + k]], sem_w0) for k in range(TOP_K)]
            r1.wait()
            w1 = [pltpu.async_copy(buf1, out_hbm.at[idx_v.at[(c + 1) * TOP_K + k]], sem_w1) for k in range(TOP_K)]
            for w in w0 + w1:
                w.wait()

    return pl.kernel(
        body,
        out_type=jax.ShapeDtypeStruct((n_out, width), rows.dtype),
        mesh=mesh,
        scratch_types=[pltpu.VMEM((n_lists, ch), jnp.int32),
                       pltpu.VMEM((ch, width), rows.dtype), pltpu.VMEM((ch, width), rows.dtype),
                       pltpu.SemaphoreType.DMA, pltpu.SemaphoreType.DMA,
                       pltpu.SemaphoreType.DMA, pltpu.SemaphoreType.DMA],
        name="sc_row_scatter",
    )(rows, idx3)


def _combine_kernel(x_ref, rows_ref, mf_ref, g_ref, b_ref, o_ref):
    mf = mf_ref[...]
    y = DEEPNORM_ALPHA * x_ref[...]
    for kk in range(TOP_K):
        y = y + mf[:, kk:kk + 1] * _unpack_rows(rows_ref[kk])
    o_ref[...] = _layer_norm(y, g_ref[...], b_ref[...])


def _combine(x1, rows, mf, ln_g, ln_b):
    t = x1.shape[0]
    tm = TM_COMB
    row = lambda i: (i, 0)
    const = lambda i: (0, 0)
    return pl.pallas_call(
        _combine_kernel,
        grid=(t // tm,),
        in_specs=[pl.BlockSpec((tm, D_MODEL), row),
                  pl.BlockSpec((TOP_K, tm, ROW_WORDS), lambda i: (0, i, 0)),
                  pl.BlockSpec((tm, LANES), row),
                  pl.BlockSpec((1, D_MODEL), const), pl.BlockSpec((1, D_MODEL), const)],
        out_specs=pl.BlockSpec((tm, D_MODEL), row),
        out_shape=jax.ShapeDtypeStruct((t, D_MODEL), F32),
        compiler_params=pltpu.CompilerParams(dimension_semantics=("arbitrary",),
                                             vmem_limit_bytes=VMEM_LIMIT),
        name="combine_ln",
    )(x1, rows, mf, ln_g, ln_b)


def _relayout_w_in(w):
    widths = (256, 256, 256, 256, SSD_W, SSD_XBC, SSD_HEADS, GLA_QK, GLA_QK, GLA_W, GLA_RANK, GLA_W)
    offs = [0]
    for wd in widths:
        offs.append(offs[-1] + wd)
    parts = [w[:, offs[i]:offs[i + 1]] for i in range(len(widths))]
    parts[6] = jnp.pad(parts[6], ((0, 0), (0, LANES - SSD_HEADS)))
    parts[10] = jnp.pad(parts[10], ((0, 0), (0, LANES - GLA_RANK)))
    return jnp.concatenate(parts, axis=1).astype(BF16)


def _rep_heads(p):
    return jnp.repeat(p, SSD_HEAD_DIM)[None, :]


def _pad_heads(p):
    return jnp.pad(p, (0, LANES - SSD_HEADS))[None, :]


def kernel(x, positions, w_in, w_out, ret_norm_w, ssd_conv_w, ssd_conv_b, ssd_dt_bias, ssd_a_log, ssd_d,
           ssd_norm_w, gla_w_gk2, gla_b_gk2, gla_norm_w, ln1_g, ln1_b, w_router, b_router, w_gate, b_gate,
           w_up, b_up, w_down, b_down, ln2_g, ln2_b):
    batch, seq, d = x.shape
    t = batch * seq
    depth = w_in.shape[0]
    assert d == D_MODEL and t % TM_POST == 0 and t % TM_COMB == 0 and t % TM_ROPE == 0
    n_assign = t * TOP_K
    nb = pl.cdiv(n_assign, BM) + N_EXPERTS
    cap = nb * BM

    cos_t, sin_t = _rope_tables(positions.reshape(t, 1))
    cos_t = cos_t.reshape(batch, seq, -1)
    sin_t = sin_t.reshape(batch, seq, -1)
    x2 = x.reshape(t, d)

    for l in range(depth):
        params = (ret_norm_w[l][None, :], ssd_conv_w[l], ssd_conv_b[l][None, :], _pad_heads(ssd_dt_bias[l]),
                  _pad_heads(ssd_a_log[l]), _rep_heads(ssd_d[l]), ssd_norm_w[l][None, :],
                  jnp.pad(gla_w_gk2[l], ((0, LANES - GLA_RANK), (0, 0))), gla_b_gk2[l][None, :],
                  gla_norm_w[l][None, :])
        h = _mixproj(x2.reshape(batch, seq, d), _relayout_w_in(w_in[l]), cos_t, sin_t, params).reshape(t, D_MIX)

        wr_p = jnp.pad(w_router[l], ((0, 0), (0, LANES - N_EXPERTS)))
        br_p = jnp.pad(b_router[l], (0, LANES - N_EXPERTS), constant_values=NEG_BIG)[None, :]
        wr_hi = wr_p.astype(BF16)
        wr_lo = (wr_p - wr_hi.astype(F32)).astype(BF16)
        x1, x1p, mi, mf, cnt = _post(h, x2, w_out[l].astype(BF16), ln1_g[l][None, :], ln1_b[l][None, :],
                                     wr_hi, jnp.concatenate([wr_hi, wr_lo], axis=1), br_p)

        counts = cnt[0, :N_EXPERTS].astype(jnp.int32)
        padded = (counts + BM - 1) // BM * BM
        end_padded = jnp.cumsum(padded)
        start_padded = end_padded - padded
        top_idx = mi[:TOP_K]
        start_of = jnp.sum(jnp.where(top_idx[None] == jnp.arange(N_EXPERTS, dtype=jnp.int32)[:, None, None],
                                     start_padded[:, None, None], 0), axis=0)
        dest = start_of + mi[TOP_K:]
        block_start = jnp.arange(nb, dtype=jnp.int32) * BM
        block_expert = jnp.minimum(jnp.sum((end_padded[None, :] <= block_start[:, None]).astype(jnp.int32), axis=1),
                                   N_EXPERTS - 1)
        n_valid = (end_padded[-1:] // BM).astype(jnp.int32)

        scatter_idx = dest.reshape(TOP_K, SC_WORKERS, -1, SC_CHUNK).transpose(1, 2, 0, 3)
        xin = _sc_scatter(x1p, scatter_idx.reshape(SC_WORKERS, -1, SC_CHUNK), cap)
        yb = _ffn(l, block_expert, n_valid, xin, w_gate, b_gate[:, :, None, :], w_up, b_up[:, :, None, :],
                  w_down, b_down[:, :, None, :])
        rows = _sc_gather(yb, dest.reshape(SC_WORKERS, -1, SC_CHUNK))
        x2 = _combine(x1, rows.reshape(TOP_K, t, ROW_WORDS), mf, ln2_g[l][None, :], ln2_b[l][None, :])
    return x2.reshape(batch, seq, d)
```

```python
import functools
import math

import jax
import jax.numpy as jnp
from jax import lax
from jax.experimental import pallas as pl
from jax.experimental.pallas import tpu as pltpu
from jax.experimental.pallas import tpu_sc as plsc

F32 = jnp.float32
BF16 = jnp.bfloat16

D_MODEL = 1024
CHUNK = 64
RET_HEADS, RET_DK, RET_DV = 4, 64, 64
RET_W = RET_HEADS * RET_DV
SSD_HEADS, SSD_HEAD_DIM, SSD_STATE, SSD_GROUPS, SSD_CONV = 8, 64, 64, 2, 4
SSD_W = SSD_HEADS * SSD_HEAD_DIM
SSD_BC = SSD_GROUPS * SSD_STATE
SSD_XBC = SSD_W + 2 * SSD_BC
GLA_HEADS, GLA_DK, GLA_DV, GLA_RANK, GLA_TEMP = 4, 32, 64, 16, 16.0
GLA_QK = GLA_HEADS * GLA_DK
GLA_W = GLA_HEADS * GLA_DV
D_MIX = RET_W + SSD_W + GLA_W
N_EXPERTS, TOP_K, D_FF = 32, 4, 1024
SWIGLU_LIMIT, SWIGLU_ALPHA = 7.0, 1.702
ROPE_BASE = 10000.0
LN_EPS, NORM_EPS = 1e-5, 1e-6
DEPTH = 2
DEEPNORM_ALPHA = (2.0 * DEPTH) ** 0.25

LANES = 128
ROPE_W = LANES
NEG_BIG = -1e30
VMEM_LIMIT = 56 * 1024 * 1024

_SEGS = (("rq", 256), ("rk", 256), ("rv", 256), ("rg", 256), ("sz", SSD_W), ("sxbc", SSD_XBC),
         ("sdt", 128), ("gq", 128), ("gk", 128), ("gv", 256), ("ggk", 128), ("gg", 256))
COL = {}
_off = 0
for _n, _w in _SEGS:
    COL[_n] = _off
    _off += _w
NP = _off

MIX_G = 8
PROJ_SLAB = 512
TM_POST = 1024
POST_SPLIT = 4
TM_COMB = 1024
TM_ROPE = 2048
BM = 896
ROW_WORDS = D_MODEL // 2
SC_WORKERS = 32
SC_CHUNK = 64


def _dot(a, b, dims=(((1,), (0,)), ((), ())), precision=None):
    return lax.dot_general(a, b, dims, precision=precision, preferred_element_type=F32)


_NT = (((1,), (1,)), ((), ()))
_TN = (((0,), (0,)), ((), ()))


def _iota(shape, dim):
    return lax.broadcasted_iota(jnp.int32, shape, dim)


def _vdiv(x, n):
    assert n & (n - 1) == 0
    return lax.shift_right_logical(x, n.bit_length() - 1)


def _vmod(x, n):
    assert n & (n - 1) == 0
    return jnp.bitwise_and(x, n - 1)


def _silu(x):
    return x * jax.nn.sigmoid(x)


def _softplus(x):
    return jnp.maximum(x, 0.0) + jnp.log(1.0 + jnp.exp(-jnp.abs(x)))


def _pack_rows(x):
    w = x.shape[1] // 2
    lo = lax.bitcast_convert_type(x[:, :w].astype(BF16).astype(F32), jnp.uint32)
    hi = lax.bitcast_convert_type(x[:, w:].astype(BF16).astype(F32), jnp.uint32)
    return lax.bitcast_convert_type(lax.shift_right_logical(lo, jnp.uint32(16)) | hi, jnp.int32)


def _unpack_rows(words):
    u = lax.bitcast_convert_type(words, jnp.uint32)
    a = lax.bitcast_convert_type(lax.shift_left(u, jnp.uint32(16)), F32)
    b = lax.bitcast_convert_type(u & jnp.uint32(0xFFFF0000), F32)
    return jnp.concatenate([a, b], axis=-1)


def _seg_sum64(x):
    first = _iota((1, LANES), 1) < 64
    outs = []
    for j in range(x.shape[-1] // LANES):
        blk = x[:, j * LANES:(j + 1) * LANES]
        lo = jnp.sum(jnp.where(first, blk, 0.0), axis=-1, keepdims=True)
        hi = jnp.sum(jnp.where(first, 0.0, blk), axis=-1, keepdims=True)
        outs.append(jnp.where(first, lo, hi))
    return jnp.concatenate(outs, axis=-1)


def _block_mask(shape, row_blk, col_blk):
    keep = _vdiv(_iota(shape, 0), row_blk) == _vdiv(_iota(shape, 1), col_blk)
    return jnp.where(keep, 1.0, 0.0).astype(BF16)


def _block_diag(x, mask):
    reps = mask.shape[0] // x.shape[0]
    return jnp.concatenate([x.astype(BF16)] * reps, axis=0) * mask


def _expand_heads(x, expand):
    hi = x.astype(BF16)
    r1 = x - hi.astype(F32)
    mid = r1.astype(BF16)
    lo = (r1 - mid.astype(F32)).astype(BF16)
    return _dot(hi, expand) + _dot(mid, expand) + _dot(lo, expand)


def _cumsum_rows(tri, x):
    hi = x.astype(BF16)
    lo = (x - hi.astype(F32)).astype(BF16)
    return _dot(tri, hi) + _dot(tri, lo)


def _rope_kernel(pos_ref, cos_ref, sin_ref):
    lane = _iota((1, ROPE_W), 1)
    half = RET_DK // 2
    k = _vmod(lane, half).astype(F32)
    inv_freq = jnp.exp(k * (-math.log(ROPE_BASE) / half))
    ang = pos_ref[...].astype(F32) * inv_freq
    first = _vmod(lane, RET_DK) < half
    cos_ref[...] = jnp.cos(ang)
    sin_ref[...] = jnp.where(first, -1.0, 1.0) * jnp.sin(ang)


def _rope_tables(pos_col):
    t = pos_col.shape[0]
    tm = TM_ROPE
    w = ROPE_W
    return pl.pallas_call(
        _rope_kernel,
        grid=(t // tm,),
        in_specs=[pl.BlockSpec((tm, 1), lambda i: (i, 0))],
        out_specs=[pl.BlockSpec((tm, w), lambda i: (i, 0))] * 2,
        out_shape=[jax.ShapeDtypeStruct((t, w), F32)] * 2,
        compiler_params=pltpu.CompilerParams(dimension_semantics=("arbitrary",)),
        name="rope_tables",
    )(pos_col)


def _mixproj_kernel(xn_ref, w_ref, cos_ref, sin_ref, retw_ref, convw_ref, convb_ref, dtb_ref, alog_ref,
                    dskip_ref, ssdw_ref, wgk_ref, bgk_ref, glaw_ref, h_ref,
                    proj_a, proj_b, ret_s, ssd_s, gla_s, stage, m_heads, m_groups, m_gla, m_expand, *,
                    chunks_per_seq):
    C = CHUNK
    i = pl.program_id(0)
    cur = jnp.maximum(i - 1, 0)

    @pl.when(i == 0)
    def _():
        m_heads[...] = _block_mask(m_heads.shape, C, 64)
        m_groups[...] = _block_mask(m_groups.shape, C * SSD_HEADS // SSD_GROUPS, SSD_STATE)
        m_gla[...] = _block_mask(m_gla.shape, C, GLA_DK)
        m_expand[...] = _block_mask(m_expand.shape, 1, SSD_HEAD_DIM)
        proj_b[...] = jnp.zeros_like(proj_b)

    @pl.when(lax.rem(cur, chunks_per_seq) == 0)
    def _():
        ret_s[...] = jnp.zeros_like(ret_s)
        ssd_s[...] = jnp.zeros_like(ssd_s)
        gla_s[...] = jnp.zeros_like(gla_s)
        for g in range(MIX_G):
            stage[g, 0:8, :] = jnp.zeros((8, SSD_XBC), F32)

    def conv_act(sq, pref):
        stage[sq, 8:8 + C, :] = pref[sq * C:(sq + 1) * C, COL["sxbc"]:COL["sxbc"] + SSD_XBC]
        acc = convb_ref[...] + convw_ref[0:1, :] * stage[sq, 5:5 + C, :]
        for j in range(1, SSD_CONV):
            acc = acc + convw_ref[j:j + 1, :] * stage[sq, 5 + j:5 + j + C, :]
        stage[sq, 0:8, :] = stage[sq, C:C + 8, :]
        return _silu(acc)

    lane256 = _iota((1, 256), 1)
    head = _vdiv(lane256, 64).astype(F32)
    log_gamma = jnp.log(1.0 - jnp.exp((-5.0 - head) * math.log(2.0)))
    row = _iota((C, 1), 0).astype(F32)
    dist = row - _vmod(lane256, 64).astype(F32)
    ret_intra = jnp.where(dist >= 0, jnp.exp(log_gamma * jnp.maximum(dist, 0.0)), 0.0)
    ret_qdec = jnp.exp(log_gamma * (row + 1.0))
    ret_kdec = jnp.exp(log_gamma * (C - 1.0 - row))
    ret_cdec = jnp.exp(log_gamma * C)
    first_half = _vmod(lane256, RET_DK) < (RET_DK // 2)

    tri = jnp.where(_iota((C, C), 0) >= _iota((C, C), 1), 1.0, 0.0).astype(BF16)
    causal4 = _iota((C, 256), 0) >= _vmod(_iota((C, 256), 1), 64)
    causal8 = _iota((C, 512), 0) >= _vmod(_iota((C, 512), 1), 64)
    eye8 = _iota((C, 512), 0) == _vmod(_iota((C, 512), 1), 64)

    a_neg = -jnp.exp(alog_ref[...])

    def rot(t, cos, sin):
        sw = jnp.where(first_half, pltpu.roll(t, 256 - 32, 1), pltpu.roll(t, 32, 1))
        return t * cos + sw * sin

    def chunk_of(sq, pref):
        def seg(name, width):
            return pref[sq * C:(sq + 1) * C, COL[name]:COL[name] + width]

        xact = conv_act(sq, pref)
        yield

        reps = RET_HEADS * RET_DK // ROPE_W
        cos = jnp.concatenate([cos_ref[sq]] * reps, axis=-1)
        sin = jnp.concatenate([sin_ref[sq]] * reps, axis=-1)
        q = rot(seg("rq", 256), cos, sin)
        k = rot(seg("rk", 256), cos, sin) * (RET_DK ** -0.5)
        v = seg("rv", 256)
        vb = v.astype(BF16)
        kbd = _block_diag(k, m_heads[...])
        scores = _dot(q.astype(BF16), kbd, _NT) * ret_intra
        vbd = _block_diag(v, m_heads[...])
        yield
        s_prev = ret_s[sq]
        o = _dot(scores.astype(BF16), vbd) + _dot((q * ret_qdec).astype(BF16), s_prev.astype(BF16))
        contrib = _dot((k * ret_kdec).astype(BF16), vb, _TN)
        yield
        keep = _vdiv(_iota((256, 256), 0), RET_DK) == _vdiv(_iota((256, 256), 1), RET_DV)
        ret_s[sq] = jnp.where(keep, ret_cdec * s_prev + contrib, 0.0)
        mu = _seg_sum64(o) * (1.0 / RET_DV)
        oc = o - mu
        var = _seg_sum64(oc * oc) * (1.0 / RET_DV)
        o = oc * lax.rsqrt(var + LN_EPS) * retw_ref[...]
        h_ref[sq, :, 0:RET_W] = (_silu(seg("rg", 256)) * o).astype(BF16)
        yield

        xs = xact[:, 0:SSD_W]
        bm = xact[:, SSD_W:SSD_W + SSD_BC]
        cm = xact[:, SSD_W + SSD_BC:SSD_XBC]
        cmb = cm.astype(BF16)
        dt_c = _softplus(seg("sdt", LANES) + dtb_ref[...])
        acum_c = _cumsum_rows(tri, dt_c * a_neg)
        both = _expand_heads(jnp.concatenate([dt_c, acum_c], axis=0), m_expand[...])
        dt = both[0:C, :]
        acum = both[C:2 * C, :]
        yield
        arow = jnp.sum(jnp.where(eye8, acum, 0.0), axis=0, keepdims=True)
        decay = jnp.exp(jnp.where(causal8, acum - arow, NEG_BIG))
        b8 = _block_diag(bm, m_groups[...])
        cb = _dot(cmb, b8, _NT)
        yield
        m = (cb * decay).astype(BF16)
        xdt = xs * dt
        s2 = ssd_s[sq]
        half = SSD_W // SSD_GROUPS
        ys = []
        for g in range(SSD_GROUPS):
            xbd = _block_diag(xdt[:, g * half:(g + 1) * half], m_heads[...])
            ys.append(_dot(m[:, g * half:(g + 1) * half], xbd))
        y = jnp.concatenate(ys, axis=-1)
        y = y + _dot(cmb, s2.astype(BF16)) * jnp.exp(acum)
        y = y + dskip_ref[...] * xs
        a_last = acum[C - 1:C, :]
        sd = jnp.exp(a_last - acum)
        contrib_s = _dot(bm.astype(BF16), (xdt * sd).astype(BF16), _TN)
        yield
        keep_s = _vdiv(_iota(s2.shape, 0), SSD_STATE) == _vdiv(_iota(s2.shape, 1), half)
        ssd_s[sq] = jnp.where(keep_s, s2 * jnp.exp(a_last) + contrib_s, 0.0)
        yz = y * _silu(seg("sz", SSD_W))
        outs = []
        for g in range(SSD_GROUPS):
            blk = yz[:, g * half:(g + 1) * half]
            ms = jnp.mean(blk * blk, axis=-1, keepdims=True)
            outs.append(blk * lax.rsqrt(ms + NORM_EPS))
        h_ref[sq, :, RET_W:RET_W + SSD_W] = (jnp.concatenate(outs, axis=-1) * ssdw_ref[...]).astype(BF16)

        yield

        gq = seg("gq", GLA_QK) * (GLA_DK ** -0.5)
        gkk = seg("gk", GLA_QK)
        gv = seg("gv", GLA_W)
        gkl = _dot(seg("ggk", 128).astype(BF16), wgk_ref[...].astype(BF16)) + bgk_ref[...]
        yield
        log_a = -_softplus(-gkl) * (1.0 / GLA_TEMP)
        b = _cumsum_rows(tri, log_a)
        yield
        q_t = (gq * jnp.exp(b)).astype(BF16)
        k_t = gkk * jnp.exp(-b)
        kbd_g = _block_diag(k_t, m_gla[...])
        att = jnp.where(causal4, _dot(q_t, kbd_g, _NT), 0.0)
        yield
        vbd_g = _block_diag(gv, m_heads[...])
        st = gla_s[sq]
        og = _dot(att.astype(BF16), vbd_g) + _dot(q_t, st.astype(BF16), _NT)
        b_last = b[C - 1:C, :]
        kd = (gkk * jnp.exp(b_last - b)).astype(BF16)
        contrib_g = _dot(gv.astype(BF16), kd, _TN)
        yield
        keep_g = _vdiv(_iota(st.shape, 0), GLA_DV) == _vdiv(_iota(st.shape, 1), GLA_DK)
        gla_s[sq] = jnp.where(keep_g, st * jnp.exp(b_last) + contrib_g, 0.0)
        ms = _seg_sum64(og * og) * (1.0 / GLA_DV)
        og = og * lax.rsqrt(ms + NORM_EPS) * glaw_ref[...]
        h_ref[sq, :, RET_W + SSD_W:D_MIX] = (_silu(seg("gg", GLA_W)) * og).astype(BF16)

    def step(p_read, p_write):
        xb = xn_ref[...].reshape(MIX_G * C, D_MODEL).astype(BF16)
        edges = list(range(0, NP, PROJ_SLAB)) + [NP]
        slabs = list(zip(edges[:-1], edges[1:]))

        live = [chunk_of(sq, p_read) for sq in range(MIX_G)]
        while live or slabs:
            live = [g for g in live if next(g, "done") != "done"]
            if slabs:
                lo, hi = slabs.pop(0)
                p_write[:, lo:hi] = _dot(xb, w_ref[:, lo:hi])

    @pl.when(lax.rem(i, 2) == 0)
    def _():
        step(proj_b, proj_a)

    @pl.when(lax.rem(i, 2) == 1)
    def _():
        step(proj_a, proj_b)


def _mixproj(x3, w_p, cos_t, sin_t, params):
    batch, seq, _ = x3.shape
    assert batch % MIX_G == 0 and seq % CHUNK == 0
    cps = seq // CHUNK
    n = (batch // MIX_G) * cps

    def cur_map(i):
        c = jnp.clip(i - 1, 0, n - 1)
        return (c // cps, c % cps, 0)

    def next_map(i):
        c = jnp.minimum(i, n - 1)
        return (c // cps, c % cps, 0)

    const = lambda i: (0, 0)
    specs = [pl.BlockSpec((MIX_G, CHUNK, D_MODEL), next_map),
             pl.BlockSpec((D_MODEL, NP), const),
             pl.BlockSpec((MIX_G, CHUNK, ROPE_W), cur_map),
             pl.BlockSpec((MIX_G, CHUNK, ROPE_W), cur_map)]
    specs += [pl.BlockSpec(p.shape, const) for p in params]
    return pl.pallas_call(
        functools.partial(_mixproj_kernel, chunks_per_seq=cps),
        grid=(n + 1,),
        in_specs=specs,
        out_specs=pl.BlockSpec((MIX_G, CHUNK, D_MIX), cur_map),
        out_shape=jax.ShapeDtypeStruct((batch, seq, D_MIX), BF16),
        scratch_shapes=[pltpu.VMEM((MIX_G * CHUNK, NP), F32),
                        pltpu.VMEM((MIX_G * CHUNK, NP), F32),
                        pltpu.VMEM((MIX_G, 256, 256), F32),
                        pltpu.VMEM((MIX_G, SSD_BC, SSD_W), F32),
                        pltpu.VMEM((MIX_G, GLA_W, GLA_QK), F32),
                        pltpu.VMEM((MIX_G, CHUNK + 8, SSD_XBC), F32),
                        pltpu.VMEM((RET_HEADS * CHUNK, 256), BF16),
                        pltpu.VMEM((SSD_HEADS * CHUNK, SSD_BC), BF16),
                        pltpu.VMEM((GLA_HEADS * CHUNK, GLA_QK), BF16),
                        pltpu.VMEM((LANES, SSD_W), BF16)],
        compiler_params=pltpu.CompilerParams(dimension_semantics=("arbitrary",),
                                             vmem_limit_bytes=VMEM_LIMIT),
        name="inproj_mixer",
    )(x3, w_p, cos_t, sin_t, *params)


def _layer_norm(y, g, b):
    mu = jnp.mean(y, axis=-1, keepdims=True)
    yc = y - mu
    var = jnp.mean(yc * yc, axis=-1, keepdims=True)
    return yc * lax.rsqrt(var + LN_EPS) * g + b


def _post_kernel(h_ref, x_ref, wout_ref, g_ref, b_ref, wrh_ref, wrc_ref, br_ref,
                 x1_ref, x1p_ref, mi_ref, mf_ref, cnt_ref, carry):
    sub = TM_POST // POST_SPLIT

    @pl.when(pl.program_id(0) == 0)
    def _():
        carry[...] = jnp.zeros_like(carry)

    lane_i = _iota((sub, LANES), 1)
    lane = lane_i.astype(F32)
    found = {}

    def sub_tile(part):
        rows = slice(part * sub, (part + 1) * sub)
        mix = _dot(h_ref[rows, :], wout_ref[...])
        yield
        x1 = _layer_norm(DEEPNORM_ALPHA * x_ref[rows, :] + mix, g_ref[...], b_ref[...])
        x1_ref[rows, :] = x1
        x1p_ref[rows, :] = _pack_rows(x1)
        x_hi = x1.astype(BF16)
        x_lo = (x1 - x_hi.astype(F32)).astype(BF16)
        both = _dot(x_hi, wrc_ref[...])
        logits = both[:, :LANES] + both[:, LANES:] + _dot(x_lo, wrh_ref[...]) + br_ref[...]
        yield
        work = logits
        vals, idxs = [], []
        multi = jnp.zeros((sub, LANES), F32)
        for _ in range(TOP_K):
            m = jnp.max(work, axis=-1, keepdims=True)
            idx = jnp.min(jnp.where(work == m, lane, float(LANES)), axis=-1, keepdims=True)
            hit = lane == idx
            multi = multi + hit.astype(F32)
            work = jnp.where(hit, -jnp.inf, work)
            vals.append(m)
            idxs.append(idx)
            yield
        exps = [jnp.exp(v - vals[0]) for v in vals]
        denom = exps[0] + exps[1] + exps[2] + exps[3]
        gates = [e / denom for e in exps]
        before = (_iota((sub, sub), 0) > _iota((sub, sub), 1)).astype(BF16)
        found[part] = (idxs, gates, _dot(before, multi.astype(BF16)), multi)

    live = [sub_tile(part) for part in range(POST_SPLIT)]
    while live:
        live = [g for g in live if next(g, "done") != "done"]

    base = carry[...]
    for part in range(POST_SPLIT):
        idxs, gates, prior_local, multi = found[part]
        prior = prior_local + base
        mi = jnp.zeros((sub, LANES), F32)
        mf = jnp.zeros((sub, LANES), F32)
        for kk in range(TOP_K):
            rank = jnp.sum(jnp.where(lane == idxs[kk], prior, 0.0), axis=-1, keepdims=True)
            mi = jnp.where(lane_i == kk, idxs[kk], mi)
            mi = jnp.where(lane_i == TOP_K + kk, rank, mi)
            mf = jnp.where(lane_i == kk, gates[kk], mf)
        mi_ref[:, part * sub:(part + 1) * sub] = jnp.transpose(mi)[0:2 * TOP_K, :].astype(jnp.int32)
        mf_ref[part * sub:(part + 1) * sub, :] = mf
        base = base + jnp.sum(multi, axis=0, keepdims=True)
    carry[...] = base
    cnt_ref[...] = jnp.broadcast_to(base, cnt_ref.shape)


def _post(h, x2, w_out_b, ln_g, ln_b, wr_hi, wr_cat, br_p):
    t = x2.shape[0]
    tm = TM_POST
    row = lambda i: (i, 0)
    const = lambda i: (0, 0)
    return pl.pallas_call(
        _post_kernel,
        grid=(t // tm,),
        in_specs=[pl.BlockSpec((tm, D_MIX), row), pl.BlockSpec((tm, D_MODEL), row),
                  pl.BlockSpec((D_MIX, D_MODEL), const), pl.BlockSpec((1, D_MODEL), const),
                  pl.BlockSpec((1, D_MODEL), const), pl.BlockSpec((D_MODEL, LANES), const),
                  pl.BlockSpec((D_MODEL, 2 * LANES), const), pl.BlockSpec((1, LANES), const)],
        out_specs=[pl.BlockSpec((tm, D_MODEL), row), pl.BlockSpec((tm, ROW_WORDS), row),
                   pl.BlockSpec((2 * TOP_K, tm), lambda i: (0, i)), pl.BlockSpec((tm, LANES), row),
                   pl.BlockSpec((8, LANES), const)],
        out_shape=[jax.ShapeDtypeStruct((t, D_MODEL), F32), jax.ShapeDtypeStruct((t, ROW_WORDS), jnp.int32),
                   jax.ShapeDtypeStruct((2 * TOP_K, t), jnp.int32), jax.ShapeDtypeStruct((t, LANES), F32),
                   jax.ShapeDtypeStruct((8, LANES), F32)],
        scratch_shapes=[pltpu.VMEM((1, LANES), F32)],
        compiler_params=pltpu.CompilerParams(dimension_semantics=("arbitrary",),
                                             vmem_limit_bytes=VMEM_LIMIT),
        name="outproj_ln_router",
    )(h, x2, w_out_b, ln_g, ln_b, wr_hi, wr_cat, br_p)


def _ffn_kernel(be_ref, nv_ref, x_ref, wg_ref, bg_ref, wu_ref, bu_ref, wd_ref, bd_ref, o_ref,
                wg_b, wu_b, wd_b):
    i = pl.program_id(0)
    valid = i < nv_ref[0]
    e = be_ref[i]
    prev = be_ref[jnp.maximum(i - 1, 0)]
    fresh = jnp.logical_or(i == 0, e != prev)

    @pl.when(jnp.logical_and(valid, fresh))
    def _():
        wg_b[...] = wg_ref[0, 0].astype(BF16)
        wu_b[...] = wu_ref[0, 0].astype(BF16)
        wd_b[...] = wd_ref[0, 0].astype(BF16)

    @pl.when(valid)
    def _():
        x = _unpack_rows(x_ref[...]).astype(BF16)
        hg = jnp.minimum(_dot(x, wg_b[...]) + bg_ref[0, 0], SWIGLU_LIMIT)
        hu = jnp.clip(_dot(x, wu_b[...]) + bu_ref[0, 0], -SWIGLU_LIMIT, SWIGLU_LIMIT)
        hh = (hu + 1.0) * hg * jax.nn.sigmoid(SWIGLU_ALPHA * hg)
        o_ref[...] = _pack_rows(_dot(hh.astype(BF16), wd_b[...]) + bd_ref[0, 0])


def _ffn(layer, block_expert, n_valid, xin, wg, bg, wu, bu, wd, bd):
    cap = xin.shape[0]
    nb = cap // BM

    def blk(i, be, nv):
        return jnp.maximum(jnp.minimum(i, nv[0] - 1), 0)

    row = lambda i, be, nv: (blk(i, be, nv), 0)
    wmap = lambda i, be, nv: (layer, be[blk(i, be, nv)], 0, 0)
    grid_spec = pltpu.PrefetchScalarGridSpec(
        num_scalar_prefetch=2,
        grid=(nb,),
        in_specs=[pl.BlockSpec((BM, ROW_WORDS), row),
                  pl.BlockSpec((1, 1, D_MODEL, D_FF), wmap), pl.BlockSpec((1, 1, 1, D_FF), wmap),
                  pl.BlockSpec((1, 1, D_MODEL, D_FF), wmap), pl.BlockSpec((1, 1, 1, D_FF), wmap),
                  pl.BlockSpec((1, 1, D_FF, D_MODEL), wmap), pl.BlockSpec((1, 1, 1, D_MODEL), wmap)],
        out_specs=pl.BlockSpec((BM, ROW_WORDS), row),
        scratch_shapes=[pltpu.VMEM((D_MODEL, D_FF), BF16), pltpu.VMEM((D_MODEL, D_FF), BF16),
                        pltpu.VMEM((D_FF, D_MODEL), BF16)],
    )
    return pl.pallas_call(
        _ffn_kernel,
        grid_spec=grid_spec,
        out_shape=jax.ShapeDtypeStruct((cap, ROW_WORDS), jnp.int32),
        compiler_params=pltpu.CompilerParams(dimension_semantics=("arbitrary",),
                                             vmem_limit_bytes=VMEM_LIMIT),
        name="expert_ffn",
    )(block_expert, n_valid, xin, wg, bg, wu, bu, wd, bd)


def _sc_gather(table, idx3):
    nw, n_chunks, ch = idx3.shape
    width = table.shape[1]
    per_worker = n_chunks * ch
    mesh = plsc.VectorSubcoreMesh(core_axis_name="c", subcore_axis_name="s")
    n_cores = mesh.num_cores
    assert nw == n_cores * mesh.num_subcores and n_chunks % 2 == 0 and ch == SC_CHUNK

    def body(table_hbm, idx_hbm, out_hbm, idx_v, rows0, rows1, sem_g0, sem_g1, sem_w0, sem_w1):
        wid = lax.axis_index("s") * n_cores + lax.axis_index("c")
        base = wid * per_worker
        pltpu.sync_copy(idx_hbm.at[wid], idx_v)

        @pl.loop(0, n_chunks, step=2)
        def _(c):
            g0 = pltpu.async_copy(table_hbm.at[idx_v.at[c]], rows0, sem_g0)
            g1 = pltpu.async_copy(table_hbm.at[idx_v.at[c + 1]], rows1, sem_g1)
            g0.wait()
            w0 = pltpu.async_copy(rows0, out_hbm.at[pl.ds(base + c * ch, ch)], sem_w0)
            g1.wait()
            w1 = pltpu.async_copy(rows1, out_hbm.at[pl.ds(base + (c + 1) * ch, ch)], sem_w1)
            w0.wait()
            w1.wait()

    return pl.kernel(
        body,
        out_type=jax.ShapeDtypeStruct((nw * per_worker, width), table.dtype),
        mesh=mesh,
        scratch_types=[pltpu.VMEM((n_chunks, ch), jnp.int32),
                       pltpu.VMEM((ch, width), table.dtype), pltpu.VMEM((ch, width), table.dtype),
                       pltpu.SemaphoreType.DMA, pltpu.SemaphoreType.DMA,
                       pltpu.SemaphoreType.DMA, pltpu.SemaphoreType.DMA],
        name="sc_row_gather",
    )(table, idx3)


def _sc_scatter(rows, idx3, n_out):
    nw, n_lists, ch = idx3.shape
    n_chunks = n_lists // TOP_K
    width = rows.shape[1]
    per_worker = n_chunks * ch
    mesh = plsc.VectorSubcoreMesh(core_axis_name="c", subcore_axis_name="s")
    n_cores = mesh.num_cores
    assert nw == n_cores * mesh.num_subcores and n_chunks % 2 == 0 and ch == SC_CHUNK
    assert nw * per_worker == rows.shape[0]

    def body(rows_hbm, idx_hbm, out_hbm, idx_v, buf0, buf1, sem_r0, sem_r1, sem_w0, sem_w1):
        wid = lax.axis_index("s") * n_cores + lax.axis_index("c")
        base = wid * per_worker
        pltpu.sync_copy(idx_hbm.at[wid], idx_v)

        @pl.loop(0, n_chunks, step=2)
        def _(c):
            r0 = pltpu.async_copy(rows_hbm.at[pl.ds(base + c * ch, ch)], buf0, sem_r0)
            r1 = pltpu.async_copy(rows_hbm.at[pl.ds(base + (c + 1) * ch, ch)], buf1, sem_r1)
            r0.wait()
            w0 = [pltpu.async_copy(buf0, out_hbm.at[idx_v.at[c * TOP_K + k]], sem_w0) for k in range(TOP_K)]
            r1.wait()
            w1 = [pltpu.async_copy(buf1, out_hbm.at[idx_v.at[(c + 1) * TOP_K + k]], sem_w1) for k in range(TOP_K)]
            for w in w0 + w1:
                w.wait()

    return pl.kernel(
        body,
        out_type=jax.ShapeDtypeStruct((n_out, width), rows.dtype),
        mesh=mesh,
        scratch_types=[pltpu.VMEM((n_lists, ch), jnp.int32),
                       pltpu.VMEM((ch, width), rows.dtype), pltpu.VMEM((ch, width), rows.dtype),
                       pltpu.SemaphoreType.DMA, pltpu.SemaphoreType.DMA,
                       pltpu.SemaphoreType.DMA, pltpu.SemaphoreType.DMA],
        name="sc_row_scatter",
    )(rows, idx3)


def _combine_kernel(x_ref, rows_ref, mf_ref, g_ref, b_ref, o_ref):
    mf = mf_ref[...]
    y = DEEPNORM_ALPHA * x_ref[...]
    for kk in range(TOP_K):
        y = y + mf[:, kk:kk + 1] * _unpack_rows(rows_ref[kk])
    o_ref[...] = _layer_norm(y, g_ref[...], b_ref[...])


def _combine(x1, rows, mf, ln_g, ln_b):
    t = x1.shape[0]
    tm = TM_COMB
    row = lambda i: (i, 0)
    const = lambda i: (0, 0)
    return pl.pallas_call(
        _combine_kernel,
        grid=(t // tm,),
        in_specs=[pl.BlockSpec((tm, D_MODEL), row),
                  pl.BlockSpec((TOP_K, tm, ROW_WORDS), lambda i: (0, i, 0)),
                  pl.BlockSpec((tm, LANES), row),
                  pl.BlockSpec((1, D_MODEL), const), pl.BlockSpec((1, D_MODEL), const)],
        out_specs=pl.BlockSpec((tm, D_MODEL), row),
        out_shape=jax.ShapeDtypeStruct((t, D_MODEL), F32),
        compiler_params=pltpu.CompilerParams(dimension_semantics=("arbitrary",),
                                             vmem_limit_bytes=VMEM_LIMIT),
        name="combine_ln",
    )(x1, rows, mf, ln_g, ln_b)


def _relayout_w_in(w):
    widths = (256, 256, 256, 256, SSD_W, SSD_XBC, SSD_HEADS, GLA_QK, GLA_QK, GLA_W, GLA_RANK, GLA_W)
    offs = [0]
    for wd in widths:
        offs.append(offs[-1] + wd)
    parts = [w[:, offs[i]:offs[i + 1]] for i in range(len(widths))]
    parts[6] = jnp.pad(parts[6], ((0, 0), (0, LANES - SSD_HEADS)))
    parts[10] = jnp.pad(parts[10], ((0, 0), (0, LANES - GLA_RANK)))
    return jnp.concatenate(parts, axis=1).astype(BF16)


def _rep_heads(p):
    return jnp.repeat(p, SSD_HEAD_DIM)[None, :]


def _pad_heads(p):
    return jnp.pad(p, (0, LANES - SSD_HEADS))[None, :]


def kernel(x, positions, w_in, w_out, ret_norm_w, ssd_conv_w, ssd_conv_b, ssd_dt_bias, ssd_a_log, ssd_d,
           ssd_norm_w, gla_w_gk2, gla_b_gk2, gla_norm_w, ln1_g, ln1_b, w_router, b_router, w_gate, b_gate,
           w_up, b_up, w_down, b_down, ln2_g, ln2_b):
    batch, seq, d = x.shape
    t = batch * seq
    depth = w_in.shape[0]
    assert d == D_MODEL and t % TM_POST == 0 and t % TM_COMB == 0 and t % TM_ROPE == 0
    n_assign = t * TOP_K
    nb = pl.cdiv(n_assign, BM) + N_EXPERTS
    cap = nb * BM

    cos_t, sin_t = _rope_tables(positions.reshape(t, 1))
    cos_t = cos_t.reshape(batch, seq, -1)
    sin_t = sin_t.reshape(batch, seq, -1)
    x2 = x.reshape(t, d)

    for l in range(depth):
        params = (ret_norm_w[l][None, :], ssd_conv_w[l], ssd_conv_b[l][None, :], _pad_heads(ssd_dt_bias[l]),
                  _pad_heads(ssd_a_log[l]), _rep_heads(ssd_d[l]), ssd_norm_w[l][None, :],
                  jnp.pad(gla_w_gk2[l], ((0, LANES - GLA_RANK), (0, 0))), gla_b_gk2[l][None, :],
                  gla_norm_w[l][None, :])
        h = _mixproj(x2.reshape(batch, seq, d), _relayout_w_in(w_in[l]), cos_t, sin_t, params).reshape(t, D_MIX)

        wr_p = jnp.pad(w_router[l], ((0, 0), (0, LANES - N_EXPERTS)))
        br_p = jnp.pad(b_router[l], (0, LANES - N_EXPERTS), constant_values=NEG_BIG)[None, :]
        wr_hi = wr_p.astype(BF16)
        wr_lo = (wr_p - wr_hi.astype(F32)).astype(BF16)
        x1, x1p, mi, mf, cnt = _post(h, x2, w_out[l].astype(BF16), ln1_g[l][None, :], ln1_b[l][None, :],
                                     wr_hi, jnp.concatenate([wr_hi, wr_lo], axis=1), br_p)

        counts = cnt[0, :N_EXPERTS].astype(jnp.int32)
        padded = (counts + BM - 1) // BM * BM
        end_padded = jnp.cumsum(padded)
        start_padded = end_padded - padded
        top_idx = mi[:TOP_K]
        start_of = jnp.sum(jnp.where(top_idx[None] == jnp.arange(N_EXPERTS, dtype=jnp.int32)[:, None, None],
                                     start_padded[:, None, None], 0), axis=0)
        dest = start_of + mi[TOP_K:]
        block_start = jnp.arange(nb, dtype=jnp.int32) * BM
        block_expert = jnp.minimum(jnp.sum((end_padded[None, :] <= block_start[:, None]).astype(jnp.int32), axis=1),
                                   N_EXPERTS - 1)
        n_valid = (end_padded[-1:] // BM).astype(jnp.int32)

        scatter_idx = dest.reshape(TOP_K, SC_WORKERS, -1, SC_CHUNK).transpose(1, 2, 0, 3)
        xin = _sc_scatter(x1p, scatter_idx.reshape(SC_WORKERS, -1, SC_CHUNK), cap)
        yb = _ffn(l, block_expert, n_valid, xin, w_gate, b_gate[:, :, None, :], w_up, b_up[:, :, None, :],
                  w_down, b_down[:, :, None, :])
        rows = _sc_gather(yb, dest.reshape(SC_WORKERS, -1, SC_CHUNK))
        x2 = _combine(x1, rows.reshape(TOP_K, t, ROW_WORDS), mf, ln2_g[l][None, :], ln2_b[l][None, :])
    return x2.reshape(batch, seq, d)
```

```python
import functools
import math

import jax
import jax.numpy as jnp
from jax import lax
from jax.experimental import pallas as pl
from jax.experimental.pallas import tpu as pltpu
from jax.experimental.pallas import tpu_sc as plsc

F32 = jnp.float32
BF16 = jnp.bfloat16

D_MODEL = 1024
CHUNK = 64
RET_HEADS, RET_DK, RET_DV = 4, 64, 64
RET_W = RET_HEADS * RET_DV
SSD_HEADS, SSD_HEAD_DIM, SSD_STATE, SSD_GROUPS, SSD_CONV = 8, 64, 64, 2, 4
SSD_W = SSD_HEADS * SSD_HEAD_DIM
SSD_BC = SSD_GROUPS * SSD_STATE
SSD_XBC = SSD_W + 2 * SSD_BC
GLA_HEADS, GLA_DK, GLA_DV, GLA_RANK, GLA_TEMP = 4, 32, 64, 16, 16.0
GLA_QK = GLA_HEADS * GLA_DK
GLA_W = GLA_HEADS * GLA_DV
D_MIX = RET_W + SSD_W + GLA_W
N_EXPERTS, TOP_K, D_FF = 32, 4, 1024
SWIGLU_LIMIT, SWIGLU_ALPHA = 7.0, 1.702
ROPE_BASE = 10000.0
LN_EPS, NORM_EPS = 1e-5, 1e-6
DEPTH = 2
DEEPNORM_ALPHA = (2.0 * DEPTH) ** 0.25

LANES = 128
ROPE_W = LANES
NEG_BIG = -1e30
VMEM_LIMIT = 56 * 1024 * 1024

_SEGS = (("rq", 256), ("rk", 256), ("rv", 256), ("rg", 256), ("sz", SSD_W), ("sxbc", SSD_XBC),
         ("sdt", 128), ("gq", 128), ("gk", 128), ("gv", 256), ("ggk", 128), ("gg", 256))
COL = {}
_off = 0
for _n, _w in _SEGS:
    COL[_n] = _off
    _off += _w
NP = _off

MIX_G = 8
PROJ_SLAB = 512
TM_POST = 1024
POST_SPLIT = 4
TM_COMB = 1024
TM_ROPE = 2048
BM = 768
FFN_SUB = 256
ROW_WORDS = D_MODEL // 2
SC_WORKERS = 32
SC_CHUNK = 64


def _dot(a, b, dims=(((1,), (0,)), ((), ())), precision=None):
    return lax.dot_general(a, b, dims, precision=precision, preferred_element_type=F32)


_NT = (((1,), (1,)), ((), ()))
_TN = (((0,), (0,)), ((), ()))


def _iota(shape, dim):
    return lax.broadcasted_iota(jnp.int32, shape, dim)


def _vdiv(x, n):
    assert n & (n - 1) == 0
    return lax.shift_right_logical(x, n.bit_length() - 1)


def _vmod(x, n):
    assert n & (n - 1) == 0
    return jnp.bitwise_and(x, n - 1)


def _silu(x):
    return x * jax.nn.sigmoid(x)


def _softplus(x):
    return jnp.maximum(x, 0.0) + jnp.log(1.0 + jnp.exp(-jnp.abs(x)))


def _pack_rows(x):
    w = x.shape[1] // 2
    lo = lax.bitcast_convert_type(x[:, :w].astype(BF16).astype(F32), jnp.uint32)
    hi = lax.bitcast_convert_type(x[:, w:].astype(BF16).astype(F32), jnp.uint32)
    return lax.bitcast_convert_type(lax.shift_right_logical(lo, jnp.uint32(16)) | hi, jnp.int32)


def _unpack_rows(words):
    u = lax.bitcast_convert_type(words, jnp.uint32)
    a = lax.bitcast_convert_type(lax.shift_left(u, jnp.uint32(16)), F32)
    b = lax.bitcast_convert_type(u & jnp.uint32(0xFFFF0000), F32)
    return jnp.concatenate([a, b], axis=-1)


def _seg_sum64(x):
    first = _iota((1, LANES), 1) < 64
    outs = []
    for j in range(x.shape[-1] // LANES):
        blk = x[:, j * LANES:(j + 1) * LANES]
        lo = jnp.sum(jnp.where(first, blk, 0.0), axis=-1, keepdims=True)
        hi = jnp.sum(jnp.where(first, 0.0, blk), axis=-1, keepdims=True)
        outs.append(jnp.where(first, lo, hi))
    return jnp.concatenate(outs, axis=-1)


def _block_mask(shape, row_blk, col_blk):
    keep = _vdiv(_iota(shape, 0), row_blk) == _vdiv(_iota(shape, 1), col_blk)
    return jnp.where(keep, 1.0, 0.0).astype(BF16)


def _block_diag(x, mask):
    reps = mask.shape[0] // x.shape[0]
    return jnp.concatenate([x.astype(BF16)] * reps, axis=0) * mask


def _expand_heads(x, expand):
    hi = x.astype(BF16)
    r1 = x - hi.astype(F32)
    mid = r1.astype(BF16)
    lo = (r1 - mid.astype(F32)).astype(BF16)
    return _dot(hi, expand) + _dot(mid, expand) + _dot(lo, expand)


def _cumsum_rows(tri, x):
    hi = x.astype(BF16)
    lo = (x - hi.astype(F32)).astype(BF16)
    return _dot(tri, hi) + _dot(tri, lo)


def _rope_kernel(pos_ref, cos_ref, sin_ref):
    lane = _iota((1, ROPE_W), 1)
    half = RET_DK // 2
    k = _vmod(lane, half).astype(F32)
    inv_freq = jnp.exp(k * (-math.log(ROPE_BASE) / half))
    ang = pos_ref[...].astype(F32) * inv_freq
    first = _vmod(lane, RET_DK) < half
    cos_ref[...] = jnp.cos(ang)
    sin_ref[...] = jnp.where(first, -1.0, 1.0) * jnp.sin(ang)


def _rope_tables(pos_col):
    t = pos_col.shape[0]
    tm = TM_ROPE
    w = ROPE_W
    return pl.pallas_call(
        _rope_kernel,
        grid=(t // tm,),
        in_specs=[pl.BlockSpec((tm, 1), lambda i: (i, 0))],
        out_specs=[pl.BlockSpec((tm, w), lambda i: (i, 0))] * 2,
        out_shape=[jax.ShapeDtypeStruct((t, w), F32)] * 2,
        compiler_params=pltpu.CompilerParams(dimension_semantics=("arbitrary",)),
        name="rope_tables",
    )(pos_col)


def _mixproj_kernel(xn_ref, w_ref, cos_ref, sin_ref, retw_ref, convw_ref, convb_ref, dtb_ref, alog_ref,
                    dskip_ref, ssdw_ref, wgk_ref, bgk_ref, glaw_ref, h_ref,
                    proj_a, proj_b, ret_s, ssd_s, gla_s, stage, m_heads, m_groups, m_gla, m_expand, *,
                    chunks_per_seq):
    C = CHUNK
    i = pl.program_id(0)
    cur = jnp.maximum(i - 1, 0)

    @pl.when(i == 0)
    def _():
        m_heads[...] = _block_mask(m_heads.shape, C, 64)
        m_groups[...] = _block_mask(m_groups.shape, C * SSD_HEADS // SSD_GROUPS, SSD_STATE)
        m_gla[...] = _block_mask(m_gla.shape, C, GLA_DK)
        m_expand[...] = _block_mask(m_expand.shape, 1, SSD_HEAD_DIM)
        proj_b[...] = jnp.zeros_like(proj_b)

    @pl.when(lax.rem(cur, chunks_per_seq) == 0)
    def _():
        ret_s[...] = jnp.zeros_like(ret_s)
        ssd_s[...] = jnp.zeros_like(ssd_s)
        gla_s[...] = jnp.zeros_like(gla_s)
        for g in range(MIX_G):
            stage[g, 0:8, :] = jnp.zeros((8, SSD_XBC), F32)

    def conv_act(sq, pref):
        stage[sq, 8:8 + C, :] = pref[sq * C:(sq + 1) * C, COL["sxbc"]:COL["sxbc"] + SSD_XBC]
        acc = convb_ref[...] + convw_ref[0:1, :] * stage[sq, 5:5 + C, :]
        for j in range(1, SSD_CONV):
            acc = acc + convw_ref[j:j + 1, :] * stage[sq, 5 + j:5 + j + C, :]
        stage[sq, 0:8, :] = stage[sq, C:C + 8, :]
        return _silu(acc)

    lane256 = _iota((1, 256), 1)
    head = _vdiv(lane256, 64).astype(F32)
    log_gamma = jnp.log(1.0 - jnp.exp((-5.0 - head) * math.log(2.0)))
    row = _iota((C, 1), 0).astype(F32)
    dist = row - _vmod(lane256, 64).astype(F32)
    ret_intra = jnp.where(dist >= 0, jnp.exp(log_gamma * jnp.maximum(dist, 0.0)), 0.0)
    ret_qdec = jnp.exp(log_gamma * (row + 1.0))
    ret_kdec = jnp.exp(log_gamma * (C - 1.0 - row))
    ret_cdec = jnp.exp(log_gamma * C)
    first_half = _vmod(lane256, RET_DK) < (RET_DK // 2)

    tri = jnp.where(_iota((C, C), 0) >= _iota((C, C), 1), 1.0, 0.0).astype(BF16)
    causal4 = _iota((C, 256), 0) >= _vmod(_iota((C, 256), 1), 64)
    causal8 = _iota((C, 512), 0) >= _vmod(_iota((C, 512), 1), 64)
    eye8 = _iota((C, 512), 0) == _vmod(_iota((C, 512), 1), 64)

    a_neg = -jnp.exp(alog_ref[...])

    def rot(t, cos, sin):
        sw = jnp.where(first_half, pltpu.roll(t, 256 - 32, 1), pltpu.roll(t, 32, 1))
        return t * cos + sw * sin

    def chunk_of(sq, pref):
        def seg(name, width):
            return pref[sq * C:(sq + 1) * C, COL[name]:COL[name] + width]

        xact = conv_act(sq, pref)
        yield

        reps = RET_HEADS * RET_DK // ROPE_W
        cos = jnp.concatenate([cos_ref[sq]] * reps, axis=-1)
        sin = jnp.concatenate([sin_ref[sq]] * reps, axis=-1)
        q = rot(seg("rq", 256), cos, sin)
        k = rot(seg("rk", 256), cos, sin) * (RET_DK ** -0.5)
        v = seg("rv", 256)
        vb = v.astype(BF16)
        kbd = _block_diag(k, m_heads[...])
        scores = _dot(q.astype(BF16), kbd, _NT) * ret_intra
        vbd = _block_diag(v, m_heads[...])
        yield
        s_prev = ret_s[sq]
        o = _dot(scores.astype(BF16), vbd) + _dot((q * ret_qdec).astype(BF16), s_prev.astype(BF16))
        contrib = _dot((k * ret_kdec).astype(BF16), vb, _TN)
        yield
        keep = _vdiv(_iota((256, 256), 0), RET_DK) == _vdiv(_iota((256, 256), 1), RET_DV)
        ret_s[sq] = jnp.where(keep, ret_cdec * s_prev + contrib, 0.0)
        mu = _seg_sum64(o) * (1.0 / RET_DV)
        oc = o - mu
        var = _seg_sum64(oc * oc) * (1.0 / RET_DV)
        o = oc * lax.rsqrt(var + LN_EPS) * retw_ref[...]
        h_ref[sq, :, 0:RET_W] = (_silu(seg("rg", 256)) * o).astype(BF16)
        yield

        xs = xact[:, 0:SSD_W]
        bm = xact[:, SSD_W:SSD_W + SSD_BC]
        cm = xact[:, SSD_W + SSD_BC:SSD_XBC]
        cmb = cm.astype(BF16)
        dt_c = _softplus(seg("sdt", LANES) + dtb_ref[...])
        acum_c = _cumsum_rows(tri, dt_c * a_neg)
        both = _expand_heads(jnp.concatenate([dt_c, acum_c], axis=0), m_expand[...])
        dt = both[0:C, :]
        acum = both[C:2 * C, :]
        yield
        arow = jnp.sum(jnp.where(eye8, acum, 0.0), axis=0, keepdims=True)
        decay = jnp.exp(jnp.where(causal8, acum - arow, NEG_BIG))
        b8 = _block_diag(bm, m_groups[...])
        cb = _dot(cmb, b8, _NT)
        yield
        m = (cb * decay).astype(BF16)
        xdt = xs * dt
        s2 = ssd_s[sq]
        half = SSD_W // SSD_GROUPS
        ys = []
        for g in range(SSD_GROUPS):
            xbd = _block_diag(xdt[:, g * half:(g + 1) * half], m_heads[...])
            ys.append(_dot(m[:, g * half:(g + 1) * half], xbd))
        y = jnp.concatenate(ys, axis=-1)
        y = y + _dot(cmb, s2.astype(BF16)) * jnp.exp(acum)
        y = y + dskip_ref[...] * xs
        a_last = acum[C - 1:C, :]
        sd = jnp.exp(a_last - acum)
        contrib_s = _dot(bm.astype(BF16), (xdt * sd).astype(BF16), _TN)
        yield
        keep_s = _vdiv(_iota(s2.shape, 0), SSD_STATE) == _vdiv(_iota(s2.shape, 1), half)
        ssd_s[sq] = jnp.where(keep_s, s2 * jnp.exp(a_last) + contrib_s, 0.0)
        yz = y * _silu(seg("sz", SSD_W))
        outs = []
        for g in range(SSD_GROUPS):
            blk = yz[:, g * half:(g + 1) * half]
            ms = jnp.mean(blk * blk, axis=-1, keepdims=True)
            outs.append(blk * lax.rsqrt(ms + NORM_EPS))
        h_ref[sq, :, RET_W:RET_W + SSD_W] = (jnp.concatenate(outs, axis=-1) * ssdw_ref[...]).astype(BF16)

        yield

        gq = seg("gq", GLA_QK) * (GLA_DK ** -0.5)
        gkk = seg("gk", GLA_QK)
        gv = seg("gv", GLA_W)
        gkl = _dot(seg("ggk", 128).astype(BF16), wgk_ref[...].astype(BF16)) + bgk_ref[...]
        yield
        log_a = -_softplus(-gkl) * (1.0 / GLA_TEMP)
        b = _cumsum_rows(tri, log_a)
        yield
        q_t = (gq * jnp.exp(b)).astype(BF16)
        k_t = gkk * jnp.exp(-b)
        kbd_g = _block_diag(k_t, m_gla[...])
        att = jnp.where(causal4, _dot(q_t, kbd_g, _NT), 0.0)
        yield
        vbd_g = _block_diag(gv, m_heads[...])
        st = gla_s[sq]
        og = _dot(att.astype(BF16), vbd_g) + _dot(q_t, st.astype(BF16), _NT)
        b_last = b[C - 1:C, :]
        kd = (gkk * jnp.exp(b_last - b)).astype(BF16)
        contrib_g = _dot(gv.astype(BF16), kd, _TN)
        yield
        keep_g = _vdiv(_iota(st.shape, 0), GLA_DV) == _vdiv(_iota(st.shape, 1), GLA_DK)
        gla_s[sq] = jnp.where(keep_g, st * jnp.exp(b_last) + contrib_g, 0.0)
        ms = _seg_sum64(og * og) * (1.0 / GLA_DV)
        og = og * lax.rsqrt(ms + NORM_EPS) * glaw_ref[...]
        h_ref[sq, :, RET_W + SSD_W:D_MIX] = (_silu(seg("gg", GLA_W)) * og).astype(BF16)

    def step(p_read, p_write):
        xb = xn_ref[...].reshape(MIX_G * C, D_MODEL).astype(BF16)
        edges = list(range(0, NP, PROJ_SLAB)) + [NP]
        slabs = list(zip(edges[:-1], edges[1:]))

        live = [chunk_of(sq, p_read) for sq in range(MIX_G)]
        while live or slabs:
            live = [g for g in live if next(g, "done") != "done"]
            if slabs:
                lo, hi = slabs.pop(0)
                p_write[:, lo:hi] = _dot(xb, w_ref[:, lo:hi])

    @pl.when(lax.rem(i, 2) == 0)
    def _():
        step(proj_b, proj_a)

    @pl.when(lax.rem(i, 2) == 1)
    def _():
        step(proj_a, proj_b)


def _mixproj(x3, w_p, cos_t, sin_t, params):
    batch, seq, _ = x3.shape
    assert batch % MIX_G == 0 and seq % CHUNK == 0
    cps = seq // CHUNK
    n = (batch // MIX_G) * cps

    def cur_map(i):
        c = jnp.clip(i - 1, 0, n - 1)
        return (c // cps, c % cps, 0)

    def next_map(i):
        c = jnp.minimum(i, n - 1)
        return (c // cps, c % cps, 0)

    const = lambda i: (0, 0)
    specs = [pl.BlockSpec((MIX_G, CHUNK, D_MODEL), next_map),
             pl.BlockSpec((D_MODEL, NP), const),
             pl.BlockSpec((MIX_G, CHUNK, ROPE_W), cur_map),
             pl.BlockSpec((MIX_G, CHUNK, ROPE_W), cur_map)]
    specs += [pl.BlockSpec(p.shape, const) for p in params]
    return pl.pallas_call(
        functools.partial(_mixproj_kernel, chunks_per_seq=cps),
        grid=(n + 1,),
        in_specs=specs,
        out_specs=pl.BlockSpec((MIX_G, CHUNK, D_MIX), cur_map),
        out_shape=jax.ShapeDtypeStruct((batch, seq, D_MIX), BF16),
        scratch_shapes=[pltpu.VMEM((MIX_G * CHUNK, NP), F32),
                        pltpu.VMEM((MIX_G * CHUNK, NP), F32),
                        pltpu.VMEM((MIX_G, 256, 256), F32),
                        pltpu.VMEM((MIX_G, SSD_BC, SSD_W), F32),
                        pltpu.VMEM((MIX_G, GLA_W, GLA_QK), F32),
                        pltpu.VMEM((MIX_G, CHUNK + 8, SSD_XBC), F32),
                        pltpu.VMEM((RET_HEADS * CHUNK, 256), BF16),
                        pltpu.VMEM((SSD_HEADS * CHUNK, SSD_BC), BF16),
                        pltpu.VMEM((GLA_HEADS * CHUNK, GLA_QK), BF16),
                        pltpu.VMEM((LANES, SSD_W), BF16)],
        compiler_params=pltpu.CompilerParams(dimension_semantics=("arbitrary",),
                                             vmem_limit_bytes=VMEM_LIMIT),
        name="inproj_mixer",
    )(x3, w_p, cos_t, sin_t, *params)


def _layer_norm(y, g, b):
    mu = jnp.mean(y, axis=-1, keepdims=True)
    yc = y - mu
    var = jnp.mean(yc * yc, axis=-1, keepdims=True)
    return yc * lax.rsqrt(var + LN_EPS) * g + b


def _post_kernel(h_ref, x_ref, wout_ref, g_ref, b_ref, wrh_ref, wrc_ref, br_ref,
                 x1_ref, x1p_ref, mi_ref, mf_ref, cnt_ref, carry):
    sub = TM_POST // POST_SPLIT

    @pl.when(pl.program_id(0) == 0)
    def _():
        carry[...] = jnp.zeros_like(carry)

    lane_i = _iota((sub, LANES), 1)
    lane = lane_i.astype(F32)
    found = {}

    def sub_tile(part):
        rows = slice(part * sub, (part + 1) * sub)
        mix = _dot(h_ref[rows, :], wout_ref[...])
        yield
        x1 = _layer_norm(DEEPNORM_ALPHA * x_ref[rows, :] + mix, g_ref[...], b_ref[...])
        x1_ref[rows, :] = x1
        x1p_ref[rows, :] = _pack_rows(x1)
        x_hi = x1.astype(BF16)
        x_lo = (x1 - x_hi.astype(F32)).astype(BF16)
        both = _dot(x_hi, wrc_ref[...])
        logits = both[:, :LANES] + both[:, LANES:] + _dot(x_lo, wrh_ref[...]) + br_ref[...]
        yield
        work = logits
        vals, idxs = [], []
        multi = jnp.zeros((sub, LANES), F32)
        for _ in range(TOP_K):
            m = jnp.max(work, axis=-1, keepdims=True)
            idx = jnp.min(jnp.where(work == m, lane, float(LANES)), axis=-1, keepdims=True)
            hit = lane == idx
            multi = multi + hit.astype(F32)
            work = jnp.where(hit, -jnp.inf, work)
            vals.append(m)
            idxs.append(idx)
            yield
        exps = [jnp.exp(v - vals[0]) for v in vals]
        denom = exps[0] + exps[1] + exps[2] + exps[3]
        gates = [e / denom for e in exps]
        before = (_iota((sub, sub), 0) > _iota((sub, sub), 1)).astype(BF16)
        found[part] = (idxs, gates, _dot(before, multi.astype(BF16)), multi)

    live = [sub_tile(part) for part in range(POST_SPLIT)]
    while live:
        live = [g for g in live if next(g, "done") != "done"]

    base = carry[...]
    for part in range(POST_SPLIT):
        idxs, gates, prior_local, multi = found[part]
        prior = prior_local + base
        mi = jnp.zeros((sub, LANES), F32)
        mf = jnp.zeros((sub, LANES), F32)
        for kk in range(TOP_K):
            rank = jnp.sum(jnp.where(lane == idxs[kk], prior, 0.0), axis=-1, keepdims=True)
            mi = jnp.where(lane_i == kk, idxs[kk], mi)
            mi = jnp.where(lane_i == TOP_K + kk, rank, mi)
            mf = jnp.where(lane_i == kk, gates[kk], mf)
        mi_ref[:, part * sub:(part + 1) * sub] = jnp.transpose(mi)[0:2 * TOP_K, :].astype(jnp.int32)
        mf_ref[part * sub:(part + 1) * sub, :] = mf
        base = base + jnp.sum(multi, axis=0, keepdims=True)
    carry[...] = base
    cnt_ref[...] = jnp.broadcast_to(base, cnt_ref.shape)


def _post(h, x2, w_out_b, ln_g, ln_b, wr_hi, wr_cat, br_p):
    t = x2.shape[0]
    tm = TM_POST
    row = lambda i: (i, 0)
    const = lambda i: (0, 0)
    return pl.pallas_call(
        _post_kernel,
        grid=(t // tm,),
        in_specs=[pl.BlockSpec((tm, D_MIX), row), pl.BlockSpec((tm, D_MODEL), row),
                  pl.BlockSpec((D_MIX, D_MODEL), const), pl.BlockSpec((1, D_MODEL), const),
                  pl.BlockSpec((1, D_MODEL), const), pl.BlockSpec((D_MODEL, LANES), const),
                  pl.BlockSpec((D_MODEL, 2 * LANES), const), pl.BlockSpec((1, LANES), const)],
        out_specs=[pl.BlockSpec((tm, D_MODEL), row), pl.BlockSpec((tm, ROW_WORDS), row),
                   pl.BlockSpec((2 * TOP_K, tm), lambda i: (0, i)), pl.BlockSpec((tm, LANES), row),
                   pl.BlockSpec((8, LANES), const)],
        out_shape=[jax.ShapeDtypeStruct((t, D_MODEL), F32), jax.ShapeDtypeStruct((t, ROW_WORDS), jnp.int32),
                   jax.ShapeDtypeStruct((2 * TOP_K, t), jnp.int32), jax.ShapeDtypeStruct((t, LANES), F32),
                   jax.ShapeDtypeStruct((8, LANES), F32)],
        scratch_shapes=[pltpu.VMEM((1, LANES), F32)],
        compiler_params=pltpu.CompilerParams(dimension_semantics=("arbitrary",),
                                             vmem_limit_bytes=VMEM_LIMIT),
        name="outproj_ln_router",
    )(h, x2, w_out_b, ln_g, ln_b, wr_hi, wr_cat, br_p)


def _ffn_kernel(be_ref, nv_ref, used_ref, x_ref, wg_ref, bg_ref, wu_ref, bu_ref, wd_ref, bd_ref, o_ref,
                wg_b, wu_b, wd_b):
    i = pl.program_id(0)
    valid = i < nv_ref[0]
    e = be_ref[i]
    prev = be_ref[jnp.maximum(i - 1, 0)]
    fresh = jnp.logical_or(i == 0, e != prev)

    @pl.when(jnp.logical_and(valid, fresh))
    def _():
        wg_b[...] = wg_ref[0, 0].astype(BF16)
        wu_b[...] = wu_ref[0, 0].astype(BF16)
        wd_b[...] = wd_ref[0, 0].astype(BF16)

    for used in range(1, BM // FFN_SUB + 1):
        @pl.when(jnp.logical_and(valid, used_ref[i] == used))
        def _(r=used * FFN_SUB):
            x = _unpack_rows(x_ref[0:r, :]).astype(BF16)
            hg = jnp.minimum(_dot(x, wg_b[...]) + bg_ref[0, 0], SWIGLU_LIMIT)
            hu = jnp.clip(_dot(x, wu_b[...]) + bu_ref[0, 0], -SWIGLU_LIMIT, SWIGLU_LIMIT)
            hh = (hu + 1.0) * hg * jax.nn.sigmoid(SWIGLU_ALPHA * hg)
            o_ref[0:r, :] = _pack_rows(_dot(hh.astype(BF16), wd_b[...]) + bd_ref[0, 0])
            if r < BM:
                o_ref[r:BM, :] = jnp.zeros((BM - r, ROW_WORDS), jnp.int32)


def _ffn(layer, block_expert, n_valid, used_subs, xin, wg, bg, wu, bu, wd, bd):
    cap = xin.shape[0]
    nb = cap // BM

    def blk(i, be, nv):
        return jnp.maximum(jnp.minimum(i, nv[0] - 1), 0)

    row = lambda i, be, nv, us: (blk(i, be, nv), 0)
    wmap = lambda i, be, nv, us: (layer, be[blk(i, be, nv)], 0, 0)
    grid_spec = pltpu.PrefetchScalarGridSpec(
        num_scalar_prefetch=3,
        grid=(nb,),
        in_specs=[pl.BlockSpec((BM, ROW_WORDS), row),
                  pl.BlockSpec((1, 1, D_MODEL, D_FF), wmap), pl.BlockSpec((1, 1, 1, D_FF), wmap),
                  pl.BlockSpec((1, 1, D_MODEL, D_FF), wmap), pl.BlockSpec((1, 1, 1, D_FF), wmap),
                  pl.BlockSpec((1, 1, D_FF, D_MODEL), wmap), pl.BlockSpec((1, 1, 1, D_MODEL), wmap)],
        out_specs=pl.BlockSpec((BM, ROW_WORDS), row),
        scratch_shapes=[pltpu.VMEM((D_MODEL, D_FF), BF16), pltpu.VMEM((D_MODEL, D_FF), BF16),
                        pltpu.VMEM((D_FF, D_MODEL), BF16)],
    )
    return pl.pallas_call(
        _ffn_kernel,
        grid_spec=grid_spec,
        out_shape=jax.ShapeDtypeStruct((cap, ROW_WORDS), jnp.int32),
        compiler_params=pltpu.CompilerParams(dimension_semantics=("arbitrary",),
                                             vmem_limit_bytes=VMEM_LIMIT),
        name="expert_ffn",
    )(block_expert, n_valid, used_subs, xin, wg, bg, wu, bu, wd, bd)


def _sc_gather(table, idx3):
    nw, n_chunks, ch = idx3.shape
    width = table.shape[1]
    per_worker = n_chunks * ch
    mesh = plsc.VectorSubcoreMesh(core_axis_name="c", subcore_axis_name="s")
    n_cores = mesh.num_cores
    assert nw == n_cores * mesh.num_subcores and n_chunks % 2 == 0 and ch == SC_CHUNK

    def body(table_hbm, idx_hbm, out_hbm, idx_v, rows0, rows1, sem_g0, sem_g1, sem_w0, sem_w1):
        wid = lax.axis_index("s") * n_cores + lax.axis_index("c")
        base = wid * per_worker
        pltpu.sync_copy(idx_hbm.at[wid], idx_v)

        @pl.loop(0, n_chunks, step=2)
        def _(c):
            g0 = pltpu.async_copy(table_hbm.at[idx_v.at[c]], rows0, sem_g0)
            g1 = pltpu.async_copy(table_hbm.at[idx_v.at[c + 1]], rows1, sem_g1)
            g0.wait()
            w0 = pltpu.async_copy(rows0, out_hbm.at[pl.ds(base + c * ch, ch)], sem_w0)
            g1.wait()
            w1 = pltpu.async_copy(rows1, out_hbm.at[pl.ds(base + (c + 1) * ch, ch)], sem_w1)
            w0.wait()
            w1.wait()

    return pl.kernel(
        body,
        out_type=jax.ShapeDtypeStruct((nw * per_worker, width), table.dtype),
        mesh=mesh,
        scratch_types=[pltpu.VMEM((n_chunks, ch), jnp.int32),
                       pltpu.VMEM((ch, width), table.dtype), pltpu.VMEM((ch, width), table.dtype),
                       pltpu.SemaphoreType.DMA, pltpu.SemaphoreType.DMA,
                       pltpu.SemaphoreType.DMA, pltpu.SemaphoreType.DMA],
        name="sc_row_gather",
    )(table, idx3)


def _sc_scatter(rows, idx3, n_out):
    nw, n_lists, ch = idx3.shape
    n_chunks = n_lists // TOP_K
    width = rows.shape[1]
    per_worker = n_chunks * ch
    mesh = plsc.VectorSubcoreMesh(core_axis_name="c", subcore_axis_name="s")
    n_cores = mesh.num_cores
    assert nw == n_cores * mesh.num_subcores and n_chunks % 2 == 0 and ch == SC_CHUNK
    assert nw * per_worker == rows.shape[0]

    def body(rows_hbm, idx_hbm, out_hbm, idx_v, buf0, buf1, sem_r0, sem_r1, sem_w0, sem_w1):
        wid = lax.axis_index("s") * n_cores + lax.axis_index("c")
        base = wid * per_worker
        pltpu.sync_copy(idx_hbm.at[wid], idx_v)

        @pl.loop(0, n_chunks, step=2)
        def _(c):
            r0 = pltpu.async_copy(rows_hbm.at[pl.ds(base + c * ch, ch)], buf0, sem_r0)
            r1 = pltpu.async_copy(rows_hbm.at[pl.ds(base + (c + 1) * ch, ch)], buf1, sem_r1)
            r0.wait()
            w0 = [pltpu.async_copy(buf0, out_hbm.at[idx_v.at[c * TOP_K + k]], sem_w0) for k in range(TOP_K)]
            r1.wait()
            w1 = [pltpu.async_copy(buf1, out_hbm.at[idx_v.at[(c + 1) * TOP_K + k]], sem_w1) for k in range(TOP_K)]
            for w in w0 + w1:
                w.wait()

    return pl.kernel(
        body,
        out_type=jax.ShapeDtypeStruct((n_out, width), rows.dtype),
        mesh=mesh,
        scratch_types=[pltpu.VMEM((n_lists, ch), jnp.int32),
                       pltpu.VMEM((ch, width), rows.dtype), pltpu.VMEM((ch, width), rows.dtype),
                       pltpu.SemaphoreType.DMA, pltpu.SemaphoreType.DMA,
                       pltpu.SemaphoreType.DMA, pltpu.SemaphoreType.DMA],
        name="sc_row_scatter",
    )(rows, idx3)


def _combine_kernel(x_ref, rows_ref, mf_ref, g_ref, b_ref, o_ref):
    mf = mf_ref[...]
    y = DEEPNORM_ALPHA * x_ref[...]
    for kk in range(TOP_K):
        y = y + mf[:, kk:kk + 1] * _unpack_rows(rows_ref[kk])
    o_ref[...] = _layer_norm(y, g_ref[...], b_ref[...])


def _combine(x1, rows, mf, ln_g, ln_b):
    t = x1.shape[0]
    tm = TM_COMB
    row = lambda i: (i, 0)
    const = lambda i: (0, 0)
    return pl.pallas_call(
        _combine_kernel,
        grid=(t // tm,),
        in_specs=[pl.BlockSpec((tm, D_MODEL), row),
                  pl.BlockSpec((TOP_K, tm, ROW_WORDS), lambda i: (0, i, 0)),
                  pl.BlockSpec((tm, LANES), row),
                  pl.BlockSpec((1, D_MODEL), const), pl.BlockSpec((1, D_MODEL), const)],
        out_specs=pl.BlockSpec((tm, D_MODEL), row),
        out_shape=jax.ShapeDtypeStruct((t, D_MODEL), F32),
        compiler_params=pltpu.CompilerParams(dimension_semantics=("arbitrary",),
                                             vmem_limit_bytes=VMEM_LIMIT),
        name="combine_ln",
    )(x1, rows, mf, ln_g, ln_b)


def _relayout_w_in(w):
    widths = (256, 256, 256, 256, SSD_W, SSD_XBC, SSD_HEADS, GLA_QK, GLA_QK, GLA_W, GLA_RANK, GLA_W)
    offs = [0]
    for wd in widths:
        offs.append(offs[-1] + wd)
    parts = [w[:, offs[i]:offs[i + 1]] for i in range(len(widths))]
    parts[6] = jnp.pad(parts[6], ((0, 0), (0, LANES - SSD_HEADS)))
    parts[10] = jnp.pad(parts[10], ((0, 0), (0, LANES - GLA_RANK)))
    return jnp.concatenate(parts, axis=1).astype(BF16)


def _rep_heads(p):
    return jnp.repeat(p, SSD_HEAD_DIM)[None, :]


def _pad_heads(p):
    return jnp.pad(p, (0, LANES - SSD_HEADS))[None, :]


def kernel(x, positions, w_in, w_out, ret_norm_w, ssd_conv_w, ssd_conv_b, ssd_dt_bias, ssd_a_log, ssd_d,
           ssd_norm_w, gla_w_gk2, gla_b_gk2, gla_norm_w, ln1_g, ln1_b, w_router, b_router, w_gate, b_gate,
           w_up, b_up, w_down, b_down, ln2_g, ln2_b):
    batch, seq, d = x.shape
    t = batch * seq
    depth = w_in.shape[0]
    assert d == D_MODEL and t % TM_POST == 0 and t % TM_COMB == 0 and t % TM_ROPE == 0
    n_assign = t * TOP_K
    nb = pl.cdiv(n_assign, BM) + N_EXPERTS
    cap = nb * BM

    cos_t, sin_t = _rope_tables(positions.reshape(t, 1))
    cos_t = cos_t.reshape(batch, seq, -1)
    sin_t = sin_t.reshape(batch, seq, -1)
    x2 = x.reshape(t, d)

    for l in range(depth):
        params = (ret_norm_w[l][None, :], ssd_conv_w[l], ssd_conv_b[l][None, :], _pad_heads(ssd_dt_bias[l]),
                  _pad_heads(ssd_a_log[l]), _rep_heads(ssd_d[l]), ssd_norm_w[l][None, :],
                  jnp.pad(gla_w_gk2[l], ((0, LANES - GLA_RANK), (0, 0))), gla_b_gk2[l][None, :],
                  gla_norm_w[l][None, :])
        h = _mixproj(x2.reshape(batch, seq, d), _relayout_w_in(w_in[l]), cos_t, sin_t, params).reshape(t, D_MIX)

        wr_p = jnp.pad(w_router[l], ((0, 0), (0, LANES - N_EXPERTS)))
        br_p = jnp.pad(b_router[l], (0, LANES - N_EXPERTS), constant_values=NEG_BIG)[None, :]
        wr_hi = wr_p.astype(BF16)
        wr_lo = (wr_p - wr_hi.astype(F32)).astype(BF16)
        x1, x1p, mi, mf, cnt = _post(h, x2, w_out[l].astype(BF16), ln1_g[l][None, :], ln1_b[l][None, :],
                                     wr_hi, jnp.concatenate([wr_hi, wr_lo], axis=1), br_p)

        counts = cnt[0, :N_EXPERTS].astype(jnp.int32)
        padded = (counts + BM - 1) // BM * BM
        end_padded = jnp.cumsum(padded)
        start_padded = end_padded - padded
        top_idx = mi[:TOP_K]
        start_of = jnp.sum(jnp.where(top_idx[None] == jnp.arange(N_EXPERTS, dtype=jnp.int32)[:, None, None],
                                     start_padded[:, None, None], 0), axis=0)
        dest = start_of + mi[TOP_K:]
        block_start = jnp.arange(nb, dtype=jnp.int32) * BM
        block_expert = jnp.minimum(jnp.sum((end_padded[None, :] <= block_start[:, None]).astype(jnp.int32), axis=1),
                                   N_EXPERTS - 1)
        n_valid = (end_padded[-1:] // BM).astype(jnp.int32)
        owner = block_expert[:, None] == jnp.arange(N_EXPERTS, dtype=jnp.int32)[None, :]
        filled = (jnp.sum(jnp.where(owner, (start_padded + counts)[None, :], 0), axis=1) - block_start)
        used_subs = jnp.clip((jnp.clip(filled, 0, BM) + FFN_SUB - 1) // FFN_SUB, 1, BM // FFN_SUB)

        scatter_idx = dest.reshape(TOP_K, SC_WORKERS, -1, SC_CHUNK).transpose(1, 2, 0, 3)
        xin = _sc_scatter(x1p, scatter_idx.reshape(SC_WORKERS, -1, SC_CHUNK), cap)
        yb = _ffn(l, block_expert, n_valid, used_subs, xin, w_gate, b_gate[:, :, None, :], w_up, b_up[:, :, None, :],
                  w_down, b_down[:, :, None, :])
        rows = _sc_gather(yb, dest.reshape(SC_WORKERS, -1, SC_CHUNK))
        x2 = _combine(x1, rows.reshape(TOP_K, t, ROW_WORDS), mf, ln2_g[l][None, :], ln2_b[l][None, :])
    return x2.reshape(batch, seq, d)
```

```python
import functools
import math

import jax
import jax.numpy as jnp
from jax import lax
from jax.experimental import pallas as pl
from jax.experimental.pallas import tpu as pltpu
from jax.experimental.pallas import tpu_sc as plsc

F32 = jnp.float32
BF16 = jnp.bfloat16

D_MODEL = 1024
CHUNK = 64
RET_HEADS, RET_DK, RET_DV = 4, 64, 64
RET_W = RET_HEADS * RET_DV
SSD_HEADS, SSD_HEAD_DIM, SSD_STATE, SSD_GROUPS, SSD_CONV = 8, 64, 64, 2, 4
SSD_W = SSD_HEADS * SSD_HEAD_DIM
SSD_BC = SSD_GROUPS * SSD_STATE
SSD_XBC = SSD_W + 2 * SSD_BC
GLA_HEADS, GLA_DK, GLA_DV, GLA_RANK, GLA_TEMP = 4, 32, 64, 16, 16.0
GLA_QK = GLA_HEADS * GLA_DK
GLA_W = GLA_HEADS * GLA_DV
D_MIX = RET_W + SSD_W + GLA_W
N_EXPERTS, TOP_K, D_FF = 32, 4, 1024
SWIGLU_LIMIT, SWIGLU_ALPHA = 7.0, 1.702
ROPE_BASE = 10000.0
LN_EPS, NORM_EPS = 1e-5, 1e-6
DEPTH = 2
DEEPNORM_ALPHA = (2.0 * DEPTH) ** 0.25

LANES = 128
ROPE_W = LANES
NEG_BIG = -1e30
VMEM_LIMIT = 56 * 1024 * 1024

_SEGS = (("rq", 256), ("rk", 256), ("rv", 256), ("rg", 256), ("sz", SSD_W), ("sxbc", SSD_XBC),
         ("sdt", 128), ("gq", 128), ("gk", 128), ("gv", 256), ("ggk", 128), ("gg", 256))
COL = {}
_off = 0
for _n, _w in _SEGS:
    COL[_n] = _off
    _off += _w
NP = _off

MIX_G = 8
PROJ_SLAB = 512
TM_POST = 1024
POST_SPLIT = 4
TM_COMB = 1024
TM_ROPE = 2048
BM = 512
ROW_WORDS = D_MODEL // 2
SC_WORKERS = 32
SC_CHUNK = 64


def _dot(a, b, dims=(((1,), (0,)), ((), ())), precision=None):
    return lax.dot_general(a, b, dims, precision=precision, preferred_element_type=F32)


_NT = (((1,), (1,)), ((), ()))
_TN = (((0,), (0,)), ((), ()))


def _iota(shape, dim):
    return lax.broadcasted_iota(jnp.int32, shape, dim)


def _vdiv(x, n):
    assert n & (n - 1) == 0
    return lax.shift_right_logical(x, n.bit_length() - 1)


def _vmod(x, n):
    assert n & (n - 1) == 0
    return jnp.bitwise_and(x, n - 1)


def _silu(x):
    return x * jax.nn.sigmoid(x)


def _softplus(x):
    return jnp.maximum(x, 0.0) + jnp.log(1.0 + jnp.exp(-jnp.abs(x)))


def _pack_rows(x):
    w = x.shape[1] // 2
    lo = lax.bitcast_convert_type(x[:, :w].astype(BF16).astype(F32), jnp.uint32)
    hi = lax.bitcast_convert_type(x[:, w:].astype(BF16).astype(F32), jnp.uint32)
    return lax.bitcast_convert_type(lax.shift_right_logical(lo, jnp.uint32(16)) | hi, jnp.int32)


def _unpack_rows(words):
    u = lax.bitcast_convert_type(words, jnp.uint32)
    a = lax.bitcast_convert_type(lax.shift_left(u, jnp.uint32(16)), F32)
    b = lax.bitcast_convert_type(u & jnp.uint32(0xFFFF0000), F32)
    return jnp.concatenate([a, b], axis=-1)


def _seg_sum64(x):
    first = _iota((1, LANES), 1) < 64
    outs = []
    for j in range(x.shape[-1] // LANES):
        blk = x[:, j * LANES:(j + 1) * LANES]
        lo = jnp.sum(jnp.where(first, blk, 0.0), axis=-1, keepdims=True)
        hi = jnp.sum(jnp.where(first, 0.0, blk), axis=-1, keepdims=True)
        outs.append(jnp.where(first, lo, hi))
    return jnp.concatenate(outs, axis=-1)


def _block_mask(shape, row_blk, col_blk):
    keep = _vdiv(_iota(shape, 0), row_blk) == _vdiv(_iota(shape, 1), col_blk)
    return jnp.where(keep, 1.0, 0.0).astype(BF16)


def _block_diag(x, mask):
    reps = mask.shape[0] // x.shape[0]
    return jnp.concatenate([x.astype(BF16)] * reps, axis=0) * mask


def _expand_heads(x, expand):
    hi = x.astype(BF16)
    r1 = x - hi.astype(F32)
    mid = r1.astype(BF16)
    lo = (r1 - mid.astype(F32)).astype(BF16)
    return _dot(hi, expand) + _dot(mid, expand) + _dot(lo, expand)


def _cumsum_rows(tri, x):
    hi = x.astype(BF16)
    lo = (x - hi.astype(F32)).astype(BF16)
    return _dot(tri, hi) + _dot(tri, lo)


def _rope_kernel(pos_ref, cos_ref, sin_ref):
    lane = _iota((1, ROPE_W), 1)
    half = RET_DK // 2
    k = _vmod(lane, half).astype(F32)
    inv_freq = jnp.exp(k * (-math.log(ROPE_BASE) / half))
    ang = pos_ref[...].astype(F32) * inv_freq
    first = _vmod(lane, RET_DK) < half
    cos_ref[...] = jnp.cos(ang)
    sin_ref[...] = jnp.where(first, -1.0, 1.0) * jnp.sin(ang)


def _rope_tables(pos_col):
    t = pos_col.shape[0]
    tm = TM_ROPE
    w = ROPE_W
    return pl.pallas_call(
        _rope_kernel,
        grid=(t // tm,),
        in_specs=[pl.BlockSpec((tm, 1), lambda i: (i, 0))],
        out_specs=[pl.BlockSpec((tm, w), lambda i: (i, 0))] * 2,
        out_shape=[jax.ShapeDtypeStruct((t, w), F32)] * 2,
        compiler_params=pltpu.CompilerParams(dimension_semantics=("arbitrary",)),
        name="rope_tables",
    )(pos_col)


def _mixproj_kernel(xn_ref, w_ref, cos_ref, sin_ref, retw_ref, convw_ref, convb_ref, dtb_ref, alog_ref,
                    dskip_ref, ssdw_ref, wgk_ref, bgk_ref, glaw_ref, h_ref,
                    proj_a, proj_b, ret_s, ssd_s, gla_s, stage, m_heads, m_groups, m_gla, m_expand, *,
                    chunks_per_seq):
    C = CHUNK
    i = pl.program_id(0)
    cur = jnp.maximum(i - 1, 0)

    @pl.when(i == 0)
    def _():
        m_heads[...] = _block_mask(m_heads.shape, C, 64)
        m_groups[...] = _block_mask(m_groups.shape, C * SSD_HEADS // SSD_GROUPS, SSD_STATE)
        m_gla[...] = _block_mask(m_gla.shape, C, GLA_DK)
        m_expand[...] = _block_mask(m_expand.shape, 1, SSD_HEAD_DIM)
        proj_b[...] = jnp.zeros_like(proj_b)

    @pl.when(lax.rem(cur, chunks_per_seq) == 0)
    def _():
        ret_s[...] = jnp.zeros_like(ret_s)
        ssd_s[...] = jnp.zeros_like(ssd_s)
        gla_s[...] = jnp.zeros_like(gla_s)
        for g in range(MIX_G):
            stage[g, 0:8, :] = jnp.zeros((8, SSD_XBC), F32)

    def conv_act(sq, pref):
        stage[sq, 8:8 + C, :] = pref[sq * C:(sq + 1) * C, COL["sxbc"]:COL["sxbc"] + SSD_XBC]
        acc = convb_ref[...] + convw_ref[0:1, :] * stage[sq, 5:5 + C, :]
        for j in range(1, SSD_CONV):
            acc = acc + convw_ref[j:j + 1, :] * stage[sq, 5 + j:5 + j + C, :]
        stage[sq, 0:8, :] = stage[sq, C:C + 8, :]
        return _silu(acc)

    lane256 = _iota((1, 256), 1)
    head = _vdiv(lane256, 64).astype(F32)
    log_gamma = jnp.log(1.0 - jnp.exp((-5.0 - head) * math.log(2.0)))
    row = _iota((C, 1), 0).astype(F32)
    dist = row - _vmod(lane256, 64).astype(F32)
    ret_intra = jnp.where(dist >= 0, jnp.exp(log_gamma * jnp.maximum(dist, 0.0)), 0.0)
    ret_qdec = jnp.exp(log_gamma * (row + 1.0))
    ret_kdec = jnp.exp(log_gamma * (C - 1.0 - row))
    ret_cdec = jnp.exp(log_gamma * C)
    first_half = _vmod(lane256, RET_DK) < (RET_DK // 2)

    tri = jnp.where(_iota((C, C), 0) >= _iota((C, C), 1), 1.0, 0.0).astype(BF16)
    causal4 = _iota((C, 256), 0) >= _vmod(_iota((C, 256), 1), 64)
    causal8 = _iota((C, 512), 0) >= _vmod(_iota((C, 512), 1), 64)
    eye8 = _iota((C, 512), 0) == _vmod(_iota((C, 512), 1), 64)

    a_neg = -jnp.exp(alog_ref[...])

    def rot(t, cos, sin):
        sw = jnp.where(first_half, pltpu.roll(t, 256 - 32, 1), pltpu.roll(t, 32, 1))
        return t * cos + sw * sin

    def chunk_of(sq, pref):
        def seg(name, width):
            return pref[sq * C:(sq + 1) * C, COL[name]:COL[name] + width]

        xact = conv_act(sq, pref)
        yield

        reps = RET_HEADS * RET_DK // ROPE_W
        cos = jnp.concatenate([cos_ref[sq]] * reps, axis=-1)
        sin = jnp.concatenate([sin_ref[sq]] * reps, axis=-1)
        q = rot(seg("rq", 256), cos, sin)
        k = rot(seg("rk", 256), cos, sin) * (RET_DK ** -0.5)
        v = seg("rv", 256)
        vb = v.astype(BF16)
        kbd = _block_diag(k, m_heads[...])
        scores = _dot(q.astype(BF16), kbd, _NT) * ret_intra
        vbd = _block_diag(v, m_heads[...])
        yield
        s_prev = ret_s[sq]
        o = _dot(scores.astype(BF16), vbd) + _dot((q * ret_qdec).astype(BF16), s_prev.astype(BF16))
        contrib = _dot((k * ret_kdec).astype(BF16), vb, _TN)
        yield
        keep = _vdiv(_iota((256, 256), 0), RET_DK) == _vdiv(_iota((256, 256), 1), RET_DV)
        ret_s[sq] = jnp.where(keep, ret_cdec * s_prev + contrib, 0.0)
        mu = _seg_sum64(o) * (1.0 / RET_DV)
        oc = o - mu
        var = _seg_sum64(oc * oc) * (1.0 / RET_DV)
        o = oc * lax.rsqrt(var + LN_EPS) * retw_ref[...]
        h_ref[sq, :, 0:RET_W] = (_silu(seg("rg", 256)) * o).astype(BF16)
        yield

        xs = xact[:, 0:SSD_W]
        bm = xact[:, SSD_W:SSD_W + SSD_BC]
        cm = xact[:, SSD_W + SSD_BC:SSD_XBC]
        cmb = cm.astype(BF16)
        dt_c = _softplus(seg("sdt", LANES) + dtb_ref[...])
        acum_c = _cumsum_rows(tri, dt_c * a_neg)
        both = _expand_heads(jnp.concatenate([dt_c, acum_c], axis=0), m_expand[...])
        dt = both[0:C, :]
        acum = both[C:2 * C, :]
        yield
        arow = jnp.sum(jnp.where(eye8, acum, 0.0), axis=0, keepdims=True)
        decay = jnp.exp(jnp.where(causal8, acum - arow, NEG_BIG))
        b8 = _block_diag(bm, m_groups[...])
        cb = _dot(cmb, b8, _NT)
        yield
        m = (cb * decay).astype(BF16)
        xdt = xs * dt
        s2 = ssd_s[sq]
        half = SSD_W // SSD_GROUPS
        ys = []
        for g in range(SSD_GROUPS):
            xbd = _block_diag(xdt[:, g * half:(g + 1) * half], m_heads[...])
            ys.append(_dot(m[:, g * half:(g + 1) * half], xbd))
        y = jnp.concatenate(ys, axis=-1)
        y = y + _dot(cmb, s2.astype(BF16)) * jnp.exp(acum)
        y = y + dskip_ref[...] * xs
        a_last = acum[C - 1:C, :]
        sd = jnp.exp(a_last - acum)
        contrib_s = _dot(bm.astype(BF16), (xdt * sd).astype(BF16), _TN)
        yield
        keep_s = _vdiv(_iota(s2.shape, 0), SSD_STATE) == _vdiv(_iota(s2.shape, 1), half)
        ssd_s[sq] = jnp.where(keep_s, s2 * jnp.exp(a_last) + contrib_s, 0.0)
        yz = y * _silu(seg("sz", SSD_W))
        outs = []
        for g in range(SSD_GROUPS):
            blk = yz[:, g * half:(g + 1) * half]
            ms = jnp.mean(blk * blk, axis=-1, keepdims=True)
            outs.append(blk * lax.rsqrt(ms + NORM_EPS))
        h_ref[sq, :, RET_W:RET_W + SSD_W] = (jnp.concatenate(outs, axis=-1) * ssdw_ref[...]).astype(BF16)

        yield

        gq = seg("gq", GLA_QK) * (GLA_DK ** -0.5)
        gkk = seg("gk", GLA_QK)
        gv = seg("gv", GLA_W)
        gkl = _dot(seg("ggk", 128).astype(BF16), wgk_ref[...].astype(BF16)) + bgk_ref[...]
        yield
        log_a = -_softplus(-gkl) * (1.0 / GLA_TEMP)
        b = _cumsum_rows(tri, log_a)
        yield
        q_t = (gq * jnp.exp(b)).astype(BF16)
        k_t = gkk * jnp.exp(-b)
        kbd_g = _block_diag(k_t, m_gla[...])
        att = jnp.where(causal4, _dot(q_t, kbd_g, _NT), 0.0)
        yield
        vbd_g = _block_diag(gv, m_heads[...])
        st = gla_s[sq]
        og = _dot(att.astype(BF16), vbd_g) + _dot(q_t, st.astype(BF16), _NT)
        b_last = b[C - 1:C, :]
        kd = (gkk * jnp.exp(b_last - b)).astype(BF16)
        contrib_g = _dot(gv.astype(BF16), kd, _TN)
        yield
        keep_g = _vdiv(_iota(st.shape, 0), GLA_DV) == _vdiv(_iota(st.shape, 1), GLA_DK)
        gla_s[sq] = jnp.where(keep_g, st * jnp.exp(b_last) + contrib_g, 0.0)
        ms = _seg_sum64(og * og) * (1.0 / GLA_DV)
        og = og * lax.rsqrt(ms + NORM_EPS) * glaw_ref[...]
        h_ref[sq, :, RET_W + SSD_W:D_MIX] = (_silu(seg("gg", GLA_W)) * og).astype(BF16)

    def step(p_read, p_write):
        xb = xn_ref[...].reshape(MIX_G * C, D_MODEL).astype(BF16)
        edges = list(range(0, NP, PROJ_SLAB)) + [NP]
        slabs = list(zip(edges[:-1], edges[1:]))

        live = [chunk_of(sq, p_read) for sq in range(MIX_G)]
        while live or slabs:
            live = [g for g in live if next(g, "done") != "done"]
            if slabs:
                lo, hi = slabs.pop(0)
                p_write[:, lo:hi] = _dot(xb, w_ref[:, lo:hi])

    @pl.when(lax.rem(i, 2) == 0)
    def _():
        step(proj_b, proj_a)

    @pl.when(lax.rem(i, 2) == 1)
    def _():
        step(proj_a, proj_b)


def _mixproj(x3, w_p, cos_t, sin_t, params):
    batch, seq, _ = x3.shape
    assert batch % MIX_G == 0 and seq % CHUNK == 0
    cps = seq // CHUNK
    n = (batch // MIX_G) * cps

    def cur_map(i):
        c = jnp.clip(i - 1, 0, n - 1)
        return (c // cps, c % cps, 0)

    def next_map(i):
        c = jnp.minimum(i, n - 1)
        return (c // cps, c % cps, 0)

    const = lambda i: (0, 0)
    specs = [pl.BlockSpec((MIX_G, CHUNK, D_MODEL), next_map),
             pl.BlockSpec((D_MODEL, NP), const),
             pl.BlockSpec((MIX_G, CHUNK, ROPE_W), cur_map),
             pl.BlockSpec((MIX_G, CHUNK, ROPE_W), cur_map)]
    specs += [pl.BlockSpec(p.shape, const) for p in params]
    return pl.pallas_call(
        functools.partial(_mixproj_kernel, chunks_per_seq=cps),
        grid=(n + 1,),
        in_specs=specs,
        out_specs=pl.BlockSpec((MIX_G, CHUNK, D_MIX), cur_map),
        out_shape=jax.ShapeDtypeStruct((batch, seq, D_MIX), BF16),
        scratch_shapes=[pltpu.VMEM((MIX_G * CHUNK, NP), F32),
                        pltpu.VMEM((MIX_G * CHUNK, NP), F32),
                        pltpu.VMEM((MIX_G, 256, 256), F32),
                        pltpu.VMEM((MIX_G, SSD_BC, SSD_W), F32),
                        pltpu.VMEM((MIX_G, GLA_W, GLA_QK), F32),
                        pltpu.VMEM((MIX_G, CHUNK + 8, SSD_XBC), F32),
                        pltpu.VMEM((RET_HEADS * CHUNK, 256), BF16),
                        pltpu.VMEM((SSD_HEADS * CHUNK, SSD_BC), BF16),
                        pltpu.VMEM((GLA_HEADS * CHUNK, GLA_QK), BF16),
                        pltpu.VMEM((LANES, SSD_W), BF16)],
        compiler_params=pltpu.CompilerParams(dimension_semantics=("arbitrary",),
                                             vmem_limit_bytes=VMEM_LIMIT),
        name="inproj_mixer",
    )(x3, w_p, cos_t, sin_t, *params)


def _layer_norm(y, g, b):
    mu = jnp.mean(y, axis=-1, keepdims=True)
    yc = y - mu
    var = jnp.mean(yc * yc, axis=-1, keepdims=True)
    return yc * lax.rsqrt(var + LN_EPS) * g + b


def _post_kernel(h_ref, x_ref, wout_ref, g_ref, b_ref, wrh_ref, wrc_ref, br_ref,
                 x1_ref, x1p_ref, mi_ref, mf_ref, cnt_ref, carry):
    sub = TM_POST // POST_SPLIT

    @pl.when(pl.program_id(0) == 0)
    def _():
        carry[...] = jnp.zeros_like(carry)

    lane_i = _iota((sub, LANES), 1)
    lane = lane_i.astype(F32)
    found = {}

    def sub_tile(part):
        rows = slice(part * sub, (part + 1) * sub)
        mix = _dot(h_ref[rows, :], wout_ref[...])
        yield
        x1 = _layer_norm(DEEPNORM_ALPHA * x_ref[rows, :] + mix, g_ref[...], b_ref[...])
        x1_ref[rows, :] = x1
        x1p_ref[rows, :] = _pack_rows(x1)
        x_hi = x1.astype(BF16)
        x_lo = (x1 - x_hi.astype(F32)).astype(BF16)
        both = _dot(x_hi, wrc_ref[...])
        logits = both[:, :LANES] + both[:, LANES:] + _dot(x_lo, wrh_ref[...]) + br_ref[...]
        yield
        work = logits
        vals, idxs = [], []
        multi = jnp.zeros((sub, LANES), F32)
        for _ in range(TOP_K):
            m = jnp.max(work, axis=-1, keepdims=True)
            idx = jnp.min(jnp.where(work == m, lane, float(LANES)), axis=-1, keepdims=True)
            hit = lane == idx
            multi = multi + hit.astype(F32)
            work = jnp.where(hit, -jnp.inf, work)
            vals.append(m)
            idxs.append(idx)
            yield
        exps = [jnp.exp(v - vals[0]) for v in vals]
        denom = exps[0] + exps[1] + exps[2] + exps[3]
        gates = [e / denom for e in exps]
        before = (_iota((sub, sub), 0) > _iota((sub, sub), 1)).astype(BF16)
        found[part] = (idxs, gates, _dot(before, multi.astype(BF16)), multi)

    live = [sub_tile(part) for part in range(POST_SPLIT)]
    while live:
        live = [g for g in live if next(g, "done") != "done"]

    base = carry[...]
    for part in range(POST_SPLIT):
        idxs, gates, prior_local, multi = found[part]
        prior = prior_local + base
        mi = jnp.zeros((sub, LANES), F32)
        mf = jnp.zeros((sub, LANES), F32)
        for kk in range(TOP_K):
            rank = jnp.sum(jnp.where(lane == idxs[kk], prior, 0.0), axis=-1, keepdims=True)
            mi = jnp.where(lane_i == kk, idxs[kk], mi)
            mi = jnp.where(lane_i == TOP_K + kk, rank, mi)
            mf = jnp.where(lane_i == kk, gates[kk], mf)
        mi_ref[:, part * sub:(part + 1) * sub] = jnp.transpose(mi)[0:2 * TOP_K, :].astype(jnp.int32)
        mf_ref[part * sub:(part + 1) * sub, :] = mf
        base = base + jnp.sum(multi, axis=0, keepdims=True)
    carry[...] = base
    cnt_ref[...] = jnp.broadcast_to(base, cnt_ref.shape)


def _post(h, x2, w_out_b, ln_g, ln_b, wr_hi, wr_cat, br_p):
    t = x2.shape[0]
    tm = TM_POST
    row = lambda i: (i, 0)
    const = lambda i: (0, 0)
    return pl.pallas_call(
        _post_kernel,
        grid=(t // tm,),
        in_specs=[pl.BlockSpec((tm, D_MIX), row), pl.BlockSpec((tm, D_MODEL), row),
                  pl.BlockSpec((D_MIX, D_MODEL), const), pl.BlockSpec((1, D_MODEL), const),
                  pl.BlockSpec((1, D_MODEL), const), pl.BlockSpec((D_MODEL, LANES), const),
                  pl.BlockSpec((D_MODEL, 2 * LANES), const), pl.BlockSpec((1, LANES), const)],
        out_specs=[pl.BlockSpec((tm, D_MODEL), row), pl.BlockSpec((tm, ROW_WORDS), row),
                   pl.BlockSpec((2 * TOP_K, tm), lambda i: (0, i)), pl.BlockSpec((tm, LANES), row),
                   pl.BlockSpec((8, LANES), const)],
        out_shape=[jax.ShapeDtypeStruct((t, D_MODEL), F32), jax.ShapeDtypeStruct((t, ROW_WORDS), jnp.int32),
                   jax.ShapeDtypeStruct((2 * TOP_K, t), jnp.int32), jax.ShapeDtypeStruct((t, LANES), F32),
                   jax.ShapeDtypeStruct((8, LANES), F32)],
        scratch_shapes=[pltpu.VMEM((1, LANES), F32)],
        compiler_params=pltpu.CompilerParams(dimension_semantics=("arbitrary",),
                                             vmem_limit_bytes=VMEM_LIMIT),
        name="outproj_ln_router",
    )(h, x2, w_out_b, ln_g, ln_b, wr_hi, wr_cat, br_p)


def _ffn_kernel(be_ref, nv_ref, par_ref, ahead_ref, nx_ref, x_ref, wg_ref, bg_ref, wu_ref, bu_ref, wd_ref, bd_ref,
                o_ref, ga, ua, da, gb, ub, db):
    del nx_ref
    wa, wb = (ga, ua, da), (gb, ub, db)
    i = pl.program_id(0)
    b = jnp.clip(i - 1, 0, be_ref.shape[0] - 1)
    valid = jnp.logical_and(i >= 1, b < nv_ref[0])
    parity = par_ref[b]
    cast_ahead = ahead_ref[b]
    f32_weights = (wg_ref, wu_ref, wd_ref)

    def cast_into(dst, j):
        dst[j][...] = f32_weights[j][0, 0].astype(BF16)

    @pl.when(i == 0)
    def _():
        for j in range(3):
            cast_into(wa, j)

    def block(cur, nxt):
        x = _unpack_rows(x_ref[...]).astype(BF16)
        if nxt is not None:
            cast_into(nxt, 0)
        hg = jnp.minimum(_dot(x, cur[0][...]) + bg_ref[0, 0], SWIGLU_LIMIT)
        if nxt is not None:
            cast_into(nxt, 1)
        hu = jnp.clip(_dot(x, cur[1][...]) + bu_ref[0, 0], -SWIGLU_LIMIT, SWIGLU_LIMIT)
        hh = (hu + 1.0) * hg * jax.nn.sigmoid(SWIGLU_ALPHA * hg)
        if nxt is not None:
            cast_into(nxt, 2)
        o_ref[...] = _pack_rows(_dot(hh.astype(BF16), cur[2][...]) + bd_ref[0, 0])

    for par, cur, other in ((0, wa, wb), (1, wb, wa)):
        for ahead in (0, 1):
            @pl.when(valid & (parity == par) & (cast_ahead == ahead))
            def _(cur=cur, nxt=other if ahead else None):
                block(cur, nxt)


def _ffn(layer, block_expert, n_valid, end_padded, xin, wg, bg, wu, bu, wd, bd):
    cap = xin.shape[0]
    nb = cap // BM

    ids = jnp.arange(nb, dtype=jnp.int32)
    first = (ids == 0) | (block_expert != jnp.roll(block_expert, 1))
    parity = (jnp.cumsum(first.astype(jnp.int32)) - 1) & 1
    after_run = jnp.sum(jnp.where(block_expert[:, None] == jnp.arange(N_EXPERTS, dtype=jnp.int32)[None, :],
                                  end_padded[None, :], 0), axis=1) // BM
    has_next = after_run < n_valid[0]
    next_expert = jnp.where(has_next, block_expert[jnp.minimum(after_run, nb - 1)], block_expert)
    ahead = ((ids + 1 == after_run) & has_next).astype(jnp.int32)

    def blk(i, nv):
        return jnp.clip(i - 1, 0, jnp.maximum(nv[0] - 1, 0))

    row = lambda i, be, nv, pa, ah, nx: (blk(i, nv), 0)
    bmap = lambda i, be, nv, pa, ah, nx: (layer, be[blk(i, nv)], 0, 0)
    wmap = lambda i, be, nv, pa, ah, nx: (layer, jnp.where(i == 0, be[0], nx[blk(i, nv)]), 0, 0)
    grid_spec = pltpu.PrefetchScalarGridSpec(
        num_scalar_prefetch=5,
        grid=(nb + 1,),
        in_specs=[pl.BlockSpec((BM, ROW_WORDS), row),
                  pl.BlockSpec((1, 1, D_MODEL, D_FF), wmap), pl.BlockSpec((1, 1, 1, D_FF), bmap),
                  pl.BlockSpec((1, 1, D_MODEL, D_FF), wmap), pl.BlockSpec((1, 1, 1, D_FF), bmap),
                  pl.BlockSpec((1, 1, D_FF, D_MODEL), wmap), pl.BlockSpec((1, 1, 1, D_MODEL), bmap)],
        out_specs=pl.BlockSpec((BM, ROW_WORDS), row),
        scratch_shapes=[pltpu.VMEM((D_MODEL, D_FF), BF16), pltpu.VMEM((D_MODEL, D_FF), BF16),
                        pltpu.VMEM((D_FF, D_MODEL), BF16)] * 2,
    )
    return pl.pallas_call(
        _ffn_kernel,
        grid_spec=grid_spec,
        out_shape=jax.ShapeDtypeStruct((cap, ROW_WORDS), jnp.int32),
        compiler_params=pltpu.CompilerParams(dimension_semantics=("arbitrary",),
                                             vmem_limit_bytes=VMEM_LIMIT),
        name="expert_ffn",
    )(block_expert, n_valid, parity, ahead, next_expert, xin, wg, bg, wu, bu, wd, bd)


def _sc_gather(table, idx3):
    nw, n_chunks, ch = idx3.shape
    width = table.shape[1]
    per_worker = n_chunks * ch
    mesh = plsc.VectorSubcoreMesh(core_axis_name="c", subcore_axis_name="s")
    n_cores = mesh.num_cores
    assert nw == n_cores * mesh.num_subcores and n_chunks % 2 == 0 and ch == SC_CHUNK

    def body(table_hbm, idx_hbm, out_hbm, idx_v, rows0, rows1, sem_g0, sem_g1, sem_w0, sem_w1):
        wid = lax.axis_index("s") * n_cores + lax.axis_index("c")
        base = wid * per_worker
        pltpu.sync_copy(idx_hbm.at[wid], idx_v)

        @pl.loop(0, n_chunks, step=2)
        def _(c):
            g0 = pltpu.async_copy(table_hbm.at[idx_v.at[c]], rows0, sem_g0)
            g1 = pltpu.async_copy(table_hbm.at[idx_v.at[c + 1]], rows1, sem_g1)
            g0.wait()
            w0 = pltpu.async_copy(rows0, out_hbm.at[pl.ds(base + c * ch, ch)], sem_w0)
            g1.wait()
            w1 = pltpu.async_copy(rows1, out_hbm.at[pl.ds(base + (c + 1) * ch, ch)], sem_w1)
            w0.wait()
            w1.wait()

    return pl.kernel(
        body,
        out_type=jax.ShapeDtypeStruct((nw * per_worker, width), table.dtype),
        mesh=mesh,
        scratch_types=[pltpu.VMEM((n_chunks, ch), jnp.int32),
                       pltpu.VMEM((ch, width), table.dtype), pltpu.VMEM((ch, width), table.dtype),
                       pltpu.SemaphoreType.DMA, pltpu.SemaphoreType.DMA,
                       pltpu.SemaphoreType.DMA, pltpu.SemaphoreType.DMA],
        name="sc_row_gather",
    )(table, idx3)


def _sc_scatter(rows, idx3, n_out):
    nw, n_lists, ch = idx3.shape
    n_chunks = n_lists // TOP_K
    width = rows.shape[1]
    per_worker = n_chunks * ch
    mesh = plsc.VectorSubcoreMesh(core_axis_name="c", subcore_axis_name="s")
    n_cores = mesh.num_cores
    assert nw == n_cores * mesh.num_subcores and n_chunks % 2 == 0 and ch == SC_CHUNK
    assert nw * per_worker == rows.shape[0]

    def body(rows_hbm, idx_hbm, out_hbm, idx_v, buf0, buf1, sem_r0, sem_r1, sem_w0, sem_w1):
        wid = lax.axis_index("s") * n_cores + lax.axis_index("c")
        base = wid * per_worker
        pltpu.sync_copy(idx_hbm.at[wid], idx_v)

        @pl.loop(0, n_chunks, step=2)
        def _(c):
            r0 = pltpu.async_copy(rows_hbm.at[pl.ds(base + c * ch, ch)], buf0, sem_r0)
            r1 = pltpu.async_copy(rows_hbm.at[pl.ds(base + (c + 1) * ch, ch)], buf1, sem_r1)
            r0.wait()
            w0 = [pltpu.async_copy(buf0, out_hbm.at[idx_v.at[c * TOP_K + k]], sem_w0) for k in range(TOP_K)]
            r1.wait()
            w1 = [pltpu.async_copy(buf1, out_hbm.at[idx_v.at[(c + 1) * TOP_K + k]], sem_w1) for k in range(TOP_K)]
            for w in w0 + w1:
                w.wait()

    return pl.kernel(
        body,
        out_type=jax.ShapeDtypeStruct((n_out, width), rows.dtype),
        mesh=mesh,
        scratch_types=[pltpu.VMEM((n_lists, ch), jnp.int32),
                       pltpu.VMEM((ch, width), rows.dtype), pltpu.VMEM((ch, width), rows.dtype),
                       pltpu.SemaphoreType.DMA, pltpu.SemaphoreType.DMA,
                       pltpu.SemaphoreType.DMA, pltpu.SemaphoreType.DMA],
        name="sc_row_scatter",
    )(rows, idx3)


def _combine_kernel(x_ref, rows_ref, mf_ref, g_ref, b_ref, o_ref):
    mf = mf_ref[...]
    y = DEEPNORM_ALPHA * x_ref[...]
    for kk in range(TOP_K):
        y = y + mf[:, kk:kk + 1] * _unpack_rows(rows_ref[kk])
    o_ref[...] = _layer_norm(y, g_ref[...], b_ref[...])


def _combine(x1, rows, mf, ln_g, ln_b):
    t = x1.shape[0]
    tm = TM_COMB
    row = lambda i: (i, 0)
    const = lambda i: (0, 0)
    return pl.pallas_call(
        _combine_kernel,
        grid=(t // tm,),
        in_specs=[pl.BlockSpec((tm, D_MODEL), row),
                  pl.BlockSpec((TOP_K, tm, ROW_WORDS), lambda i: (0, i, 0)),
                  pl.BlockSpec((tm, LANES), row),
                  pl.BlockSpec((1, D_MODEL), const), pl.BlockSpec((1, D_MODEL), const)],
        out_specs=pl.BlockSpec((tm, D_MODEL), row),
        out_shape=jax.ShapeDtypeStruct((t, D_MODEL), F32),
        compiler_params=pltpu.CompilerParams(dimension_semantics=("arbitrary",),
                                             vmem_limit_bytes=VMEM_LIMIT),
        name="combine_ln",
    )(x1, rows, mf, ln_g, ln_b)


def _relayout_w_in(w):
    widths = (256, 256, 256, 256, SSD_W, SSD_XBC, SSD_HEADS, GLA_QK, GLA_QK, GLA_W, GLA_RANK, GLA_W)
    offs = [0]
    for wd in widths:
        offs.append(offs[-1] + wd)
    parts = [w[:, offs[i]:offs[i + 1]] for i in range(len(widths))]
    parts[6] = jnp.pad(parts[6], ((0, 0), (0, LANES - SSD_HEADS)))
    parts[10] = jnp.pad(parts[10], ((0, 0), (0, LANES - GLA_RANK)))
    return jnp.concatenate(parts, axis=1).astype(BF16)


def _rep_heads(p):
    return jnp.repeat(p, SSD_HEAD_DIM)[None, :]


def _pad_heads(p):
    return jnp.pad(p, (0, LANES - SSD_HEADS))[None, :]


def kernel(x, positions, w_in, w_out, ret_norm_w, ssd_conv_w, ssd_conv_b, ssd_dt_bias, ssd_a_log, ssd_d,
           ssd_norm_w, gla_w_gk2, gla_b_gk2, gla_norm_w, ln1_g, ln1_b, w_router, b_router, w_gate, b_gate,
           w_up, b_up, w_down, b_down, ln2_g, ln2_b):
    batch, seq, d = x.shape
    t = batch * seq
    depth = w_in.shape[0]
    assert d == D_MODEL and t % TM_POST == 0 and t % TM_COMB == 0 and t % TM_ROPE == 0
    n_assign = t * TOP_K
    nb = pl.cdiv(n_assign, BM) + N_EXPERTS
    cap = nb * BM

    cos_t, sin_t = _rope_tables(positions.reshape(t, 1))
    cos_t = cos_t.reshape(batch, seq, -1)
    sin_t = sin_t.reshape(batch, seq, -1)
    x2 = x.reshape(t, d)

    for l in range(depth):
        params = (ret_norm_w[l][None, :], ssd_conv_w[l], ssd_conv_b[l][None, :], _pad_heads(ssd_dt_bias[l]),
                  _pad_heads(ssd_a_log[l]), _rep_heads(ssd_d[l]), ssd_norm_w[l][None, :],
                  jnp.pad(gla_w_gk2[l], ((0, LANES - GLA_RANK), (0, 0))), gla_b_gk2[l][None, :],
                  gla_norm_w[l][None, :])
        h = _mixproj(x2.reshape(batch, seq, d), _relayout_w_in(w_in[l]), cos_t, sin_t, params).reshape(t, D_MIX)

        wr_p = jnp.pad(w_router[l], ((0, 0), (0, LANES - N_EXPERTS)))
        br_p = jnp.pad(b_router[l], (0, LANES - N_EXPERTS), constant_values=NEG_BIG)[None, :]
        wr_hi = wr_p.astype(BF16)
        wr_lo = (wr_p - wr_hi.astype(F32)).astype(BF16)
        x1, x1p, mi, mf, cnt = _post(h, x2, w_out[l].astype(BF16), ln1_g[l][None, :], ln1_b[l][None, :],
                                     wr_hi, jnp.concatenate([wr_hi, wr_lo], axis=1), br_p)

        counts = cnt[0, :N_EXPERTS].astype(jnp.int32)
        padded = (counts + BM - 1) // BM * BM
        end_padded = jnp.cumsum(padded)
        start_padded = end_padded - padded
        top_idx = mi[:TOP_K]
        start_of = jnp.sum(jnp.where(top_idx[None] == jnp.arange(N_EXPERTS, dtype=jnp.int32)[:, None, None],
                                     start_padded[:, None, None], 0), axis=0)
        dest = start_of + mi[TOP_K:]
        block_start = jnp.arange(nb, dtype=jnp.int32) * BM
        block_expert = jnp.minimum(jnp.sum((end_padded[None, :] <= block_start[:, None]).astype(jnp.int32), axis=1),
                                   N_EXPERTS - 1)
        n_valid = (end_padded[-1:] // BM).astype(jnp.int32)

        scatter_idx = dest.reshape(TOP_K, SC_WORKERS, -1, SC_CHUNK).transpose(1, 2, 0, 3)
        xin = _sc_scatter(x1p, scatter_idx.reshape(SC_WORKERS, -1, SC_CHUNK), cap)
        yb = _ffn(l, block_expert, n_valid, end_padded, xin, w_gate, b_gate[:, :, None, :], w_up, b_up[:, :, None, :],
                  w_down, b_down[:, :, None, :])
        rows = _sc_gather(yb, dest.reshape(SC_WORKERS, -1, SC_CHUNK))
        x2 = _combine(x1, rows.reshape(TOP_K, t, ROW_WORDS), mf, ln2_g[l][None, :], ln2_b[l][None, :])
    return x2.reshape(batch, seq, d)
```

```python
import functools
import math

import jax
import jax.numpy as jnp
from jax import lax
from jax.experimental import pallas as pl
from jax.experimental.pallas import tpu as pltpu
from jax.experimental.pallas import tpu_sc as plsc

F32 = jnp.float32
BF16 = jnp.bfloat16

D_MODEL = 1024
CHUNK = 64
RET_HEADS, RET_DK, RET_DV = 4, 64, 64
RET_W = RET_HEADS * RET_DV
SSD_HEADS, SSD_HEAD_DIM, SSD_STATE, SSD_GROUPS, SSD_CONV = 8, 64, 64, 2, 4
SSD_W = SSD_HEADS * SSD_HEAD_DIM
SSD_BC = SSD_GROUPS * SSD_STATE
SSD_XBC = SSD_W + 2 * SSD_BC
GLA_HEADS, GLA_DK, GLA_DV, GLA_RANK, GLA_TEMP = 4, 32, 64, 16, 16.0
GLA_QK = GLA_HEADS * GLA_DK
GLA_W = GLA_HEADS * GLA_DV
D_MIX = RET_W + SSD_W + GLA_W
N_EXPERTS, TOP_K, D_FF = 32, 4, 1024
SWIGLU_LIMIT, SWIGLU_ALPHA = 7.0, 1.702
ROPE_BASE = 10000.0
LN_EPS, NORM_EPS = 1e-5, 1e-6
DEPTH = 2
DEEPNORM_ALPHA = (2.0 * DEPTH) ** 0.25

LANES = 128
ROPE_W = LANES
NEG_BIG = -1e30
VMEM_LIMIT = 56 * 1024 * 1024

_SEGS = (("rq", 256), ("rk", 256), ("rv", 256), ("rg", 256), ("sz", SSD_W), ("sxbc", SSD_XBC),
         ("sdt", 128), ("gq", 128), ("gk", 128), ("gv", 256), ("ggk", 128), ("gg", 256))
COL = {}
_off = 0
for _n, _w in _SEGS:
    COL[_n] = _off
    _off += _w
NP = _off

MIX_G = 8
PROJ_SLAB = 512
TM_POST = 1024
POST_SPLIT = 4
TM_COMB = 1024
TM_ROPE = 2048
BM = 768
ROW_WORDS = D_MODEL // 2
SC_WORKERS = 32
SC_CHUNK = 64


def _dot(a, b, dims=(((1,), (0,)), ((), ())), precision=None):
    return lax.dot_general(a, b, dims, precision=precision, preferred_element_type=F32)


_NT = (((1,), (1,)), ((), ()))
_TN = (((0,), (0,)), ((), ()))


def _iota(shape, dim):
    return lax.broadcasted_iota(jnp.int32, shape, dim)


def _vdiv(x, n):
    assert n & (n - 1) == 0
    return lax.shift_right_logical(x, n.bit_length() - 1)


def _vmod(x, n):
    assert n & (n - 1) == 0
    return jnp.bitwise_and(x, n - 1)


def _silu(x):
    return x * jax.nn.sigmoid(x)


def _softplus(x):
    return jnp.maximum(x, 0.0) + jnp.log(1.0 + jnp.exp(-jnp.abs(x)))


def _pack_rows(x):
    w = x.shape[1] // 2
    lo = lax.bitcast_convert_type(x[:, :w].astype(BF16).astype(F32), jnp.uint32)
    hi = lax.bitcast_convert_type(x[:, w:].astype(BF16).astype(F32), jnp.uint32)
    return lax.bitcast_convert_type(lax.shift_right_logical(lo, jnp.uint32(16)) | hi, jnp.int32)


def _unpack_rows(words):
    u = lax.bitcast_convert_type(words, jnp.uint32)
    a = lax.bitcast_convert_type(lax.shift_left(u, jnp.uint32(16)), F32)
    b = lax.bitcast_convert_type(u & jnp.uint32(0xFFFF0000), F32)
    return jnp.concatenate([a, b], axis=-1)


def _seg_sum64(x):
    first = _iota((1, LANES), 1) < 64
    outs = []
    for j in range(x.shape[-1] // LANES):
        blk = x[:, j * LANES:(j + 1) * LANES]
        lo = jnp.sum(jnp.where(first, blk, 0.0), axis=-1, keepdims=True)
        hi = jnp.sum(jnp.where(first, 0.0, blk), axis=-1, keepdims=True)
        outs.append(jnp.where(first, lo, hi))
    return jnp.concatenate(outs, axis=-1)


def _block_mask(shape, row_blk, col_blk):
    keep = _vdiv(_iota(shape, 0), row_blk) == _vdiv(_iota(shape, 1), col_blk)
    return jnp.where(keep, 1.0, 0.0).astype(BF16)


def _block_diag(x, mask):
    reps = mask.shape[0] // x.shape[0]
    return jnp.concatenate([x.astype(BF16)] * reps, axis=0) * mask


def _expand_heads(x, expand):
    hi = x.astype(BF16)
    r1 = x - hi.astype(F32)
    mid = r1.astype(BF16)
    lo = (r1 - mid.astype(F32)).astype(BF16)
    return _dot(hi, expand) + _dot(mid, expand) + _dot(lo, expand)


def _cumsum_rows(tri, x):
    hi = x.astype(BF16)
    lo = (x - hi.astype(F32)).astype(BF16)
    return _dot(tri, hi) + _dot(tri, lo)


def _rope_kernel(pos_ref, cos_ref, sin_ref):
    lane = _iota((1, ROPE_W), 1)
    half = RET_DK // 2
    k = _vmod(lane, half).astype(F32)
    inv_freq = jnp.exp(k * (-math.log(ROPE_BASE) / half))
    ang = pos_ref[...].astype(F32) * inv_freq
    first = _vmod(lane, RET_DK) < half
    cos_ref[...] = jnp.cos(ang)
    sin_ref[...] = jnp.where(first, -1.0, 1.0) * jnp.sin(ang)


def _rope_tables(pos_col):
    t = pos_col.shape[0]
    tm = TM_ROPE
    w = ROPE_W
    return pl.pallas_call(
        _rope_kernel,
        grid=(t // tm,),
        in_specs=[pl.BlockSpec((tm, 1), lambda i: (i, 0))],
        out_specs=[pl.BlockSpec((tm, w), lambda i: (i, 0))] * 2,
        out_shape=[jax.ShapeDtypeStruct((t, w), F32)] * 2,
        compiler_params=pltpu.CompilerParams(dimension_semantics=("arbitrary",)),
        name="rope_tables",
    )(pos_col)


def _mixproj_kernel(xn_ref, w_ref, cos_ref, sin_ref, retw_ref, convw_ref, convb_ref, dtb_ref, alog_ref,
                    dskip_ref, ssdw_ref, wgk_ref, bgk_ref, glaw_ref, h_ref,
                    proj_a, proj_b, ret_s, ssd_s, gla_s, stage, m_heads, m_groups, m_gla, m_expand, *,
                    chunks_per_seq):
    C = CHUNK
    i = pl.program_id(0)
    cur = jnp.maximum(i - 1, 0)

    @pl.when(i == 0)
    def _():
        m_heads[...] = _block_mask(m_heads.shape, C, 64)
        m_groups[...] = _block_mask(m_groups.shape, C * SSD_HEADS // SSD_GROUPS, SSD_STATE)
        m_gla[...] = _block_mask(m_gla.shape, C, GLA_DK)
        m_expand[...] = _block_mask(m_expand.shape, 1, SSD_HEAD_DIM)
        proj_b[...] = jnp.zeros_like(proj_b)

    @pl.when(lax.rem(cur, chunks_per_seq) == 0)
    def _():
        ret_s[...] = jnp.zeros_like(ret_s)
        ssd_s[...] = jnp.zeros_like(ssd_s)
        gla_s[...] = jnp.zeros_like(gla_s)
        for g in range(MIX_G):
            stage[g, 0:8, :] = jnp.zeros((8, SSD_XBC), F32)

    def conv_act(sq, pref):
        stage[sq, 8:8 + C, :] = pref[sq * C:(sq + 1) * C, COL["sxbc"]:COL["sxbc"] + SSD_XBC]
        acc = convb_ref[...] + convw_ref[0:1, :] * stage[sq, 5:5 + C, :]
        for j in range(1, SSD_CONV):
            acc = acc + convw_ref[j:j + 1, :] * stage[sq, 5 + j:5 + j + C, :]
        stage[sq, 0:8, :] = stage[sq, C:C + 8, :]
        return _silu(acc)

    lane256 = _iota((1, 256), 1)
    head = _vdiv(lane256, 64).astype(F32)
    log_gamma = jnp.log(1.0 - jnp.exp((-5.0 - head) * math.log(2.0)))
    row = _iota((C, 1), 0).astype(F32)
    dist = row - _vmod(lane256, 64).astype(F32)
    ret_intra = jnp.where(dist >= 0, jnp.exp(log_gamma * jnp.maximum(dist, 0.0)), 0.0)
    ret_qdec = jnp.exp(log_gamma * (row + 1.0))
    ret_kdec = jnp.exp(log_gamma * (C - 1.0 - row))
    ret_cdec = jnp.exp(log_gamma * C)
    first_half = _vmod(lane256, RET_DK) < (RET_DK // 2)

    tri = jnp.where(_iota((C, C), 0) >= _iota((C, C), 1), 1.0, 0.0).astype(BF16)
    causal4 = _iota((C, 256), 0) >= _vmod(_iota((C, 256), 1), 64)
    causal8 = _iota((C, 512), 0) >= _vmod(_iota((C, 512), 1), 64)
    eye8 = _iota((C, 512), 0) == _vmod(_iota((C, 512), 1), 64)

    a_neg = -jnp.exp(alog_ref[...])

    def rot(t, cos, sin):
        sw = jnp.where(first_half, pltpu.roll(t, 256 - 32, 1), pltpu.roll(t, 32, 1))
        return t * cos + sw * sin

    def chunk_of(sq, pref):
        def seg(name, width):
            return pref[sq * C:(sq + 1) * C, COL[name]:COL[name] + width]

        xact = conv_act(sq, pref)
        yield

        reps = RET_HEADS * RET_DK // ROPE_W
        cos = jnp.concatenate([cos_ref[sq]] * reps, axis=-1)
        sin = jnp.concatenate([sin_ref[sq]] * reps, axis=-1)
        q = rot(seg("rq", 256), cos, sin)
        k = rot(seg("rk", 256), cos, sin) * (RET_DK ** -0.5)
        v = seg("rv", 256)
        vb = v.astype(BF16)
        kbd = _block_diag(k, m_heads[...])
        scores = _dot(q.astype(BF16), kbd, _NT) * ret_intra
        vbd = _block_diag(v, m_heads[...])
        yield
        s_prev = ret_s[sq]
        o = _dot(scores.astype(BF16), vbd) + _dot((q * ret_qdec).astype(BF16), s_prev.astype(BF16))
        contrib = _dot((k * ret_kdec).astype(BF16), vb, _TN)
        yield
        keep = _vdiv(_iota((256, 256), 0), RET_DK) == _vdiv(_iota((256, 256), 1), RET_DV)
        ret_s[sq] = jnp.where(keep, ret_cdec * s_prev + contrib, 0.0)
        mu = _seg_sum64(o) * (1.0 / RET_DV)
        oc = o - mu
        var = _seg_sum64(oc * oc) * (1.0 / RET_DV)
        o = oc * lax.rsqrt(var + LN_EPS) * retw_ref[...]
        h_ref[sq, :, 0:RET_W] = (_silu(seg("rg", 256)) * o).astype(BF16)
        yield

        xs = xact[:, 0:SSD_W]
        bm = xact[:, SSD_W:SSD_W + SSD_BC]
        cm = xact[:, SSD_W + SSD_BC:SSD_XBC]
        cmb = cm.astype(BF16)
        dt_c = _softplus(seg("sdt", LANES) + dtb_ref[...])
        acum_c = _cumsum_rows(tri, dt_c * a_neg)
        both = _expand_heads(jnp.concatenate([dt_c, acum_c], axis=0), m_expand[...])
        dt = both[0:C, :]
        acum = both[C:2 * C, :]
        yield
        arow = jnp.sum(jnp.where(eye8, acum, 0.0), axis=0, keepdims=True)
        decay = jnp.exp(jnp.where(causal8, acum - arow, NEG_BIG))
        b8 = _block_diag(bm, m_groups[...])
        cb = _dot(cmb, b8, _NT)
        yield
        m = (cb * decay).astype(BF16)
        xdt = xs * dt
        s2 = ssd_s[sq]
        half = SSD_W // SSD_GROUPS
        ys = []
        for g in range(SSD_GROUPS):
            xbd = _block_diag(xdt[:, g * half:(g + 1) * half], m_heads[...])
            ys.append(_dot(m[:, g * half:(g + 1) * half], xbd))
        y = jnp.concatenate(ys, axis=-1)
        y = y + _dot(cmb, s2.astype(BF16)) * jnp.exp(acum)
        y = y + dskip_ref[...] * xs
        a_last = acum[C - 1:C, :]
        sd = jnp.exp(a_last - acum)
        contrib_s = _dot(bm.astype(BF16), (xdt * sd).astype(BF16), _TN)
        yield
        keep_s = _vdiv(_iota(s2.shape, 0), SSD_STATE) == _vdiv(_iota(s2.shape, 1), half)
        ssd_s[sq] = jnp.where(keep_s, s2 * jnp.exp(a_last) + contrib_s, 0.0)
        yz = y * _silu(seg("sz", SSD_W))
        outs = []
        for g in range(SSD_GROUPS):
            blk = yz[:, g * half:(g + 1) * half]
            ms = jnp.mean(blk * blk, axis=-1, keepdims=True)
            outs.append(blk * lax.rsqrt(ms + NORM_EPS))
        h_ref[sq, :, RET_W:RET_W + SSD_W] = (jnp.concatenate(outs, axis=-1) * ssdw_ref[...]).astype(BF16)

        yield

        gq = seg("gq", GLA_QK) * (GLA_DK ** -0.5)
        gkk = seg("gk", GLA_QK)
        gv = seg("gv", GLA_W)
        gkl = _dot(seg("ggk", 128).astype(BF16), wgk_ref[...].astype(BF16)) + bgk_ref[...]
        yield
        log_a = -_softplus(-gkl) * (1.0 / GLA_TEMP)
        b = _cumsum_rows(tri, log_a)
        yield
        q_t = (gq * jnp.exp(b)).astype(BF16)
        k_t = gkk * jnp.exp(-b)
        kbd_g = _block_diag(k_t, m_gla[...])
        att = jnp.where(causal4, _dot(q_t, kbd_g, _NT), 0.0)
        yield
        vbd_g = _block_diag(gv, m_heads[...])
        st = gla_s[sq]
        og = _dot(att.astype(BF16), vbd_g) + _dot(q_t, st.astype(BF16), _NT)
        b_last = b[C - 1:C, :]
        kd = (gkk * jnp.exp(b_last - b)).astype(BF16)
        contrib_g = _dot(gv.astype(BF16), kd, _TN)
        yield
        keep_g = _vdiv(_iota(st.shape, 0), GLA_DV) == _vdiv(_iota(st.shape, 1), GLA_DK)
        gla_s[sq] = jnp.where(keep_g, st * jnp.exp(b_last) + contrib_g, 0.0)
        ms = _seg_sum64(og * og) * (1.0 / GLA_DV)
        og = og * lax.rsqrt(ms + NORM_EPS) * glaw_ref[...]
        h_ref[sq, :, RET_W + SSD_W:D_MIX] = (_silu(seg("gg", GLA_W)) * og).astype(BF16)

    def step(p_read, p_write):
        xb = xn_ref[...].reshape(MIX_G * C, D_MODEL).astype(BF16)
        edges = list(range(0, NP, PROJ_SLAB)) + [NP]
        slabs = list(zip(edges[:-1], edges[1:]))

        live = [chunk_of(sq, p_read) for sq in range(MIX_G)]
        while live or slabs:
            live = [g for g in live if next(g, "done") != "done"]
            if slabs:
                lo, hi = slabs.pop(0)
                p_write[:, lo:hi] = _dot(xb, w_ref[:, lo:hi])

    @pl.when(lax.rem(i, 2) == 0)
    def _():
        step(proj_b, proj_a)

    @pl.when(lax.rem(i, 2) == 1)
    def _():
        step(proj_a, proj_b)


def _mixproj(x3, w_p, cos_t, sin_t, params):
    batch, seq, _ = x3.shape
    assert batch % MIX_G == 0 and seq % CHUNK == 0
    cps = seq // CHUNK
    n = (batch // MIX_G) * cps

    def cur_map(i):
        c = jnp.clip(i - 1, 0, n - 1)
        return (c // cps, c % cps, 0)

    def next_map(i):
        c = jnp.minimum(i, n - 1)
        return (c // cps, c % cps, 0)

    const = lambda i: (0, 0)
    specs = [pl.BlockSpec((MIX_G, CHUNK, D_MODEL), next_map),
             pl.BlockSpec((D_MODEL, NP), const),
             pl.BlockSpec((MIX_G, CHUNK, ROPE_W), cur_map),
             pl.BlockSpec((MIX_G, CHUNK, ROPE_W), cur_map)]
    specs += [pl.BlockSpec(p.shape, const) for p in params]
    return pl.pallas_call(
        functools.partial(_mixproj_kernel, chunks_per_seq=cps),
        grid=(n + 1,),
        in_specs=specs,
        out_specs=pl.BlockSpec((MIX_G, CHUNK, D_MIX), cur_map),
        out_shape=jax.ShapeDtypeStruct((batch, seq, D_MIX), BF16),
        scratch_shapes=[pltpu.VMEM((MIX_G * CHUNK, NP), F32),
                        pltpu.VMEM((MIX_G * CHUNK, NP), F32),
                        pltpu.VMEM((MIX_G, 256, 256), F32),
                        pltpu.VMEM((MIX_G, SSD_BC, SSD_W), F32),
                        pltpu.VMEM((MIX_G, GLA_W, GLA_QK), F32),
                        pltpu.VMEM((MIX_G, CHUNK + 8, SSD_XBC), F32),
                        pltpu.VMEM((RET_HEADS * CHUNK, 256), BF16),
                        pltpu.VMEM((SSD_HEADS * CHUNK, SSD_BC), BF16),
                        pltpu.VMEM((GLA_HEADS * CHUNK, GLA_QK), BF16),
                        pltpu.VMEM((LANES, SSD_W), BF16)],
        compiler_params=pltpu.CompilerParams(dimension_semantics=("arbitrary",),
                                             vmem_limit_bytes=VMEM_LIMIT),
        name="inproj_mixer",
    )(x3, w_p, cos_t, sin_t, *params)


def _layer_norm(y, g, b):
    mu = jnp.mean(y, axis=-1, keepdims=True)
    yc = y - mu
    var = jnp.mean(yc * yc, axis=-1, keepdims=True)
    return yc * lax.rsqrt(var + LN_EPS) * g + b


def _post_kernel(h_ref, x_ref, wout_ref, g_ref, b_ref, wrh_ref, wrc_ref, br_ref,
                 x1_ref, x1p_ref, mi_ref, mf_ref, cnt_ref, carry):
    sub = TM_POST // POST_SPLIT

    @pl.when(pl.program_id(0) == 0)
    def _():
        carry[...] = jnp.zeros_like(carry)

    lane_i = _iota((sub, LANES), 1)
    lane = lane_i.astype(F32)
    found = {}

    def sub_tile(part):
        rows = slice(part * sub, (part + 1) * sub)
        mix = _dot(h_ref[rows, :], wout_ref[...])
        yield
        x1 = _layer_norm(DEEPNORM_ALPHA * x_ref[rows, :] + mix, g_ref[...], b_ref[...])
        x1_ref[rows, :] = x1
        x1p_ref[rows, :] = _pack_rows(x1)
        x_hi = x1.astype(BF16)
        x_lo = (x1 - x_hi.astype(F32)).astype(BF16)
        both = _dot(x_hi, wrc_ref[...])
        logits = both[:, :LANES] + both[:, LANES:] + _dot(x_lo, wrh_ref[...]) + br_ref[...]
        yield
        work = logits
        vals, idxs = [], []
        multi = jnp.zeros((sub, LANES), F32)
        for _ in range(TOP_K):
            m = jnp.max(work, axis=-1, keepdims=True)
            idx = jnp.min(jnp.where(work == m, lane, float(LANES)), axis=-1, keepdims=True)
            hit = lane == idx
            multi = multi + hit.astype(F32)
            work = jnp.where(hit, -jnp.inf, work)
            vals.append(m)
            idxs.append(idx)
            yield
        exps = [jnp.exp(v - vals[0]) for v in vals]
        denom = exps[0] + exps[1] + exps[2] + exps[3]
        gates = [e / denom for e in exps]
        before = (_iota((sub, sub), 0) > _iota((sub, sub), 1)).astype(BF16)
        found[part] = (idxs, gates, _dot(before, multi.astype(BF16)), multi)

    live = [sub_tile(part) for part in range(POST_SPLIT)]
    while live:
        live = [g for g in live if next(g, "done") != "done"]

    base = carry[...]
    for part in range(POST_SPLIT):
        idxs, gates, prior_local, multi = found[part]
        prior = prior_local + base
        mi = jnp.zeros((sub, LANES), F32)
        mf = jnp.zeros((sub, LANES), F32)
        for kk in range(TOP_K):
            rank = jnp.sum(jnp.where(lane == idxs[kk], prior, 0.0), axis=-1, keepdims=True)
            mi = jnp.where(lane_i == kk, idxs[kk], mi)
            mi = jnp.where(lane_i == TOP_K + kk, rank, mi)
            mf = jnp.where(lane_i == kk, gates[kk], mf)
        mi_ref[:, part * sub:(part + 1) * sub] = jnp.transpose(mi)[0:2 * TOP_K, :].astype(jnp.int32)
        mf_ref[part * sub:(part + 1) * sub, :] = mf
        base = base + jnp.sum(multi, axis=0, keepdims=True)
    carry[...] = base
    cnt_ref[...] = jnp.broadcast_to(base, cnt_ref.shape)


def _post(h, x2, w_out_b, ln_g, ln_b, wr_hi, wr_cat, br_p):
    t = x2.shape[0]
    tm = TM_POST
    row = lambda i: (i, 0)
    const = lambda i: (0, 0)
    return pl.pallas_call(
        _post_kernel,
        grid=(t // tm,),
        in_specs=[pl.BlockSpec((tm, D_MIX), row), pl.BlockSpec((tm, D_MODEL), row),
                  pl.BlockSpec((D_MIX, D_MODEL), const), pl.BlockSpec((1, D_MODEL), const),
                  pl.BlockSpec((1, D_MODEL), const), pl.BlockSpec((D_MODEL, LANES), const),
                  pl.BlockSpec((D_MODEL, 2 * LANES), const), pl.BlockSpec((1, LANES), const)],
        out_specs=[pl.BlockSpec((tm, D_MODEL), row), pl.BlockSpec((tm, ROW_WORDS), row),
                   pl.BlockSpec((2 * TOP_K, tm), lambda i: (0, i)), pl.BlockSpec((tm, LANES), row),
                   pl.BlockSpec((8, LANES), const)],
        out_shape=[jax.ShapeDtypeStruct((t, D_MODEL), F32), jax.ShapeDtypeStruct((t, ROW_WORDS), jnp.int32),
                   jax.ShapeDtypeStruct((2 * TOP_K, t), jnp.int32), jax.ShapeDtypeStruct((t, LANES), F32),
                   jax.ShapeDtypeStruct((8, LANES), F32)],
        scratch_shapes=[pltpu.VMEM((1, LANES), F32)],
        compiler_params=pltpu.CompilerParams(dimension_semantics=("arbitrary",),
                                             vmem_limit_bytes=VMEM_LIMIT),
        name="outproj_ln_router",
    )(h, x2, w_out_b, ln_g, ln_b, wr_hi, wr_cat, br_p)


def _ffn_kernel(be_ref, nv_ref, par_ref, ahead_ref, nx_ref, x_ref, wg_ref, bg_ref, wu_ref, bu_ref, wd_ref, bd_ref,
                o_ref, ga, ua, da, gb, ub, db):
    del nx_ref
    wa, wb = (ga, ua, da), (gb, ub, db)
    i = pl.program_id(0)
    b = jnp.clip(i - 1, 0, be_ref.shape[0] - 1)
    valid = jnp.logical_and(i >= 1, b < nv_ref[0])
    parity = par_ref[b]
    cast_ahead = ahead_ref[b]
    f32_weights = (wg_ref, wu_ref, wd_ref)

    def cast_into(dst, j):
        dst[j][...] = f32_weights[j][0, 0].astype(BF16)

    @pl.when(i == 0)
    def _():
        for j in range(3):
            cast_into(wa, j)

    def block(cur, nxt):
        x = _unpack_rows(x_ref[...]).astype(BF16)
        if nxt is not None:
            cast_into(nxt, 0)
        hg = jnp.minimum(_dot(x, cur[0][...]) + bg_ref[0, 0], SWIGLU_LIMIT)
        if nxt is not None:
            cast_into(nxt, 1)
        hu = jnp.clip(_dot(x, cur[1][...]) + bu_ref[0, 0], -SWIGLU_LIMIT, SWIGLU_LIMIT)
        hh = (hu + 1.0) * hg * jax.nn.sigmoid(SWIGLU_ALPHA * hg)
        if nxt is not None:
            cast_into(nxt, 2)
        o_ref[...] = _pack_rows(_dot(hh.astype(BF16), cur[2][...]) + bd_ref[0, 0])

    for par, cur, other in ((0, wa, wb), (1, wb, wa)):
        for ahead in (0, 1):
            @pl.when(valid & (parity == par) & (cast_ahead == ahead))
            def _(cur=cur, nxt=other if ahead else None):
                block(cur, nxt)


def _ffn(layer, block_expert, n_valid, end_padded, xin, wg, bg, wu, bu, wd, bd):
    cap = xin.shape[0]
    nb = cap // BM

    ids = jnp.arange(nb, dtype=jnp.int32)
    first = (ids == 0) | (block_expert != jnp.roll(block_expert, 1))
    parity = (jnp.cumsum(first.astype(jnp.int32)) - 1) & 1
    after_run = jnp.sum(jnp.where(block_expert[:, None] == jnp.arange(N_EXPERTS, dtype=jnp.int32)[None, :],
                                  end_padded[None, :], 0), axis=1) // BM
    has_next = after_run < n_valid[0]
    next_expert = jnp.where(has_next, block_expert[jnp.minimum(after_run, nb - 1)], block_expert)
    ahead = ((ids + 1 == after_run) & has_next).astype(jnp.int32)

    def blk(i, nv):
        return jnp.clip(i - 1, 0, jnp.maximum(nv[0] - 1, 0))

    row = lambda i, be, nv, pa, ah, nx: (blk(i, nv), 0)
    bmap = lambda i, be, nv, pa, ah, nx: (layer, be[blk(i, nv)], 0, 0)
    wmap = lambda i, be, nv, pa, ah, nx: (layer, jnp.where(i == 0, be[0], nx[blk(i, nv)]), 0, 0)
    grid_spec = pltpu.PrefetchScalarGridSpec(
        num_scalar_prefetch=5,
        grid=(nb + 1,),
        in_specs=[pl.BlockSpec((BM, ROW_WORDS), row),
                  pl.BlockSpec((1, 1, D_MODEL, D_FF), wmap), pl.BlockSpec((1, 1, 1, D_FF), bmap),
                  pl.BlockSpec((1, 1, D_MODEL, D_FF), wmap), pl.BlockSpec((1, 1, 1, D_FF), bmap),
                  pl.BlockSpec((1, 1, D_FF, D_MODEL), wmap), pl.BlockSpec((1, 1, 1, D_MODEL), bmap)],
        out_specs=pl.BlockSpec((BM, ROW_WORDS), row),
        scratch_shapes=[pltpu.VMEM((D_MODEL, D_FF), BF16), pltpu.VMEM((D_MODEL, D_FF), BF16),
                        pltpu.VMEM((D_FF, D_MODEL), BF16)] * 2,
    )
    return pl.pallas_call(
        _ffn_kernel,
        grid_spec=grid_spec,
        out_shape=jax.ShapeDtypeStruct((cap, ROW_WORDS), jnp.int32),
        compiler_params=pltpu.CompilerParams(dimension_semantics=("arbitrary",),
                                             vmem_limit_bytes=VMEM_LIMIT),
        name="expert_ffn",
    )(block_expert, n_valid, parity, ahead, next_expert, xin, wg, bg, wu, bu, wd, bd)


def _sc_gather(table, idx3):
    nw, n_chunks, ch = idx3.shape
    width = table.shape[1]
    per_worker = n_chunks * ch
    mesh = plsc.VectorSubcoreMesh(core_axis_name="c", subcore_axis_name="s")
    n_cores = mesh.num_cores
    assert nw == n_cores * mesh.num_subcores and n_chunks % 2 == 0 and ch == SC_CHUNK

    def body(table_hbm, idx_hbm, out_hbm, idx_v, rows0, rows1, sem_g0, sem_g1, sem_w0, sem_w1):
        wid = lax.axis_index("s") * n_cores + lax.axis_index("c")
        base = wid * per_worker
        pltpu.sync_copy(idx_hbm.at[wid], idx_v)

        @pl.loop(0, n_chunks, step=2)
        def _(c):
            g0 = pltpu.async_copy(table_hbm.at[idx_v.at[c]], rows0, sem_g0)
            g1 = pltpu.async_copy(table_hbm.at[idx_v.at[c + 1]], rows1, sem_g1)
            g0.wait()
            w0 = pltpu.async_copy(rows0, out_hbm.at[pl.ds(base + c * ch, ch)], sem_w0)
            g1.wait()
            w1 = pltpu.async_copy(rows1, out_hbm.at[pl.ds(base + (c + 1) * ch, ch)], sem_w1)
            w0.wait()
            w1.wait()

    return pl.kernel(
        body,
        out_type=jax.ShapeDtypeStruct((nw * per_worker, width), table.dtype),
        mesh=mesh,
        scratch_types=[pltpu.VMEM((n_chunks, ch), jnp.int32),
                       pltpu.VMEM((ch, width), table.dtype), pltpu.VMEM((ch, width), table.dtype),
                       pltpu.SemaphoreType.DMA, pltpu.SemaphoreType.DMA,
                       pltpu.SemaphoreType.DMA, pltpu.SemaphoreType.DMA],
        name="sc_row_gather",
    )(table, idx3)


def _sc_scatter(rows, idx3, n_out):
    nw, n_lists, ch = idx3.shape
    n_chunks = n_lists // TOP_K
    width = rows.shape[1]
    per_worker = n_chunks * ch
    mesh = plsc.VectorSubcoreMesh(core_axis_name="c", subcore_axis_name="s")
    n_cores = mesh.num_cores
    assert nw == n_cores * mesh.num_subcores and n_chunks % 2 == 0 and ch == SC_CHUNK
    assert nw * per_worker == rows.shape[0]

    def body(rows_hbm, idx_hbm, out_hbm, idx_v, buf0, buf1, sem_r0, sem_r1, sem_w0, sem_w1):
        wid = lax.axis_index("s") * n_cores + lax.axis_index("c")
        base = wid * per_worker
        pltpu.sync_copy(idx_hbm.at[wid], idx_v)

        @pl.loop(0, n_chunks, step=2)
        def _(c):
            r0 = pltpu.async_copy(rows_hbm.at[pl.ds(base + c * ch, ch)], buf0, sem_r0)
            r1 = pltpu.async_copy(rows_hbm.at[pl.ds(base + (c + 1) * ch, ch)], buf1, sem_r1)
            r0.wait()
            w0 = [pltpu.async_copy(buf0, out_hbm.at[idx_v.at[c * TOP_K + k]], sem_w0) for k in range(TOP_K)]
            r1.wait()
            w1 = [pltpu.async_copy(buf1, out_hbm.at[idx_v.at[(c + 1) * TOP_K + k]], sem_w1) for k in range(TOP_K)]
            for w in w0 + w1:
                w.wait()

    return pl.kernel(
        body,
        out_type=jax.ShapeDtypeStruct((n_out, width), rows.dtype),
        mesh=mesh,
        scratch_types=[pltpu.VMEM((n_lists, ch), jnp.int32),
                       pltpu.VMEM((ch, width), rows.dtype), pltpu.VMEM((ch, width), rows.dtype),
                       pltpu.SemaphoreType.DMA, pltpu.SemaphoreType.DMA,
                       pltpu.SemaphoreType.DMA, pltpu.SemaphoreType.DMA],
        name="sc_row_scatter",
    )(rows, idx3)


def _combine_kernel(x_ref, rows_ref, mf_ref, g_ref, b_ref, o_ref):
    mf = mf_ref[...]
    y = DEEPNORM_ALPHA * x_ref[...]
    for kk in range(TOP_K):
        y = y + mf[:, kk:kk + 1] * _unpack_rows(rows_ref[kk])
    o_ref[...] = _layer_norm(y, g_ref[...], b_ref[...])


def _combine(x1, rows, mf, ln_g, ln_b):
    t = x1.shape[0]
    tm = TM_COMB
    row = lambda i: (i, 0)
    const = lambda i: (0, 0)
    return pl.pallas_call(
        _combine_kernel,
        grid=(t // tm,),
        in_specs=[pl.BlockSpec((tm, D_MODEL), row),
                  pl.BlockSpec((TOP_K, tm, ROW_WORDS), lambda i: (0, i, 0)),
                  pl.BlockSpec((tm, LANES), row),
                  pl.BlockSpec((1, D_MODEL), const), pl.BlockSpec((1, D_MODEL), const)],
        out_specs=pl.BlockSpec((tm, D_MODEL), row),
        out_shape=jax.ShapeDtypeStruct((t, D_MODEL), F32),
        compiler_params=pltpu.CompilerParams(dimension_semantics=("arbitrary",),
                                             vmem_limit_bytes=VMEM_LIMIT),
        name="combine_ln",
    )(x1, rows, mf, ln_g, ln_b)


def _relayout_w_in(w):
    widths = (256, 256, 256, 256, SSD_W, SSD_XBC, SSD_HEADS, GLA_QK, GLA_QK, GLA_W, GLA_RANK, GLA_W)
    offs = [0]
    for wd in widths:
        offs.append(offs[-1] + wd)
    parts = [w[:, offs[i]:offs[i + 1]] for i in range(len(widths))]
    parts[6] = jnp.pad(parts[6], ((0, 0), (0, LANES - SSD_HEADS)))
    parts[10] = jnp.pad(parts[10], ((0, 0), (0, LANES - GLA_RANK)))
    return jnp.concatenate(parts, axis=1).astype(BF16)


def _rep_heads(p):
    return jnp.repeat(p, SSD_HEAD_DIM)[None, :]


def _pad_heads(p):
    return jnp.pad(p, (0, LANES - SSD_HEADS))[None, :]


def kernel(x, positions, w_in, w_out, ret_norm_w, ssd_conv_w, ssd_conv_b, ssd_dt_bias, ssd_a_log, ssd_d,
           ssd_norm_w, gla_w_gk2, gla_b_gk2, gla_norm_w, ln1_g, ln1_b, w_router, b_router, w_gate, b_gate,
           w_up, b_up, w_down, b_down, ln2_g, ln2_b):
    batch, seq, d = x.shape
    t = batch * seq
    depth = w_in.shape[0]
    assert d == D_MODEL and t % TM_POST == 0 and t % TM_COMB == 0 and t % TM_ROPE == 0
    n_assign = t * TOP_K
    nb = pl.cdiv(n_assign, BM) + N_EXPERTS
    cap = nb * BM

    cos_t, sin_t = _rope_tables(positions.reshape(t, 1))
    cos_t = cos_t.reshape(batch, seq, -1)
    sin_t = sin_t.reshape(batch, seq, -1)
    x2 = x.reshape(t, d)

    for l in range(depth):
        params = (ret_norm_w[l][None, :], ssd_conv_w[l], ssd_conv_b[l][None, :], _pad_heads(ssd_dt_bias[l]),
                  _pad_heads(ssd_a_log[l]), _rep_heads(ssd_d[l]), ssd_norm_w[l][None, :],
                  jnp.pad(gla_w_gk2[l], ((0, LANES - GLA_RANK), (0, 0))), gla_b_gk2[l][None, :],
                  gla_norm_w[l][None, :])
        h = _mixproj(x2.reshape(batch, seq, d), _relayout_w_in(w_in[l]), cos_t, sin_t, params).reshape(t, D_MIX)

        wr_p = jnp.pad(w_router[l], ((0, 0), (0, LANES - N_EXPERTS)))
        br_p = jnp.pad(b_router[l], (0, LANES - N_EXPERTS), constant_values=NEG_BIG)[None, :]
        wr_hi = wr_p.astype(BF16)
        wr_lo = (wr_p - wr_hi.astype(F32)).astype(BF16)
        x1, x1p, mi, mf, cnt = _post(h, x2, w_out[l].astype(BF16), ln1_g[l][None, :], ln1_b[l][None, :],
                                     wr_hi, jnp.concatenate([wr_hi, wr_lo], axis=1), br_p)

        counts = cnt[0, :N_EXPERTS].astype(jnp.int32)
        padded = (counts + BM - 1) // BM * BM
        end_padded = jnp.cumsum(padded)
        start_padded = end_padded - padded
        top_idx = mi[:TOP_K]
        start_of = jnp.sum(jnp.where(top_idx[None] == jnp.arange(N_EXPERTS, dtype=jnp.int32)[:, None, None],
                                     start_padded[:, None, None], 0), axis=0)
        dest = start_of + mi[TOP_K:]
        block_start = jnp.arange(nb, dtype=jnp.int32) * BM
        block_expert = jnp.minimum(jnp.sum((end_padded[None, :] <= block_start[:, None]).astype(jnp.int32), axis=1),
                                   N_EXPERTS - 1)
        n_valid = (end_padded[-1:] // BM).astype(jnp.int32)

        scatter_idx = dest.reshape(TOP_K, SC_WORKERS, -1, SC_CHUNK).transpose(1, 2, 0, 3)
        xin = _sc_scatter(x1p, scatter_idx.reshape(SC_WORKERS, -1, SC_CHUNK), cap)
        yb = _ffn(l, block_expert, n_valid, end_padded, xin, w_gate, b_gate[:, :, None, :], w_up, b_up[:, :, None, :],
                  w_down, b_down[:, :, None, :])
        rows = _sc_gather(yb, dest.reshape(SC_WORKERS, -1, SC_CHUNK))
        x2 = _combine(x1, rows.reshape(TOP_K, t, ROW_WORDS), mf, ln2_g[l][None, :], ln2_b[l][None, :])
    return x2.reshape(batch, seq, d)
```

```python
import functools
import math

import jax
import jax.numpy as jnp
from jax import lax
from jax.experimental import pallas as pl
from jax.experimental.pallas import tpu as pltpu
from jax.experimental.pallas import tpu_sc as plsc

F32 = jnp.float32
BF16 = jnp.bfloat16

D_MODEL = 1024
CHUNK = 64
RET_HEADS, RET_DK, RET_DV = 4, 64, 64
RET_W = RET_HEADS * RET_DV
SSD_HEADS, SSD_HEAD_DIM, SSD_STATE, SSD_GROUPS, SSD_CONV = 8, 64, 64, 2, 4
SSD_W = SSD_HEADS * SSD_HEAD_DIM
SSD_BC = SSD_GROUPS * SSD_STATE
SSD_XBC = SSD_W + 2 * SSD_BC
GLA_HEADS, GLA_DK, GLA_DV, GLA_RANK, GLA_TEMP = 4, 32, 64, 16, 16.0
GLA_QK = GLA_HEADS * GLA_DK
GLA_W = GLA_HEADS * GLA_DV
D_MIX = RET_W + SSD_W + GLA_W
N_EXPERTS, TOP_K, D_FF = 32, 4, 1024
SWIGLU_LIMIT, SWIGLU_ALPHA = 7.0, 1.702
ROPE_BASE = 10000.0
LN_EPS, NORM_EPS = 1e-5, 1e-6
DEPTH = 2
DEEPNORM_ALPHA = (2.0 * DEPTH) ** 0.25

LANES = 128
ROPE_W = LANES
NEG_BIG = -1e30
VMEM_LIMIT = 56 * 1024 * 1024

_SEGS = (("rq", 256), ("rk", 256), ("rv", 256), ("rg", 256), ("sz", SSD_W), ("sxbc", SSD_XBC),
         ("sdt", 128), ("gq", 128), ("gk", 128), ("gv", 256), ("ggk", 128), ("gg", 256))
COL = {}
_off = 0
for _n, _w in _SEGS:
    COL[_n] = _off
    _off += _w
NP = _off

MIX_G = 8
PROJ_SLAB = 512
TM_POST = 1024
POST_SPLIT = 4
TM_COMB = 1024
TM_ROPE = 2048
BM = 768
FFN_SUB = 256
ROW_WORDS = D_MODEL // 2
SC_WORKERS = 32
SC_CHUNK = 64


def _dot(a, b, dims=(((1,), (0,)), ((), ())), precision=None):
    return lax.dot_general(a, b, dims, precision=precision, preferred_element_type=F32)


_NT = (((1,), (1,)), ((), ()))
_TN = (((0,), (0,)), ((), ()))


def _iota(shape, dim):
    return lax.broadcasted_iota(jnp.int32, shape, dim)


def _vdiv(x, n):
    assert n & (n - 1) == 0
    return lax.shift_right_logical(x, n.bit_length() - 1)


def _vmod(x, n):
    assert n & (n - 1) == 0
    return jnp.bitwise_and(x, n - 1)


def _silu(x):
    return x * jax.nn.sigmoid(x)


def _softplus(x):
    return jnp.maximum(x, 0.0) + jnp.log(1.0 + jnp.exp(-jnp.abs(x)))


def _pack_rows(x):
    w = x.shape[1] // 2
    lo = lax.bitcast_convert_type(x[:, :w].astype(BF16).astype(F32), jnp.uint32)
    hi = lax.bitcast_convert_type(x[:, w:].astype(BF16).astype(F32), jnp.uint32)
    return lax.bitcast_convert_type(lax.shift_right_logical(lo, jnp.uint32(16)) | hi, jnp.int32)


def _unpack_rows(words):
    u = lax.bitcast_convert_type(words, jnp.uint32)
    a = lax.bitcast_convert_type(lax.shift_left(u, jnp.uint32(16)), F32)
    b = lax.bitcast_convert_type(u & jnp.uint32(0xFFFF0000), F32)
    return jnp.concatenate([a, b], axis=-1)


def _seg_sum64(x):
    first = _iota((1, LANES), 1) < 64
    outs = []
    for j in range(x.shape[-1] // LANES):
        blk = x[:, j * LANES:(j + 1) * LANES]
        lo = jnp.sum(jnp.where(first, blk, 0.0), axis=-1, keepdims=True)
        hi = jnp.sum(jnp.where(first, 0.0, blk), axis=-1, keepdims=True)
        outs.append(jnp.where(first, lo, hi))
    return jnp.concatenate(outs, axis=-1)


def _block_mask(shape, row_blk, col_blk):
    keep = _vdiv(_iota(shape, 0), row_blk) == _vdiv(_iota(shape, 1), col_blk)
    return jnp.where(keep, 1.0, 0.0).astype(BF16)


def _block_diag(x, mask):
    reps = mask.shape[0] // x.shape[0]
    return jnp.concatenate([x.astype(BF16)] * reps, axis=0) * mask


def _expand_heads(x, expand):
    hi = x.astype(BF16)
    r1 = x - hi.astype(F32)
    mid = r1.astype(BF16)
    lo = (r1 - mid.astype(F32)).astype(BF16)
    return _dot(hi, expand) + _dot(mid, expand) + _dot(lo, expand)


def _cumsum_rows(tri, x):
    hi = x.astype(BF16)
    lo = (x - hi.astype(F32)).astype(BF16)
    return _dot(tri, hi) + _dot(tri, lo)


def _rope_kernel(pos_ref, cos_ref, sin_ref):
    lane = _iota((1, ROPE_W), 1)
    half = RET_DK // 2
    k = _vmod(lane, half).astype(F32)
    inv_freq = jnp.exp(k * (-math.log(ROPE_BASE) / half))
    ang = pos_ref[...].astype(F32) * inv_freq
    first = _vmod(lane, RET_DK) < half
    cos_ref[...] = jnp.cos(ang)
    sin_ref[...] = jnp.where(first, -1.0, 1.0) * jnp.sin(ang)


def _rope_tables(pos_col):
    t = pos_col.shape[0]
    tm = TM_ROPE
    w = ROPE_W
    return pl.pallas_call(
        _rope_kernel,
        grid=(t // tm,),
        in_specs=[pl.BlockSpec((tm, 1), lambda i: (i, 0))],
        out_specs=[pl.BlockSpec((tm, w), lambda i: (i, 0))] * 2,
        out_shape=[jax.ShapeDtypeStruct((t, w), F32)] * 2,
        compiler_params=pltpu.CompilerParams(dimension_semantics=("arbitrary",)),
        name="rope_tables",
    )(pos_col)


def _mixproj_kernel(xn_ref, w_ref, cos_ref, sin_ref, retw_ref, convw_ref, convb_ref, dtb_ref, alog_ref,
                    dskip_ref, ssdw_ref, wgk_ref, bgk_ref, glaw_ref, h_ref,
                    proj_a, proj_b, ret_s, ssd_s, gla_s, stage, m_heads, m_groups, m_gla, m_expand, *,
                    chunks_per_seq):
    C = CHUNK
    i = pl.program_id(0)
    cur = jnp.maximum(i - 1, 0)

    @pl.when(i == 0)
    def _():
        m_heads[...] = _block_mask(m_heads.shape, C, 64)
        m_groups[...] = _block_mask(m_groups.shape, C * SSD_HEADS // SSD_GROUPS, SSD_STATE)
        m_gla[...] = _block_mask(m_gla.shape, C, GLA_DK)
        m_expand[...] = _block_mask(m_expand.shape, 1, SSD_HEAD_DIM)
        proj_b[...] = jnp.zeros_like(proj_b)

    @pl.when(lax.rem(cur, chunks_per_seq) == 0)
    def _():
        ret_s[...] = jnp.zeros_like(ret_s)
        ssd_s[...] = jnp.zeros_like(ssd_s)
        gla_s[...] = jnp.zeros_like(gla_s)
        for g in range(MIX_G):
            stage[g, 0:8, :] = jnp.zeros((8, SSD_XBC), F32)

    def conv_act(sq, pref):
        stage[sq, 8:8 + C, :] = pref[sq * C:(sq + 1) * C, COL["sxbc"]:COL["sxbc"] + SSD_XBC]
        acc = convb_ref[...] + convw_ref[0:1, :] * stage[sq, 5:5 + C, :]
        for j in range(1, SSD_CONV):
            acc = acc + convw_ref[j:j + 1, :] * stage[sq, 5 + j:5 + j + C, :]
        stage[sq, 0:8, :] = stage[sq, C:C + 8, :]
        return _silu(acc)

    lane256 = _iota((1, 256), 1)
    head = _vdiv(lane256, 64).astype(F32)
    log_gamma = jnp.log(1.0 - jnp.exp((-5.0 - head) * math.log(2.0)))
    row = _iota((C, 1), 0).astype(F32)
    dist = row - _vmod(lane256, 64).astype(F32)
    ret_intra = jnp.where(dist >= 0, jnp.exp(log_gamma * jnp.maximum(dist, 0.0)), 0.0)
    ret_qdec = jnp.exp(log_gamma * (row + 1.0))
    ret_kdec = jnp.exp(log_gamma * (C - 1.0 - row))
    ret_cdec = jnp.exp(log_gamma * C)
    first_half = _vmod(lane256, RET_DK) < (RET_DK // 2)

    tri = jnp.where(_iota((C, C), 0) >= _iota((C, C), 1), 1.0, 0.0).astype(BF16)
    causal4 = _iota((C, 256), 0) >= _vmod(_iota((C, 256), 1), 64)
    causal8 = _iota((C, 512), 0) >= _vmod(_iota((C, 512), 1), 64)
    eye8 = _iota((C, 512), 0) == _vmod(_iota((C, 512), 1), 64)

    a_neg = -jnp.exp(alog_ref[...])

    def rot(t, cos, sin):
        sw = jnp.where(first_half, pltpu.roll(t, 256 - 32, 1), pltpu.roll(t, 32, 1))
        return t * cos + sw * sin

    def chunk_of(sq, pref):
        def seg(name, width):
            return pref[sq * C:(sq + 1) * C, COL[name]:COL[name] + width]

        xact = conv_act(sq, pref)
        yield

        reps = RET_HEADS * RET_DK // ROPE_W
        cos = jnp.concatenate([cos_ref[sq]] * reps, axis=-1)
        sin = jnp.concatenate([sin_ref[sq]] * reps, axis=-1)
        q = rot(seg("rq", 256), cos, sin)
        k = rot(seg("rk", 256), cos, sin) * (RET_DK ** -0.5)
        v = seg("rv", 256)
        vb = v.astype(BF16)
        kbd = _block_diag(k, m_heads[...])
        scores = _dot(q.astype(BF16), kbd, _NT) * ret_intra
        vbd = _block_diag(v, m_heads[...])
        yield
        s_prev = ret_s[sq]
        o = _dot(scores.astype(BF16), vbd) + _dot((q * ret_qdec).astype(BF16), s_prev.astype(BF16))
        contrib = _dot((k * ret_kdec).astype(BF16), vb, _TN)
        yield
        keep = _vdiv(_iota((256, 256), 0), RET_DK) == _vdiv(_iota((256, 256), 1), RET_DV)
        ret_s[sq] = jnp.where(keep, ret_cdec * s_prev + contrib, 0.0)
        mu = _seg_sum64(o) * (1.0 / RET_DV)
        oc = o - mu
        var = _seg_sum64(oc * oc) * (1.0 / RET_DV)
        o = oc * lax.rsqrt(var + LN_EPS) * retw_ref[...]
        h_ref[sq, :, 0:RET_W] = (_silu(seg("rg", 256)) * o).astype(BF16)
        yield

        xs = xact[:, 0:SSD_W]
        bm = xact[:, SSD_W:SSD_W + SSD_BC]
        cm = xact[:, SSD_W + SSD_BC:SSD_XBC]
        cmb = cm.astype(BF16)
        dt_c = _softplus(seg("sdt", LANES) + dtb_ref[...])
        acum_c = _cumsum_rows(tri, dt_c * a_neg)
        both = _expand_heads(jnp.concatenate([dt_c, acum_c], axis=0), m_expand[...])
        dt = both[0:C, :]
        acum = both[C:2 * C, :]
        yield
        arow = jnp.sum(jnp.where(eye8, acum, 0.0), axis=0, keepdims=True)
        decay = jnp.exp(jnp.where(causal8, acum - arow, NEG_BIG))
        b8 = _block_diag(bm, m_groups[...])
        cb = _dot(cmb, b8, _NT)
        yield
        m = (cb * decay).astype(BF16)
        xdt = xs * dt
        s2 = ssd_s[sq]
        half = SSD_W // SSD_GROUPS
        ys = []
        for g in range(SSD_GROUPS):
            xbd = _block_diag(xdt[:, g * half:(g + 1) * half], m_heads[...])
            ys.append(_dot(m[:, g * half:(g + 1) * half], xbd))
        y = jnp.concatenate(ys, axis=-1)
        y = y + _dot(cmb, s2.astype(BF16)) * jnp.exp(acum)
        y = y + dskip_ref[...] * xs
        a_last = acum[C - 1:C, :]
        sd = jnp.exp(a_last - acum)
        contrib_s = _dot(bm.astype(BF16), (xdt * sd).astype(BF16), _TN)
        yield
        keep_s = _vdiv(_iota(s2.shape, 0), SSD_STATE) == _vdiv(_iota(s2.shape, 1), half)
        ssd_s[sq] = jnp.where(keep_s, s2 * jnp.exp(a_last) + contrib_s, 0.0)
        yz = y * _silu(seg("sz", SSD_W))
        outs = []
        for g in range(SSD_GROUPS):
            blk = yz[:, g * half:(g + 1) * half]
            ms = jnp.mean(blk * blk, axis=-1, keepdims=True)
            outs.append(blk * lax.rsqrt(ms + NORM_EPS))
        h_ref[sq, :, RET_W:RET_W + SSD_W] = (jnp.concatenate(outs, axis=-1) * ssdw_ref[...]).astype(BF16)

        yield

        gq = seg("gq", GLA_QK) * (GLA_DK ** -0.5)
        gkk = seg("gk", GLA_QK)
        gv = seg("gv", GLA_W)
        gkl = _dot(seg("ggk", 128).astype(BF16), wgk_ref[...].astype(BF16)) + bgk_ref[...]
        yield
        log_a = -_softplus(-gkl) * (1.0 / GLA_TEMP)
        b = _cumsum_rows(tri, log_a)
        yield
        q_t = (gq * jnp.exp(b)).astype(BF16)
        k_t = gkk * jnp.exp(-b)
        kbd_g = _block_diag(k_t, m_gla[...])
        att = jnp.where(causal4, _dot(q_t, kbd_g, _NT), 0.0)
        yield
        vbd_g = _block_diag(gv, m_heads[...])
        st = gla_s[sq]
        og = _dot(att.astype(BF16), vbd_g) + _dot(q_t, st.astype(BF16), _NT)
        b_last = b[C - 1:C, :]
        kd = (gkk * jnp.exp(b_last - b)).astype(BF16)
        contrib_g = _dot(gv.astype(BF16), kd, _TN)
        yield
        keep_g = _vdiv(_iota(st.shape, 0), GLA_DV) == _vdiv(_iota(st.shape, 1), GLA_DK)
        gla_s[sq] = jnp.where(keep_g, st * jnp.exp(b_last) + contrib_g, 0.0)
        ms = _seg_sum64(og * og) * (1.0 / GLA_DV)
        og = og * lax.rsqrt(ms + NORM_EPS) * glaw_ref[...]
        h_ref[sq, :, RET_W + SSD_W:D_MIX] = (_silu(seg("gg", GLA_W)) * og).astype(BF16)

    def step(p_read, p_write):
        xb = xn_ref[...].reshape(MIX_G * C, D_MODEL).astype(BF16)
        edges = list(range(0, NP, PROJ_SLAB)) + [NP]
        slabs = list(zip(edges[:-1], edges[1:]))

        live = [chunk_of(sq, p_read) for sq in range(MIX_G)]
        while live or slabs:
            live = [g for g in live if next(g, "done") != "done"]
            if slabs:
                lo, hi = slabs.pop(0)
                p_write[:, lo:hi] = _dot(xb, w_ref[:, lo:hi])

    @pl.when(lax.rem(i, 2) == 0)
    def _():
        step(proj_b, proj_a)

    @pl.when(lax.rem(i, 2) == 1)
    def _():
        step(proj_a, proj_b)


def _mixproj(x3, w_p, cos_t, sin_t, params):
    batch, seq, _ = x3.shape
    assert batch % MIX_G == 0 and seq % CHUNK == 0
    cps = seq // CHUNK
    n = (batch // MIX_G) * cps

    def cur_map(i):
        c = jnp.clip(i - 1, 0, n - 1)
        return (c // cps, c % cps, 0)

    def next_map(i):
        c = jnp.minimum(i, n - 1)
        return (c // cps, c % cps, 0)

    const = lambda i: (0, 0)
    specs = [pl.BlockSpec((MIX_G, CHUNK, D_MODEL), next_map),
             pl.BlockSpec((D_MODEL, NP), const),
             pl.BlockSpec((MIX_G, CHUNK, ROPE_W), cur_map),
             pl.BlockSpec((MIX_G, CHUNK, ROPE_W), cur_map)]
    specs += [pl.BlockSpec(p.shape, const) for p in params]
    return pl.pallas_call(
        functools.partial(_mixproj_kernel, chunks_per_seq=cps),
        grid=(n + 1,),
        in_specs=specs,
        out_specs=pl.BlockSpec((MIX_G, CHUNK, D_MIX), cur_map),
        out_shape=jax.ShapeDtypeStruct((batch, seq, D_MIX), BF16),
        scratch_shapes=[pltpu.VMEM((MIX_G * CHUNK, NP), F32),
                        pltpu.VMEM((MIX_G * CHUNK, NP), F32),
                        pltpu.VMEM((MIX_G, 256, 256), F32),
                        pltpu.VMEM((MIX_G, SSD_BC, SSD_W), F32),
                        pltpu.VMEM((MIX_G, GLA_W, GLA_QK), F32),
                        pltpu.VMEM((MIX_G, CHUNK + 8, SSD_XBC), F32),
                        pltpu.VMEM((RET_HEADS * CHUNK, 256), BF16),
                        pltpu.VMEM((SSD_HEADS * CHUNK, SSD_BC), BF16),
                        pltpu.VMEM((GLA_HEADS * CHUNK, GLA_QK), BF16),
                        pltpu.VMEM((LANES, SSD_W), BF16)],
        compiler_params=pltpu.CompilerParams(dimension_semantics=("arbitrary",),
                                             vmem_limit_bytes=VMEM_LIMIT),
        name="inproj_mixer",
    )(x3, w_p, cos_t, sin_t, *params)


def _layer_norm(y, g, b):
    mu = jnp.mean(y, axis=-1, keepdims=True)
    yc = y - mu
    var = jnp.mean(yc * yc, axis=-1, keepdims=True)
    return yc * lax.rsqrt(var + LN_EPS) * g + b


def _post_kernel(h_ref, x_ref, wout_ref, g_ref, b_ref, wrh_ref, wrc_ref, br_ref,
                 x1_ref, x1p_ref, mi_ref, mf_ref, cnt_ref, carry):
    sub = TM_POST // POST_SPLIT

    @pl.when(pl.program_id(0) == 0)
    def _():
        carry[...] = jnp.zeros_like(carry)

    lane_i = _iota((sub, LANES), 1)
    lane = lane_i.astype(F32)
    found = {}

    def sub_tile(part):
        rows = slice(part * sub, (part + 1) * sub)
        mix = _dot(h_ref[rows, :], wout_ref[...])
        yield
        x1 = _layer_norm(DEEPNORM_ALPHA * x_ref[rows, :] + mix, g_ref[...], b_ref[...])
        x1_ref[rows, :] = x1
        x1p_ref[rows, :] = _pack_rows(x1)
        x_hi = x1.astype(BF16)
        x_lo = (x1 - x_hi.astype(F32)).astype(BF16)
        both = _dot(x_hi, wrc_ref[...])
        logits = both[:, :LANES] + both[:, LANES:] + _dot(x_lo, wrh_ref[...]) + br_ref[...]
        yield
        work = logits
        vals, idxs = [], []
        multi = jnp.zeros((sub, LANES), F32)
        for _ in range(TOP_K):
            m = jnp.max(work, axis=-1, keepdims=True)
            idx = jnp.min(jnp.where(work == m, lane, float(LANES)), axis=-1, keepdims=True)
            hit = lane == idx
            multi = multi + hit.astype(F32)
            work = jnp.where(hit, -jnp.inf, work)
            vals.append(m)
            idxs.append(idx)
            yield
        exps = [jnp.exp(v - vals[0]) for v in vals]
        denom = exps[0] + exps[1] + exps[2] + exps[3]
        gates = [e / denom for e in exps]
        before = (_iota((sub, sub), 0) > _iota((sub, sub), 1)).astype(BF16)
        found[part] = (idxs, gates, _dot(before, multi.astype(BF16)), multi)

    live = [sub_tile(part) for part in range(POST_SPLIT)]
    while live:
        live = [g for g in live if next(g, "done") != "done"]

    base = carry[...]
    for part in range(POST_SPLIT):
        idxs, gates, prior_local, multi = found[part]
        prior = prior_local + base
        mi = jnp.zeros((sub, LANES), F32)
        mf = jnp.zeros((sub, LANES), F32)
        for kk in range(TOP_K):
            rank = jnp.sum(jnp.where(lane == idxs[kk], prior, 0.0), axis=-1, keepdims=True)
            mi = jnp.where(lane_i == kk, idxs[kk], mi)
            mi = jnp.where(lane_i == TOP_K + kk, rank, mi)
            mf = jnp.where(lane_i == kk, gates[kk], mf)
        mi_ref[:, part * sub:(part + 1) * sub] = jnp.transpose(mi)[0:2 * TOP_K, :].astype(jnp.int32)
        mf_ref[part * sub:(part + 1) * sub, :] = mf
        base = base + jnp.sum(multi, axis=0, keepdims=True)
    carry[...] = base
    cnt_ref[...] = jnp.broadcast_to(base, cnt_ref.shape)


def _post(h, x2, w_out_b, ln_g, ln_b, wr_hi, wr_cat, br_p):
    t = x2.shape[0]
    tm = TM_POST
    row = lambda i: (i, 0)
    const = lambda i: (0, 0)
    return pl.pallas_call(
        _post_kernel,
        grid=(t // tm,),
        in_specs=[pl.BlockSpec((tm, D_MIX), row), pl.BlockSpec((tm, D_MODEL), row),
                  pl.BlockSpec((D_MIX, D_MODEL), const), pl.BlockSpec((1, D_MODEL), const),
                  pl.BlockSpec((1, D_MODEL), const), pl.BlockSpec((D_MODEL, LANES), const),
                  pl.BlockSpec((D_MODEL, 2 * LANES), const), pl.BlockSpec((1, LANES), const)],
        out_specs=[pl.BlockSpec((tm, D_MODEL), row), pl.BlockSpec((tm, ROW_WORDS), row),
                   pl.BlockSpec((2 * TOP_K, tm), lambda i: (0, i)), pl.BlockSpec((tm, LANES), row),
                   pl.BlockSpec((8, LANES), const)],
        out_shape=[jax.ShapeDtypeStruct((t, D_MODEL), F32), jax.ShapeDtypeStruct((t, ROW_WORDS), jnp.int32),
                   jax.ShapeDtypeStruct((2 * TOP_K, t), jnp.int32), jax.ShapeDtypeStruct((t, LANES), F32),
                   jax.ShapeDtypeStruct((8, LANES), F32)],
        scratch_shapes=[pltpu.VMEM((1, LANES), F32)],
        compiler_params=pltpu.CompilerParams(dimension_semantics=("arbitrary",),
                                             vmem_limit_bytes=VMEM_LIMIT),
        name="outproj_ln_router",
    )(h, x2, w_out_b, ln_g, ln_b, wr_hi, wr_cat, br_p)


def _ffn_kernel(be_ref, nv_ref, par_ref, ahead_ref, nx_ref, used_ref, x_ref, wg_ref, bg_ref, wu_ref, bu_ref, wd_ref, bd_ref,
                o_ref, ga, ua, da, gb, ub, db):
    del nx_ref
    wa, wb = (ga, ua, da), (gb, ub, db)
    i = pl.program_id(0)
    b = jnp.clip(i - 1, 0, be_ref.shape[0] - 1)
    valid = jnp.logical_and(i >= 1, b < nv_ref[0])
    parity = par_ref[b]
    cast_ahead = ahead_ref[b]
    f32_weights = (wg_ref, wu_ref, wd_ref)

    def cast_into(dst, j):
        dst[j][...] = f32_weights[j][0, 0].astype(BF16)

    @pl.when(i == 0)
    def _():
        for j in range(3):
            cast_into(wa, j)

    def block(cur, nxt, r=BM):
        x = _unpack_rows(x_ref[0:r, :]).astype(BF16)
        if nxt is not None:
            cast_into(nxt, 0)
        hg = jnp.minimum(_dot(x, cur[0][...]) + bg_ref[0, 0], SWIGLU_LIMIT)
        if nxt is not None:
            cast_into(nxt, 1)
        hu = jnp.clip(_dot(x, cur[1][...]) + bu_ref[0, 0], -SWIGLU_LIMIT, SWIGLU_LIMIT)
        hh = (hu + 1.0) * hg * jax.nn.sigmoid(SWIGLU_ALPHA * hg)
        if nxt is not None:
            cast_into(nxt, 2)
        o_ref[0:r, :] = _pack_rows(_dot(hh.astype(BF16), cur[2][...]) + bd_ref[0, 0])
        if r < BM:
            o_ref[r:BM, :] = jnp.zeros((BM - r, ROW_WORDS), jnp.int32)

    for par, cur, other in ((0, wa, wb), (1, wb, wa)):
        @pl.when(valid & (parity == par) & (cast_ahead == 0))
        def _(cur=cur):
            block(cur, None)

        for used in range(1, BM // FFN_SUB + 1):
            @pl.when(valid & (parity == par) & (cast_ahead == 1) & (used_ref[b] == used))
            def _(cur=cur, nxt=other, r=used * FFN_SUB):
                block(cur, nxt, r)


def _ffn(layer, block_expert, n_valid, end_padded, used_subs, xin, wg, bg, wu, bu, wd, bd):
    cap = xin.shape[0]
    nb = cap // BM

    ids = jnp.arange(nb, dtype=jnp.int32)
    first = (ids == 0) | (block_expert != jnp.roll(block_expert, 1))
    parity = (jnp.cumsum(first.astype(jnp.int32)) - 1) & 1
    after_run = jnp.sum(jnp.where(block_expert[:, None] == jnp.arange(N_EXPERTS, dtype=jnp.int32)[None, :],
                                  end_padded[None, :], 0), axis=1) // BM
    has_next = after_run < n_valid[0]
    next_expert = jnp.where(has_next, block_expert[jnp.minimum(after_run, nb - 1)], block_expert)
    ahead = ((ids + 1 == after_run) & has_next).astype(jnp.int32)

    def blk(i, nv):
        return jnp.clip(i - 1, 0, jnp.maximum(nv[0] - 1, 0))

    row = lambda i, be, nv, pa, ah, nx, us: (blk(i, nv), 0)
    bmap = lambda i, be, nv, pa, ah, nx, us: (layer, be[blk(i, nv)], 0, 0)
    wmap = lambda i, be, nv, pa, ah, nx, us: (layer, jnp.where(i == 0, be[0], nx[blk(i, nv)]), 0, 0)
    grid_spec = pltpu.PrefetchScalarGridSpec(
        num_scalar_prefetch=6,
        grid=(nb + 1,),
        in_specs=[pl.BlockSpec((BM, ROW_WORDS), row),
                  pl.BlockSpec((1, 1, D_MODEL, D_FF), wmap), pl.BlockSpec((1, 1, 1, D_FF), bmap),
                  pl.BlockSpec((1, 1, D_MODEL, D_FF), wmap), pl.BlockSpec((1, 1, 1, D_FF), bmap),
                  pl.BlockSpec((1, 1, D_FF, D_MODEL), wmap), pl.BlockSpec((1, 1, 1, D_MODEL), bmap)],
        out_specs=pl.BlockSpec((BM, ROW_WORDS), row),
        scratch_shapes=[pltpu.VMEM((D_MODEL, D_FF), BF16), pltpu.VMEM((D_MODEL, D_FF), BF16),
                        pltpu.VMEM((D_FF, D_MODEL), BF16)] * 2,
    )
    return pl.pallas_call(
        _ffn_kernel,
        grid_spec=grid_spec,
        out_shape=jax.ShapeDtypeStruct((cap, ROW_WORDS), jnp.int32),
        compiler_params=pltpu.CompilerParams(dimension_semantics=("arbitrary",),
                                             vmem_limit_bytes=VMEM_LIMIT),
        name="expert_ffn",
    )(block_expert, n_valid, parity, ahead, next_expert, used_subs, xin, wg, bg, wu, bu, wd, bd)


def _sc_gather(table, idx3):
    nw, n_chunks, ch = idx3.shape
    width = table.shape[1]
    per_worker = n_chunks * ch
    mesh = plsc.VectorSubcoreMesh(core_axis_name="c", subcore_axis_name="s")
    n_cores = mesh.num_cores
    assert nw == n_cores * mesh.num_subcores and n_chunks % 2 == 0 and ch == SC_CHUNK

    def body(table_hbm, idx_hbm, out_hbm, idx_v, rows0, rows1, sem_g0, sem_g1, sem_w0, sem_w1):
        wid = lax.axis_index("s") * n_cores + lax.axis_index("c")
        base = wid * per_worker
        pltpu.sync_copy(idx_hbm.at[wid], idx_v)

        @pl.loop(0, n_chunks, step=2)
        def _(c):
            g0 = pltpu.async_copy(table_hbm.at[idx_v.at[c]], rows0, sem_g0)
            g1 = pltpu.async_copy(table_hbm.at[idx_v.at[c + 1]], rows1, sem_g1)
            g0.wait()
            w0 = pltpu.async_copy(rows0, out_hbm.at[pl.ds(base + c * ch, ch)], sem_w0)
            g1.wait()
            w1 = pltpu.async_copy(rows1, out_hbm.at[pl.ds(base + (c + 1) * ch, ch)], sem_w1)
            w0.wait()
            w1.wait()

    return pl.kernel(
        body,
        out_type=jax.ShapeDtypeStruct((nw * per_worker, width), table.dtype),
        mesh=mesh,
        scratch_types=[pltpu.VMEM((n_chunks, ch), jnp.int32),
                       pltpu.VMEM((ch, width), table.dtype), pltpu.VMEM((ch, width), table.dtype),
                       pltpu.SemaphoreType.DMA, pltpu.SemaphoreType.DMA,
                       pltpu.SemaphoreType.DMA, pltpu.SemaphoreType.DMA],
        name="sc_row_gather",
    )(table, idx3)


def _sc_scatter(rows, idx3, n_out):
    nw, n_lists, ch = idx3.shape
    n_chunks = n_lists // TOP_K
    width = rows.shape[1]
    per_worker = n_chunks * ch
    mesh = plsc.VectorSubcoreMesh(core_axis_name="c", subcore_axis_name="s")
    n_cores = mesh.num_cores
    assert nw == n_cores * mesh.num_subcores and n_chunks % 2 == 0 and ch == SC_CHUNK
    assert nw * per_worker == rows.shape[0]

    def body(rows_hbm, idx_hbm, out_hbm, idx_v, buf0, buf1, sem_r0, sem_r1, sem_w0, sem_w1):
        wid = lax.axis_index("s") * n_cores + lax.axis_index("c")
        base = wid * per_worker
        pltpu.sync_copy(idx_hbm.at[wid], idx_v)

        @pl.loop(0, n_chunks, step=2)
        def _(c):
            r0 = pltpu.async_copy(rows_hbm.at[pl.ds(base + c * ch, ch)], buf0, sem_r0)
            r1 = pltpu.async_copy(rows_hbm.at[pl.ds(base + (c + 1) * ch, ch)], buf1, sem_r1)
            r0.wait()
            w0 = [pltpu.async_copy(buf0, out_hbm.at[idx_v.at[c * TOP_K + k]], sem_w0) for k in range(TOP_K)]
            r1.wait()
            w1 = [pltpu.async_copy(buf1, out_hbm.at[idx_v.at[(c + 1) * TOP_K + k]], sem_w1) for k in range(TOP_K)]
            for w in w0 + w1:
                w.wait()

    return pl.kernel(
        body,
        out_type=jax.ShapeDtypeStruct((n_out, width), rows.dtype),
        mesh=mesh,
        scratch_types=[pltpu.VMEM((n_lists, ch), jnp.int32),
                       pltpu.VMEM((ch, width), rows.dtype), pltpu.VMEM((ch, width), rows.dtype),
                       pltpu.SemaphoreType.DMA, pltpu.SemaphoreType.DMA,
                       pltpu.SemaphoreType.DMA, pltpu.SemaphoreType.DMA],
        name="sc_row_scatter",
    )(rows, idx3)


def _combine_kernel(x_ref, rows_ref, mf_ref, g_ref, b_ref, o_ref):
    mf = mf_ref[...]
    y = DEEPNORM_ALPHA * x_ref[...]
    for kk in range(TOP_K):
        y = y + mf[:, kk:kk + 1] * _unpack_rows(rows_ref[kk])
    o_ref[...] = _layer_norm(y, g_ref[...], b_ref[...])


def _combine(x1, rows, mf, ln_g, ln_b):
    t = x1.shape[0]
    tm = TM_COMB
    row = lambda i: (i, 0)
    const = lambda i: (0, 0)
    return pl.pallas_call(
        _combine_kernel,
        grid=(t // tm,),
        in_specs=[pl.BlockSpec((tm, D_MODEL), row),
                  pl.BlockSpec((TOP_K, tm, ROW_WORDS), lambda i: (0, i, 0)),
                  pl.BlockSpec((tm, LANES), row),
                  pl.BlockSpec((1, D_MODEL), const), pl.BlockSpec((1, D_MODEL), const)],
        out_specs=pl.BlockSpec((tm, D_MODEL), row),
        out_shape=jax.ShapeDtypeStruct((t, D_MODEL), F32),
        compiler_params=pltpu.CompilerParams(dimension_semantics=("arbitrary",),
                                             vmem_limit_bytes=VMEM_LIMIT),
        name="combine_ln",
    )(x1, rows, mf, ln_g, ln_b)


def _relayout_w_in(w):
    widths = (256, 256, 256, 256, SSD_W, SSD_XBC, SSD_HEADS, GLA_QK, GLA_QK, GLA_W, GLA_RANK, GLA_W)
    offs = [0]
    for wd in widths:
        offs.append(offs[-1] + wd)
    parts = [w[:, offs[i]:offs[i + 1]] for i in range(len(widths))]
    parts[6] = jnp.pad(parts[6], ((0, 0), (0, LANES - SSD_HEADS)))
    parts[10] = jnp.pad(parts[10], ((0, 0), (0, LANES - GLA_RANK)))
    return jnp.concatenate(parts, axis=1).astype(BF16)


def _rep_heads(p):
    return jnp.repeat(p, SSD_HEAD_DIM)[None, :]


def _pad_heads(p):
    return jnp.pad(p, (0, LANES - SSD_HEADS))[None, :]


def kernel(x, positions, w_in, w_out, ret_norm_w, ssd_conv_w, ssd_conv_b, ssd_dt_bias, ssd_a_log, ssd_d,
           ssd_norm_w, gla_w_gk2, gla_b_gk2, gla_norm_w, ln1_g, ln1_b, w_router, b_router, w_gate, b_gate,
           w_up, b_up, w_down, b_down, ln2_g, ln2_b):
    batch, seq, d = x.shape
    t = batch * seq
    depth = w_in.shape[0]
    assert d == D_MODEL and t % TM_POST == 0 and t % TM_COMB == 0 and t % TM_ROPE == 0
    n_assign = t * TOP_K
    nb = pl.cdiv(n_assign, BM) + N_EXPERTS
    cap = nb * BM

    cos_t, sin_t = _rope_tables(positions.reshape(t, 1))
    cos_t = cos_t.reshape(batch, seq, -1)
    sin_t = sin_t.reshape(batch, seq, -1)
    x2 = x.reshape(t, d)

    for l in range(depth):
        params = (ret_norm_w[l][None, :], ssd_conv_w[l], ssd_conv_b[l][None, :], _pad_heads(ssd_dt_bias[l]),
                  _pad_heads(ssd_a_log[l]), _rep_heads(ssd_d[l]), ssd_norm_w[l][None, :],
                  jnp.pad(gla_w_gk2[l], ((0, LANES - GLA_RANK), (0, 0))), gla_b_gk2[l][None, :],
                  gla_norm_w[l][None, :])
        h = _mixproj(x2.reshape(batch, seq, d), _relayout_w_in(w_in[l]), cos_t, sin_t, params).reshape(t, D_MIX)

        wr_p = jnp.pad(w_router[l], ((0, 0), (0, LANES - N_EXPERTS)))
        br_p = jnp.pad(b_router[l], (0, LANES - N_EXPERTS), constant_values=NEG_BIG)[None, :]
        wr_hi = wr_p.astype(BF16)
        wr_lo = (wr_p - wr_hi.astype(F32)).astype(BF16)
        x1, x1p, mi, mf, cnt = _post(h, x2, w_out[l].astype(BF16), ln1_g[l][None, :], ln1_b[l][None, :],
                                     wr_hi, jnp.concatenate([wr_hi, wr_lo], axis=1), br_p)

        counts = cnt[0, :N_EXPERTS].astype(jnp.int32)
        padded = (counts + BM - 1) // BM * BM
        end_padded = jnp.cumsum(padded)
        start_padded = end_padded - padded
        top_idx = mi[:TOP_K]
        start_of = jnp.sum(jnp.where(top_idx[None] == jnp.arange(N_EXPERTS, dtype=jnp.int32)[:, None, None],
                                     start_padded[:, None, None], 0), axis=0)
        dest = start_of + mi[TOP_K:]
        block_start = jnp.arange(nb, dtype=jnp.int32) * BM
        block_expert = jnp.minimum(jnp.sum((end_padded[None, :] <= block_start[:, None]).astype(jnp.int32), axis=1),
                                   N_EXPERTS - 1)
        n_valid = (end_padded[-1:] // BM).astype(jnp.int32)

        scatter_idx = dest.reshape(TOP_K, SC_WORKERS, -1, SC_CHUNK).transpose(1, 2, 0, 3)
        xin = _sc_scatter(x1p, scatter_idx.reshape(SC_WORKERS, -1, SC_CHUNK), cap)
        owner = block_expert[:, None] == jnp.arange(N_EXPERTS, dtype=jnp.int32)[None, :]
        filled = jnp.sum(jnp.where(owner, (start_padded + counts)[None, :], 0), axis=1) - block_start
        used_subs = jnp.clip((jnp.clip(filled, 0, BM) + FFN_SUB - 1) // FFN_SUB, 1, BM // FFN_SUB)
        yb = _ffn(l, block_expert, n_valid, end_padded, used_subs, xin, w_gate, b_gate[:, :, None, :], w_up, b_up[:, :, None, :],
                  w_down, b_down[:, :, None, :])
        rows = _sc_gather(yb, dest.reshape(SC_WORKERS, -1, SC_CHUNK))
        x2 = _combine(x1, rows.reshape(TOP_K, t, ROW_WORDS), mf, ln2_g[l][None, :], ln2_b[l][None, :])
    return x2.reshape(batch, seq, d)
```
